```python
import math
import jax, jax.numpy as jnp
from jax import lax
import numpy as np

D_MODEL = 1024
BATCH = 16
SEQ = 2048
DEPTH = 1

MLA_HEADS = 8
MLA_Q_RANK = 384
MLA_KV_RANK = 256
MLA_NOPE = 64
MLA_ROPE = 32
MLA_V = 64
ROPE_BASE = 10000.0
DIFF_HEADS = 4
DIFF_D = 64
IN_SPLITS = (MLA_Q_RANK, MLA_KV_RANK, MLA_ROPE,
             2 * DIFF_HEADS * DIFF_D, 2 * DIFF_HEADS * DIFF_D, DIFF_HEADS * 2 * DIFF_D)
IN_WIDTH = sum(IN_SPLITS)
IN_OFFSETS = tuple(int(v) for v in np.cumsum(IN_SPLITS)[:-1])
REL_BUCKETS = 32
REL_MAX_DIST = 128
MEM_LEN = 256
MEM_HEADS = 4
MEM_HEAD_DIM = D_MODEL // MEM_HEADS
D_FF = 2816
Q_BLOCK = 128
ALPHA = (2.0 * DEPTH) ** 0.25
BETA = (8.0 * DEPTH) ** -0.25
LN_EPS = 1e-5
RMS_EPS = 1e-6

kernel_name = "hybrid_mla_diffattn_macaron_deepnorm"


def layer_norm(x, g, b):
    xf = x.astype(jnp.float32)
    mu = jnp.mean(xf, axis=-1, keepdims=True)
    var = jnp.mean(jnp.square(xf - mu), axis=-1, keepdims=True)
    y = (xf - mu) * lax.rsqrt(var + LN_EPS) * g.astype(jnp.float32) + b.astype(jnp.float32)
    return y.astype(x.dtype)


def rms_norm(x, g):
    xf = x.astype(jnp.float32)
    y = xf * lax.rsqrt(jnp.mean(jnp.square(xf), axis=-1, keepdims=True) + RMS_EPS)
    return (y * g.astype(jnp.float32)).astype(x.dtype)


def swiglu(x, w1, w3, w2):
    return (jax.nn.silu(x @ w1) * (x @ w3)) @ w2


def rope(x, cos, sin):
    half = x.shape[-1] // 2
    x1, x2 = x[..., :half], x[..., half:]
    c, s = cos[None, :, None, :], sin[None, :, None, :]
    return jnp.concatenate([x1 * c - x2 * s, x2 * c + x1 * s], axis=-1)


def t5_bucket(q_pos, k_pos):
    n = jnp.maximum(q_pos[:, None] - k_pos[None, :], 0)
    max_exact = REL_BUCKETS // 2
    nf = jnp.maximum(n, 1).astype(jnp.float32)
    large = max_exact + (jnp.log(nf / max_exact) / math.log(REL_MAX_DIST / max_exact)
                         * (REL_BUCKETS - max_exact)).astype(jnp.int32)
    large = jnp.minimum(large, REL_BUCKETS - 1)
    return jnp.where(n < max_exact, n, large)


def parallel_mixer(h, rel_bias, w_in, q_norm_g, kv_norm_g, w_q_b, w_kv_b,
                   lam_q1, lam_k1, lam_q2, lam_k2, lam_init, diff_norm_g,
                   w_gate, b_gate, w_up_mla, w_up_diff, w_o):
    B, S, _ = h.shape
    proj = h @ w_in
    c_q, c_kv, k_rope, q_d, k_d, v_d = jnp.split(proj, IN_OFFSETS, axis=-1)

    q = (rms_norm(c_q, q_norm_g) @ w_q_b).reshape(B, S, MLA_HEADS, MLA_NOPE + MLA_ROPE)
    q_nope, q_rope = q[..., :MLA_NOPE], q[..., MLA_NOPE:]
    kv = (rms_norm(c_kv, kv_norm_g) @ w_kv_b).reshape(B, S, MLA_HEADS, MLA_NOPE + MLA_V)
    k_nope, v_m = kv[..., :MLA_NOPE], kv[..., MLA_NOPE:]
    pos = jnp.arange(S)
    inv_freq = ROPE_BASE ** (-jnp.arange(0, MLA_ROPE, 2, dtype=jnp.float32) / MLA_ROPE)
    ang = pos.astype(jnp.float32)[:, None] * inv_freq[None, :]
    cos, sin = jnp.cos(ang).astype(h.dtype), jnp.sin(ang).astype(h.dtype)
    q_rope = rope(q_rope, cos, sin)
    k_rope = rope(k_rope[:, :, None, :], cos, sin)[:, :, 0]

    q_d = q_d.reshape(B, S, 2 * DIFF_HEADS, DIFF_D)
    k_d = k_d.reshape(B, S, 2 * DIFF_HEADS, DIFF_D)
    v_d = v_d.reshape(B, S, DIFF_HEADS, 2 * DIFF_D)
    f32 = jnp.float32
    lam = (jnp.exp(jnp.sum(lam_q1.astype(f32) * lam_k1.astype(f32)))
           - jnp.exp(jnp.sum(lam_q2.astype(f32) * lam_k2.astype(f32))) + lam_init)

    mla_scale = (MLA_NOPE + MLA_ROPE) ** -0.5
    diff_scale = DIFF_D ** -0.5
    outs_m, outs_d = [], []
    for i in range(S // Q_BLOCK):
        lo, hi = i * Q_BLOCK, (i + 1) * Q_BLOCK
        q_pos = jnp.arange(lo, hi)
        k_pos = jnp.arange(hi)
        causal = q_pos[:, None] >= k_pos[None, :]
        s = (jnp.einsum('bqhd,bkhd->bhqk', q_nope[:, lo:hi], k_nope[:, :hi])
             + jnp.einsum('bqhr,bkr->bhqk', q_rope[:, lo:hi], k_rope[:, :hi]))
        p = jax.nn.softmax(jnp.where(causal, s.astype(f32) * mla_scale, -jnp.inf), axis=-1)
        outs_m.append(jnp.einsum('bhqk,bkhd->bqhd', p.astype(v_m.dtype), v_m[:, :hi]))
        bias = jnp.transpose(rel_bias[t5_bucket(q_pos, k_pos)], (2, 0, 1)).astype(f32)
        s = jnp.einsum('bqhd,bkhd->bhqk', q_d[:, lo:hi], k_d[:, :hi]).astype(f32) * diff_scale + bias
        p = jax.nn.softmax(jnp.where(causal, s, -jnp.inf), axis=-1)
        p = p.reshape(B, DIFF_HEADS, 2, Q_BLOCK, hi)
        a = p[:, :, 0] - lam * p[:, :, 1]
        outs_d.append(jnp.einsum('bhqk,bkhd->bqhd', a.astype(v_d.dtype), v_d[:, :hi]))

    o_m = jnp.concatenate(outs_m, axis=1).reshape(B, S, MLA_HEADS * MLA_V)
    o_d = jnp.concatenate(outs_d, axis=1)
    o_d = (rms_norm(o_d, diff_norm_g) * (1.0 - lam_init)).reshape(B, S, DIFF_HEADS * 2 * DIFF_D)

    y_m = o_m @ w_up_mla
    y_d = o_d @ w_up_diff
    g = jax.nn.sigmoid(h @ w_gate + b_gate)
    g_m, g_d = g[..., :D_MODEL], g[..., D_MODEL:]
    return (g_m * y_m + g_d * y_d) @ w_o


def memory_attention(h, mem, w_q, w_kv, w_o):
    B, S, _ = h.shape
    M = mem.shape[1]
    q = (h @ w_q).reshape(B, S, MEM_HEADS, MEM_HEAD_DIM)
    kv = mem @ w_kv
    k = kv[..., :D_MODEL].reshape(B, M, MEM_HEADS, MEM_HEAD_DIM)
    v = kv[..., D_MODEL:].reshape(B, M, MEM_HEADS, MEM_HEAD_DIM)
    s = jnp.einsum('bqhd,bkhd->bhqk', q, k).astype(jnp.float32) * MEM_HEAD_DIM ** -0.5
    p = jax.nn.softmax(s, axis=-1)
    o = jnp.einsum('bhqk,bkhd->bqhd', p.astype(v.dtype), v).reshape(B, S, D_MODEL)
    return o @ w_o


def setup_inputs(seed: int = 0) -> dict:
    key = jax.random.key(seed)
    ks = iter(jax.random.split(key, 64))
    L, D, F = DEPTH, D_MODEL, D_FF

    def dense(shape, fan_in, scale=1.0):
        return jax.random.normal(next(ks), shape, jnp.float32) * (fan_in ** -0.5) * scale

    def gain(shape):
        return 1.0 + 0.05 * jax.random.normal(next(ks), shape, jnp.float32)

    def small(shape, scale=0.05):
        return scale * jax.random.normal(next(ks), shape, jnp.float32)

    mla_in = MLA_HEADS * MLA_V
    diff_in = DIFF_HEADS * 2 * DIFF_D
    return {
        "x": jax.random.normal(next(ks), (BATCH, SEQ, D), jnp.float32),
        "mem": jax.random.normal(next(ks), (BATCH, MEM_LEN, D), jnp.float32),
        "rel_bias": small((REL_BUCKETS, 2 * DIFF_HEADS), 0.3),
        "ffn1_w1": dense((L, D, F), D),
        "ffn1_w3": dense((L, D, F), D),
        "ffn1_w2": dense((L, F, D), F, BETA),
        "ln1_g": gain((L, D)),
        "ln1_b": small((L, D)),
        "w_in": dense((L, D, IN_WIDTH), D),
        "q_norm_g": gain((L, MLA_Q_RANK)),
        "kv_norm_g": gain((L, MLA_KV_RANK)),
        "w_q_b": dense((L, MLA_Q_RANK, MLA_HEADS * (MLA_NOPE + MLA_ROPE)), MLA_Q_RANK),
        "w_kv_b": dense((L, MLA_KV_RANK, MLA_HEADS * (MLA_NOPE + MLA_V)), MLA_KV_RANK),
        "lam_q1": small((L, DIFF_D), 0.1),
        "lam_k1": small((L, DIFF_D), 0.1),
        "lam_q2": small((L, DIFF_D), 0.1),
        "lam_k2": small((L, DIFF_D), 0.1),
        "diff_norm_g": gain((L, 2 * DIFF_D)),
        "w_gate": dense((L, D, 2 * D), D),
        "b_gate": small((L, 2 * D), 0.1),
        "w_up_mla": dense((L, mla_in, D), mla_in, BETA),
        "w_up_diff": dense((L, diff_in, D), diff_in, BETA),
        "w_o": dense((L, D, D), D, BETA),
        "ln2_g": gain((L, D)),
        "ln2_b": small((L, D)),
        "mem_w_q": dense((L, D, D), D),
        "mem_w_kv": dense((L, D, 2 * D), D),
        "mem_w_o": dense((L, D, D), D, BETA),
        "ln3_g": gain((L, D)),
        "ln3_b": small((L, D)),
        "ffn2_w1": dense((L, D, F), D),
        "ffn2_w3": dense((L, D, F), D),
        "ffn2_w2": dense((L, F, D), F, BETA),
        "ln4_g": gain((L, D)),
        "ln4_b": small((L, D)),
    }


def reference(x, mem, rel_bias, ffn1_w1, ffn1_w3, ffn1_w2, ln1_g, ln1_b,
              w_in, q_norm_g, kv_norm_g, w_q_b, w_kv_b, lam_q1, lam_k1, lam_q2, lam_k2,
              diff_norm_g, w_gate, b_gate, w_up_mla, w_up_diff, w_o, ln2_g, ln2_b,
              mem_w_q, mem_w_kv, mem_w_o, ln3_g, ln3_b,
              ffn2_w1, ffn2_w3, ffn2_w2, ln4_g, ln4_b):
    h = x
    for l in range(DEPTH):
        lam_init = 0.8 - 0.6 * math.exp(-0.3 * l)
        h = layer_norm(ALPHA * h + 0.5 * swiglu(h, ffn1_w1[l], ffn1_w3[l], ffn1_w2[l]),
                       ln1_g[l], ln1_b[l])
        mix = parallel_mixer(h, rel_bias, w_in[l], q_norm_g[l], kv_norm_g[l], w_q_b[l], w_kv_b[l],
                             lam_q1[l], lam_k1[l], lam_q2[l], lam_k2[l], lam_init, diff_norm_g[l],
                             w_gate[l], b_gate[l], w_up_mla[l], w_up_diff[l], w_o[l])
        h = layer_norm(ALPHA * h + mix, ln2_g[l], ln2_b[l])
        h = layer_norm(ALPHA * h + memory_attention(h, mem, mem_w_q[l], mem_w_kv[l], mem_w_o[l]),
                       ln3_g[l], ln3_b[l])
        h = layer_norm(ALPHA * h + 0.5 * swiglu(h, ffn2_w1[l], ffn2_w3[l], ffn2_w2[l]),
                       ln4_g[l], ln4_b[l])
    return h
```

```python
import functools
import math

import jax
import jax.numpy as jnp
from jax import lax
from jax.experimental import pallas as pl
from jax.experimental.pallas import tpu as pltpu

F32 = jnp.float32
BF16 = jnp.bfloat16

DEPTH = 1
MLA_HEADS = 8
MLA_Q_RANK = 384
MLA_KV_RANK = 256
MLA_NOPE = 64
MLA_ROPE = 32
MLA_V = 64
ROPE_BASE = 10000.0
DIFF_HEADS = 4
DIFF_D = 64
REL_BUCKETS = 32
REL_MAX_DIST = 128
MEM_HEADS = 4
ALPHA = (2.0 * DEPTH) ** 0.25
LN_EPS = 1e-5
RMS_EPS = 1e-6
LAM_INIT = 0.8 - 0.6 * math.exp(-0.3 * 0)

LANES = 128
HEAD_PAD = 128
VMEM_LIMIT_BYTES = 56 * 1024 * 1024

ROW_TILE = 512
FFN_COL_TILE = 1408
ATTN_TQ = 512
ATTN_TK = 256


def _params(semantics):
    return pltpu.CompilerParams(dimension_semantics=semantics, vmem_limit_bytes=VMEM_LIMIT_BYTES)


def _dot(a, b):
    return jnp.dot(a, b, preferred_element_type=F32)


def _dot_nt(a, b):
    return lax.dot_general(a, b, (((1,), (1,)), ((), ())), preferred_element_type=F32)


def _dot_tn(a, b):
    return lax.dot_general(a, b, (((0,), (0,)), ((), ())), preferred_element_type=F32)


def _layer_norm(y, g, b):
    mu = jnp.mean(y, axis=-1, keepdims=True)
    d = y - mu
    var = jnp.mean(d * d, axis=-1, keepdims=True)
    return d * lax.rsqrt(var + LN_EPS) * g + b


def _rms_norm_rows(x, g):
    return x * lax.rsqrt(jnp.mean(x * x, axis=-1, keepdims=True) + RMS_EPS) * g


def _ffn_ln_kernel(x_ref, w1_ref, w3_ref, w2_ref, g_ref, b_ref, o_ref, xb_ref, acc_ref):
    j = pl.program_id(1)

    @pl.when(j == 0)
    def _():
        xb_ref[...] = x_ref[...].astype(BF16)

    xb = xb_ref[...]
    u = _dot(xb, w1_ref[...])
    v = _dot(xb, w3_ref[...])
    act = (u / (1.0 + jnp.exp(-u))) * v
    part = _dot(act.astype(BF16), w2_ref[...])

    @pl.when(j == 0)
    def _():
        acc_ref[...] = part

    @pl.when(j > 0)
    def _():
        acc_ref[...] += part

    @pl.when(j == pl.num_programs(1) - 1)
    def _():
        y = ALPHA * x_ref[...] + 0.5 * acc_ref[...]
        o_ref[...] = _layer_norm(y, g_ref[...], b_ref[...])


def _ffn_ln(x, w1, w3, w2, g, b):
    n, d = x.shape
    f = w1.shape[1]
    tm, tf = ROW_TILE, FFN_COL_TILE
    assert n % tm == 0 and f % tf == 0
    return pl.pallas_call(
        _ffn_ln_kernel,
        grid=(n // tm, f // tf),
        in_specs=[
            pl.BlockSpec((tm, d), lambda i, j: (i, 0)),
            pl.BlockSpec((d, tf), lambda i, j: (0, j)),
            pl.BlockSpec((d, tf), lambda i, j: (0, j)),
            pl.BlockSpec((tf, d), lambda i, j: (j, 0)),
            pl.BlockSpec((1, d), lambda i, j: (0, 0)),
            pl.BlockSpec((1, d), lambda i, j: (0, 0)),
        ],
        out_specs=pl.BlockSpec((tm, d), lambda i, j: (i, 0)),
        out_shape=jax.ShapeDtypeStruct((n, d), F32),
        scratch_shapes=[pltpu.VMEM((tm, d), BF16), pltpu.VMEM((tm, d), F32)],
        compiler_params=_params(("arbitrary", "arbitrary")),
        name="ffn_ln",
    )(x, w1.astype(BF16), w3.astype(BF16), w2.astype(BF16), g.reshape(1, d), b.reshape(1, d))


def _in_proj_kernel(h_ref, wnat_ref, wt_ref, wqab_ref, wkn_ref, wvt_ref, gq_ref, gkv_ref,
                    cos_ref, sin_ref, cost_ref, sint_ref,
                    qtm_ref, km_ref, vtm_ref, qtd_ref, kd_ref, vtd_ref):
    tm = h_ref.shape[0]
    tk = vtm_ref.shape[-1]
    hb = h_ref[...].astype(BF16)

    pn = _dot(hb, wnat_ref[...])
    o1 = MLA_Q_RANK
    o2 = o1 + MLA_KV_RANK
    o3 = o2 + HEAD_PAD
    o4 = o3 + HEAD_PAD
    c_q, c_kv, kra, krb, kd = pn[:, :o1], pn[:, o1:o2], pn[:, o2:o3], pn[:, o3:o4], pn[:, o4:]
    kd_ref[...] = kd.astype(BF16)

    pt = _dot_nt(wt_ref[...], hb)
    nqd = qtd_ref.shape[0]
    qtd_ref[...] = (pt[:nqd] * (DIFF_D ** -0.5)).astype(BF16)
    for c in range(tm // tk):
        vtd_ref[c] = pt[nqd:, c * tk:(c + 1) * tk].astype(BF16)

    cqn = _rms_norm_rows(c_q, gq_ref[...]).astype(BF16)
    qab = _dot_nt(wqab_ref[...], cqn)
    cos_t, sin_t = cost_ref[...], sint_ref[...]
    nq = MLA_HEADS * HEAD_PAD
    for hh in range(MLA_HEADS):
        lo, hi = hh * HEAD_PAD, (hh + 1) * HEAD_PAD
        qtm_ref[lo:hi, :] = (qab[lo:hi] * cos_t + qab[nq + lo:nq + hi] * sin_t).astype(BF16)

    ckvn = _rms_norm_rows(c_kv, gkv_ref[...]).astype(BF16)
    kn = _dot(ckvn, wkn_ref[...])
    kr = kra * cos_ref[...] + krb * sin_ref[...]
    for hh in range(MLA_HEADS):
        lo, hi = hh * HEAD_PAD, (hh + 1) * HEAD_PAD
        km_ref[:, lo:hi] = (kn[:, lo:hi] + kr).astype(BF16)
    vt = _dot_nt(wvt_ref[...], ckvn)
    for c in range(tm // tk):
        vtm_ref[c] = vt[:, c * tk:(c + 1) * tk].astype(BF16)


def _in_proj_weights(w_in, w_q_b, w_kv_b):
    d = w_in.shape[0]
    o_cq = MLA_Q_RANK
    o_ckv = o_cq + MLA_KV_RANK
    o_kr = o_ckv + MLA_ROPE
    n_d = 2 * DIFF_HEADS * DIFF_D
    o_qd = o_kr + n_d
    o_kd = o_qd + n_d
    half = MLA_ROPE // 2
    pad = HEAD_PAD - MLA_NOPE - MLA_ROPE
    kr = w_in[:, o_ckv:o_kr]
    z_nope = jnp.zeros((d, MLA_NOPE), F32)
    z_pad = jnp.zeros((d, pad), F32)
    kra = jnp.concatenate([z_nope, kr, z_pad], axis=1)
    krb = jnp.concatenate([z_nope, -kr[:, half:], kr[:, :half], z_pad], axis=1)
    w_nat = jnp.concatenate([w_in[:, :o_ckv], kra, krb, w_in[:, o_qd:o_kd]], axis=1).astype(BF16)
    w_t = jnp.concatenate([w_in[:, o_kr:o_qd], w_in[:, o_kd:]], axis=1).T.astype(BF16)

    r = w_q_b.shape[0]
    wq = w_q_b.reshape(r, MLA_HEADS, MLA_NOPE + MLA_ROPE)
    nope = wq[..., :MLA_NOPE]
    r1 = wq[..., MLA_NOPE:MLA_NOPE + half]
    r2 = wq[..., MLA_NOPE + half:]
    zq_pad = jnp.zeros((r, MLA_HEADS, pad), F32)
    wqa = jnp.concatenate([nope, r1, r2, zq_pad], axis=-1).reshape(r, MLA_HEADS * HEAD_PAD)
    wqb = jnp.concatenate([jnp.zeros_like(nope), -r2, r1, zq_pad], axis=-1).reshape(r, MLA_HEADS * HEAD_PAD)
    wqab_t = jnp.concatenate([wqa, wqb], axis=1).T.astype(BF16)

    rk = w_kv_b.shape[0]
    wkv = w_kv_b.reshape(rk, MLA_HEADS, MLA_NOPE + MLA_V)
    wkn = jnp.concatenate([wkv[..., :MLA_NOPE], jnp.zeros((rk, MLA_HEADS, HEAD_PAD - MLA_NOPE), F32)],
                          axis=-1).reshape(rk, MLA_HEADS * HEAD_PAD).astype(BF16)
    wv_t = wkv[..., MLA_NOPE:].reshape(rk, MLA_HEADS * MLA_V).T.astype(BF16)
    return w_nat, w_t, wqab_t, wkn, wv_t


def _rope_tables(s):
    half = MLA_ROPE // 2
    inv_freq = ROPE_BASE ** (-jnp.arange(0, MLA_ROPE, 2, dtype=F32) / MLA_ROPE)
    ang = jnp.arange(s).astype(F32)[:, None] * inv_freq[None, :]
    cos, sin = jnp.cos(ang), jnp.sin(ang)
    pad = HEAD_PAD - MLA_NOPE - MLA_ROPE
    cos_p = jnp.concatenate([jnp.ones((s, MLA_NOPE), F32), cos, cos, jnp.zeros((s, pad), F32)], axis=1)
    sin_p = jnp.concatenate([jnp.zeros((s, MLA_NOPE), F32), sin, sin, jnp.zeros((s, pad), F32)], axis=1)
    assert half * 2 == MLA_ROPE
    return cos_p, sin_p


def _in_proj(h, bsz, s, w_in, q_norm_g, kv_norm_g, w_q_b, w_kv_b):
    n, d = h.shape
    tm, tk = ROW_TILE, ATTN_TK
    assert s % tm == 0 and tm % tk == 0
    w_nat, w_t, wqab_t, wkn, wv_t = _in_proj_weights(w_in, w_q_b, w_kv_b)
    cos_p, sin_p = _rope_tables(s)
    spt = s // tm
    nqm = MLA_HEADS * HEAD_PAD
    nvm = MLA_HEADS * MLA_V
    nd = 2 * DIFF_HEADS * DIFF_D

    def full(a):
        return pl.BlockSpec(a.shape, lambda b, i: (0,) * a.ndim)

    gq = q_norm_g.reshape(1, -1)
    gkv = kv_norm_g.reshape(1, -1)
    out_shape = (
        jax.ShapeDtypeStruct((bsz, nqm, s), BF16),
        jax.ShapeDtypeStruct((bsz, s, nqm), BF16),
        jax.ShapeDtypeStruct((bsz, s // tk, nvm, tk), BF16),
        jax.ShapeDtypeStruct((bsz, nd, s), BF16),
        jax.ShapeDtypeStruct((bsz, s, nd), BF16),
        jax.ShapeDtypeStruct((bsz, s // tk, nd, tk), BF16),
    )
    out_specs = (
        pl.BlockSpec((None, nqm, tm), lambda b, i: (b, 0, i)),
        pl.BlockSpec((None, tm, nqm), lambda b, i: (b, i, 0)),
        pl.BlockSpec((None, tm // tk, nvm, tk), lambda b, i: (b, i, 0, 0)),
        pl.BlockSpec((None, nd, tm), lambda b, i: (b, 0, i)),
        pl.BlockSpec((None, tm, nd), lambda b, i: (b, i, 0)),
        pl.BlockSpec((None, tm // tk, nd, tk), lambda b, i: (b, i, 0, 0)),
    )
    return pl.pallas_call(
        _in_proj_kernel,
        grid=(bsz, spt),
        in_specs=[
            pl.BlockSpec((tm, d), lambda b, i: (b * spt + i, 0)),
            full(w_nat), full(w_t), full(wqab_t), full(wkn), full(wv_t), full(gq), full(gkv),
            pl.BlockSpec((tm, HEAD_PAD), lambda b, i: (i, 0)),
            pl.BlockSpec((tm, HEAD_PAD), lambda b, i: (i, 0)),
            pl.BlockSpec((HEAD_PAD, tm), lambda b, i: (0, i)),
            pl.BlockSpec((HEAD_PAD, tm), lambda b, i: (0, i)),
        ],
        out_specs=out_specs,
        out_shape=out_shape,
        compiler_params=_params(("arbitrary", "arbitrary")),
        name="in_proj",
    )(h, w_nat, w_t, wqab_t, wkn, wv_t, gq, gkv, cos_p, sin_p, cos_p.T, sin_p.T)


def _near_tiles(tq, tk):
    assert tk >= REL_MAX_DIST and tq % tk == 0
    return tq // tk + 1


def _rel_bias_kernel(rb_ref, o_ref, *, tq, tk):
    c = pl.program_id(0) - 1
    kk = lax.broadcasted_iota(jnp.int32, (tk, tq), 0)
    qq = lax.broadcasted_iota(jnp.int32, (tk, tq), 1)
    n = jnp.maximum(qq - kk - c * tk, 0)
    max_exact = REL_BUCKETS // 2
    nf = jnp.maximum(n, 1).astype(F32)
    large = max_exact + (jnp.log(nf / max_exact) / math.log(REL_MAX_DIST / max_exact)
                         * (REL_BUCKETS - max_exact)).astype(jnp.int32)
    large = jnp.minimum(large, REL_BUCKETS - 1)
    bucket = jnp.where(n < max_exact, n, large)
    for m in range(o_ref.shape[0]):
        acc = jnp.zeros((tk, tq), F32)
        for bkt in range(REL_BUCKETS):
            acc = jnp.where(bucket == bkt, rb_ref[bkt, m], acc)
        o_ref[m] = acc


def _rel_bias_tiles(rel_bias, tq, tk):
    nmaps = rel_bias.shape[1]
    nt = _near_tiles(tq, tk)
    return pl.pallas_call(
        functools.partial(_rel_bias_kernel, tq=tq, tk=tk),
        grid=(nt,),
        in_specs=[pl.BlockSpec(memory_space=pltpu.SMEM)],
        out_specs=pl.BlockSpec((nmaps, None, tk, tq), lambda t: (0, t, 0, 0)),
        out_shape=jax.ShapeDtypeStruct((nmaps, nt, tk, tq), F32),
        compiler_params=_params(("arbitrary",)),
        name="rel_bias",
    )(rel_bias)


def _online_softmax_step(s, vt, m_ref, l_ref, acc_ref):
    m_prev = m_ref[...]
    m_new = jnp.maximum(m_prev, jnp.max(s, axis=0, keepdims=True))
    alpha = jnp.exp(m_prev - m_new)
    p = jnp.exp(s - m_new)
    l_ref[...] = alpha * l_ref[...] + jnp.sum(p, axis=0, keepdims=True)
    acc_ref[...] = alpha * acc_ref[...] + _dot(vt, p.astype(BF16))
    m_ref[...] = m_new


def _causal_mask(s, c, tk):
    kk = lax.broadcasted_iota(jnp.int32, s.shape, 0)
    qq = lax.broadcasted_iota(jnp.int32, s.shape, 1)
    return jnp.where(kk - qq <= -c * tk, s, -jnp.inf)


def _mla_attn_kernel(qt_ref, k_ref, vt_ref, o_ref, m_ref, l_ref, acc_ref, *, tq, tk):
    i = pl.program_id(2)
    r = tq // tk
    scale = (MLA_NOPE + MLA_ROPE) ** -0.5
    qt = qt_ref[...]
    m_ref[...] = jnp.full(m_ref.shape, -jnp.inf, F32)
    l_ref[...] = jnp.zeros(l_ref.shape, F32)
    acc_ref[...] = jnp.zeros(acc_ref.shape, F32)

    def step(j, c):
        k = k_ref[pl.ds(pl.multiple_of(j * tk, tk), tk), :]
        s = _dot(k, qt) * scale
        if c is not None:
            s = _causal_mask(s, c, tk)
        _online_softmax_step(s, vt_ref[j], m_ref, l_ref, acc_ref)

    def far(j, carry):
        step(j, None)
        return carry

    lax.fori_loop(0, i * r, far, 0)
    for c in range(r):
        step(i * r + c, c)
    o_ref[...] = (acc_ref[...] / l_ref[...]).astype(BF16)


def _mla_attn(qt, k, vt, tq, tk):
    bsz, nq, s = qt.shape
    nkt = s // tk
    return pl.pallas_call(
        functools.partial(_mla_attn_kernel, tq=tq, tk=tk),
        grid=(MLA_HEADS, bsz, s // tq),
        in_specs=[
            pl.BlockSpec((None, HEAD_PAD, tq), lambda h, b, i: (b, h, i)),
            pl.BlockSpec((None, s, HEAD_PAD), lambda h, b, i: (b, 0, h)),
            pl.BlockSpec((None, nkt, MLA_V, tk), lambda h, b, i: (b, 0, h, 0)),
        ],
        out_specs=pl.BlockSpec((None, MLA_V, tq), lambda h, b, i: (b, h, i)),
        out_shape=jax.ShapeDtypeStruct((bsz, MLA_HEADS * MLA_V, s), BF16),
        scratch_shapes=[pltpu.VMEM((1, tq), F32), pltpu.VMEM((1, tq), F32), pltpu.VMEM((MLA_V, tq), F32)],
        compiler_params=_params(("arbitrary", "arbitrary", "arbitrary")),
        name="mla_attn",
    )(qt, k, vt)


def _diff_attn_kernel(rb_ref, qt_ref, k_ref, vt_ref, bias_ref, lamv_ref, gn_ref, o_ref, m_ref, l_ref, acc_ref,
                      *, tq, tk):
    head = pl.program_id(0)
    i = pl.program_id(2)
    r = tq // tk
    qt = qt_ref[...]
    row = lax.broadcasted_iota(jnp.int32, qt.shape, 0)
    zero = jnp.zeros_like(qt)
    qts = (jnp.where(row < DIFF_D, qt, zero), jnp.where(row >= DIFF_D, qt, zero))
    m_ref[...] = jnp.full(m_ref.shape, -jnp.inf, F32)
    l_ref[...] = jnp.zeros(l_ref.shape, F32)
    acc_ref[...] = jnp.zeros(acc_ref.shape, F32)
    far_bias = [rb_ref[REL_BUCKETS - 1, 2 * head + a] for a in range(2)]

    def step(j, c):
        k = k_ref[pl.ds(pl.multiple_of(j * tk, tk), tk), :]
        vt = vt_ref[j]
        for a in range(2):
            s = _dot(k, qts[a])
            if c is None:
                s = s + far_bias[a]
            else:
                s = s + bias_ref[a, c + 1]
                if c >= 0:
                    s = _causal_mask(s, c, tk)
            _online_softmax_step(s, vt, m_ref.at[a], l_ref.at[a], acc_ref.at[a])

    def far(j, carry):
        step(j, None)
        return carry

    lax.fori_loop(0, jnp.maximum(i * r - 1, 0), far, 0)

    @pl.when(i > 0)
    def _():
        step(i * r - 1, -1)

    for c in range(r):
        step(i * r + c, c)

    lv = lamv_ref[...]
    lam = (jnp.exp(jnp.sum(lv[0:1] * lv[1:2], axis=1, keepdims=True))
           - jnp.exp(jnp.sum(lv[2:3] * lv[3:4], axis=1, keepdims=True)) + LAM_INIT)
    o = acc_ref[0] / l_ref[0] - lam * (acc_ref[1] / l_ref[1])
    o = o * lax.rsqrt(jnp.mean(o * o, axis=0, keepdims=True) + RMS_EPS) * gn_ref[...] * (1.0 - LAM_INIT)
    o_ref[...] = o.astype(BF16)


def _diff_attn(rel_bias, qt, k, vt, bias, lamv, gn, tq, tk):
    bsz, nd, s = qt.shape
    nkt = s // tk
    dv = 2 * DIFF_D
    nt = bias.shape[1]
    return pl.pallas_call(
        functools.partial(_diff_attn_kernel, tq=tq, tk=tk),
        grid=(DIFF_HEADS, bsz, s // tq),
        in_specs=[
            pl.BlockSpec(memory_space=pltpu.SMEM),
            pl.BlockSpec((None, dv, tq), lambda h, b, i: (b, h, i)),
            pl.BlockSpec((None, s, dv), lambda h, b, i: (b, 0, h)),
            pl.BlockSpec((None, nkt, dv, tk), lambda h, b, i: (b, 0, h, 0)),
            pl.BlockSpec((2, nt, tk, tq), lambda h, b, i: (h, 0, 0, 0)),
            pl.BlockSpec(lamv.shape, lambda h, b, i: (0, 0)),
            pl.BlockSpec(gn.shape, lambda h, b, i: (0, 0)),
        ],
        out_specs=pl.BlockSpec((None, dv, tq), lambda h, b, i: (b, h, i)),
        out_shape=jax.ShapeDtypeStruct((bsz, nd, s), BF16),
        scratch_shapes=[pltpu.VMEM((2, 1, tq), F32), pltpu.VMEM((2, 1, tq), F32),
                        pltpu.VMEM((2, dv, tq), F32)],
        compiler_params=_params(("arbitrary", "arbitrary", "arbitrary")),
        name="diff_attn",
    )(rel_bias, qt, k, vt, bias, lamv, gn)


def _post_mixer_kernel(h_ref, omt_ref, odt_ref, wum_ref, wud_ref, wg_ref, bg_ref, wo_ref, g_ref, b_ref,
                       o_ref):
    h = h_ref[...]
    d = h.shape[1]
    y_m = _dot_tn(omt_ref[...], wum_ref[...])
    y_d = _dot_tn(odt_ref[...], wud_ref[...])
    z = _dot(h.astype(BF16), wg_ref[...]) + bg_ref[...]
    gate = 1.0 / (1.0 + jnp.exp(-z))
    mixed = gate[:, :d] * y_m + gate[:, d:] * y_d
    mix = _dot(mixed.astype(BF16), wo_ref[...])
    o_ref[...] = _layer_norm(ALPHA * h + mix, g_ref[...], b_ref[...])


def _post_mixer(h, omt, odt, bsz, s, w_up_mla, w_up_diff, w_gate, b_gate, w_o, g, b):
    n, d = h.shape
    tm = ROW_TILE
    spt = s // tm
    ws = [w_up_mla.astype(BF16), w_up_diff.astype(BF16), w_gate.astype(BF16), b_gate.reshape(1, -1),
          w_o.astype(BF16), g.reshape(1, d), b.reshape(1, d)]

    def full(a):
        return pl.BlockSpec(a.shape, lambda bb, i: (0,) * a.ndim)

    return pl.pallas_call(
        _post_mixer_kernel,
        grid=(bsz, spt),
        in_specs=[
            pl.BlockSpec((tm, d), lambda bb, i: (bb * spt + i, 0)),
            pl.BlockSpec((None, omt.shape[1], tm), lambda bb, i: (bb, 0, i)),
            pl.BlockSpec((None, odt.shape[1], tm), lambda bb, i: (bb, 0, i)),
        ] + [full(a) for a in ws],
        out_specs=pl.BlockSpec((tm, d), lambda bb, i: (bb * spt + i, 0)),
        out_shape=jax.ShapeDtypeStruct((n, d), F32),
        compiler_params=_params(("arbitrary", "arbitrary")),
        name="post_mixer",
    )(h, omt, odt, *ws)


def _mem_kv_kernel(mem_ref, w_ref, k_ref, v_ref):
    d = k_ref.shape[1]
    kv = _dot(mem_ref[...].astype(BF16), w_ref[...])
    k_ref[...] = kv[:, :d].astype(BF16)
    v_ref[...] = kv[:, d:].astype(BF16)


def _mem_kv(mem, w_kv):
    bsz, m, d = mem.shape
    w = w_kv.astype(BF16)
    spec = pl.BlockSpec((None, m, d), lambda b: (b, 0, 0))
    return pl.pallas_call(
        _mem_kv_kernel,
        grid=(bsz,),
        in_specs=[spec, pl.BlockSpec(w.shape, lambda b: (0, 0))],
        out_specs=(spec, spec),
        out_shape=(jax.ShapeDtypeStruct((bsz, m, d), BF16), jax.ShapeDtypeStruct((bsz, m, d), BF16)),
        compiler_params=_params(("arbitrary",)),
        name="mem_kv",
    )(mem, w)


def _mem_attn_kernel(h_ref, k_ref, v_ref, wq_ref, wo_ref, g_ref, b_ref, o_ref):
    h = h_ref[...]
    d = h.shape[1]
    hd = d // MEM_HEADS
    q = _dot(h.astype(BF16), wq_ref[...]).astype(BF16)
    outs = []
    for hh in range(MEM_HEADS):
        lo, hi = hh * hd, (hh + 1) * hd
        s = _dot_nt(q[:, lo:hi], k_ref[:, lo:hi]) * (hd ** -0.5)
        p = jnp.exp(s - jnp.max(s, axis=-1, keepdims=True))
        p = p / jnp.sum(p, axis=-1, keepdims=True)
        outs.append(_dot(p.astype(BF16), v_ref[:, lo:hi]).astype(BF16))
    att = _dot(jnp.concatenate(outs, axis=1), wo_ref[...])
    o_ref[...] = _layer_norm(ALPHA * h + att, g_ref[...], b_ref[...])


def _mem_attn(h, km, vm, bsz, s, w_q, w_o, g, b):
    n, d = h.shape
    tm = ROW_TILE
    spt = s // tm
    m = km.shape[1]
    ws = [w_q.astype(BF16), w_o.astype(BF16), g.reshape(1, d), b.reshape(1, d)]

    def full(a):
        return pl.BlockSpec(a.shape, lambda bb, i: (0,) * a.ndim)

    return pl.pallas_call(
        _mem_attn_kernel,
        grid=(bsz, spt),
        in_specs=[
            pl.BlockSpec((tm, d), lambda bb, i: (bb * spt + i, 0)),
            pl.BlockSpec((None, m, d), lambda bb, i: (bb, 0, 0)),
            pl.BlockSpec((None, m, d), lambda bb, i: (bb, 0, 0)),
        ] + [full(a) for a in ws],
        out_specs=pl.BlockSpec((tm, d), lambda bb, i: (bb * spt + i, 0)),
        out_shape=jax.ShapeDtypeStruct((n, d), F32),
        compiler_params=_params(("arbitrary", "arbitrary")),
        name="mem_attn",
    )(h, km, vm, *ws)


def kernel(x, mem, rel_bias, ffn1_w1, ffn1_w3, ffn1_w2, ln1_g, ln1_b, w_in, q_norm_g, kv_norm_g, w_q_b, w_kv_b, lam_q1, lam_k1, lam_q2, lam_k2, diff_norm_g, w_gate, b_gate, w_up_mla, w_up_diff, w_o, ln2_g, ln2_b, mem_w_q, mem_w_kv, mem_w_o, ln3_g, ln3_b, ffn2_w1, ffn2_w3, ffn2_w2, ln4_g, ln4_b):
    bsz, s, d = x.shape
    assert ffn1_w1.shape[0] == DEPTH
    l = 0
    h = x.reshape(bsz * s, d)

    h = _ffn_ln(h, ffn1_w1[l], ffn1_w3[l], ffn1_w2[l], ln1_g[l], ln1_b[l])

    qtm, km, vtm, qtd, kd, vtd = _in_proj(h, bsz, s, w_in[l], q_norm_g[l], kv_norm_g[l], w_q_b[l], w_kv_b[l])
    bias = _rel_bias_tiles(rel_bias, ATTN_TQ, ATTN_TK)
    omt = _mla_attn(qtm, km, vtm, ATTN_TQ, ATTN_TK)
    lamv = jnp.stack([lam_q1[l], lam_k1[l], lam_q2[l], lam_k2[l]], axis=0)
    odt = _diff_attn(rel_bias, qtd, kd, vtd, bias, lamv, diff_norm_g[l].reshape(-1, 1), ATTN_TQ, ATTN_TK)
    h = _post_mixer(h, omt, odt, bsz, s, w_up_mla[l], w_up_diff[l], w_gate[l], b_gate[l], w_o[l],
                    ln2_g[l], ln2_b[l])

    kmem, vmem = _mem_kv(mem, mem_w_kv[l])
    h = _mem_attn(h, kmem, vmem, bsz, s, mem_w_q[l], mem_w_o[l], ln3_g[l], ln3_b[l])

    h = _ffn_ln(h, ffn2_w1[l], ffn2_w3[l], ffn2_w2[l], ln4_g[l], ln4_b[l])
    return h.reshape(bsz, s, d)
```

```python
import functools
import math

import jax
import jax.numpy as jnp
from jax import lax
from jax.experimental import pallas as pl
from jax.experimental.pallas import tpu as pltpu

F32 = jnp.float32
BF16 = jnp.bfloat16

DEPTH = 1
MLA_HEADS = 8
MLA_Q_RANK = 384
MLA_KV_RANK = 256
MLA_NOPE = 64
MLA_ROPE = 32
MLA_V = 64
ROPE_BASE = 10000.0
DIFF_HEADS = 4
DIFF_D = 64
REL_BUCKETS = 32
REL_MAX_DIST = 128
MEM_HEADS = 4
ALPHA = (2.0 * DEPTH) ** 0.25
LN_EPS = 1e-5
RMS_EPS = 1e-6
LAM_INIT = 0.8 - 0.6 * math.exp(-0.3 * 0)

LANES = 128
HEAD_PAD = 128
VMEM_LIMIT_BYTES = 56 * 1024 * 1024

ROW_TILE = 512
FFN_COL_TILE = 1408
ATTN_TQ = 512
ATTN_TK = 256


def _params(semantics):
    return pltpu.CompilerParams(dimension_semantics=semantics, vmem_limit_bytes=VMEM_LIMIT_BYTES)


def _dot(a, b):
    return jnp.dot(a, b, preferred_element_type=F32)


def _dot_nt(a, b):
    return lax.dot_general(a, b, (((1,), (1,)), ((), ())), preferred_element_type=F32)


def _dot_tn(a, b):
    return lax.dot_general(a, b, (((0,), (0,)), ((), ())), preferred_element_type=F32)


def _layer_norm(y, g, b):
    mu = jnp.mean(y, axis=-1, keepdims=True)
    d = y - mu
    var = jnp.mean(d * d, axis=-1, keepdims=True)
    return d * lax.rsqrt(var + LN_EPS) * g + b


def _rms_norm_rows(x, g):
    return x * lax.rsqrt(jnp.mean(x * x, axis=-1, keepdims=True) + RMS_EPS) * g


def _ffn_ln_kernel(x_ref, w1_ref, w3_ref, w2_ref, g_ref, b_ref, o_ref, xb_ref, acc_ref):
    j = pl.program_id(1)

    @pl.when(j == 0)
    def _():
        xb_ref[...] = x_ref[...].astype(BF16)

    xb = xb_ref[...]
    u = _dot(xb, w1_ref[...])
    v = _dot(xb, w3_ref[...])
    act = (u / (1.0 + jnp.exp(-u))) * v
    part = _dot(act.astype(BF16), w2_ref[...])

    @pl.when(j == 0)
    def _():
        acc_ref[...] = part

    @pl.when(j > 0)
    def _():
        acc_ref[...] += part

    @pl.when(j == pl.num_programs(1) - 1)
    def _():
        y = ALPHA * x_ref[...] + 0.5 * acc_ref[...]
        o_ref[...] = _layer_norm(y, g_ref[...], b_ref[...])


def _ffn_ln(x, w1, w3, w2, g, b):
    n, d = x.shape
    f = w1.shape[1]
    tm, tf = ROW_TILE, FFN_COL_TILE
    assert n % tm == 0 and f % tf == 0
    return pl.pallas_call(
        _ffn_ln_kernel,
        grid=(n // tm, f // tf),
        in_specs=[
            pl.BlockSpec((tm, d), lambda i, j: (i, 0)),
            pl.BlockSpec((d, tf), lambda i, j: (0, j)),
            pl.BlockSpec((d, tf), lambda i, j: (0, j)),
            pl.BlockSpec((tf, d), lambda i, j: (j, 0)),
            pl.BlockSpec((1, d), lambda i, j: (0, 0)),
            pl.BlockSpec((1, d), lambda i, j: (0, 0)),
        ],
        out_specs=pl.BlockSpec((tm, d), lambda i, j: (i, 0)),
        out_shape=jax.ShapeDtypeStruct((n, d), F32),
        scratch_shapes=[pltpu.VMEM((tm, d), BF16), pltpu.VMEM((tm, d), F32)],
        compiler_params=_params(("arbitrary", "arbitrary")),
        name="ffn_ln",
    )(x, w1.astype(BF16), w3.astype(BF16), w2.astype(BF16), g.reshape(1, d), b.reshape(1, d))


def _in_proj_kernel(h_ref, wnat_ref, wt_ref, wqab_ref, wkn_ref, wvt_ref, gq_ref, gkv_ref,
                    cos_ref, sin_ref, cost_ref, sint_ref,
                    qtm_ref, km_ref, vtm_ref, qtd_ref, kd_ref, vtd_ref):
    tm = h_ref.shape[0]
    tk = vtm_ref.shape[-1]
    hb = h_ref[...].astype(BF16)

    pn = _dot(hb, wnat_ref[...])
    o1 = MLA_Q_RANK
    o2 = o1 + MLA_KV_RANK
    o3 = o2 + HEAD_PAD
    o4 = o3 + HEAD_PAD
    c_q, c_kv, kra, krb, kd = pn[:, :o1], pn[:, o1:o2], pn[:, o2:o3], pn[:, o3:o4], pn[:, o4:]
    kd_ref[...] = kd.astype(BF16)

    pt = _dot_nt(wt_ref[...], hb)
    nqd = qtd_ref.shape[0]
    qtd_ref[...] = (pt[:nqd] * (DIFF_D ** -0.5)).astype(BF16)
    for c in range(tm // tk):
        vtd_ref[c] = pt[nqd:, c * tk:(c + 1) * tk].astype(BF16)

    cqn = _rms_norm_rows(c_q, gq_ref[...]).astype(BF16)
    qab = _dot_nt(wqab_ref[...], cqn)
    cos_t, sin_t = cost_ref[...], sint_ref[...]
    nq = MLA_HEADS * HEAD_PAD
    for hh in range(MLA_HEADS):
        lo, hi = hh * HEAD_PAD, (hh + 1) * HEAD_PAD
        qtm_ref[lo:hi, :] = (qab[lo:hi] * cos_t + qab[nq + lo:nq + hi] * sin_t).astype(BF16)

    ckvn = _rms_norm_rows(c_kv, gkv_ref[...]).astype(BF16)
    kn = _dot(ckvn, wkn_ref[...])
    kr = kra * cos_ref[...] + krb * sin_ref[...]
    for hh in range(MLA_HEADS):
        lo, hi = hh * HEAD_PAD, (hh + 1) * HEAD_PAD
        km_ref[:, lo:hi] = (kn[:, lo:hi] + kr).astype(BF16)
    vt = _dot_nt(wvt_ref[...], ckvn)
    for c in range(tm // tk):
        vtm_ref[c] = vt[:, c * tk:(c + 1) * tk].astype(BF16)


def _in_proj_weights(w_in, w_q_b, w_kv_b):
    d = w_in.shape[0]
    o_cq = MLA_Q_RANK
    o_ckv = o_cq + MLA_KV_RANK
    o_kr = o_ckv + MLA_ROPE
    n_d = 2 * DIFF_HEADS * DIFF_D
    o_qd = o_kr + n_d
    o_kd = o_qd + n_d
    half = MLA_ROPE // 2
    pad = HEAD_PAD - MLA_NOPE - MLA_ROPE
    kr = w_in[:, o_ckv:o_kr]
    z_nope = jnp.zeros((d, MLA_NOPE), F32)
    z_pad = jnp.zeros((d, pad), F32)
    kra = jnp.concatenate([z_nope, kr, z_pad], axis=1)
    krb = jnp.concatenate([z_nope, -kr[:, half:], kr[:, :half], z_pad], axis=1)
    w_nat = jnp.concatenate([w_in[:, :o_ckv], kra, krb, w_in[:, o_qd:o_kd]], axis=1).astype(BF16)
    w_t = jnp.concatenate([w_in[:, o_kr:o_qd], w_in[:, o_kd:]], axis=1).T.astype(BF16)

    r = w_q_b.shape[0]
    wq = w_q_b.reshape(r, MLA_HEADS, MLA_NOPE + MLA_ROPE)
    nope = wq[..., :MLA_NOPE]
    r1 = wq[..., MLA_NOPE:MLA_NOPE + half]
    r2 = wq[..., MLA_NOPE + half:]
    zq_pad = jnp.zeros((r, MLA_HEADS, pad), F32)
    wqa = jnp.concatenate([nope, r1, r2, zq_pad], axis=-1).reshape(r, MLA_HEADS * HEAD_PAD)
    wqb = jnp.concatenate([jnp.zeros_like(nope), -r2, r1, zq_pad], axis=-1).reshape(r, MLA_HEADS * HEAD_PAD)
    wqab_t = jnp.concatenate([wqa, wqb], axis=1).T.astype(BF16)

    rk = w_kv_b.shape[0]
    wkv = w_kv_b.reshape(rk, MLA_HEADS, MLA_NOPE + MLA_V)
    wkn = jnp.concatenate([wkv[..., :MLA_NOPE], jnp.zeros((rk, MLA_HEADS, HEAD_PAD - MLA_NOPE), F32)],
                          axis=-1).reshape(rk, MLA_HEADS * HEAD_PAD).astype(BF16)
    wv_t = wkv[..., MLA_NOPE:].reshape(rk, MLA_HEADS * MLA_V).T.astype(BF16)
    return w_nat, w_t, wqab_t, wkn, wv_t


def _rope_tables(s):
    half = MLA_ROPE // 2
    inv_freq = ROPE_BASE ** (-jnp.arange(0, MLA_ROPE, 2, dtype=F32) / MLA_ROPE)
    ang = jnp.arange(s).astype(F32)[:, None] * inv_freq[None, :]
    cos, sin = jnp.cos(ang), jnp.sin(ang)
    pad = HEAD_PAD - MLA_NOPE - MLA_ROPE
    cos_p = jnp.concatenate([jnp.ones((s, MLA_NOPE), F32), cos, cos, jnp.zeros((s, pad), F32)], axis=1)
    sin_p = jnp.concatenate([jnp.zeros((s, MLA_NOPE), F32), sin, sin, jnp.zeros((s, pad), F32)], axis=1)
    assert half * 2 == MLA_ROPE
    return cos_p, sin_p


def _in_proj(h, bsz, s, w_in, q_norm_g, kv_norm_g, w_q_b, w_kv_b):
    n, d = h.shape
    tm, tk = ROW_TILE, ATTN_TK
    assert s % tm == 0 and tm % tk == 0
    w_nat, w_t, wqab_t, wkn, wv_t = _in_proj_weights(w_in, w_q_b, w_kv_b)
    cos_p, sin_p = _rope_tables(s)
    spt = s // tm
    nqm = MLA_HEADS * HEAD_PAD
    nvm = MLA_HEADS * MLA_V
    nd = 2 * DIFF_HEADS * DIFF_D

    def full(a):
        return pl.BlockSpec(a.shape, lambda b, i: (0,) * a.ndim)

    gq = q_norm_g.reshape(1, -1)
    gkv = kv_norm_g.reshape(1, -1)
    out_shape = (
        jax.ShapeDtypeStruct((bsz, nqm, s), BF16),
        jax.ShapeDtypeStruct((bsz, s, nqm), BF16),
        jax.ShapeDtypeStruct((bsz, s // tk, nvm, tk), BF16),
        jax.ShapeDtypeStruct((bsz, nd, s), BF16),
        jax.ShapeDtypeStruct((bsz, s, nd), BF16),
        jax.ShapeDtypeStruct((bsz, s // tk, nd, tk), BF16),
    )
    out_specs = (
        pl.BlockSpec((None, nqm, tm), lambda b, i: (b, 0, i)),
        pl.BlockSpec((None, tm, nqm), lambda b, i: (b, i, 0)),
        pl.BlockSpec((None, tm // tk, nvm, tk), lambda b, i: (b, i, 0, 0)),
        pl.BlockSpec((None, nd, tm), lambda b, i: (b, 0, i)),
        pl.BlockSpec((None, tm, nd), lambda b, i: (b, i, 0)),
        pl.BlockSpec((None, tm // tk, nd, tk), lambda b, i: (b, i, 0, 0)),
    )
    return pl.pallas_call(
        _in_proj_kernel,
        grid=(bsz, spt),
        in_specs=[
            pl.BlockSpec((tm, d), lambda b, i: (b * spt + i, 0)),
            full(w_nat), full(w_t), full(wqab_t), full(wkn), full(wv_t), full(gq), full(gkv),
            pl.BlockSpec((tm, HEAD_PAD), lambda b, i: (i, 0)),
            pl.BlockSpec((tm, HEAD_PAD), lambda b, i: (i, 0)),
            pl.BlockSpec((HEAD_PAD, tm), lambda b, i: (0, i)),
            pl.BlockSpec((HEAD_PAD, tm), lambda b, i: (0, i)),
        ],
        out_specs=out_specs,
        out_shape=out_shape,
        compiler_params=_params(("arbitrary", "arbitrary")),
        name="in_proj",
    )(h, w_nat, w_t, wqab_t, wkn, wv_t, gq, gkv, cos_p, sin_p, cos_p.T, sin_p.T)


BIAS_TILES = 4


def _rel_bias_kernel(rb_ref, o_ref, *, tq, tk):
    c = pl.program_id(0) - 2
    kk = lax.broadcasted_iota(jnp.int32, (tk, tq), 0)
    qq = lax.broadcasted_iota(jnp.int32, (tk, tq), 1)
    n = jnp.maximum(qq - kk - c * tk, 0)
    max_exact = REL_BUCKETS // 2
    nf = jnp.maximum(n, 1).astype(F32)
    large = max_exact + (jnp.log(nf / max_exact) / math.log(REL_MAX_DIST / max_exact)
                         * (REL_BUCKETS - max_exact)).astype(jnp.int32)
    large = jnp.minimum(large, REL_BUCKETS - 1)
    bucket = jnp.where(n < max_exact, n, large)
    for m in range(o_ref.shape[0]):
        acc = jnp.zeros((tk, tq), F32)
        for bkt in range(REL_BUCKETS):
            acc = jnp.where(bucket == bkt, rb_ref[bkt, m], acc)
        o_ref[m] = acc


def _rel_bias_tiles(rel_bias, tq, tk):
    assert tq == 2 * tk and tk + 1 >= REL_MAX_DIST
    nmaps = rel_bias.shape[1]
    return pl.pallas_call(
        functools.partial(_rel_bias_kernel, tq=tq, tk=tk),
        grid=(BIAS_TILES,),
        in_specs=[pl.BlockSpec(memory_space=pltpu.SMEM)],
        out_specs=pl.BlockSpec((nmaps, None, tk, tq), lambda t: (0, t, 0, 0)),
        out_shape=jax.ShapeDtypeStruct((nmaps, BIAS_TILES, tk, tq), F32),
        compiler_params=_params(("arbitrary",)),
        name="rel_bias",
    )(rel_bias)


def _causal_mask(s, c, tk):
    kk = lax.broadcasted_iota(jnp.int32, s.shape, 0)
    qq = lax.broadcasted_iota(jnp.int32, s.shape, 1)
    return jnp.where(kk - qq <= -c * tk, s, -jnp.inf)


def _flash_pipeline(i, qts, k_ref, vt_ref, logits_fn, s_ref, p_ref, al_ref, m_ref, l_ref, acc_ref, *, tk):
    nmaps = len(qts)
    m_ref[...] = jnp.full(m_ref.shape, -jnp.inf, F32)
    l_ref[...] = jnp.zeros(l_ref.shape, F32)
    acc_ref[...] = jnp.zeros(acc_ref.shape, F32)
    p_ref[1] = jnp.zeros(p_ref.shape[1:], BF16)
    al_ref[1] = jnp.ones(al_ref.shape[1:], F32)

    def scores(t, slot):
        k = k_ref[pl.ds(pl.multiple_of(t * tk, tk), tk), :]
        for a in range(nmaps):
            s_ref[slot, a] = _dot(k, qts[a])

    def softmax(t, slot, c):
        for a in range(nmaps):
            s = logits_fn(a, s_ref[slot, a], t, c)
            m_prev = m_ref[a]
            m_new = jnp.maximum(m_prev, jnp.max(s, axis=0, keepdims=True))
            alpha = jnp.exp(m_prev - m_new)
            p = jnp.exp(s - m_new)
            l_ref[a] = alpha * l_ref[a] + jnp.sum(p, axis=0, keepdims=True)
            m_ref[a] = m_new
            al_ref[slot, a] = alpha
            p_ref[slot, a] = p.astype(BF16)

    def values(t, slot):
        vt = vt_ref[t]
        for a in range(nmaps):
            acc_ref[a] = al_ref[slot, a] * acc_ref[a] + _dot(vt, p_ref[slot, a])

    def pair(u, diag):
        t0 = 2 * u
        values(jnp.maximum(t0 - 1, 0), 1)
        scores(t0 + 1, 1)
        softmax(t0, 0, 0 if diag else None)
        values(t0, 0)
        if not diag:
            scores(t0 + 2, 0)
        softmax(t0 + 1, 1, 1 if diag else None)

    def body(u, carry):
        pair(u, False)
        return carry

    scores(0, 0)
    lax.fori_loop(0, i, body, 0)
    pair(i, True)
    values(2 * i + 1, 1)


def _flash_scratch(nmaps, dv, tq, tk):
    return [pltpu.VMEM((2, nmaps, tk, tq), F32), pltpu.VMEM((2, nmaps, tk, tq), BF16),
            pltpu.VMEM((2, nmaps, 1, tq), F32), pltpu.VMEM((nmaps, 1, tq), F32),
            pltpu.VMEM((nmaps, 1, tq), F32), pltpu.VMEM((nmaps, dv, tq), F32)]


def _mla_attn_kernel(qt_ref, k_ref, vt_ref, o_ref, *scratch, tk):
    i = pl.program_id(2)
    scale = (MLA_NOPE + MLA_ROPE) ** -0.5

    def logits(a, s, t, c):
        s = s * scale
        return s if c is None else _causal_mask(s, c, tk)

    _flash_pipeline(i, (qt_ref[...],), k_ref, vt_ref, logits, *scratch, tk=tk)
    l_ref, acc_ref = scratch[-2], scratch[-1]
    o_ref[...] = (acc_ref[0] / l_ref[0]).astype(BF16)


def _mla_attn(qt, k, vt, tq, tk):
    bsz, nq, s = qt.shape
    nkt = s // tk
    assert tq == 2 * tk
    return pl.pallas_call(
        functools.partial(_mla_attn_kernel, tk=tk),
        grid=(MLA_HEADS, bsz, s // tq),
        in_specs=[
            pl.BlockSpec((None, HEAD_PAD, tq), lambda h, b, i: (b, h, i)),
            pl.BlockSpec((None, s, HEAD_PAD), lambda h, b, i: (b, 0, h)),
            pl.BlockSpec((None, nkt, MLA_V, tk), lambda h, b, i: (b, 0, h, 0)),
        ],
        out_specs=pl.BlockSpec((None, MLA_V, tq), lambda h, b, i: (b, h, i)),
        out_shape=jax.ShapeDtypeStruct((bsz, MLA_HEADS * MLA_V, s), BF16),
        scratch_shapes=_flash_scratch(1, MLA_V, tq, tk),
        compiler_params=_params(("arbitrary", "arbitrary", "arbitrary")),
        name="mla_attn",
    )(qt, k, vt)


def _diff_attn_kernel(qt_ref, k_ref, vt_ref, bias_ref, lamv_ref, gn_ref, o_ref, *scratch, tk):
    i = pl.program_id(2)
    qt = qt_ref[...]
    row = lax.broadcasted_iota(jnp.int32, qt.shape, 0)
    zero = jnp.zeros_like(qt)
    qts = (jnp.where(row < DIFF_D, qt, zero), jnp.where(row >= DIFF_D, qt, zero))

    def logits(a, s, t, c):
        if c is None:
            return s + bias_ref[a, jnp.maximum(t - 2 * i + 2, 0)]
        return _causal_mask(s + bias_ref[a, c + 2], c, tk)

    _flash_pipeline(i, qts, k_ref, vt_ref, logits, *scratch, tk=tk)
    l_ref, acc_ref = scratch[-2], scratch[-1]
    lv = lamv_ref[...]
    lam = (jnp.exp(jnp.sum(lv[0:1] * lv[1:2], axis=1, keepdims=True))
           - jnp.exp(jnp.sum(lv[2:3] * lv[3:4], axis=1, keepdims=True)) + LAM_INIT)
    o = acc_ref[0] / l_ref[0] - lam * (acc_ref[1] / l_ref[1])
    o = o * lax.rsqrt(jnp.mean(o * o, axis=0, keepdims=True) + RMS_EPS) * gn_ref[...] * (1.0 - LAM_INIT)
    o_ref[...] = o.astype(BF16)


def _diff_attn(qt, k, vt, bias, lamv, gn, tq, tk):
    bsz, nd, s = qt.shape
    nkt = s // tk
    dv = 2 * DIFF_D
    assert tq == 2 * tk
    return pl.pallas_call(
        functools.partial(_diff_attn_kernel, tk=tk),
        grid=(DIFF_HEADS, bsz, s // tq),
        in_specs=[
            pl.BlockSpec((None, dv, tq), lambda h, b, i: (b, h, i)),
            pl.BlockSpec((None, s, dv), lambda h, b, i: (b, 0, h)),
            pl.BlockSpec((None, nkt, dv, tk), lambda h, b, i: (b, 0, h, 0)),
            pl.BlockSpec((2, BIAS_TILES, tk, tq), lambda h, b, i: (h, 0, 0, 0)),
            pl.BlockSpec(lamv.shape, lambda h, b, i: (0, 0)),
            pl.BlockSpec(gn.shape, lambda h, b, i: (0, 0)),
        ],
        out_specs=pl.BlockSpec((None, dv, tq), lambda h, b, i: (b, h, i)),
        out_shape=jax.ShapeDtypeStruct((bsz, nd, s), BF16),
        scratch_shapes=_flash_scratch(2, dv, tq, tk),
        compiler_params=_params(("arbitrary", "arbitrary", "arbitrary")),
        name="diff_attn",
    )(qt, k, vt, bias, lamv, gn)


def _post_mixer_kernel(h_ref, omt_ref, odt_ref, wum_ref, wud_ref, wg_ref, bg_ref, wo_ref, g_ref, b_ref,
                       o_ref):
    h = h_ref[...]
    d = h.shape[1]
    y_m = _dot_tn(omt_ref[...], wum_ref[...])
    y_d = _dot_tn(odt_ref[...], wud_ref[...])
    z = _dot(h.astype(BF16), wg_ref[...]) + bg_ref[...]
    gate = 1.0 / (1.0 + jnp.exp(-z))
    mixed = gate[:, :d] * y_m + gate[:, d:] * y_d
    mix = _dot(mixed.astype(BF16), wo_ref[...])
    o_ref[...] = _layer_norm(ALPHA * h + mix, g_ref[...], b_ref[...])


def _post_mixer(h, omt, odt, bsz, s, w_up_mla, w_up_diff, w_gate, b_gate, w_o, g, b):
    n, d = h.shape
    tm = ROW_TILE
    spt = s // tm
    ws = [w_up_mla.astype(BF16), w_up_diff.astype(BF16), w_gate.astype(BF16), b_gate.reshape(1, -1),
          w_o.astype(BF16), g.reshape(1, d), b.reshape(1, d)]

    def full(a):
        return pl.BlockSpec(a.shape, lambda bb, i: (0,) * a.ndim)

    return pl.pallas_call(
        _post_mixer_kernel,
        grid=(bsz, spt),
        in_specs=[
            pl.BlockSpec((tm, d), lambda bb, i: (bb * spt + i, 0)),
            pl.BlockSpec((None, omt.shape[1], tm), lambda bb, i: (bb, 0, i)),
            pl.BlockSpec((None, odt.shape[1], tm), lambda bb, i: (bb, 0, i)),
        ] + [full(a) for a in ws],
        out_specs=pl.BlockSpec((tm, d), lambda bb, i: (bb * spt + i, 0)),
        out_shape=jax.ShapeDtypeStruct((n, d), F32),
        compiler_params=_params(("arbitrary", "arbitrary")),
        name="post_mixer",
    )(h, omt, odt, *ws)


def _mem_kv_kernel(mem_ref, w_ref, k_ref, v_ref):
    d = k_ref.shape[1]
    kv = _dot(mem_ref[...].astype(BF16), w_ref[...])
    k_ref[...] = kv[:, :d].astype(BF16)
    v_ref[...] = kv[:, d:].astype(BF16)


def _mem_kv(mem, w_kv):
    bsz, m, d = mem.shape
    w = w_kv.astype(BF16)
    spec = pl.BlockSpec((None, m, d), lambda b: (b, 0, 0))
    return pl.pallas_call(
        _mem_kv_kernel,
        grid=(bsz,),
        in_specs=[spec, pl.BlockSpec(w.shape, lambda b: (0, 0))],
        out_specs=(spec, spec),
        out_shape=(jax.ShapeDtypeStruct((bsz, m, d), BF16), jax.ShapeDtypeStruct((bsz, m, d), BF16)),
        compiler_params=_params(("arbitrary",)),
        name="mem_kv",
    )(mem, w)


def _mem_attn_kernel(h_ref, k_ref, v_ref, wq_ref, wo_ref, g_ref, b_ref, o_ref):
    h = h_ref[...]
    d = h.shape[1]
    hd = d // MEM_HEADS
    q = _dot(h.astype(BF16), wq_ref[...]).astype(BF16)
    outs = []
    for hh in range(MEM_HEADS):
        lo, hi = hh * hd, (hh + 1) * hd
        s = _dot_nt(q[:, lo:hi], k_ref[:, lo:hi]) * (hd ** -0.5)
        p = jnp.exp(s - jnp.max(s, axis=-1, keepdims=True))
        p = p / jnp.sum(p, axis=-1, keepdims=True)
        outs.append(_dot(p.astype(BF16), v_ref[:, lo:hi]).astype(BF16))
    att = _dot(jnp.concatenate(outs, axis=1), wo_ref[...])
    o_ref[...] = _layer_norm(ALPHA * h + att, g_ref[...], b_ref[...])


def _mem_attn(h, km, vm, bsz, s, w_q, w_o, g, b):
    n, d = h.shape
    tm = ROW_TILE
    spt = s // tm
    m = km.shape[1]
    ws = [w_q.astype(BF16), w_o.astype(BF16), g.reshape(1, d), b.reshape(1, d)]

    def full(a):
        return pl.BlockSpec(a.shape, lambda bb, i: (0,) * a.ndim)

    return pl.pallas_call(
        _mem_attn_kernel,
        grid=(bsz, spt),
        in_specs=[
            pl.BlockSpec((tm, d), lambda bb, i: (bb * spt + i, 0)),
            pl.BlockSpec((None, m, d), lambda bb, i: (bb, 0, 0)),
            pl.BlockSpec((None, m, d), lambda bb, i: (bb, 0, 0)),
        ] + [full(a) for a in ws],
        out_specs=pl.BlockSpec((tm, d), lambda bb, i: (bb * spt + i, 0)),
        out_shape=jax.ShapeDtypeStruct((n, d), F32),
        compiler_params=_params(("arbitrary", "arbitrary")),
        name="mem_attn",
    )(h, km, vm, *ws)


def kernel(x, mem, rel_bias, ffn1_w1, ffn1_w3, ffn1_w2, ln1_g, ln1_b, w_in, q_norm_g, kv_norm_g, w_q_b, w_kv_b, lam_q1, lam_k1, lam_q2, lam_k2, diff_norm_g, w_gate, b_gate, w_up_mla, w_up_diff, w_o, ln2_g, ln2_b, mem_w_q, mem_w_kv, mem_w_o, ln3_g, ln3_b, ffn2_w1, ffn2_w3, ffn2_w2, ln4_g, ln4_b):
    bsz, s, d = x.shape
    assert ffn1_w1.shape[0] == DEPTH
    l = 0
    h = x.reshape(bsz * s, d)

    h = _ffn_ln(h, ffn1_w1[l], ffn1_w3[l], ffn1_w2[l], ln1_g[l], ln1_b[l])

    qtm, km, vtm, qtd, kd, vtd = _in_proj(h, bsz, s, w_in[l], q_norm_g[l], kv_norm_g[l], w_q_b[l], w_kv_b[l])
    bias = _rel_bias_tiles(rel_bias, ATTN_TQ, ATTN_TK)
    omt = _mla_attn(qtm, km, vtm, ATTN_TQ, ATTN_TK)
    lamv = jnp.stack([lam_q1[l], lam_k1[l], lam_q2[l], lam_k2[l]], axis=0)
    odt = _diff_attn(qtd, kd, vtd, bias, lamv, diff_norm_g[l].reshape(-1, 1), ATTN_TQ, ATTN_TK)
    h = _post_mixer(h, omt, odt, bsz, s, w_up_mla[l], w_up_diff[l], w_gate[l], b_gate[l], w_o[l],
                    ln2_g[l], ln2_b[l])

    kmem, vmem = _mem_kv(mem, mem_w_kv[l])
    h = _mem_attn(h, kmem, vmem, bsz, s, mem_w_q[l], mem_w_o[l], ln3_g[l], ln3_b[l])

    h = _ffn_ln(h, ffn2_w1[l], ffn2_w3[l], ffn2_w2[l], ln4_g[l], ln4_b[l])
    return h.reshape(bsz, s, d)
```

```python
import functools
import math

import jax
import jax.numpy as jnp
from jax import lax
from jax.experimental import pallas as pl
from jax.experimental.pallas import tpu as pltpu

F32 = jnp.float32
BF16 = jnp.bfloat16

DEPTH = 1
MLA_HEADS = 8
MLA_Q_RANK = 384
MLA_KV_RANK = 256
MLA_NOPE = 64
MLA_ROPE = 32
MLA_V = 64
ROPE_BASE = 10000.0
DIFF_HEADS = 4
DIFF_D = 64
REL_BUCKETS = 32
REL_MAX_DIST = 128
MEM_HEADS = 4
ALPHA = (2.0 * DEPTH) ** 0.25
LN_EPS = 1e-5
RMS_EPS = 1e-6
LAM_INIT = 0.8 - 0.6 * math.exp(-0.3 * 0)
LOG2E = math.log2(math.e)

LANES = 128
HEAD_PAD = 128
VMEM_LIMIT_BYTES = 56 * 1024 * 1024

ROW_TILE = 512
ATTN_TQ = 512
ATTN_TK = 256


def _params(semantics):
    return pltpu.CompilerParams(dimension_semantics=semantics, vmem_limit_bytes=VMEM_LIMIT_BYTES)


def _dot(a, b):
    return jnp.dot(a, b, preferred_element_type=F32)


def _dot_nt(a, b):
    return lax.dot_general(a, b, (((1,), (1,)), ((), ())), preferred_element_type=F32)


def _dot_tn(a, b):
    return lax.dot_general(a, b, (((0,), (0,)), ((), ())), preferred_element_type=F32)


def _layer_norm(y, g, b):
    mu = jnp.mean(y, axis=-1, keepdims=True)
    d = y - mu
    var = jnp.mean(d * d, axis=-1, keepdims=True)
    return d * lax.rsqrt(var + LN_EPS) * g + b


def _rms_norm_rows(x, g):
    return x * lax.rsqrt(jnp.mean(x * x, axis=-1, keepdims=True) + RMS_EPS) * g


def _ffn_ln_kernel(x_ref, w1_ref, w3_ref, w2_ref, g_ref, b_ref, o_ref):
    x = x_ref[...]
    xb = x.astype(BF16)
    u = _dot(xb, w1_ref[...])
    v = _dot(xb, w3_ref[...])
    act = (u / (1.0 + jnp.exp(-u))) * v
    y = ALPHA * x + 0.5 * _dot(act.astype(BF16), w2_ref[...])
    o_ref[...] = _layer_norm(y, g_ref[...], b_ref[...])


def _resident(a):
    return pl.BlockSpec(a.shape, lambda *_: (0,) * a.ndim, pipeline_mode=pl.Buffered(1))


def _ffn_ln(x, w1, w3, w2, g, b):
    n, d = x.shape
    tm = ROW_TILE
    assert n % tm == 0
    ws = [w1.astype(BF16), w3.astype(BF16), w2.astype(BF16), g.reshape(1, d), b.reshape(1, d)]
    return pl.pallas_call(
        _ffn_ln_kernel,
        grid=(n // tm,),
        in_specs=[pl.BlockSpec((tm, d), lambda i: (i, 0))] + [_resident(a) for a in ws],
        out_specs=pl.BlockSpec((tm, d), lambda i: (i, 0)),
        out_shape=jax.ShapeDtypeStruct((n, d), F32),
        compiler_params=_params(("arbitrary",)),
        name="ffn_ln",
    )(x, *ws)


def _in_proj_kernel(h_ref, wnat_ref, wt_ref, wqab_ref, wkn_ref, wvt_ref, gq_ref, gkv_ref,
                    cos_ref, sin_ref, cost_ref, sint_ref,
                    qtm_ref, km_ref, vtm_ref, qtd_ref, kd_ref, vtd_ref):
    tm = h_ref.shape[0]
    tk = vtm_ref.shape[-1]
    hb = h_ref[...].astype(BF16)

    pn = _dot(hb, wnat_ref[...])
    o1 = MLA_Q_RANK
    o2 = o1 + MLA_KV_RANK
    o3 = o2 + HEAD_PAD
    o4 = o3 + HEAD_PAD
    c_q, c_kv, kra, krb, kd = pn[:, :o1], pn[:, o1:o2], pn[:, o2:o3], pn[:, o3:o4], pn[:, o4:]
    kd_ref[...] = kd.astype(BF16)

    pt = _dot_nt(wt_ref[...], hb)
    nqd = qtd_ref.shape[0]
    qtd_ref[...] = (pt[:nqd] * (DIFF_D ** -0.5 * LOG2E)).astype(BF16)
    for c in range(tm // tk):
        vtd_ref[c] = pt[nqd:, c * tk:(c + 1) * tk].astype(BF16)

    cqn = _rms_norm_rows(c_q, gq_ref[...]).astype(BF16)
    qab = _dot_nt(wqab_ref[...], cqn)
    cos_t, sin_t = cost_ref[...], sint_ref[...]
    nq = MLA_HEADS * HEAD_PAD
    for hh in range(MLA_HEADS):
        lo, hi = hh * HEAD_PAD, (hh + 1) * HEAD_PAD
        q = qab[lo:hi] * cos_t + qab[nq + lo:nq + hi] * sin_t
        qtm_ref[lo:hi, :] = (q * ((MLA_NOPE + MLA_ROPE) ** -0.5 * LOG2E)).astype(BF16)

    ckvn = _rms_norm_rows(c_kv, gkv_ref[...]).astype(BF16)
    kn = _dot(ckvn, wkn_ref[...])
    kr = kra * cos_ref[...] + krb * sin_ref[...]
    for hh in range(MLA_HEADS):
        lo, hi = hh * HEAD_PAD, (hh + 1) * HEAD_PAD
        km_ref[:, lo:hi] = (kn[:, lo:hi] + kr).astype(BF16)
    vt = _dot_nt(wvt_ref[...], ckvn)
    for c in range(tm // tk):
        vtm_ref[c] = vt[:, c * tk:(c + 1) * tk].astype(BF16)


def _in_proj_weights(w_in, w_q_b, w_kv_b):
    d = w_in.shape[0]
    o_cq = MLA_Q_RANK
    o_ckv = o_cq + MLA_KV_RANK
    o_kr = o_ckv + MLA_ROPE
    n_d = 2 * DIFF_HEADS * DIFF_D
    o_qd = o_kr + n_d
    o_kd = o_qd + n_d
    half = MLA_ROPE // 2
    pad = HEAD_PAD - MLA_NOPE - MLA_ROPE
    kr = w_in[:, o_ckv:o_kr]
    z_nope = jnp.zeros((d, MLA_NOPE), F32)
    z_pad = jnp.zeros((d, pad), F32)
    kra = jnp.concatenate([z_nope, kr, z_pad], axis=1)
    krb = jnp.concatenate([z_nope, -kr[:, half:], kr[:, :half], z_pad], axis=1)
    w_nat = jnp.concatenate([w_in[:, :o_ckv], kra, krb, w_in[:, o_qd:o_kd]], axis=1).astype(BF16)
    w_t = jnp.concatenate([w_in[:, o_kr:o_qd], w_in[:, o_kd:]], axis=1).T.astype(BF16)

    r = w_q_b.shape[0]
    wq = w_q_b.reshape(r, MLA_HEADS, MLA_NOPE + MLA_ROPE)
    nope = wq[..., :MLA_NOPE]
    r1 = wq[..., MLA_NOPE:MLA_NOPE + half]
    r2 = wq[..., MLA_NOPE + half:]
    zq_pad = jnp.zeros((r, MLA_HEADS, pad), F32)
    wqa = jnp.concatenate([nope, r1, r2, zq_pad], axis=-1).reshape(r, MLA_HEADS * HEAD_PAD)
    wqb = jnp.concatenate([jnp.zeros_like(nope), -r2, r1, zq_pad], axis=-1).reshape(r, MLA_HEADS * HEAD_PAD)
    wqab_t = jnp.concatenate([wqa, wqb], axis=1).T.astype(BF16)

    rk = w_kv_b.shape[0]
    wkv = w_kv_b.reshape(rk, MLA_HEADS, MLA_NOPE + MLA_V)
    wkn = jnp.concatenate([wkv[..., :MLA_NOPE], jnp.zeros((rk, MLA_HEADS, HEAD_PAD - MLA_NOPE), F32)],
                          axis=-1).reshape(rk, MLA_HEADS * HEAD_PAD).astype(BF16)
    wv_t = wkv[..., MLA_NOPE:].reshape(rk, MLA_HEADS * MLA_V).T.astype(BF16)
    return w_nat, w_t, wqab_t, wkn, wv_t


def _rope_tables(s):
    half = MLA_ROPE // 2
    inv_freq = ROPE_BASE ** (-jnp.arange(0, MLA_ROPE, 2, dtype=F32) / MLA_ROPE)
    ang = jnp.arange(s).astype(F32)[:, None] * inv_freq[None, :]
    cos, sin = jnp.cos(ang), jnp.sin(ang)
    pad = HEAD_PAD - MLA_NOPE - MLA_ROPE
    cos_p = jnp.concatenate([jnp.ones((s, MLA_NOPE), F32), cos, cos, jnp.zeros((s, pad), F32)], axis=1)
    sin_p = jnp.concatenate([jnp.zeros((s, MLA_NOPE), F32), sin, sin, jnp.zeros((s, pad), F32)], axis=1)
    assert half * 2 == MLA_ROPE
    return cos_p, sin_p


def _in_proj(h, bsz, s, w_in, q_norm_g, kv_norm_g, w_q_b, w_kv_b):
    n, d = h.shape
    tm, tk = ROW_TILE, ATTN_TK
    assert s % tm == 0 and tm % tk == 0
    w_nat, w_t, wqab_t, wkn, wv_t = _in_proj_weights(w_in, w_q_b, w_kv_b)
    cos_p, sin_p = _rope_tables(s)
    spt = s // tm
    nqm = MLA_HEADS * HEAD_PAD
    nvm = MLA_HEADS * MLA_V
    nd = 2 * DIFF_HEADS * DIFF_D

    def full(a):
        return pl.BlockSpec(a.shape, lambda b, i: (0,) * a.ndim)

    gq = q_norm_g.reshape(1, -1)
    gkv = kv_norm_g.reshape(1, -1)
    out_shape = (
        jax.ShapeDtypeStruct((bsz, nqm, s), BF16),
        jax.ShapeDtypeStruct((bsz, s, nqm), BF16),
        jax.ShapeDtypeStruct((bsz, s // tk, nvm, tk), BF16),
        jax.ShapeDtypeStruct((bsz, nd, s), BF16),
        jax.ShapeDtypeStruct((bsz, s, nd), BF16),
        jax.ShapeDtypeStruct((bsz, s // tk, nd, tk), BF16),
    )
    out_specs = (
        pl.BlockSpec((None, nqm, tm), lambda b, i: (b, 0, i)),
        pl.BlockSpec((None, tm, nqm), lambda b, i: (b, i, 0)),
        pl.BlockSpec((None, tm // tk, nvm, tk), lambda b, i: (b, i, 0, 0)),
        pl.BlockSpec((None, nd, tm), lambda b, i: (b, 0, i)),
        pl.BlockSpec((None, tm, nd), lambda b, i: (b, i, 0)),
        pl.BlockSpec((None, tm // tk, nd, tk), lambda b, i: (b, i, 0, 0)),
    )
    return pl.pallas_call(
        _in_proj_kernel,
        grid=(bsz, spt),
        in_specs=[
            pl.BlockSpec((tm, d), lambda b, i: (b * spt + i, 0)),
            full(w_nat), full(w_t), full(wqab_t), full(wkn), full(wv_t), full(gq), full(gkv),
            pl.BlockSpec((tm, HEAD_PAD), lambda b, i: (i, 0)),
            pl.BlockSpec((tm, HEAD_PAD), lambda b, i: (i, 0)),
            pl.BlockSpec((HEAD_PAD, tm), lambda b, i: (0, i)),
            pl.BlockSpec((HEAD_PAD, tm), lambda b, i: (0, i)),
        ],
        out_specs=out_specs,
        out_shape=out_shape,
        compiler_params=_params(("arbitrary", "arbitrary")),
        name="in_proj",
    )(h, w_nat, w_t, wqab_t, wkn, wv_t, gq, gkv, cos_p, sin_p, cos_p.T, sin_p.T)


BIAS_TILES = 4


def _rel_bias_kernel(rb_ref, o_ref, *, tq, tk):
    c = pl.program_id(0) - 2
    kk = lax.broadcasted_iota(jnp.int32, (tk, tq), 0)
    qq = lax.broadcasted_iota(jnp.int32, (tk, tq), 1)
    n = jnp.maximum(qq - kk - c * tk, 0)
    max_exact = REL_BUCKETS // 2
    nf = jnp.maximum(n, 1).astype(F32)
    large = max_exact + (jnp.log(nf / max_exact) / math.log(REL_MAX_DIST / max_exact)
                         * (REL_BUCKETS - max_exact)).astype(jnp.int32)
    large = jnp.minimum(large, REL_BUCKETS - 1)
    bucket = jnp.where(n < max_exact, n, large)
    for m in range(o_ref.shape[0]):
        acc = jnp.zeros((tk, tq), F32)
        for bkt in range(REL_BUCKETS):
            acc = jnp.where(bucket == bkt, rb_ref[bkt, m], acc)
        o_ref[m] = acc * LOG2E


def _rel_bias_tiles(rel_bias, tq, tk):
    assert tq == 2 * tk and tk + 1 >= REL_MAX_DIST
    nmaps = rel_bias.shape[1]
    return pl.pallas_call(
        functools.partial(_rel_bias_kernel, tq=tq, tk=tk),
        grid=(BIAS_TILES,),
        in_specs=[pl.BlockSpec(memory_space=pltpu.SMEM)],
        out_specs=pl.BlockSpec((nmaps, None, tk, tq), lambda t: (0, t, 0, 0)),
        out_shape=jax.ShapeDtypeStruct((nmaps, BIAS_TILES, tk, tq), F32),
        compiler_params=_params(("arbitrary",)),
        name="rel_bias",
    )(rel_bias)


ONES_ROWS = 16


def _flash_pipeline(q_fn, k_ref, vt_ref, logits_fn, finalize_fn, s_ref, p_ref, al_ref, m_ref, acc_ref,
                    *, nmaps, nq, tq, tk):
    tiles = [(i, t, t - 2 * i) for i in range(nq) for t in range(2 * i + 2)]
    ones = jnp.ones((ONES_ROWS, tk), BF16)

    def lane0(c):
        return tk if c == 1 else 0

    def scores(n):
        i, t, c = tiles[n]
        slot, lo = n % 2, lane0(c)
        k = k_ref[t * tk:(t + 1) * tk, :]
        for a in range(nmaps):
            s_ref[slot, a, :, lo:] = _dot(k, q_fn(a, i, lo))

    def softmax(n):
        i, t, c = tiles[n]
        slot, lo = n % 2, lane0(c)
        for a in range(nmaps):
            s = logits_fn(a, s_ref[slot, a, :, lo:], c, lo)
            m_new = jnp.max(s, axis=0, keepdims=True)
            if t > 0:
                m_prev = m_ref[a, :, lo:]
                m_new = jnp.maximum(m_prev, m_new)
                al_ref[slot, a, :, lo:] = jnp.exp2(m_prev - m_new)
            m_ref[a, :, lo:] = m_new
            p_ref[slot, a, :, lo:] = jnp.exp2(s - m_new).astype(BF16)

    def values(n):
        i, t, c = tiles[n]
        slot, lo = n % 2, lane0(c)
        vt = jnp.concatenate([vt_ref[t], ones], axis=0)
        for a in range(nmaps):
            pv = _dot(vt, p_ref[slot, a, :, lo:])
            if t > 0:
                pv = al_ref[slot, a, :, lo:] * acc_ref[a, :, lo:] + pv
            acc_ref[a, :, lo:] = pv
        if c == 1:
            finalize_fn(i)

    scores(0)
    for n in range(len(tiles)):
        if n > 0:
            values(n - 1)
        if n + 1 < len(tiles):
            scores(n + 1)
        softmax(n)
    values(len(tiles) - 1)


def _tri_mask(s):
    kk = lax.broadcasted_iota(jnp.int32, s.shape, 0)
    qq = lax.broadcasted_iota(jnp.int32, s.shape, 1)
    return jnp.where(kk <= qq, s, -jnp.inf)


def _flash_scratch(nmaps, dv, tq, tk):
    return [pltpu.VMEM((2, nmaps, tk, tq), F32), pltpu.VMEM((2, nmaps, tk, tq), BF16),
            pltpu.VMEM((2, nmaps, 1, tq), F32), pltpu.VMEM((nmaps, 1, tq), F32),
            pltpu.VMEM((nmaps, dv + ONES_ROWS, tq), F32)]


def _mla_attn_kernel(qt_ref, k_ref, vt_ref, o_ref, *scratch, tq, tk):
    acc_ref = scratch[-1]

    def q_fn(a, i, lo):
        return qt_ref[:, i * tq + lo:(i + 1) * tq]

    def logits(a, s, c, lo):
        return _tri_mask(s) if c >= 0 else s

    def finalize(i):
        acc = acc_ref[0]
        o_ref[:, i * tq:(i + 1) * tq] = (acc[:MLA_V] / acc[MLA_V:MLA_V + 1]).astype(BF16)

    _flash_pipeline(q_fn, k_ref, vt_ref, logits, finalize, *scratch,
                    nmaps=1, nq=qt_ref.shape[1] // tq, tq=tq, tk=tk)


def _mla_attn(qt, k, vt, tq, tk):
    bsz, nq, s = qt.shape
    nkt = s // tk
    assert tq == 2 * tk and s % tq == 0
    return pl.pallas_call(
        functools.partial(_mla_attn_kernel, tq=tq, tk=tk),
        grid=(MLA_HEADS, bsz),
        in_specs=[
            pl.BlockSpec((None, HEAD_PAD, s), lambda h, b: (b, h, 0)),
            pl.BlockSpec((None, s, HEAD_PAD), lambda h, b: (b, 0, h)),
            pl.BlockSpec((None, nkt, MLA_V, tk), lambda h, b: (b, 0, h, 0)),
        ],
        out_specs=pl.BlockSpec((None, MLA_V, s), lambda h, b: (b, h, 0)),
        out_shape=jax.ShapeDtypeStruct((bsz, MLA_HEADS * MLA_V, s), BF16),
        scratch_shapes=_flash_scratch(1, MLA_V, tq, tk),
        compiler_params=_params(("arbitrary", "arbitrary")),
        name="mla_attn",
    )(qt, k, vt)


def _diff_attn_kernel(qt_ref, k_ref, vt_ref, bias_ref, lamv_ref, gn_ref, o_ref, *scratch, tq, tk):
    acc_ref = scratch[-1]
    dv = 2 * DIFF_D

    def q_fn(a, i, lo):
        qt = qt_ref[:, i * tq + lo:(i + 1) * tq]
        row = lax.broadcasted_iota(jnp.int32, qt.shape, 0)
        keep = (row < DIFF_D) if a == 0 else (row >= DIFF_D)
        return jnp.where(keep, qt, jnp.zeros_like(qt))

    def logits(a, s, c, lo):
        s = s + bias_ref[a, max(c + 2, 0), :, lo:]
        return _tri_mask(s) if c >= 0 else s

    lv = lamv_ref[...]
    lam = (jnp.exp(jnp.sum(lv[0:1] * lv[1:2], axis=1, keepdims=True))
           - jnp.exp(jnp.sum(lv[2:3] * lv[3:4], axis=1, keepdims=True)) + LAM_INIT)

    def finalize(i):
        a0, a1 = acc_ref[0], acc_ref[1]
        o = a0[:dv] / a0[dv:dv + 1] - lam * (a1[:dv] / a1[dv:dv + 1])
        o = o * lax.rsqrt(jnp.mean(o * o, axis=0, keepdims=True) + RMS_EPS) * gn_ref[...] * (1.0 - LAM_INIT)
        o_ref[:, i * tq:(i + 1) * tq] = o.astype(BF16)

    _flash_pipeline(q_fn, k_ref, vt_ref, logits, finalize, *scratch,
                    nmaps=2, nq=qt_ref.shape[1] // tq, tq=tq, tk=tk)


def _diff_attn(qt, k, vt, bias, lamv, gn, tq, tk):
    bsz, nd, s = qt.shape
    nkt = s // tk
    dv = 2 * DIFF_D
    assert tq == 2 * tk and s % tq == 0
    return pl.pallas_call(
        functools.partial(_diff_attn_kernel, tq=tq, tk=tk),
        grid=(DIFF_HEADS, bsz),
        in_specs=[
            pl.BlockSpec((None, dv, s), lambda h, b: (b, h, 0)),
            pl.BlockSpec((None, s, dv), lambda h, b: (b, 0, h)),
            pl.BlockSpec((None, nkt, dv, tk), lambda h, b: (b, 0, h, 0)),
            pl.BlockSpec((2, BIAS_TILES, tk, tq), lambda h, b: (h, 0, 0, 0)),
            pl.BlockSpec(lamv.shape, lambda h, b: (0, 0)),
            pl.BlockSpec(gn.shape, lambda h, b: (0, 0)),
        ],
        out_specs=pl.BlockSpec((None, dv, s), lambda h, b: (b, h, 0)),
        out_shape=jax.ShapeDtypeStruct((bsz, nd, s), BF16),
        scratch_shapes=_flash_scratch(2, dv, tq, tk),
        compiler_params=_params(("arbitrary", "arbitrary")),
        name="diff_attn",
    )(qt, k, vt, bias, lamv, gn)


def _post_mixer_kernel(h_ref, omt_ref, odt_ref, wum_ref, wud_ref, wg_ref, bg_ref, wo_ref, g_ref, b_ref,
                       o_ref):
    h = h_ref[...]
    d = h.shape[1]
    y_m = _dot_tn(omt_ref[...], wum_ref[...])
    y_d = _dot_tn(odt_ref[...], wud_ref[...])
    z = _dot(h.astype(BF16), wg_ref[...]) + bg_ref[...]
    gate = 1.0 / (1.0 + jnp.exp(-z))
    mixed = gate[:, :d] * y_m + gate[:, d:] * y_d
    mix = _dot(mixed.astype(BF16), wo_ref[...])
    o_ref[...] = _layer_norm(ALPHA * h + mix, g_ref[...], b_ref[...])


def _post_mixer(h, omt, odt, bsz, s, w_up_mla, w_up_diff, w_gate, b_gate, w_o, g, b):
    n, d = h.shape
    tm = ROW_TILE
    spt = s // tm
    ws = [w_up_mla.astype(BF16), w_up_diff.astype(BF16), w_gate.astype(BF16), b_gate.reshape(1, -1),
          w_o.astype(BF16), g.reshape(1, d), b.reshape(1, d)]

    def full(a):
        return pl.BlockSpec(a.shape, lambda bb, i: (0,) * a.ndim)

    return pl.pallas_call(
        _post_mixer_kernel,
        grid=(bsz, spt),
        in_specs=[
            pl.BlockSpec((tm, d), lambda bb, i: (bb * spt + i, 0)),
            pl.BlockSpec((None, omt.shape[1], tm), lambda bb, i: (bb, 0, i)),
            pl.BlockSpec((None, odt.shape[1], tm), lambda bb, i: (bb, 0, i)),
        ] + [full(a) for a in ws],
        out_specs=pl.BlockSpec((tm, d), lambda bb, i: (bb * spt + i, 0)),
        out_shape=jax.ShapeDtypeStruct((n, d), F32),
        compiler_params=_params(("arbitrary", "arbitrary")),
        name="post_mixer",
    )(h, omt, odt, *ws)


def _mem_kv_kernel(mem_ref, w_ref, k_ref, v_ref):
    d = k_ref.shape[1]
    kv = _dot(mem_ref[...].astype(BF16), w_ref[...])
    k_ref[...] = kv[:, :d].astype(BF16)
    v_ref[...] = kv[:, d:].astype(BF16)


def _mem_kv(mem, w_kv):
    bsz, m, d = mem.shape
    w = w_kv.astype(BF16)
    spec = pl.BlockSpec((None, m, d), lambda b: (b, 0, 0))
    return pl.pallas_call(
        _mem_kv_kernel,
        grid=(bsz,),
        in_specs=[spec, pl.BlockSpec(w.shape, lambda b: (0, 0))],
        out_specs=(spec, spec),
        out_shape=(jax.ShapeDtypeStruct((bsz, m, d), BF16), jax.ShapeDtypeStruct((bsz, m, d), BF16)),
        compiler_params=_params(("arbitrary",)),
        name="mem_kv",
    )(mem, w)


def _mem_attn_kernel(h_ref, k_ref, v_ref, wq_ref, wo_ref, g_ref, b_ref, o_ref):
    h = h_ref[...]
    d = h.shape[1]
    hd = d // MEM_HEADS
    q = _dot(h.astype(BF16), wq_ref[...]).astype(BF16)
    outs = []
    for hh in range(MEM_HEADS):
        lo, hi = hh * hd, (hh + 1) * hd
        s = _dot_nt(q[:, lo:hi], k_ref[:, lo:hi]) * (hd ** -0.5)
        p = jnp.exp(s - jnp.max(s, axis=-1, keepdims=True))
        p = p / jnp.sum(p, axis=-1, keepdims=True)
        outs.append(_dot(p.astype(BF16), v_ref[:, lo:hi]).astype(BF16))
    att = _dot(jnp.concatenate(outs, axis=1), wo_ref[...])
    o_ref[...] = _layer_norm(ALPHA * h + att, g_ref[...], b_ref[...])


def _mem_attn(h, km, vm, bsz, s, w_q, w_o, g, b):
    n, d = h.shape
    tm = ROW_TILE
    spt = s // tm
    m = km.shape[1]
    ws = [w_q.astype(BF16), w_o.astype(BF16), g.reshape(1, d), b.reshape(1, d)]

    def full(a):
        return pl.BlockSpec(a.shape, lambda bb, i: (0,) * a.ndim)

    return pl.pallas_call(
        _mem_attn_kernel,
        grid=(bsz, spt),
        in_specs=[
            pl.BlockSpec((tm, d), lambda bb, i: (bb * spt + i, 0)),
            pl.BlockSpec((None, m, d), lambda bb, i: (bb, 0, 0)),
            pl.BlockSpec((None, m, d), lambda bb, i: (bb, 0, 0)),
        ] + [full(a) for a in ws],
        out_specs=pl.BlockSpec((tm, d), lambda bb, i: (bb * spt + i, 0)),
        out_shape=jax.ShapeDtypeStruct((n, d), F32),
        compiler_params=_params(("arbitrary", "arbitrary")),
        name="mem_attn",
    )(h, km, vm, *ws)


def kernel(x, mem, rel_bias, ffn1_w1, ffn1_w3, ffn1_w2, ln1_g, ln1_b, w_in, q_norm_g, kv_norm_g, w_q_b, w_kv_b, lam_q1, lam_k1, lam_q2, lam_k2, diff_norm_g, w_gate, b_gate, w_up_mla, w_up_diff, w_o, ln2_g, ln2_b, mem_w_q, mem_w_kv, mem_w_o, ln3_g, ln3_b, ffn2_w1, ffn2_w3, ffn2_w2, ln4_g, ln4_b):
    bsz, s, d = x.shape
    assert ffn1_w1.shape[0] == DEPTH
    l = 0
    h = x.reshape(bsz * s, d)

    h = _ffn_ln(h, ffn1_w1[l], ffn1_w3[l], ffn1_w2[l], ln1_g[l], ln1_b[l])

    qtm, km, vtm, qtd, kd, vtd = _in_proj(h, bsz, s, w_in[l], q_norm_g[l], kv_norm_g[l], w_q_b[l], w_kv_b[l])
    bias = _rel_bias_tiles(rel_bias, ATTN_TQ, ATTN_TK)
    omt = _mla_attn(qtm, km, vtm, ATTN_TQ, ATTN_TK)
    lamv = jnp.stack([lam_q1[l], lam_k1[l], lam_q2[l], lam_k2[l]], axis=0)
    odt = _diff_attn(qtd, kd, vtd, bias, lamv, diff_norm_g[l].reshape(-1, 1), ATTN_TQ, ATTN_TK)
    h = _post_mixer(h, omt, odt, bsz, s, w_up_mla[l], w_up_diff[l], w_gate[l], b_gate[l], w_o[l],
                    ln2_g[l], ln2_b[l])

    kmem, vmem = _mem_kv(mem, mem_w_kv[l])
    h = _mem_attn(h, kmem, vmem, bsz, s, mem_w_q[l], mem_w_o[l], ln3_g[l], ln3_b[l])

    h = _ffn_ln(h, ffn2_w1[l], ffn2_w3[l], ffn2_w2[l], ln4_g[l], ln4_b[l])
    return h.reshape(bsz, s, d)
```

```python
import functools
import math

import jax
import jax.numpy as jnp
from jax import lax
from jax.experimental import pallas as pl
from jax.experimental.pallas import tpu as pltpu

F32 = jnp.float32
BF16 = jnp.bfloat16

DEPTH = 1
MLA_HEADS = 8
MLA_Q_RANK = 384
MLA_KV_RANK = 256
MLA_NOPE = 64
MLA_ROPE = 32
MLA_V = 64
ROPE_BASE = 10000.0
DIFF_HEADS = 4
DIFF_D = 64
REL_BUCKETS = 32
REL_MAX_DIST = 128
MEM_HEADS = 4
ALPHA = (2.0 * DEPTH) ** 0.25
LN_EPS = 1e-5
RMS_EPS = 1e-6
LAM_INIT = 0.8 - 0.6 * math.exp(-0.3 * 0)
LOG2E = math.log2(math.e)

LANES = 128
HEAD_PAD = 128
VMEM_LIMIT_BYTES = 56 * 1024 * 1024

ROW_TILE = 512
ATTN_TQ = 512
ATTN_TK = 256


def _params(semantics):
    return pltpu.CompilerParams(dimension_semantics=semantics, vmem_limit_bytes=VMEM_LIMIT_BYTES)


def _dot(a, b):
    return jnp.dot(a, b, preferred_element_type=F32)


def _dot_nt(a, b):
    return lax.dot_general(a, b, (((1,), (1,)), ((), ())), preferred_element_type=F32)


def _dot_tn(a, b):
    return lax.dot_general(a, b, (((0,), (0,)), ((), ())), preferred_element_type=F32)


def _layer_norm(y, g, b):
    mu = jnp.mean(y, axis=-1, keepdims=True)
    d = y - mu
    var = jnp.mean(d * d, axis=-1, keepdims=True)
    return d * lax.rsqrt(var + LN_EPS) * g + b


def _rms_norm_rows(x, g):
    return x * lax.rsqrt(jnp.mean(x * x, axis=-1, keepdims=True) + RMS_EPS) * g


def _ffn_ln_kernel(x_ref, w1_ref, w3_ref, w2_ref, g_ref, b_ref, o_ref):
    x = x_ref[...]
    xb = x.astype(BF16)
    u = _dot(xb, w1_ref[...])
    v = _dot(xb, w3_ref[...])
    act = (u / (1.0 + jnp.exp(-u))) * v
    y = ALPHA * x + 0.5 * _dot(act.astype(BF16), w2_ref[...])
    o_ref[...] = _layer_norm(y, g_ref[...], b_ref[...])


def _resident(a):
    return pl.BlockSpec(a.shape, lambda *_: (0,) * a.ndim, pipeline_mode=pl.Buffered(1))


def _ffn_ln(x, w1, w3, w2, g, b):
    n, d = x.shape
    tm = ROW_TILE
    assert n % tm == 0
    ws = [w1.astype(BF16), w3.astype(BF16), w2.astype(BF16), g.reshape(1, d), b.reshape(1, d)]
    return pl.pallas_call(
        _ffn_ln_kernel,
        grid=(n // tm,),
        in_specs=[pl.BlockSpec((tm, d), lambda i: (i, 0))] + [_resident(a) for a in ws],
        out_specs=pl.BlockSpec((tm, d), lambda i: (i, 0)),
        out_shape=jax.ShapeDtypeStruct((n, d), F32),
        compiler_params=_params(("arbitrary",)),
        name="ffn_ln",
    )(x, *ws)


def _in_proj_kernel(h_ref, wnat_ref, wt_ref, wqab_ref, wkn_ref, wvt_ref, gq_ref, gkv_ref,
                    cos_ref, sin_ref, cost_ref, sint_ref,
                    qtm_ref, km_ref, vtm_ref, qtd_ref, kd_ref, vtd_ref):
    tm = h_ref.shape[0]
    tk = vtm_ref.shape[-1]
    hb = h_ref[...].astype(BF16)

    pn = _dot(hb, wnat_ref[...])
    o1 = MLA_Q_RANK
    o2 = o1 + MLA_KV_RANK
    o3 = o2 + HEAD_PAD
    o4 = o3 + HEAD_PAD
    c_q, c_kv, kra, krb, kd = pn[:, :o1], pn[:, o1:o2], pn[:, o2:o3], pn[:, o3:o4], pn[:, o4:]
    kd_ref[...] = kd.astype(BF16)

    pt = _dot_nt(wt_ref[...], hb)
    nqd = qtd_ref.shape[0]
    qtd_ref[...] = (pt[:nqd] * (DIFF_D ** -0.5 * LOG2E)).astype(BF16)
    for c in range(tm // tk):
        vtd_ref[c] = pt[nqd:, c * tk:(c + 1) * tk].astype(BF16)

    cqn = _rms_norm_rows(c_q, gq_ref[...]).astype(BF16)
    qab = _dot_nt(wqab_ref[...], cqn)
    cos_t, sin_t = cost_ref[...], sint_ref[...]
    nq = MLA_HEADS * HEAD_PAD
    for hh in range(MLA_HEADS):
        lo, hi = hh * HEAD_PAD, (hh + 1) * HEAD_PAD
        q = qab[lo:hi] * cos_t + qab[nq + lo:nq + hi] * sin_t
        qtm_ref[lo:hi, :] = (q * ((MLA_NOPE + MLA_ROPE) ** -0.5 * LOG2E)).astype(BF16)

    ckvn = _rms_norm_rows(c_kv, gkv_ref[...]).astype(BF16)
    kn = _dot(ckvn, wkn_ref[...])
    kr = kra * cos_ref[...] + krb * sin_ref[...]
    for hh in range(MLA_HEADS):
        lo, hi = hh * HEAD_PAD, (hh + 1) * HEAD_PAD
        km_ref[:, lo:hi] = (kn[:, lo:hi] + kr).astype(BF16)
    vt = _dot_nt(wvt_ref[...], ckvn)
    for c in range(tm // tk):
        vtm_ref[c] = vt[:, c * tk:(c + 1) * tk].astype(BF16)


def _in_proj_weights(w_in, w_q_b, w_kv_b):
    d = w_in.shape[0]
    o_cq = MLA_Q_RANK
    o_ckv = o_cq + MLA_KV_RANK
    o_kr = o_ckv + MLA_ROPE
    n_d = 2 * DIFF_HEADS * DIFF_D
    o_qd = o_kr + n_d
    o_kd = o_qd + n_d
    half = MLA_ROPE // 2
    pad = HEAD_PAD - MLA_NOPE - MLA_ROPE
    kr = w_in[:, o_ckv:o_kr]
    z_nope = jnp.zeros((d, MLA_NOPE), F32)
    z_pad = jnp.zeros((d, pad), F32)
    kra = jnp.concatenate([z_nope, kr, z_pad], axis=1)
    krb = jnp.concatenate([z_nope, -kr[:, half:], kr[:, :half], z_pad], axis=1)
    w_nat = jnp.concatenate([w_in[:, :o_ckv], kra, krb, w_in[:, o_qd:o_kd]], axis=1).astype(BF16)
    w_t = jnp.concatenate([w_in[:, o_kr:o_qd], w_in[:, o_kd:]], axis=1).T.astype(BF16)

    r = w_q_b.shape[0]
    wq = w_q_b.reshape(r, MLA_HEADS, MLA_NOPE + MLA_ROPE)
    nope = wq[..., :MLA_NOPE]
    r1 = wq[..., MLA_NOPE:MLA_NOPE + half]
    r2 = wq[..., MLA_NOPE + half:]
    zq_pad = jnp.zeros((r, MLA_HEADS, pad), F32)
    wqa = jnp.concatenate([nope, r1, r2, zq_pad], axis=-1).reshape(r, MLA_HEADS * HEAD_PAD)
    wqb = jnp.concatenate([jnp.zeros_like(nope), -r2, r1, zq_pad], axis=-1).reshape(r, MLA_HEADS * HEAD_PAD)
    wqab_t = jnp.concatenate([wqa, wqb], axis=1).T.astype(BF16)

    rk = w_kv_b.shape[0]
    wkv = w_kv_b.reshape(rk, MLA_HEADS, MLA_NOPE + MLA_V)
    wkn = jnp.concatenate([wkv[..., :MLA_NOPE], jnp.zeros((rk, MLA_HEADS, HEAD_PAD - MLA_NOPE), F32)],
                          axis=-1).reshape(rk, MLA_HEADS * HEAD_PAD).astype(BF16)
    wv_t = wkv[..., MLA_NOPE:].reshape(rk, MLA_HEADS * MLA_V).T.astype(BF16)
    return w_nat, w_t, wqab_t, wkn, wv_t


def _rope_tables(s):
    half = MLA_ROPE // 2
    inv_freq = ROPE_BASE ** (-jnp.arange(0, MLA_ROPE, 2, dtype=F32) / MLA_ROPE)
    ang = jnp.arange(s).astype(F32)[:, None] * inv_freq[None, :]
    cos, sin = jnp.cos(ang), jnp.sin(ang)
    pad = HEAD_PAD - MLA_NOPE - MLA_ROPE
    cos_p = jnp.concatenate([jnp.ones((s, MLA_NOPE), F32), cos, cos, jnp.zeros((s, pad), F32)], axis=1)
    sin_p = jnp.concatenate([jnp.zeros((s, MLA_NOPE), F32), sin, sin, jnp.zeros((s, pad), F32)], axis=1)
    assert half * 2 == MLA_ROPE
    return cos_p, sin_p


def _in_proj(h, bsz, s, w_in, q_norm_g, kv_norm_g, w_q_b, w_kv_b):
    n, d = h.shape
    tm, tk = ROW_TILE, ATTN_TK
    assert s % tm == 0 and tm % tk == 0
    w_nat, w_t, wqab_t, wkn, wv_t = _in_proj_weights(w_in, w_q_b, w_kv_b)
    cos_p, sin_p = _rope_tables(s)
    spt = s // tm
    nqm = MLA_HEADS * HEAD_PAD
    nvm = MLA_HEADS * MLA_V
    nd = 2 * DIFF_HEADS * DIFF_D

    def full(a):
        return pl.BlockSpec(a.shape, lambda b, i: (0,) * a.ndim)

    gq = q_norm_g.reshape(1, -1)
    gkv = kv_norm_g.reshape(1, -1)
    out_shape = (
        jax.ShapeDtypeStruct((bsz, nqm, s), BF16),
        jax.ShapeDtypeStruct((bsz, s, nqm), BF16),
        jax.ShapeDtypeStruct((bsz, s // tk, nvm, tk), BF16),
        jax.ShapeDtypeStruct((bsz, nd, s), BF16),
        jax.ShapeDtypeStruct((bsz, s, nd), BF16),
        jax.ShapeDtypeStruct((bsz, s // tk, nd, tk), BF16),
    )
    out_specs = (
        pl.BlockSpec((None, nqm, tm), lambda b, i: (b, 0, i)),
        pl.BlockSpec((None, tm, nqm), lambda b, i: (b, i, 0)),
        pl.BlockSpec((None, tm // tk, nvm, tk), lambda b, i: (b, i, 0, 0)),
        pl.BlockSpec((None, nd, tm), lambda b, i: (b, 0, i)),
        pl.BlockSpec((None, tm, nd), lambda b, i: (b, i, 0)),
        pl.BlockSpec((None, tm // tk, nd, tk), lambda b, i: (b, i, 0, 0)),
    )
    return pl.pallas_call(
        _in_proj_kernel,
        grid=(bsz, spt),
        in_specs=[
            pl.BlockSpec((tm, d), lambda b, i: (b * spt + i, 0)),
            full(w_nat), full(w_t), full(wqab_t), full(wkn), full(wv_t), full(gq), full(gkv),
            pl.BlockSpec((tm, HEAD_PAD), lambda b, i: (i, 0)),
            pl.BlockSpec((tm, HEAD_PAD), lambda b, i: (i, 0)),
            pl.BlockSpec((HEAD_PAD, tm), lambda b, i: (0, i)),
            pl.BlockSpec((HEAD_PAD, tm), lambda b, i: (0, i)),
        ],
        out_specs=out_specs,
        out_shape=out_shape,
        compiler_params=_params(("arbitrary", "arbitrary")),
        name="in_proj",
    )(h, w_nat, w_t, wqab_t, wkn, wv_t, gq, gkv, cos_p, sin_p, cos_p.T, sin_p.T)


BIAS_TILES = 3


def _rel_bias_kernel(rb_ref, o_ref, *, tq, tk):
    c = pl.program_id(0) - 1
    kk = lax.broadcasted_iota(jnp.int32, (tk, tq), 0)
    qq = lax.broadcasted_iota(jnp.int32, (tk, tq), 1)
    n = jnp.maximum(qq - kk - c * tk, 0)
    max_exact = REL_BUCKETS // 2
    nf = jnp.maximum(n, 1).astype(F32)
    large = max_exact + (jnp.log(nf / max_exact) / math.log(REL_MAX_DIST / max_exact)
                         * (REL_BUCKETS - max_exact)).astype(jnp.int32)
    large = jnp.minimum(large, REL_BUCKETS - 1)
    bucket = jnp.where(n < max_exact, n, large)
    for m in range(o_ref.shape[0]):
        acc = jnp.zeros((tk, tq), F32)
        for bkt in range(REL_BUCKETS):
            acc = jnp.where(bucket == bkt, rb_ref[bkt, m], acc)
        o_ref[m] = (acc - rb_ref[REL_BUCKETS - 1, m]) * LOG2E


def _rel_bias_tiles(rel_bias, tq, tk):
    assert tq == 2 * tk and tk + 1 >= REL_MAX_DIST
    nmaps = rel_bias.shape[1]
    return pl.pallas_call(
        functools.partial(_rel_bias_kernel, tq=tq, tk=tk),
        grid=(BIAS_TILES,),
        in_specs=[pl.BlockSpec(memory_space=pltpu.SMEM)],
        out_specs=pl.BlockSpec((nmaps, None, tk, tq), lambda t: (0, t, 0, 0)),
        out_shape=jax.ShapeDtypeStruct((nmaps, BIAS_TILES, tk, tq), F32),
        compiler_params=_params(("arbitrary",)),
        name="rel_bias",
    )(rel_bias)


SCORES_AHEAD = 3
ONES_ROWS = 16


def _flash_pipeline(q_fn, k_ref, vt_ref, logits_fn, finalize_fn, s_ref, mt_ref, p_ref, al_ref, m_ref, acc_ref,
                    *, nq, tq, tk):
    tiles = [(i, t, t - 2 * i) for i in range(nq) for t in range(2 * i + 2)]
    ones = jnp.ones((ONES_ROWS, tk), BF16)
    ahead = s_ref.shape[0] - 1

    def lane0(c):
        return tk if c == 1 else 0

    def scores(n):
        i, t, c = tiles[n]
        slot, lo = n % (ahead + 1), lane0(c)
        s = logits_fn(_dot(k_ref[t * tk:(t + 1) * tk, :], q_fn(i, lo)), c, lo)
        s_ref[slot, :, lo:] = s
        mt_ref[slot, :, lo:] = jnp.max(s, axis=0, keepdims=True)

    def softmax(n):
        i, t, c = tiles[n]
        sslot, slot, lo = n % (ahead + 1), n % 2, lane0(c)
        m_new = mt_ref[sslot, :, lo:]
        if t > 0:
            m_prev = m_ref[:, lo:]
            m_new = jnp.maximum(m_prev, m_new)
            al_ref[slot, :, lo:] = jnp.exp2(m_prev - m_new)
        m_ref[:, lo:] = m_new
        p_ref[slot, :, lo:] = jnp.exp2(s_ref[sslot, :, lo:] - m_new).astype(BF16)

    def values(n):
        i, t, c = tiles[n]
        slot, lo = n % 2, lane0(c)
        pv = _dot(jnp.concatenate([vt_ref[t], ones], axis=0), p_ref[slot, :, lo:])
        if t > 0:
            pv = al_ref[slot, :, lo:] * acc_ref[:, lo:] + pv
        acc_ref[:, lo:] = pv
        if c == 1:
            finalize_fn(i)

    for n in range(ahead):
        scores(n)
    for n in range(len(tiles)):
        if n > 0:
            values(n - 1)
        if n + ahead < len(tiles):
            scores(n + ahead)
        softmax(n)
    values(len(tiles) - 1)


def _tri_mask(s):
    kk = lax.broadcasted_iota(jnp.int32, s.shape, 0)
    qq = lax.broadcasted_iota(jnp.int32, s.shape, 1)
    return jnp.where(kk <= qq, s, -jnp.inf)


def _flash_scratch(dv, tq, tk):
    return [pltpu.VMEM((SCORES_AHEAD + 1, tk, tq), F32), pltpu.VMEM((SCORES_AHEAD + 1, 1, tq), F32),
            pltpu.VMEM((2, tk, tq), BF16), pltpu.VMEM((2, 1, tq), F32),
            pltpu.VMEM((1, tq), F32), pltpu.VMEM((dv + ONES_ROWS, tq), F32)]


def _mla_attn_kernel(qt_ref, k_ref, vt_ref, o_ref, *scratch, tq, tk):
    acc_ref = scratch[-1]

    def q_fn(i, lo):
        return qt_ref[:, i * tq + lo:(i + 1) * tq]

    def logits(s, c, lo):
        return _tri_mask(s) if c >= 0 else s

    def finalize(i):
        acc = acc_ref[...]
        o_ref[:, i * tq:(i + 1) * tq] = (acc[:MLA_V] / acc[MLA_V:MLA_V + 1]).astype(BF16)

    _flash_pipeline(q_fn, k_ref, vt_ref, logits, finalize, *scratch, nq=qt_ref.shape[1] // tq, tq=tq, tk=tk)


def _mla_attn(qt, k, vt, tq, tk):
    bsz, nq, s = qt.shape
    nkt = s // tk
    assert tq == 2 * tk and s % tq == 0
    return pl.pallas_call(
        functools.partial(_mla_attn_kernel, tq=tq, tk=tk),
        grid=(MLA_HEADS, bsz),
        in_specs=[
            pl.BlockSpec((None, HEAD_PAD, s), lambda h, b: (b, h, 0)),
            pl.BlockSpec((None, s, HEAD_PAD), lambda h, b: (b, 0, h)),
            pl.BlockSpec((None, nkt, MLA_V, tk), lambda h, b: (b, 0, h, 0)),
        ],
        out_specs=pl.BlockSpec((None, MLA_V, s), lambda h, b: (b, h, 0)),
        out_shape=jax.ShapeDtypeStruct((bsz, MLA_HEADS * MLA_V, s), BF16),
        scratch_shapes=_flash_scratch(MLA_V, tq, tk),
        compiler_params=_params(("arbitrary", "arbitrary")),
        name="mla_attn",
    )(qt, k, vt)


def _diff_attn_kernel(qt_ref, k_ref, vt_ref, bias_ref, lamv_ref, gn_ref, o_ref, om_ref, qm_ref, *scratch,
                      tq, tk):
    a = pl.program_id(2)
    acc_ref = scratch[-1]
    dv = 2 * DIFF_D

    qt = qt_ref[...]
    row = lax.broadcasted_iota(jnp.int32, qt.shape, 0)
    keep = jnp.logical_and(row >= a * DIFF_D, row < (a + 1) * DIFF_D)
    qm_ref[...] = jnp.where(keep, qt, jnp.zeros_like(qt))

    def q_fn(i, lo):
        return qm_ref[:, i * tq + lo:(i + 1) * tq]

    def logits(s, c, lo):
        if c >= -1:
            s = s + bias_ref[c + 1, :, lo:]
        return _tri_mask(s) if c >= 0 else s

    def finalize(i):
        acc = acc_ref[...]
        om_ref[a, :, i * tq:(i + 1) * tq] = acc[:dv] / acc[dv:dv + 1]

    _flash_pipeline(q_fn, k_ref, vt_ref, logits, finalize, *scratch, nq=qt_ref.shape[1] // tq, tq=tq, tk=tk)

    @pl.when(a == 1)
    def _():
        lv = lamv_ref[...]
        lam = (jnp.exp(jnp.sum(lv[0:1] * lv[1:2], axis=1, keepdims=True))
               - jnp.exp(jnp.sum(lv[2:3] * lv[3:4], axis=1, keepdims=True)) + LAM_INIT)
        d = om_ref[0] - lam * om_ref[1]
        d = d * lax.rsqrt(jnp.mean(d * d, axis=0, keepdims=True) + RMS_EPS) * gn_ref[...] * (1.0 - LAM_INIT)
        o_ref[...] = d.astype(BF16)


def _diff_attn(qt, k, vt, bias, lamv, gn, tq, tk):
    bsz, nd, s = qt.shape
    nkt = s // tk
    dv = 2 * DIFF_D
    assert tq == 2 * tk and s % tq == 0
    return pl.pallas_call(
        functools.partial(_diff_attn_kernel, tq=tq, tk=tk),
        grid=(DIFF_HEADS, bsz, 2),
        in_specs=[
            pl.BlockSpec((None, dv, s), lambda h, b, a: (b, h, 0)),
            pl.BlockSpec((None, s, dv), lambda h, b, a: (b, 0, h)),
            pl.BlockSpec((None, nkt, dv, tk), lambda h, b, a: (b, 0, h, 0)),
            pl.BlockSpec((None, BIAS_TILES, tk, tq), lambda h, b, a: (2 * h + a, 0, 0, 0)),
            pl.BlockSpec(lamv.shape, lambda h, b, a: (0, 0)),
            pl.BlockSpec(gn.shape, lambda h, b, a: (0, 0)),
        ],
        out_specs=pl.BlockSpec((None, dv, s), lambda h, b, a: (b, h, 0)),
        out_shape=jax.ShapeDtypeStruct((bsz, nd, s), BF16),
        scratch_shapes=[pltpu.VMEM((2, dv, s), F32), pltpu.VMEM((dv, s), BF16)] + _flash_scratch(dv, tq, tk),
        compiler_params=_params(("arbitrary", "arbitrary", "arbitrary")),
        name="diff_attn",
    )(qt, k, vt, bias, lamv, gn)


def _post_mixer_kernel(h_ref, omt_ref, odt_ref, wum_ref, wud_ref, wg_ref, bg_ref, wo_ref, g_ref, b_ref,
                       o_ref):
    h = h_ref[...]
    d = h.shape[1]
    y_m = _dot_tn(omt_ref[...], wum_ref[...])
    y_d = _dot_tn(odt_ref[...], wud_ref[...])
    z = _dot(h.astype(BF16), wg_ref[...]) + bg_ref[...]
    gate = 1.0 / (1.0 + jnp.exp(-z))
    mixed = gate[:, :d] * y_m + gate[:, d:] * y_d
    mix = _dot(mixed.astype(BF16), wo_ref[...])
    o_ref[...] = _layer_norm(ALPHA * h + mix, g_ref[...], b_ref[...])


def _post_mixer(h, omt, odt, bsz, s, w_up_mla, w_up_diff, w_gate, b_gate, w_o, g, b):
    n, d = h.shape
    tm = ROW_TILE
    spt = s // tm
    ws = [w_up_mla.astype(BF16), w_up_diff.astype(BF16), w_gate.astype(BF16), b_gate.reshape(1, -1),
          w_o.astype(BF16), g.reshape(1, d), b.reshape(1, d)]

    def full(a):
        return pl.BlockSpec(a.shape, lambda bb, i: (0,) * a.ndim)

    return pl.pallas_call(
        _post_mixer_kernel,
        grid=(bsz, spt),
        in_specs=[
            pl.BlockSpec((tm, d), lambda bb, i: (bb * spt + i, 0)),
            pl.BlockSpec((None, omt.shape[1], tm), lambda bb, i: (bb, 0, i)),
            pl.BlockSpec((None, odt.shape[1], tm), lambda bb, i: (bb, 0, i)),
        ] + [full(a) for a in ws],
        out_specs=pl.BlockSpec((tm, d), lambda bb, i: (bb * spt + i, 0)),
        out_shape=jax.ShapeDtypeStruct((n, d), F32),
        compiler_params=_params(("arbitrary", "arbitrary")),
        name="post_mixer",
    )(h, omt, odt, *ws)


def _mem_kv_kernel(mem_ref, w_ref, k_ref, v_ref):
    d = k_ref.shape[1]
    kv = _dot(mem_ref[...].astype(BF16), w_ref[...])
    k_ref[...] = kv[:, :d].astype(BF16)
    v_ref[...] = kv[:, d:].astype(BF16)


def _mem_kv(mem, w_kv):
    bsz, m, d = mem.shape
    w = w_kv.astype(BF16)
    spec = pl.BlockSpec((None, m, d), lambda b: (b, 0, 0))
    return pl.pallas_call(
        _mem_kv_kernel,
        grid=(bsz,),
        in_specs=[spec, pl.BlockSpec(w.shape, lambda b: (0, 0))],
        out_specs=(spec, spec),
        out_shape=(jax.ShapeDtypeStruct((bsz, m, d), BF16), jax.ShapeDtypeStruct((bsz, m, d), BF16)),
        compiler_params=_params(("arbitrary",)),
        name="mem_kv",
    )(mem, w)


def _mem_attn_kernel(h_ref, k_ref, v_ref, wq_ref, wo_ref, g_ref, b_ref, o_ref):
    h = h_ref[...]
    d = h.shape[1]
    hd = d // MEM_HEADS
    q = _dot(h.astype(BF16), wq_ref[...]).astype(BF16)
    outs = []
    for hh in range(MEM_HEADS):
        lo, hi = hh * hd, (hh + 1) * hd
        s = _dot_nt(q[:, lo:hi], k_ref[:, lo:hi]) * (hd ** -0.5)
        p = jnp.exp(s - jnp.max(s, axis=-1, keepdims=True))
        p = p / jnp.sum(p, axis=-1, keepdims=True)
        outs.append(_dot(p.astype(BF16), v_ref[:, lo:hi]).astype(BF16))
    att = _dot(jnp.concatenate(outs, axis=1), wo_ref[...])
    o_ref[...] = _layer_norm(ALPHA * h + att, g_ref[...], b_ref[...])


def _mem_attn(h, km, vm, bsz, s, w_q, w_o, g, b):
    n, d = h.shape
    tm = ROW_TILE
    spt = s // tm
    m = km.shape[1]
    ws = [w_q.astype(BF16), w_o.astype(BF16), g.reshape(1, d), b.reshape(1, d)]

    def full(a):
        return pl.BlockSpec(a.shape, lambda bb, i: (0,) * a.ndim)

    return pl.pallas_call(
        _mem_attn_kernel,
        grid=(bsz, spt),
        in_specs=[
            pl.BlockSpec((tm, d), lambda bb, i: (bb * spt + i, 0)),
            pl.BlockSpec((None, m, d), lambda bb, i: (bb, 0, 0)),
            pl.BlockSpec((None, m, d), lambda bb, i: (bb, 0, 0)),
        ] + [full(a) for a in ws],
        out_specs=pl.BlockSpec((tm, d), lambda bb, i: (bb * spt + i, 0)),
        out_shape=jax.ShapeDtypeStruct((n, d), F32),
        compiler_params=_params(("arbitrary", "arbitrary")),
        name="mem_attn",
    )(h, km, vm, *ws)


def kernel(x, mem, rel_bias, ffn1_w1, ffn1_w3, ffn1_w2, ln1_g, ln1_b, w_in, q_norm_g, kv_norm_g, w_q_b, w_kv_b, lam_q1, lam_k1, lam_q2, lam_k2, diff_norm_g, w_gate, b_gate, w_up_mla, w_up_diff, w_o, ln2_g, ln2_b, mem_w_q, mem_w_kv, mem_w_o, ln3_g, ln3_b, ffn2_w1, ffn2_w3, ffn2_w2, ln4_g, ln4_b):
    bsz, s, d = x.shape
    assert ffn1_w1.shape[0] == DEPTH
    l = 0
    h = x.reshape(bsz * s, d)

    h = _ffn_ln(h, ffn1_w1[l], ffn1_w3[l], ffn1_w2[l], ln1_g[l], ln1_b[l])

    qtm, km, vtm, qtd, kd, vtd = _in_proj(h, bsz, s, w_in[l], q_norm_g[l], kv_norm_g[l], w_q_b[l], w_kv_b[l])
    bias = _rel_bias_tiles(rel_bias, ATTN_TQ, ATTN_TK)
    omt = _mla_attn(qtm, km, vtm, ATTN_TQ, ATTN_TK)
    lamv = jnp.stack([lam_q1[l], lam_k1[l], lam_q2[l], lam_k2[l]], axis=0)
    odt = _diff_attn(qtd, kd, vtd, bias, lamv, diff_norm_g[l].reshape(-1, 1), ATTN_TQ, ATTN_TK)
    h = _post_mixer(h, omt, odt, bsz, s, w_up_mla[l], w_up_diff[l], w_gate[l], b_gate[l], w_o[l],
                    ln2_g[l], ln2_b[l])

    kmem, vmem = _mem_kv(mem, mem_w_kv[l])
    h = _mem_attn(h, kmem, vmem, bsz, s, mem_w_q[l], mem_w_o[l], ln3_g[l], ln3_b[l])

    h = _ffn_ln(h, ffn2_w1[l], ffn2_w3[l], ffn2_w2[l], ln4_g[l], ln4_b[l])
    return h.reshape(bsz, s, d)
```

```python
import functools
import math

import jax
import jax.numpy as jnp
from jax import lax
from jax.experimental import pallas as pl
from jax.experimental.pallas import tpu as pltpu

F32 = jnp.float32
BF16 = jnp.bfloat16

DEPTH = 1
MLA_HEADS = 8
MLA_Q_RANK = 384
MLA_KV_RANK = 256
MLA_NOPE = 64
MLA_ROPE = 32
MLA_V = 64
ROPE_BASE = 10000.0
DIFF_HEADS = 4
DIFF_D = 64
REL_BUCKETS = 32
REL_MAX_DIST = 128
MEM_HEADS = 4
ALPHA = (2.0 * DEPTH) ** 0.25
LN_EPS = 1e-5
RMS_EPS = 1e-6
LAM_INIT = 0.8 - 0.6 * math.exp(-0.3 * 0)
LOG2E = math.log2(math.e)

LANES = 128
HEAD_PAD = 128
VMEM_LIMIT_BYTES = 56 * 1024 * 1024

ROW_TILE = 512
ROW_SUBTILES = 2
ATTN_TQ = 512
ATTN_TK = 256


def _params(semantics):
    return pltpu.CompilerParams(dimension_semantics=semantics, vmem_limit_bytes=VMEM_LIMIT_BYTES)


def _dot(a, b):
    return jnp.dot(a, b, preferred_element_type=F32)


def _dot_nt(a, b):
    return lax.dot_general(a, b, (((1,), (1,)), ((), ())), preferred_element_type=F32)


def _dot_tn(a, b):
    return lax.dot_general(a, b, (((0,), (0,)), ((), ())), preferred_element_type=F32)


def _layer_norm(y, g, b):
    mu = jnp.mean(y, axis=-1, keepdims=True)
    d = y - mu
    var = jnp.mean(d * d, axis=-1, keepdims=True)
    return d * lax.rsqrt(var + LN_EPS) * g + b


def _rms_norm_rows(x, g):
    return x * lax.rsqrt(jnp.mean(x * x, axis=-1, keepdims=True) + RMS_EPS) * g


def _subtiles(rows):
    return [slice(r, r + ROW_TILE) for r in range(0, rows, ROW_TILE)]


def _resident(a):
    return pl.BlockSpec(a.shape, lambda *_: (0,) * a.ndim, pipeline_mode=pl.Buffered(1))


def _ffn_ln_kernel(x_ref, w1_ref, w3_ref, w2_ref, g_ref, b_ref, o_ref):
    x = x_ref[...]
    xb = x.astype(BF16)
    u = _dot(xb, w1_ref[...])
    v = _dot(xb, w3_ref[...])
    act = (u / (1.0 + jnp.exp(-u))) * v
    y = ALPHA * x + 0.5 * _dot(act.astype(BF16), w2_ref[...])
    o_ref[...] = _layer_norm(y, g_ref[...], b_ref[...])


def _ffn_ln(x, w1, w3, w2, g, b):
    n, d = x.shape
    tm = ROW_TILE
    assert n % tm == 0
    ws = [w1.astype(BF16), w3.astype(BF16), w2.astype(BF16), g.reshape(1, d), b.reshape(1, d)]
    return pl.pallas_call(
        _ffn_ln_kernel,
        grid=(n // tm,),
        in_specs=[pl.BlockSpec((tm, d), lambda i: (i, 0))] + [_resident(a) for a in ws],
        out_specs=pl.BlockSpec((tm, d), lambda i: (i, 0)),
        out_shape=jax.ShapeDtypeStruct((n, d), F32),
        compiler_params=_params(("arbitrary",)),
        name="ffn_ln",
    )(x, *ws)


def _in_proj_kernel(h_ref, wnat_ref, wt_ref, wq_ref, wkn_ref, wvt_ref, gq_ref, gkv_ref,
                    cosk_ref, sink_ref, cosq_ref, sinq_ref,
                    qtm_ref, km_ref, vtm_ref, qtd_ref, kd_ref, vtd_ref):
    tm = h_ref.shape[0]
    tk = vtm_ref.shape[-1]
    half = MLA_ROPE // 2
    hb = h_ref[...].astype(BF16)

    pn = _dot(hb, wnat_ref[...])
    o1 = MLA_Q_RANK
    o2 = o1 + MLA_KV_RANK
    o3 = o2 + HEAD_PAD
    c_q, c_kv, krg, kd = pn[:, :o1], pn[:, o1:o2], pn[:, o2:o3], pn[:, o3:]
    kd_ref[...] = kd.astype(BF16)

    pt = _dot_nt(wt_ref[...], hb)
    nqd = qtd_ref.shape[0]
    qtd_ref[...] = (pt[:nqd] * (DIFF_D ** -0.5 * LOG2E)).astype(BF16)
    for c in range(tm // tk):
        vtd_ref[c] = pt[nqd:, c * tk:(c + 1) * tk].astype(BF16)

    cqn = _rms_norm_rows(c_q, gq_ref[...]).astype(BF16)
    qt = _dot_nt(wq_ref[...], cqn)
    cos_q, sin_q = cosq_ref[...], sinq_ref[...]
    q_scale = (MLA_NOPE + MLA_ROPE) ** -0.5 * LOG2E
    for hh in range(MLA_HEADS):
        lo = hh * HEAD_PAD
        r0 = lo + MLA_NOPE
        r1, r2 = r0 + half, r0 + MLA_ROPE
        x1, x2 = qt[r0:r1], qt[r1:r2]
        qtm_ref[lo:r0, :] = (qt[lo:r0] * q_scale).astype(BF16)
        qtm_ref[r0:r1, :] = ((x1 * cos_q - x2 * sin_q) * q_scale).astype(BF16)
        qtm_ref[r1:r2, :] = ((x2 * cos_q + x1 * sin_q) * q_scale).astype(BF16)
        qtm_ref[r2:lo + HEAD_PAD, :] = jnp.zeros((lo + HEAD_PAD - r2, tm), BF16)

    ckvn = _rms_norm_rows(c_kv, gkv_ref[...]).astype(BF16)
    kn = _dot(ckvn, wkn_ref[...])
    kr = krg * cosk_ref[...] + pltpu.roll(krg, HEAD_PAD - MLA_ROPE, 1) * sink_ref[...]
    for hh in range(MLA_HEADS):
        lo, hi = hh * HEAD_PAD, (hh + 1) * HEAD_PAD
        km_ref[:, lo:hi] = (kn[:, lo:hi] + kr).astype(BF16)
    vt = _dot_nt(wvt_ref[...], ckvn)
    for c in range(tm // tk):
        vtm_ref[c] = vt[:, c * tk:(c + 1) * tk].astype(BF16)


def _in_proj_weights(w_in, w_q_b, w_kv_b):
    d = w_in.shape[0]
    o_cq = MLA_Q_RANK
    o_ckv = o_cq + MLA_KV_RANK
    o_kr = o_ckv + MLA_ROPE
    n_d = 2 * DIFF_HEADS * DIFF_D
    o_qd = o_kr + n_d
    o_kd = o_qd + n_d
    half = MLA_ROPE // 2
    pad = HEAD_PAD - MLA_NOPE - MLA_ROPE
    assert pad == MLA_ROPE
    kr = w_in[:, o_ckv:o_kr]
    krg = jnp.concatenate([jnp.zeros((d, MLA_NOPE), F32), kr, -kr[:, half:], kr[:, :half]], axis=1)
    w_nat = jnp.concatenate([w_in[:, :o_ckv], krg, w_in[:, o_qd:o_kd]], axis=1).astype(BF16)
    w_t = jnp.concatenate([w_in[:, o_kr:o_qd], w_in[:, o_kd:]], axis=1).T.astype(BF16)

    r = w_q_b.shape[0]
    wq = w_q_b.reshape(r, MLA_HEADS, MLA_NOPE + MLA_ROPE)
    wq = jnp.concatenate([wq, jnp.zeros((r, MLA_HEADS, pad), F32)], axis=-1)
    wq_t = wq.reshape(r, MLA_HEADS * HEAD_PAD).T.astype(BF16)

    rk = w_kv_b.shape[0]
    wkv = w_kv_b.reshape(rk, MLA_HEADS, MLA_NOPE + MLA_V)
    wkn = jnp.concatenate([wkv[..., :MLA_NOPE], jnp.zeros((rk, MLA_HEADS, HEAD_PAD - MLA_NOPE), F32)],
                          axis=-1).reshape(rk, MLA_HEADS * HEAD_PAD).astype(BF16)
    wv_t = wkv[..., MLA_NOPE:].reshape(rk, MLA_HEADS * MLA_V).T.astype(BF16)
    return w_nat, w_t, wq_t, wkn, wv_t


def _rope_tables(s):
    inv_freq = ROPE_BASE ** (-jnp.arange(0, MLA_ROPE, 2, dtype=F32) / MLA_ROPE)
    ang = jnp.arange(s).astype(F32)[:, None] * inv_freq[None, :]
    cos, sin = jnp.cos(ang), jnp.sin(ang)
    z_lo = jnp.zeros((s, MLA_NOPE), F32)
    z_hi = jnp.zeros((s, HEAD_PAD - MLA_NOPE - MLA_ROPE), F32)
    cos_k = jnp.concatenate([z_lo, cos, cos, z_hi], axis=1)
    sin_k = jnp.concatenate([z_lo, sin, sin, z_hi], axis=1)
    return cos_k, sin_k, cos.T, sin.T


def _in_proj(h, bsz, s, w_in, q_norm_g, kv_norm_g, w_q_b, w_kv_b):
    n, d = h.shape
    tm, tk = ROW_TILE, ATTN_TK
    assert s % tm == 0 and tm % tk == 0
    w_nat, w_t, wq_t, wkn, wv_t = _in_proj_weights(w_in, w_q_b, w_kv_b)
    cos_k, sin_k, cos_q, sin_q = _rope_tables(s)
    half = MLA_ROPE // 2
    spt = s // tm
    nqm = MLA_HEADS * HEAD_PAD
    nvm = MLA_HEADS * MLA_V
    nd = 2 * DIFF_HEADS * DIFF_D

    gq = q_norm_g.reshape(1, -1)
    gkv = kv_norm_g.reshape(1, -1)
    out_shape = (
        jax.ShapeDtypeStruct((bsz, nqm, s), BF16),
        jax.ShapeDtypeStruct((bsz, s, nqm), BF16),
        jax.ShapeDtypeStruct((bsz, s // tk, nvm, tk), BF16),
        jax.ShapeDtypeStruct((bsz, nd, s), BF16),
        jax.ShapeDtypeStruct((bsz, s, nd), BF16),
        jax.ShapeDtypeStruct((bsz, s // tk, nd, tk), BF16),
    )
    out_specs = (
        pl.BlockSpec((None, nqm, tm), lambda b, i: (b, 0, i)),
        pl.BlockSpec((None, tm, nqm), lambda b, i: (b, i, 0)),
        pl.BlockSpec((None, tm // tk, nvm, tk), lambda b, i: (b, i, 0, 0)),
        pl.BlockSpec((None, nd, tm), lambda b, i: (b, 0, i)),
        pl.BlockSpec((None, tm, nd), lambda b, i: (b, i, 0)),
        pl.BlockSpec((None, tm // tk, nd, tk), lambda b, i: (b, i, 0, 0)),
    )
    return pl.pallas_call(
        _in_proj_kernel,
        grid=(bsz, spt),
        in_specs=[
            pl.BlockSpec((tm, d), lambda b, i: (b * spt + i, 0)),
            _resident(w_nat), _resident(w_t), _resident(wq_t), _resident(wkn), _resident(wv_t),
            _resident(gq), _resident(gkv),
            pl.BlockSpec((tm, HEAD_PAD), lambda b, i: (i, 0)),
            pl.BlockSpec((tm, HEAD_PAD), lambda b, i: (i, 0)),
            pl.BlockSpec((half, tm), lambda b, i: (0, i)),
            pl.BlockSpec((half, tm), lambda b, i: (0, i)),
        ],
        out_specs=out_specs,
        out_shape=out_shape,
        compiler_params=_params(("arbitrary", "arbitrary")),
        name="in_proj",
    )(h, w_nat, w_t, wq_t, wkn, wv_t, gq, gkv, cos_k, sin_k, cos_q, sin_q)


BIAS_TILES = 3


def _rel_bias_kernel(rb_ref, o_ref, *, tq, tk):
    c = pl.program_id(0) - 1
    kk = lax.broadcasted_iota(jnp.int32, (tk, tq), 0)
    qq = lax.broadcasted_iota(jnp.int32, (tk, tq), 1)
    n = jnp.maximum(qq - kk - c * tk, 0)
    max_exact = REL_BUCKETS // 2
    nf = jnp.maximum(n, 1).astype(F32)
    large = max_exact + (jnp.log(nf / max_exact) / math.log(REL_MAX_DIST / max_exact)
                         * (REL_BUCKETS - max_exact)).astype(jnp.int32)
    large = jnp.minimum(large, REL_BUCKETS - 1)
    bucket = jnp.where(n < max_exact, n, large)
    for m in range(o_ref.shape[0]):
        acc = jnp.zeros((tk, tq), F32)
        for bkt in range(REL_BUCKETS):
            acc = jnp.where(bucket == bkt, rb_ref[bkt, m], acc)
        o_ref[m] = (acc - rb_ref[REL_BUCKETS - 1, m]) * LOG2E


def _rel_bias_tiles(rel_bias, tq, tk):
    assert tq == 2 * tk and tk + 1 >= REL_MAX_DIST
    nmaps = rel_bias.shape[1]
    return pl.pallas_call(
        functools.partial(_rel_bias_kernel, tq=tq, tk=tk),
        grid=(BIAS_TILES,),
        in_specs=[pl.BlockSpec(memory_space=pltpu.SMEM)],
        out_specs=pl.BlockSpec((nmaps, None, tk, tq), lambda t: (0, t, 0, 0)),
        out_shape=jax.ShapeDtypeStruct((nmaps, BIAS_TILES, tk, tq), F32),
        compiler_params=_params(("arbitrary",)),
        name="rel_bias",
    )(rel_bias)


SCORES_AHEAD = 3
VALUES_BEHIND = 1
ONES_ROWS = 16


def _flash_pipeline(q_fn, k_ref, vt_ref, logits_fn, finalize_fn, s_ref, mt_ref, p_ref, al_ref, m_ref, acc_ref,
                    *, nq, tq, tk):
    tiles = [(i, t, t - 2 * i) for i in range(nq) for t in range(2 * i + 2)]
    ones = jnp.ones((ONES_ROWS, tk), BF16)
    ahead = s_ref.shape[0] - 1
    behind = p_ref.shape[0] - 1

    def lane0(c):
        return tk if c == 1 else 0

    def scores(n):
        i, t, c = tiles[n]
        slot, lo = n % (ahead + 1), lane0(c)
        s = logits_fn(_dot(k_ref[t * tk:(t + 1) * tk, :], q_fn(i, lo)), c, lo)
        s_ref[slot, :, lo:] = s
        mt_ref[slot, :, lo:] = jnp.max(s, axis=0, keepdims=True)

    def softmax(n):
        i, t, c = tiles[n]
        sslot, slot, lo = n % (ahead + 1), n % (behind + 1), lane0(c)
        m_new = mt_ref[sslot, :, lo:]
        if t > 0:
            m_prev = m_ref[:, lo:]
            m_new = jnp.maximum(m_prev, m_new)
            al_ref[slot, :, lo:] = jnp.exp2(m_prev - m_new)
        m_ref[:, lo:] = m_new
        p_ref[slot, :, lo:] = jnp.exp2(s_ref[sslot, :, lo:] - m_new).astype(BF16)

    def values(n):
        i, t, c = tiles[n]
        slot, lo = n % (behind + 1), lane0(c)
        pv = _dot(jnp.concatenate([vt_ref[t], ones], axis=0), p_ref[slot, :, lo:])
        if t > 0:
            pv = al_ref[slot, :, lo:] * acc_ref[:, lo:] + pv
        acc_ref[:, lo:] = pv
        if c == 1:
            finalize_fn(i)

    for n in range(ahead):
        scores(n)
    for n in range(len(tiles) + behind):
        if behind <= n:
            values(n - behind)
        if n + ahead < len(tiles):
            scores(n + ahead)
        if n < len(tiles):
            softmax(n)


def _tri_mask(s):
    kk = lax.broadcasted_iota(jnp.int32, s.shape, 0)
    qq = lax.broadcasted_iota(jnp.int32, s.shape, 1)
    return jnp.where(kk <= qq, s, -jnp.inf)


def _flash_scratch(dv, tq, tk):
    return [pltpu.VMEM((SCORES_AHEAD + 1, tk, tq), F32), pltpu.VMEM((SCORES_AHEAD + 1, 1, tq), F32),
            pltpu.VMEM((VALUES_BEHIND + 1, tk, tq), BF16), pltpu.VMEM((VALUES_BEHIND + 1, 1, tq), F32),
            pltpu.VMEM((1, tq), F32), pltpu.VMEM((dv + ONES_ROWS, tq), F32)]


def _mla_attn_kernel(qt_ref, k_ref, vt_ref, o_ref, *scratch, tq, tk):
    acc_ref = scratch[-1]

    def q_fn(i, lo):
        return qt_ref[:, i * tq + lo:(i + 1) * tq]

    def logits(s, c, lo):
        return _tri_mask(s) if c >= 0 else s

    def finalize(i):
        acc = acc_ref[...]
        o_ref[:, i * tq:(i + 1) * tq] = (acc[:MLA_V] / acc[MLA_V:MLA_V + 1]).astype(BF16)

    _flash_pipeline(q_fn, k_ref, vt_ref, logits, finalize, *scratch, nq=qt_ref.shape[1] // tq, tq=tq, tk=tk)


def _mla_attn(qt, k, vt, tq, tk):
    bsz, nq, s = qt.shape
    nkt = s // tk
    assert tq == 2 * tk and s % tq == 0
    return pl.pallas_call(
        functools.partial(_mla_attn_kernel, tq=tq, tk=tk),
        grid=(MLA_HEADS, bsz),
        in_specs=[
            pl.BlockSpec((None, HEAD_PAD, s), lambda h, b: (b, h, 0)),
            pl.BlockSpec((None, s, HEAD_PAD), lambda h, b: (b, 0, h)),
            pl.BlockSpec((None, nkt, MLA_V, tk), lambda h, b: (b, 0, h, 0)),
        ],
        out_specs=pl.BlockSpec((None, MLA_V, s), lambda h, b: (b, h, 0)),
        out_shape=jax.ShapeDtypeStruct((bsz, MLA_HEADS * MLA_V, s), BF16),
        scratch_shapes=_flash_scratch(MLA_V, tq, tk),
        compiler_params=_params(("arbitrary", "arbitrary")),
        name="mla_attn",
    )(qt, k, vt)


def _diff_attn_kernel(qt_ref, k_ref, vt_ref, bias_ref, lamv_ref, gn_ref, o_ref, om_ref, qm_ref, *scratch,
                      tq, tk):
    a = pl.program_id(2)
    acc_ref = scratch[-1]
    dv = 2 * DIFF_D

    qt = qt_ref[...]
    row = lax.broadcasted_iota(jnp.int32, qt.shape, 0)
    keep = jnp.logical_and(row >= a * DIFF_D, row < (a + 1) * DIFF_D)
    qm_ref[...] = jnp.where(keep, qt, jnp.zeros_like(qt))

    def q_fn(i, lo):
        return qm_ref[:, i * tq + lo:(i + 1) * tq]

    def logits(s, c, lo):
        if c >= -1:
            s = s + bias_ref[c + 1, :, lo:]
        return _tri_mask(s) if c >= 0 else s

    def finalize(i):
        acc = acc_ref[...]
        om_ref[a, :, i * tq:(i + 1) * tq] = acc[:dv] / acc[dv:dv + 1]

    _flash_pipeline(q_fn, k_ref, vt_ref, logits, finalize, *scratch, nq=qt_ref.shape[1] // tq, tq=tq, tk=tk)

    @pl.when(a == 1)
    def _():
        lv = lamv_ref[...]
        lam = (jnp.exp(jnp.sum(lv[0:1] * lv[1:2], axis=1, keepdims=True))
               - jnp.exp(jnp.sum(lv[2:3] * lv[3:4], axis=1, keepdims=True)) + LAM_INIT)
        d = om_ref[0] - lam * om_ref[1]
        d = d * lax.rsqrt(jnp.mean(d * d, axis=0, keepdims=True) + RMS_EPS) * gn_ref[...] * (1.0 - LAM_INIT)
        o_ref[...] = d.astype(BF16)


def _diff_attn(qt, k, vt, bias, lamv, gn, tq, tk):
    bsz, nd, s = qt.shape
    nkt = s // tk
    dv = 2 * DIFF_D
    assert tq == 2 * tk and s % tq == 0
    return pl.pallas_call(
        functools.partial(_diff_attn_kernel, tq=tq, tk=tk),
        grid=(DIFF_HEADS, bsz, 2),
        in_specs=[
            pl.BlockSpec((None, dv, s), lambda h, b, a: (b, h, 0)),
            pl.BlockSpec((None, s, dv), lambda h, b, a: (b, 0, h)),
            pl.BlockSpec((None, nkt, dv, tk), lambda h, b, a: (b, 0, h, 0)),
            pl.BlockSpec((None, BIAS_TILES, tk, tq), lambda h, b, a: (2 * h + a, 0, 0, 0)),
            pl.BlockSpec(lamv.shape, lambda h, b, a: (0, 0)),
            pl.BlockSpec(gn.shape, lambda h, b, a: (0, 0)),
        ],
        out_specs=pl.BlockSpec((None, dv, s), lambda h, b, a: (b, h, 0)),
        out_shape=jax.ShapeDtypeStruct((bsz, nd, s), BF16),
        scratch_shapes=[pltpu.VMEM((2, dv, s), F32), pltpu.VMEM((dv, s), BF16)] + _flash_scratch(dv, tq, tk),
        compiler_params=_params(("arbitrary", "arbitrary", "arbitrary")),
        name="diff_attn",
    )(qt, k, vt, bias, lamv, gn)


def _post_mixer_kernel(h_ref, omt_ref, odt_ref, wum_ref, wud_ref, wg_ref, bg_ref, wo_ref, g_ref, b_ref,
                       o_ref):
    d = h_ref.shape[1]
    for rows in _subtiles(h_ref.shape[0]):
        h = h_ref[rows, :]
        y_m = _dot_tn(omt_ref[:, rows], wum_ref[...])
        y_d = _dot_tn(odt_ref[:, rows], wud_ref[...])
        z = _dot(h.astype(BF16), wg_ref[...]) + bg_ref[...]
        gate = 1.0 / (1.0 + jnp.exp(-z))
        mixed = gate[:, :d] * y_m + gate[:, d:] * y_d
        mix = _dot(mixed.astype(BF16), wo_ref[...])
        o_ref[rows, :] = _layer_norm(ALPHA * h + mix, g_ref[...], b_ref[...])


def _post_mixer(h, omt, odt, bsz, s, w_up_mla, w_up_diff, w_gate, b_gate, w_o, g, b):
    n, d = h.shape
    tm = ROW_TILE * ROW_SUBTILES
    spt = s // tm
    ws = [w_up_mla.astype(BF16), w_up_diff.astype(BF16), w_gate.astype(BF16), b_gate.reshape(1, -1),
          w_o.astype(BF16), g.reshape(1, d), b.reshape(1, d)]

    return pl.pallas_call(
        _post_mixer_kernel,
        grid=(bsz, spt),
        in_specs=[
            pl.BlockSpec((tm, d), lambda bb, i: (bb * spt + i, 0)),
            pl.BlockSpec((None, omt.shape[1], tm), lambda bb, i: (bb, 0, i)),
            pl.BlockSpec((None, odt.shape[1], tm), lambda bb, i: (bb, 0, i)),
        ] + [_resident(a) for a in ws],
        out_specs=pl.BlockSpec((tm, d), lambda bb, i: (bb * spt + i, 0)),
        out_shape=jax.ShapeDtypeStruct((n, d), F32),
        compiler_params=_params(("arbitrary", "arbitrary")),
        name="post_mixer",
    )(h, omt, odt, *ws)


def _mem_kv_kernel(mem_ref, w_ref, k_ref, v_ref):
    d = k_ref.shape[1]
    kv = _dot(mem_ref[...].astype(BF16), w_ref[...])
    k_ref[...] = kv[:, :d].astype(BF16)
    v_ref[...] = kv[:, d:].astype(BF16)


def _mem_kv(mem, w_kv):
    bsz, m, d = mem.shape
    w = w_kv.astype(BF16)
    spec = pl.BlockSpec((None, m, d), lambda b: (b, 0, 0))
    return pl.pallas_call(
        _mem_kv_kernel,
        grid=(bsz,),
        in_specs=[spec, pl.BlockSpec(w.shape, lambda b: (0, 0))],
        out_specs=(spec, spec),
        out_shape=(jax.ShapeDtypeStruct((bsz, m, d), BF16), jax.ShapeDtypeStruct((bsz, m, d), BF16)),
        compiler_params=_params(("arbitrary",)),
        name="mem_kv",
    )(mem, w)


def _mem_attn_kernel(h_ref, k_ref, v_ref, wq_ref, wo_ref, g_ref, b_ref, o_ref):
    d = h_ref.shape[1]
    hd = d // MEM_HEADS
    for rows in _subtiles(h_ref.shape[0]):
        h = h_ref[rows, :]
        q = _dot(h.astype(BF16), wq_ref[...]).astype(BF16)
        outs = []
        for hh in range(MEM_HEADS):
            lo, hi = hh * hd, (hh + 1) * hd
            s = _dot_nt(q[:, lo:hi], k_ref[:, lo:hi]) * (hd ** -0.5)
            p = jnp.exp(s - jnp.max(s, axis=-1, keepdims=True))
            p = p / jnp.sum(p, axis=-1, keepdims=True)
            outs.append(_dot(p.astype(BF16), v_ref[:, lo:hi]).astype(BF16))
        att = _dot(jnp.concatenate(outs, axis=1), wo_ref[...])
        o_ref[rows, :] = _layer_norm(ALPHA * h + att, g_ref[...], b_ref[...])


def _mem_attn(h, km, vm, bsz, s, w_q, w_o, g, b):
    n, d = h.shape
    tm = ROW_TILE * ROW_SUBTILES
    spt = s // tm
    m = km.shape[1]
    ws = [w_q.astype(BF16), w_o.astype(BF16), g.reshape(1, d), b.reshape(1, d)]

    return pl.pallas_call(
        _mem_attn_kernel,
        grid=(bsz, spt),
        in_specs=[
            pl.BlockSpec((tm, d), lambda bb, i: (bb * spt + i, 0)),
            pl.BlockSpec((None, m, d), lambda bb, i: (bb, 0, 0)),
            pl.BlockSpec((None, m, d), lambda bb, i: (bb, 0, 0)),
        ] + [_resident(a) for a in ws],
        out_specs=pl.BlockSpec((tm, d), lambda bb, i: (bb * spt + i, 0)),
        out_shape=jax.ShapeDtypeStruct((n, d), F32),
        compiler_params=_params(("arbitrary", "arbitrary")),
        name="mem_attn",
    )(h, km, vm, *ws)


def kernel(x, mem, rel_bias, ffn1_w1, ffn1_w3, ffn1_w2, ln1_g, ln1_b, w_in, q_norm_g, kv_norm_g, w_q_b, w_kv_b, lam_q1, lam_k1, lam_q2, lam_k2, diff_norm_g, w_gate, b_gate, w_up_mla, w_up_diff, w_o, ln2_g, ln2_b, mem_w_q, mem_w_kv, mem_w_o, ln3_g, ln3_b, ffn2_w1, ffn2_w3, ffn2_w2, ln4_g, ln4_b):
    bsz, s, d = x.shape
    assert ffn1_w1.shape[0] == DEPTH
    l = 0
    h = x.reshape(bsz * s, d)

    h = _ffn_ln(h, ffn1_w1[l], ffn1_w3[l], ffn1_w2[l], ln1_g[l], ln1_b[l])

    qtm, km, vtm, qtd, kd, vtd = _in_proj(h, bsz, s, w_in[l], q_norm_g[l], kv_norm_g[l], w_q_b[l], w_kv_b[l])
    bias = _rel_bias_tiles(rel_bias, ATTN_TQ, ATTN_TK)
    omt = _mla_attn(qtm, km, vtm, ATTN_TQ, ATTN_TK)
    lamv = jnp.stack([lam_q1[l], lam_k1[l], lam_q2[l], lam_k2[l]], axis=0)
    odt = _diff_attn(qtd, kd, vtd, bias, lamv, diff_norm_g[l].reshape(-1, 1), ATTN_TQ, ATTN_TK)
    h = _post_mixer(h, omt, odt, bsz, s, w_up_mla[l], w_up_diff[l], w_gate[l], b_gate[l], w_o[l],
                    ln2_g[l], ln2_b[l])

    kmem, vmem = _mem_kv(mem, mem_w_kv[l])
    h = _mem_attn(h, kmem, vmem, bsz, s, mem_w_q[l], mem_w_o[l], ln3_g[l], ln3_b[l])

    h = _ffn_ln(h, ffn2_w1[l], ffn2_w3[l], ffn2_w2[l], ln4_g[l], ln4_b[l])
    return h.reshape(bsz, s, d)
```

```python
import functools
import math

import jax
import jax.numpy as jnp
from jax import lax
from jax.experimental import pallas as pl
from jax.experimental.pallas import tpu as pltpu

F32 = jnp.float32
BF16 = jnp.bfloat16

DEPTH = 1
MLA_HEADS = 8
MLA_Q_RANK = 384
MLA_KV_RANK = 256
MLA_NOPE = 64
MLA_ROPE = 32
MLA_V = 64
ROPE_BASE = 10000.0
DIFF_HEADS = 4
DIFF_D = 64
REL_BUCKETS = 32
REL_MAX_DIST = 128
MEM_HEADS = 4
ALPHA = (2.0 * DEPTH) ** 0.25
LN_EPS = 1e-5
RMS_EPS = 1e-6
LAM_INIT = 0.8 - 0.6 * math.exp(-0.3 * 0)
LOG2E = math.log2(math.e)

HEAD_PAD = 128
VMEM_LIMIT_BYTES = 56 * 1024 * 1024

ROW_TILE = 1024
CHAIN_ROWS = 256
MEM_CHAIN_ROWS = 512
ATTN_TQ = 512
ATTN_TK = 256


def _params(semantics):
    return pltpu.CompilerParams(dimension_semantics=semantics, vmem_limit_bytes=VMEM_LIMIT_BYTES)


def _dot(a, b):
    return jnp.dot(a, b, preferred_element_type=F32)


def _dot_nt(a, b):
    return lax.dot_general(a, b, (((1,), (1,)), ((), ())), preferred_element_type=F32)


def _dot_tn(a, b):
    return lax.dot_general(a, b, (((0,), (0,)), ((), ())), preferred_element_type=F32)


def _layer_norm(y, g, b):
    mu = jnp.mean(y, axis=-1, keepdims=True)
    d = y - mu
    var = jnp.mean(d * d, axis=-1, keepdims=True)
    return d * lax.rsqrt(var + LN_EPS) * g + b


def _rms_norm_rows(x, g):
    return x * lax.rsqrt(jnp.mean(x * x, axis=-1, keepdims=True) + RMS_EPS) * g


def _chains(rows, chain=CHAIN_ROWS):
    return [slice(r, r + chain) for r in range(0, rows, chain)]


def _resident(a):
    return pl.BlockSpec(a.shape, lambda *_: (0,) * a.ndim, pipeline_mode=pl.Buffered(1))


def _ffn_ln_kernel(x_ref, w1_ref, w3_ref, w2_ref, g_ref, b_ref, o_ref):
    for rows in _chains(x_ref.shape[0]):
        x = x_ref[rows, :]
        xb = x.astype(BF16)
        u = _dot(xb, w1_ref[...])
        v = _dot(xb, w3_ref[...])
        act = (u / (1.0 + jnp.exp(-u))) * v
        y = ALPHA * x + 0.5 * _dot(act.astype(BF16), w2_ref[...])
        o_ref[rows, :] = _layer_norm(y, g_ref[...], b_ref[...])


def _ffn_ln(x, w1, w3, w2, g, b):
    n, d = x.shape
    tm = ROW_TILE
    assert n % tm == 0
    ws = [w1.astype(BF16), w3.astype(BF16), w2.astype(BF16), g.reshape(1, d), b.reshape(1, d)]
    return pl.pallas_call(
        _ffn_ln_kernel,
        grid=(n // tm,),
        in_specs=[pl.BlockSpec((tm, d), lambda i: (i, 0))] + [_resident(a) for a in ws],
        out_specs=pl.BlockSpec((tm, d), lambda i: (i, 0)),
        out_shape=jax.ShapeDtypeStruct((n, d), F32),
        compiler_params=_params(("arbitrary",)),
        name="ffn_ln",
    )(x, *ws)


def _in_proj_kernel(h_ref, wnat_ref, wt_ref, wq_ref, wkn_ref, wvt_ref, gq_ref, gkv_ref,
                    cosk_ref, sink_ref, cosq_ref, sinq_ref,
                    qtm_ref, km_ref, vtm_ref, qtd_ref, kd_ref, vtd_ref):
    tk = vtm_ref.shape[-1]
    half = MLA_ROPE // 2
    nqd = qtd_ref.shape[0]
    o1 = MLA_Q_RANK
    o2 = o1 + MLA_KV_RANK
    o3 = o2 + HEAD_PAD
    q_scale = (MLA_NOPE + MLA_ROPE) ** -0.5 * LOG2E
    for rows in _chains(h_ref.shape[0]):
        key_tiles = range(rows.start // tk, rows.stop // tk)
        hb = h_ref[rows, :].astype(BF16)

        pn = _dot(hb, wnat_ref[...])
        c_q, c_kv, krg, kd = pn[:, :o1], pn[:, o1:o2], pn[:, o2:o3], pn[:, o3:]
        kd_ref[rows, :] = kd.astype(BF16)

        pt = _dot_nt(wt_ref[...], hb)
        qtd_ref[:, rows] = (pt[:nqd] * (DIFF_D ** -0.5 * LOG2E)).astype(BF16)
        for c, t in enumerate(key_tiles):
            vtd_ref[t] = pt[nqd:, c * tk:(c + 1) * tk].astype(BF16)

        cqn = _rms_norm_rows(c_q, gq_ref[...]).astype(BF16)
        qt = _dot_nt(wq_ref[...], cqn)
        cos_q, sin_q = cosq_ref[:, rows], sinq_ref[:, rows]
        for hh in range(MLA_HEADS):
            lo = hh * HEAD_PAD
            r0 = lo + MLA_NOPE
            r1, r2 = r0 + half, r0 + MLA_ROPE
            x1, x2 = qt[r0:r1], qt[r1:r2]
            qtm_ref[lo:r0, rows] = (qt[lo:r0] * q_scale).astype(BF16)
            qtm_ref[r0:r1, rows] = ((x1 * cos_q - x2 * sin_q) * q_scale).astype(BF16)
            qtm_ref[r1:r2, rows] = ((x2 * cos_q + x1 * sin_q) * q_scale).astype(BF16)
            qtm_ref[r2:lo + HEAD_PAD, rows] = jnp.zeros((lo + HEAD_PAD - r2, CHAIN_ROWS), BF16)

        ckvn = _rms_norm_rows(c_kv, gkv_ref[...]).astype(BF16)
        kn = _dot(ckvn, wkn_ref[...])
        kr = krg * cosk_ref[rows, :] + pltpu.roll(krg, HEAD_PAD - MLA_ROPE, 1) * sink_ref[rows, :]
        for hh in range(MLA_HEADS):
            lo, hi = hh * HEAD_PAD, (hh + 1) * HEAD_PAD
            km_ref[rows, lo:hi] = (kn[:, lo:hi] + kr).astype(BF16)
        vt = _dot_nt(wvt_ref[...], ckvn)
        for c, t in enumerate(key_tiles):
            vtm_ref[t] = vt[:, c * tk:(c + 1) * tk].astype(BF16)


def _in_proj_weights(w_in, w_q_b, w_kv_b):
    d = w_in.shape[0]
    o_cq = MLA_Q_RANK
    o_ckv = o_cq + MLA_KV_RANK
    o_kr = o_ckv + MLA_ROPE
    n_d = 2 * DIFF_HEADS * DIFF_D
    o_qd = o_kr + n_d
    o_kd = o_qd + n_d
    half = MLA_ROPE // 2
    pad = HEAD_PAD - MLA_NOPE - MLA_ROPE
    assert pad == MLA_ROPE
    kr = w_in[:, o_ckv:o_kr]
    krg = jnp.concatenate([jnp.zeros((d, MLA_NOPE), F32), kr, -kr[:, half:], kr[:, :half]], axis=1)
    w_nat = jnp.concatenate([w_in[:, :o_ckv], krg, w_in[:, o_qd:o_kd]], axis=1).astype(BF16)
    w_t = jnp.concatenate([w_in[:, o_kr:o_qd], w_in[:, o_kd:]], axis=1).T.astype(BF16)

    r = w_q_b.shape[0]
    wq = w_q_b.reshape(r, MLA_HEADS, MLA_NOPE + MLA_ROPE)
    wq = jnp.concatenate([wq, jnp.zeros((r, MLA_HEADS, pad), F32)], axis=-1)
    wq_t = wq.reshape(r, MLA_HEADS * HEAD_PAD).T.astype(BF16)

    rk = w_kv_b.shape[0]
    wkv = w_kv_b.reshape(rk, MLA_HEADS, MLA_NOPE + MLA_V)
    wkn = jnp.concatenate([wkv[..., :MLA_NOPE], jnp.zeros((rk, MLA_HEADS, HEAD_PAD - MLA_NOPE), F32)],
                          axis=-1).reshape(rk, MLA_HEADS * HEAD_PAD).astype(BF16)
    wv_t = wkv[..., MLA_NOPE:].reshape(rk, MLA_HEADS * MLA_V).T.astype(BF16)
    return w_nat, w_t, wq_t, wkn, wv_t


def _rope_tables(s):
    inv_freq = ROPE_BASE ** (-jnp.arange(0, MLA_ROPE, 2, dtype=F32) / MLA_ROPE)
    ang = jnp.arange(s).astype(F32)[:, None] * inv_freq[None, :]
    cos, sin = jnp.cos(ang), jnp.sin(ang)
    z_lo = jnp.zeros((s, MLA_NOPE), F32)
    z_hi = jnp.zeros((s, HEAD_PAD - MLA_NOPE - MLA_ROPE), F32)
    cos_k = jnp.concatenate([z_lo, cos, cos, z_hi], axis=1)
    sin_k = jnp.concatenate([z_lo, sin, sin, z_hi], axis=1)
    return cos_k, sin_k, cos.T, sin.T


def _in_proj(h, bsz, s, w_in, q_norm_g, kv_norm_g, w_q_b, w_kv_b):
    n, d = h.shape
    tm, tk = ROW_TILE, ATTN_TK
    assert s % tm == 0 and CHAIN_ROWS % tk == 0
    w_nat, w_t, wq_t, wkn, wv_t = _in_proj_weights(w_in, w_q_b, w_kv_b)
    cos_k, sin_k, cos_q, sin_q = _rope_tables(s)
    half = MLA_ROPE // 2
    spt = s // tm
    nqm = MLA_HEADS * HEAD_PAD
    nvm = MLA_HEADS * MLA_V
    nd = 2 * DIFF_HEADS * DIFF_D

    gq = q_norm_g.reshape(1, -1)
    gkv = kv_norm_g.reshape(1, -1)
    out_shape = (
        jax.ShapeDtypeStruct((bsz, nqm, s), BF16),
        jax.ShapeDtypeStruct((bsz, s, nqm), BF16),
        jax.ShapeDtypeStruct((bsz, s // tk, nvm, tk), BF16),
        jax.ShapeDtypeStruct((bsz, nd, s), BF16),
        jax.ShapeDtypeStruct((bsz, s, nd), BF16),
        jax.ShapeDtypeStruct((bsz, s // tk, nd, tk), BF16),
    )
    out_specs = (
        pl.BlockSpec((None, nqm, tm), lambda b, i: (b, 0, i)),
        pl.BlockSpec((None, tm, nqm), lambda b, i: (b, i, 0)),
        pl.BlockSpec((None, tm // tk, nvm, tk), lambda b, i: (b, i, 0, 0)),
        pl.BlockSpec((None, nd, tm), lambda b, i: (b, 0, i)),
        pl.BlockSpec((None, tm, nd), lambda b, i: (b, i, 0)),
        pl.BlockSpec((None, tm // tk, nd, tk), lambda b, i: (b, i, 0, 0)),
    )
    return pl.pallas_call(
        _in_proj_kernel,
        grid=(bsz, spt),
        in_specs=[
            pl.BlockSpec((tm, d), lambda b, i: (b * spt + i, 0)),
            _resident(w_nat), _resident(w_t), _resident(wq_t), _resident(wkn), _resident(wv_t),
            _resident(gq), _resident(gkv),
            pl.BlockSpec((tm, HEAD_PAD), lambda b, i: (i, 0)),
            pl.BlockSpec((tm, HEAD_PAD), lambda b, i: (i, 0)),
            pl.BlockSpec((half, tm), lambda b, i: (0, i)),
            pl.BlockSpec((half, tm), lambda b, i: (0, i)),
        ],
        out_specs=out_specs,
        out_shape=out_shape,
        compiler_params=_params(("arbitrary", "arbitrary")),
        name="in_proj",
    )(h, w_nat, w_t, wq_t, wkn, wv_t, gq, gkv, cos_k, sin_k, cos_q, sin_q)


BIAS_TILES = 3


def _rel_bias_kernel(rb_ref, o_ref, *, tq, tk):
    c = pl.program_id(0) - 1
    kk = lax.broadcasted_iota(jnp.int32, (tk, tq), 0)
    qq = lax.broadcasted_iota(jnp.int32, (tk, tq), 1)
    n = jnp.maximum(qq - kk - c * tk, 0)
    max_exact = REL_BUCKETS // 2
    nf = jnp.maximum(n, 1).astype(F32)
    large = max_exact + (jnp.log(nf / max_exact) / math.log(REL_MAX_DIST / max_exact)
                         * (REL_BUCKETS - max_exact)).astype(jnp.int32)
    large = jnp.minimum(large, REL_BUCKETS - 1)
    bucket = jnp.where(n < max_exact, n, large)
    for m in range(o_ref.shape[0]):
        acc = jnp.zeros((tk, tq), F32)
        for bkt in range(REL_BUCKETS):
            acc = jnp.where(bucket == bkt, rb_ref[bkt, m], acc)
        o_ref[m] = (acc - rb_ref[REL_BUCKETS - 1, m]) * LOG2E


def _rel_bias_tiles(rel_bias, tq, tk):
    assert tq == 2 * tk and tk + 1 >= REL_MAX_DIST
    nmaps = rel_bias.shape[1]
    return pl.pallas_call(
        functools.partial(_rel_bias_kernel, tq=tq, tk=tk),
        grid=(BIAS_TILES,),
        in_specs=[pl.BlockSpec(memory_space=pltpu.SMEM)],
        out_specs=pl.BlockSpec((nmaps, None, tk, tq), lambda t: (0, t, 0, 0)),
        out_shape=jax.ShapeDtypeStruct((nmaps, BIAS_TILES, tk, tq), F32),
        compiler_params=_params(("arbitrary",)),
        name="rel_bias",
    )(rel_bias)


SCORES_AHEAD = 3
VALUES_BEHIND = 1
ONES_ROWS = 16


def _flash_pipeline(q_fn, k_ref, vt_ref, logits_fn, finalize_fn, s_ref, mt_ref, p_ref, al_ref, m_ref, acc_ref,
                    *, nq, tq, tk):
    tiles = [(i, t, t - 2 * i) for i in range(nq) for t in range(2 * i + 2)]
    ones = jnp.ones((ONES_ROWS, tk), BF16)
    ahead = s_ref.shape[0] - 1
    behind = p_ref.shape[0] - 1

    def lane0(c):
        return tk if c == 1 else 0

    def scores(n):
        i, t, c = tiles[n]
        slot, lo = n % (ahead + 1), lane0(c)
        s = logits_fn(_dot(k_ref[t * tk:(t + 1) * tk, :], q_fn(i, lo)), c, lo)
        s_ref[slot, :, lo:] = s
        mt_ref[slot, :, lo:] = jnp.max(s, axis=0, keepdims=True)

    def softmax(n):
        i, t, c = tiles[n]
        sslot, slot, lo = n % (ahead + 1), n % (behind + 1), lane0(c)
        m_new = mt_ref[sslot, :, lo:]
        if t > 0:
            m_prev = m_ref[:, lo:]
            m_new = jnp.maximum(m_prev, m_new)
            al_ref[slot, :, lo:] = jnp.exp2(m_prev - m_new)
        m_ref[:, lo:] = m_new
        p_ref[slot, :, lo:] = jnp.exp2(s_ref[sslot, :, lo:] - m_new).astype(BF16)

    def values(n):
        i, t, c = tiles[n]
        slot, lo = n % (behind + 1), lane0(c)
        pv = _dot(jnp.concatenate([vt_ref[t], ones], axis=0), p_ref[slot, :, lo:])
        if t > 0:
            pv = al_ref[slot, :, lo:] * acc_ref[:, lo:] + pv
        acc_ref[:, lo:] = pv
        if c == 1:
            finalize_fn(i)

    for n in range(ahead):
        scores(n)
    for n in range(len(tiles) + behind):
        if behind <= n:
            values(n - behind)
        if n + ahead < len(tiles):
            scores(n + ahead)
        if n < len(tiles):
            softmax(n)


def _tri_mask(s):
    kk = lax.broadcasted_iota(jnp.int32, s.shape, 0)
    qq = lax.broadcasted_iota(jnp.int32, s.shape, 1)
    return jnp.where(kk <= qq, s, -jnp.inf)


def _flash_scratch(dv, tq, tk):
    return [pltpu.VMEM((SCORES_AHEAD + 1, tk, tq), F32), pltpu.VMEM((SCORES_AHEAD + 1, 1, tq), F32),
            pltpu.VMEM((VALUES_BEHIND + 1, tk, tq), BF16), pltpu.VMEM((VALUES_BEHIND + 1, 1, tq), F32),
            pltpu.VMEM((1, tq), F32), pltpu.VMEM((dv + ONES_ROWS, tq), F32)]


def _mla_attn_kernel(qt_ref, k_ref, vt_ref, o_ref, *scratch, tq, tk):
    acc_ref = scratch[-1]

    def q_fn(i, lo):
        return qt_ref[:, i * tq + lo:(i + 1) * tq]

    def logits(s, c, lo):
        return _tri_mask(s) if c >= 0 else s

    def finalize(i):
        acc = acc_ref[...]
        o_ref[:, i * tq:(i + 1) * tq] = (acc[:MLA_V] / acc[MLA_V:MLA_V + 1]).astype(BF16)

    _flash_pipeline(q_fn, k_ref, vt_ref, logits, finalize, *scratch, nq=qt_ref.shape[1] // tq, tq=tq, tk=tk)


def _mla_attn(qt, k, vt, tq, tk):
    bsz, nq, s = qt.shape
    nkt = s // tk
    assert tq == 2 * tk and s % tq == 0
    return pl.pallas_call(
        functools.partial(_mla_attn_kernel, tq=tq, tk=tk),
        grid=(MLA_HEADS, bsz),
        in_specs=[
            pl.BlockSpec((None, HEAD_PAD, s), lambda h, b: (b, h, 0)),
            pl.BlockSpec((None, s, HEAD_PAD), lambda h, b: (b, 0, h)),
            pl.BlockSpec((None, nkt, MLA_V, tk), lambda h, b: (b, 0, h, 0)),
        ],
        out_specs=pl.BlockSpec((None, MLA_V, s), lambda h, b: (b, h, 0)),
        out_shape=jax.ShapeDtypeStruct((bsz, MLA_HEADS * MLA_V, s), BF16),
        scratch_shapes=_flash_scratch(MLA_V, tq, tk),
        compiler_params=_params(("arbitrary", "arbitrary")),
        name="mla_attn",
    )(qt, k, vt)


def _diff_attn_kernel(qt_ref, k_ref, vt_ref, bias_ref, lamv_ref, gn_ref, o_ref, om_ref, qm_ref, *scratch,
                      tq, tk):
    a = pl.program_id(2)
    acc_ref = scratch[-1]
    dv = 2 * DIFF_D

    qt = qt_ref[...]
    row = lax.broadcasted_iota(jnp.int32, qt.shape, 0)
    keep = jnp.logical_and(row >= a * DIFF_D, row < (a + 1) * DIFF_D)
    qm_ref[...] = jnp.where(keep, qt, jnp.zeros_like(qt))

    def q_fn(i, lo):
        return qm_ref[:, i * tq + lo:(i + 1) * tq]

    def logits(s, c, lo):
        if c >= -1:
            s = s + bias_ref[c + 1, :, lo:]
        return _tri_mask(s) if c >= 0 else s

    def finalize(i):
        acc = acc_ref[...]
        om_ref[a, :, i * tq:(i + 1) * tq] = acc[:dv] / acc[dv:dv + 1]

    _flash_pipeline(q_fn, k_ref, vt_ref, logits, finalize, *scratch, nq=qt_ref.shape[1] // tq, tq=tq, tk=tk)

    @pl.when(a == 1)
    def _():
        lv = lamv_ref[...]
        lam = (jnp.exp(jnp.sum(lv[0:1] * lv[1:2], axis=1, keepdims=True))
               - jnp.exp(jnp.sum(lv[2:3] * lv[3:4], axis=1, keepdims=True)) + LAM_INIT)
        d = om_ref[0] - lam * om_ref[1]
        d = d * lax.rsqrt(jnp.mean(d * d, axis=0, keepdims=True) + RMS_EPS) * gn_ref[...] * (1.0 - LAM_INIT)
        o_ref[...] = d.astype(BF16)


def _diff_attn(qt, k, vt, bias, lamv, gn, tq, tk):
    bsz, nd, s = qt.shape
    nkt = s // tk
    dv = 2 * DIFF_D
    assert tq == 2 * tk and s % tq == 0
    return pl.pallas_call(
        functools.partial(_diff_attn_kernel, tq=tq, tk=tk),
        grid=(DIFF_HEADS, bsz, 2),
        in_specs=[
            pl.BlockSpec((None, dv, s), lambda h, b, a: (b, h, 0)),
            pl.BlockSpec((None, s, dv), lambda h, b, a: (b, 0, h)),
            pl.BlockSpec((None, nkt, dv, tk), lambda h, b, a: (b, 0, h, 0)),
            pl.BlockSpec((None, BIAS_TILES, tk, tq), lambda h, b, a: (2 * h + a, 0, 0, 0)),
            pl.BlockSpec(lamv.shape, lambda h, b, a: (0, 0)),
            pl.BlockSpec(gn.shape, lambda h, b, a: (0, 0)),
        ],
        out_specs=pl.BlockSpec((None, dv, s), lambda h, b, a: (b, h, 0)),
        out_shape=jax.ShapeDtypeStruct((bsz, nd, s), BF16),
        scratch_shapes=[pltpu.VMEM((2, dv, s), F32), pltpu.VMEM((dv, s), BF16)] + _flash_scratch(dv, tq, tk),
        compiler_params=_params(("arbitrary", "arbitrary", "arbitrary")),
        name="diff_attn",
    )(qt, k, vt, bias, lamv, gn)


def _post_mixer_kernel(h_ref, omt_ref, odt_ref, wum_ref, wud_ref, wg_ref, bg_ref, wo_ref, g_ref, b_ref,
                       o_ref):
    d = h_ref.shape[1]
    for rows in _chains(h_ref.shape[0]):
        h = h_ref[rows, :]
        y_m = _dot_tn(omt_ref[:, rows], wum_ref[...])
        y_d = _dot_tn(odt_ref[:, rows], wud_ref[...])
        z = _dot(h.astype(BF16), wg_ref[...]) + bg_ref[...]
        gate = 1.0 / (1.0 + jnp.exp(-z))
        mixed = gate[:, :d] * y_m + gate[:, d:] * y_d
        mix = _dot(mixed.astype(BF16), wo_ref[...])
        o_ref[rows, :] = _layer_norm(ALPHA * h + mix, g_ref[...], b_ref[...])


def _post_mixer(h, omt, odt, bsz, s, w_up_mla, w_up_diff, w_gate, b_gate, w_o, g, b):
    n, d = h.shape
    tm = ROW_TILE
    spt = s // tm
    ws = [w_up_mla.astype(BF16), w_up_diff.astype(BF16), w_gate.astype(BF16), b_gate.reshape(1, -1),
          w_o.astype(BF16), g.reshape(1, d), b.reshape(1, d)]

    return pl.pallas_call(
        _post_mixer_kernel,
        grid=(bsz, spt),
        in_specs=[
            pl.BlockSpec((tm, d), lambda bb, i: (bb * spt + i, 0)),
            pl.BlockSpec((None, omt.shape[1], tm), lambda bb, i: (bb, 0, i)),
            pl.BlockSpec((None, odt.shape[1], tm), lambda bb, i: (bb, 0, i)),
        ] + [_resident(a) for a in ws],
        out_specs=pl.BlockSpec((tm, d), lambda bb, i: (bb * spt + i, 0)),
        out_shape=jax.ShapeDtypeStruct((n, d), F32),
        compiler_params=_params(("arbitrary", "arbitrary")),
        name="post_mixer",
    )(h, omt, odt, *ws)


def _mem_kv_kernel(mem_ref, w_ref, k_ref, v_ref):
    d = k_ref.shape[1]
    kv = _dot(mem_ref[...].astype(BF16), w_ref[...])
    k_ref[...] = kv[:, :d].astype(BF16)
    v_ref[...] = kv[:, d:].astype(BF16)


def _mem_kv(mem, w_kv):
    bsz, m, d = mem.shape
    w = w_kv.astype(BF16)
    spec = pl.BlockSpec((None, m, d), lambda b: (b, 0, 0))
    return pl.pallas_call(
        _mem_kv_kernel,
        grid=(bsz,),
        in_specs=[spec, pl.BlockSpec(w.shape, lambda b: (0, 0))],
        out_specs=(spec, spec),
        out_shape=(jax.ShapeDtypeStruct((bsz, m, d), BF16), jax.ShapeDtypeStruct((bsz, m, d), BF16)),
        compiler_params=_params(("arbitrary",)),
        name="mem_kv",
    )(mem, w)


def _mem_attn_kernel(h_ref, k_ref, v_ref, wq_ref, wo_ref, g_ref, b_ref, o_ref):
    d = h_ref.shape[1]
    hd = d // MEM_HEADS
    for rows in _chains(h_ref.shape[0], MEM_CHAIN_ROWS):
        h = h_ref[rows, :]
        q = _dot(h.astype(BF16), wq_ref[...]).astype(BF16)
        outs = []
        for hh in range(MEM_HEADS):
            lo, hi = hh * hd, (hh + 1) * hd
            s = _dot_nt(q[:, lo:hi], k_ref[:, lo:hi]) * (hd ** -0.5)
            p = jnp.exp(s - jnp.max(s, axis=-1, keepdims=True))
            p = p / jnp.sum(p, axis=-1, keepdims=True)
            outs.append(_dot(p.astype(BF16), v_ref[:, lo:hi]).astype(BF16))
        att = _dot(jnp.concatenate(outs, axis=1), wo_ref[...])
        o_ref[rows, :] = _layer_norm(ALPHA * h + att, g_ref[...], b_ref[...])


def _mem_attn(h, km, vm, bsz, s, w_q, w_o, g, b):
    n, d = h.shape
    tm = ROW_TILE
    spt = s // tm
    m = km.shape[1]
    ws = [w_q.astype(BF16), w_o.astype(BF16), g.reshape(1, d), b.reshape(1, d)]

    return pl.pallas_call(
        _mem_attn_kernel,
        grid=(bsz, spt),
        in_specs=[
            pl.BlockSpec((tm, d), lambda bb, i: (bb * spt + i, 0)),
            pl.BlockSpec((None, m, d), lambda bb, i: (bb, 0, 0)),
            pl.BlockSpec((None, m, d), lambda bb, i: (bb, 0, 0)),
        ] + [_resident(a) for a in ws],
        out_specs=pl.BlockSpec((tm, d), lambda bb, i: (bb * spt + i, 0)),
        out_shape=jax.ShapeDtypeStruct((n, d), F32),
        compiler_params=_params(("arbitrary", "arbitrary")),
        name="mem_attn",
    )(h, km, vm, *ws)


def kernel(x, mem, rel_bias, ffn1_w1, ffn1_w3, ffn1_w2, ln1_g, ln1_b, w_in, q_norm_g, kv_norm_g, w_q_b, w_kv_b, lam_q1, lam_k1, lam_q2, lam_k2, diff_norm_g, w_gate, b_gate, w_up_mla, w_up_diff, w_o, ln2_g, ln2_b, mem_w_q, mem_w_kv, mem_w_o, ln3_g, ln3_b, ffn2_w1, ffn2_w3, ffn2_w2, ln4_g, ln4_b):
    bsz, s, d = x.shape
    assert ffn1_w1.shape[0] == DEPTH
    l = 0
    h = x.reshape(bsz * s, d)

    h = _ffn_ln(h, ffn1_w1[l], ffn1_w3[l], ffn1_w2[l], ln1_g[l], ln1_b[l])

    qtm, km, vtm, qtd, kd, vtd = _in_proj(h, bsz, s, w_in[l], q_norm_g[l], kv_norm_g[l], w_q_b[l], w_kv_b[l])
    bias = _rel_bias_tiles(rel_bias, ATTN_TQ, ATTN_TK)
    omt = _mla_attn(qtm, km, vtm, ATTN_TQ, ATTN_TK)
    lamv = jnp.stack([lam_q1[l], lam_k1[l], lam_q2[l], lam_k2[l]], axis=0)
    odt = _diff_attn(qtd, kd, vtd, bias, lamv, diff_norm_g[l].reshape(-1, 1), ATTN_TQ, ATTN_TK)
    h = _post_mixer(h, omt, odt, bsz, s, w_up_mla[l], w_up_diff[l], w_gate[l], b_gate[l], w_o[l],
                    ln2_g[l], ln2_b[l])

    kmem, vmem = _mem_kv(mem, mem_w_kv[l])
    h = _mem_attn(h, kmem, vmem, bsz, s, mem_w_q[l], mem_w_o[l], ln3_g[l], ln3_b[l])

    h = _ffn_ln(h, ffn2_w1[l], ffn2_w3[l], ffn2_w2[l], ln4_g[l], ln4_b[l])
    return h.reshape(bsz, s, d)
```

```python
import functools
import math

import jax
import jax.numpy as jnp
from jax import lax
from jax.experimental import pallas as pl
from jax.experimental.pallas import tpu as pltpu

F32 = jnp.float32
BF16 = jnp.bfloat16

DEPTH = 1
MLA_HEADS = 8
MLA_Q_RANK = 384
MLA_KV_RANK = 256
MLA_NOPE = 64
MLA_ROPE = 32
MLA_V = 64
ROPE_BASE = 10000.0
DIFF_HEADS = 4
DIFF_D = 64
REL_BUCKETS = 32
REL_MAX_DIST = 128
MEM_HEADS = 4
ALPHA = (2.0 * DEPTH) ** 0.25
LN_EPS = 1e-5
RMS_EPS = 1e-6
LAM_INIT = 0.8 - 0.6 * math.exp(-0.3 * 0)
LOG2E = math.log2(math.e)

HEAD_PAD = 128
VMEM_LIMIT_BYTES = 56 * 1024 * 1024

ROW_TILE = 1024
CHAIN_ROWS = 256
MEM_CHAIN_ROWS = 512
ATTN_TQ = 512
ATTN_TK = 256
MLA_HEADS_PER_STEP = 4


def _params(semantics):
    return pltpu.CompilerParams(dimension_semantics=semantics, vmem_limit_bytes=VMEM_LIMIT_BYTES)


def _dot(a, b):
    return jnp.dot(a, b, preferred_element_type=F32)


def _dot_nt(a, b):
    return lax.dot_general(a, b, (((1,), (1,)), ((), ())), preferred_element_type=F32)


def _dot_tn(a, b):
    return lax.dot_general(a, b, (((0,), (0,)), ((), ())), preferred_element_type=F32)


def _layer_norm(y, g, b):
    mu = jnp.mean(y, axis=-1, keepdims=True)
    d = y - mu
    var = jnp.mean(d * d, axis=-1, keepdims=True)
    return d * lax.rsqrt(var + LN_EPS) * g + b


def _rms_norm_rows(x, g):
    return x * lax.rsqrt(jnp.mean(x * x, axis=-1, keepdims=True) + RMS_EPS) * g


def _chains(rows, chain=CHAIN_ROWS):
    return [slice(r, r + chain) for r in range(0, rows, chain)]


def _resident(a):
    return pl.BlockSpec(a.shape, lambda *_: (0,) * a.ndim, pipeline_mode=pl.Buffered(1))


def _ffn_ln_kernel(x_ref, w1_ref, w3_ref, w2_ref, g_ref, b_ref, o_ref):
    for rows in _chains(x_ref.shape[0]):
        x = x_ref[rows, :]
        xb = x.astype(BF16)
        u = _dot(xb, w1_ref[...])
        v = _dot(xb, w3_ref[...])
        act = (u / (1.0 + jnp.exp(-u))) * v
        y = ALPHA * x + 0.5 * _dot(act.astype(BF16), w2_ref[...])
        o_ref[rows, :] = _layer_norm(y, g_ref[...], b_ref[...])


def _ffn_ln(x, w1, w3, w2, g, b):
    n, d = x.shape
    tm = ROW_TILE
    assert n % tm == 0
    ws = [w1.astype(BF16), w3.astype(BF16), w2.astype(BF16), g.reshape(1, d), b.reshape(1, d)]
    return pl.pallas_call(
        _ffn_ln_kernel,
        grid=(n // tm,),
        in_specs=[pl.BlockSpec((tm, d), lambda i: (i, 0))] + [_resident(a) for a in ws],
        out_specs=pl.BlockSpec((tm, d), lambda i: (i, 0)),
        out_shape=jax.ShapeDtypeStruct((n, d), F32),
        compiler_params=_params(("arbitrary",)),
        name="ffn_ln",
    )(x, *ws)


def _in_proj_kernel(h_ref, wnat_ref, wt_ref, wq_ref, wkn_ref, wvt_ref, gq_ref, gkv_ref,
                    cosk_ref, sink_ref, cosq_ref, sinq_ref,
                    qtm_ref, km_ref, vtm_ref, qtd_ref, kd_ref, vtd_ref):
    tk = vtm_ref.shape[-1]
    half = MLA_ROPE // 2
    nqd = qtd_ref.shape[0]
    o1 = MLA_Q_RANK
    o2 = o1 + MLA_KV_RANK
    o3 = o2 + HEAD_PAD
    q_scale = (MLA_NOPE + MLA_ROPE) ** -0.5 * LOG2E
    for rows in _chains(h_ref.shape[0]):
        key_tiles = range(rows.start // tk, rows.stop // tk)
        hb = h_ref[rows, :].astype(BF16)

        pn = _dot(hb, wnat_ref[...])
        c_q, c_kv, krg, kd = pn[:, :o1], pn[:, o1:o2], pn[:, o2:o3], pn[:, o3:]
        kd_ref[rows, :] = kd.astype(BF16)

        pt = _dot_nt(wt_ref[...], hb)
        qtd_ref[:, rows] = (pt[:nqd] * (DIFF_D ** -0.5 * LOG2E)).astype(BF16)
        for c, t in enumerate(key_tiles):
            vtd_ref[t] = pt[nqd:, c * tk:(c + 1) * tk].astype(BF16)

        cqn = _rms_norm_rows(c_q, gq_ref[...]).astype(BF16)
        qt = _dot_nt(wq_ref[...], cqn)
        cos_q, sin_q = cosq_ref[:, rows], sinq_ref[:, rows]
        for hh in range(MLA_HEADS):
            lo = hh * HEAD_PAD
            r0 = lo + MLA_NOPE
            r1, r2 = r0 + half, r0 + MLA_ROPE
            x1, x2 = qt[r0:r1], qt[r1:r2]
            qtm_ref[lo:r0, rows] = (qt[lo:r0] * q_scale).astype(BF16)
            qtm_ref[r0:r1, rows] = ((x1 * cos_q - x2 * sin_q) * q_scale).astype(BF16)
            qtm_ref[r1:r2, rows] = ((x2 * cos_q + x1 * sin_q) * q_scale).astype(BF16)
            qtm_ref[r2:lo + HEAD_PAD, rows] = jnp.zeros((lo + HEAD_PAD - r2, CHAIN_ROWS), BF16)

        ckvn = _rms_norm_rows(c_kv, gkv_ref[...]).astype(BF16)
        kn = _dot(ckvn, wkn_ref[...])
        kr = krg * cosk_ref[rows, :] + pltpu.roll(krg, HEAD_PAD - MLA_ROPE, 1) * sink_ref[rows, :]
        for hh in range(MLA_HEADS):
            lo, hi = hh * HEAD_PAD, (hh + 1) * HEAD_PAD
            km_ref[rows, lo:hi] = (kn[:, lo:hi] + kr).astype(BF16)
        vt = _dot_nt(wvt_ref[...], ckvn)
        for c, t in enumerate(key_tiles):
            vtm_ref[t] = vt[:, c * tk:(c + 1) * tk].astype(BF16)


def _in_proj_weights(w_in, w_q_b, w_kv_b):
    d = w_in.shape[0]
    o_cq = MLA_Q_RANK
    o_ckv = o_cq + MLA_KV_RANK
    o_kr = o_ckv + MLA_ROPE
    n_d = 2 * DIFF_HEADS * DIFF_D
    o_qd = o_kr + n_d
    o_kd = o_qd + n_d
    half = MLA_ROPE // 2
    pad = HEAD_PAD - MLA_NOPE - MLA_ROPE
    assert pad == MLA_ROPE
    kr = w_in[:, o_ckv:o_kr]
    krg = jnp.concatenate([jnp.zeros((d, MLA_NOPE), F32), kr, -kr[:, half:], kr[:, :half]], axis=1)
    w_nat = jnp.concatenate([w_in[:, :o_ckv], krg, w_in[:, o_qd:o_kd]], axis=1).astype(BF16)
    w_t = jnp.concatenate([w_in[:, o_kr:o_qd], w_in[:, o_kd:]], axis=1).T.astype(BF16)

    r = w_q_b.shape[0]
    wq = w_q_b.reshape(r, MLA_HEADS, MLA_NOPE + MLA_ROPE)
    wq = jnp.concatenate([wq, jnp.zeros((r, MLA_HEADS, pad), F32)], axis=-1)
    wq_t = wq.reshape(r, MLA_HEADS * HEAD_PAD).T.astype(BF16)

    rk = w_kv_b.shape[0]
    wkv = w_kv_b.reshape(rk, MLA_HEADS, MLA_NOPE + MLA_V)
    wkn = jnp.concatenate([wkv[..., :MLA_NOPE], jnp.zeros((rk, MLA_HEADS, HEAD_PAD - MLA_NOPE), F32)],
                          axis=-1).reshape(rk, MLA_HEADS * HEAD_PAD).astype(BF16)
    wv_t = wkv[..., MLA_NOPE:].reshape(rk, MLA_HEADS * MLA_V).T.astype(BF16)
    return w_nat, w_t, wq_t, wkn, wv_t


def _rope_tables(s):
    inv_freq = ROPE_BASE ** (-jnp.arange(0, MLA_ROPE, 2, dtype=F32) / MLA_ROPE)
    ang = jnp.arange(s).astype(F32)[:, None] * inv_freq[None, :]
    cos, sin = jnp.cos(ang), jnp.sin(ang)
    z_lo = jnp.zeros((s, MLA_NOPE), F32)
    z_hi = jnp.zeros((s, HEAD_PAD - MLA_NOPE - MLA_ROPE), F32)
    cos_k = jnp.concatenate([z_lo, cos, cos, z_hi], axis=1)
    sin_k = jnp.concatenate([z_lo, sin, sin, z_hi], axis=1)
    return cos_k, sin_k, cos.T, sin.T


def _in_proj(h, bsz, s, w_in, q_norm_g, kv_norm_g, w_q_b, w_kv_b):
    n, d = h.shape
    tm, tk = ROW_TILE, ATTN_TK
    assert s % tm == 0 and CHAIN_ROWS % tk == 0
    w_nat, w_t, wq_t, wkn, wv_t = _in_proj_weights(w_in, w_q_b, w_kv_b)
    cos_k, sin_k, cos_q, sin_q = _rope_tables(s)
    half = MLA_ROPE // 2
    spt = s // tm
    nqm = MLA_HEADS * HEAD_PAD
    nvm = MLA_HEADS * MLA_V
    nd = 2 * DIFF_HEADS * DIFF_D

    gq = q_norm_g.reshape(1, -1)
    gkv = kv_norm_g.reshape(1, -1)
    out_shape = (
        jax.ShapeDtypeStruct((bsz, nqm, s), BF16),
        jax.ShapeDtypeStruct((bsz, s, nqm), BF16),
        jax.ShapeDtypeStruct((bsz, s // tk, nvm, tk), BF16),
        jax.ShapeDtypeStruct((bsz, nd, s), BF16),
        jax.ShapeDtypeStruct((bsz, s, nd), BF16),
        jax.ShapeDtypeStruct((bsz, s // tk, nd, tk), BF16),
    )
    out_specs = (
        pl.BlockSpec((None, nqm, tm), lambda b, i: (b, 0, i)),
        pl.BlockSpec((None, tm, nqm), lambda b, i: (b, i, 0)),
        pl.BlockSpec((None, tm // tk, nvm, tk), lambda b, i: (b, i, 0, 0)),
        pl.BlockSpec((None, nd, tm), lambda b, i: (b, 0, i)),
        pl.BlockSpec((None, tm, nd), lambda b, i: (b, i, 0)),
        pl.BlockSpec((None, tm // tk, nd, tk), lambda b, i: (b, i, 0, 0)),
    )
    return pl.pallas_call(
        _in_proj_kernel,
        grid=(bsz, spt),
        in_specs=[
            pl.BlockSpec((tm, d), lambda b, i: (b * spt + i, 0)),
            _resident(w_nat), _resident(w_t), _resident(wq_t), _resident(wkn), _resident(wv_t),
            _resident(gq), _resident(gkv),
            pl.BlockSpec((tm, HEAD_PAD), lambda b, i: (i, 0)),
            pl.BlockSpec((tm, HEAD_PAD), lambda b, i: (i, 0)),
            pl.BlockSpec((half, tm), lambda b, i: (0, i)),
            pl.BlockSpec((half, tm), lambda b, i: (0, i)),
        ],
        out_specs=out_specs,
        out_shape=out_shape,
        compiler_params=_params(("arbitrary", "arbitrary")),
        name="in_proj",
    )(h, w_nat, w_t, wq_t, wkn, wv_t, gq, gkv, cos_k, sin_k, cos_q, sin_q)


BIAS_TILES = 3


def _rel_bias_kernel(rb_ref, o_ref, *, tq, tk):
    c = pl.program_id(0) - 1
    kk = lax.broadcasted_iota(jnp.int32, (tk, tq), 0)
    qq = lax.broadcasted_iota(jnp.int32, (tk, tq), 1)
    n = jnp.maximum(qq - kk - c * tk, 0)
    max_exact = REL_BUCKETS // 2
    nf = jnp.maximum(n, 1).astype(F32)
    large = max_exact + (jnp.log(nf / max_exact) / math.log(REL_MAX_DIST / max_exact)
                         * (REL_BUCKETS - max_exact)).astype(jnp.int32)
    large = jnp.minimum(large, REL_BUCKETS - 1)
    bucket = jnp.where(n < max_exact, n, large)
    for m in range(o_ref.shape[0]):
        acc = jnp.zeros((tk, tq), F32)
        for bkt in range(REL_BUCKETS):
            acc = jnp.where(bucket == bkt, rb_ref[bkt, m], acc)
        o_ref[m] = (acc - rb_ref[REL_BUCKETS - 1, m]) * LOG2E


def _rel_bias_tiles(rel_bias, tq, tk):
    assert tq == 2 * tk and tk + 1 >= REL_MAX_DIST
    nmaps = rel_bias.shape[1]
    return pl.pallas_call(
        functools.partial(_rel_bias_kernel, tq=tq, tk=tk),
        grid=(BIAS_TILES,),
        in_specs=[pl.BlockSpec(memory_space=pltpu.SMEM)],
        out_specs=pl.BlockSpec((nmaps, None, tk, tq), lambda t: (0, t, 0, 0)),
        out_shape=jax.ShapeDtypeStruct((nmaps, BIAS_TILES, tk, tq), F32),
        compiler_params=_params(("arbitrary",)),
        name="rel_bias",
    )(rel_bias)


SCORES_AHEAD = 3
VALUES_BEHIND = 1
ONES_ROWS = 16


def _flash_pipeline(q_fn, k_fn, vt_fn, logits_fn, finalize_fn, s_ref, mt_ref, p_ref, al_ref, m_ref, acc_ref,
                    *, nstreams, nq, tq, tk):
    tiles = [(g, i, t, t - 2 * i) for g in range(nstreams) for i in range(nq) for t in range(2 * i + 2)]
    ones = jnp.ones((ONES_ROWS, tk), BF16)
    ahead = s_ref.shape[0] - 1
    behind = p_ref.shape[0] - 1

    def lane0(c):
        return tk if c == 1 else 0

    def scores(n):
        g, i, t, c = tiles[n]
        slot, lo = n % (ahead + 1), lane0(c)
        s = logits_fn(g, _dot(k_fn(g, t), q_fn(g, i, lo)), c, lo)
        s_ref[slot, :, lo:] = s
        mt_ref[slot, :, lo:] = jnp.max(s, axis=0, keepdims=True)

    def softmax(n):
        g, i, t, c = tiles[n]
        sslot, slot, lo = n % (ahead + 1), n % (behind + 1), lane0(c)
        m_new = mt_ref[sslot, :, lo:]
        if t > 0:
            m_prev = m_ref[:, lo:]
            m_new = jnp.maximum(m_prev, m_new)
            al_ref[slot, :, lo:] = jnp.exp2(m_prev - m_new)
        m_ref[:, lo:] = m_new
        p_ref[slot, :, lo:] = jnp.exp2(s_ref[sslot, :, lo:] - m_new).astype(BF16)

    def values(n):
        g, i, t, c = tiles[n]
        slot, lo = n % (behind + 1), lane0(c)
        pv = _dot(jnp.concatenate([vt_fn(g, t), ones], axis=0), p_ref[slot, :, lo:])
        if t > 0:
            pv = al_ref[slot, :, lo:] * acc_ref[:, lo:] + pv
        acc_ref[:, lo:] = pv
        if c == 1:
            finalize_fn(g, i)

    for n in range(ahead):
        scores(n)
    for n in range(len(tiles) + behind):
        if behind <= n:
            values(n - behind)
        if n + ahead < len(tiles):
            scores(n + ahead)
        if n < len(tiles):
            softmax(n)


def _tri_mask(s):
    kk = lax.broadcasted_iota(jnp.int32, s.shape, 0)
    qq = lax.broadcasted_iota(jnp.int32, s.shape, 1)
    return jnp.where(kk <= qq, s, -jnp.inf)


def _flash_scratch(dv, tq, tk):
    return [pltpu.VMEM((SCORES_AHEAD + 1, tk, tq), F32), pltpu.VMEM((SCORES_AHEAD + 1, 1, tq), F32),
            pltpu.VMEM((VALUES_BEHIND + 1, tk, tq), BF16), pltpu.VMEM((VALUES_BEHIND + 1, 1, tq), F32),
            pltpu.VMEM((1, tq), F32), pltpu.VMEM((dv + ONES_ROWS, tq), F32)]


def _mla_attn_kernel(qt_ref, k_ref, vt_ref, o_ref, *scratch, heads, tq, tk):
    acc_ref = scratch[-1]

    def q_fn(g, i, lo):
        return qt_ref[g * HEAD_PAD:(g + 1) * HEAD_PAD, i * tq + lo:(i + 1) * tq]

    def k_fn(g, t):
        return k_ref[t * tk:(t + 1) * tk, g * HEAD_PAD:(g + 1) * HEAD_PAD]

    def vt_fn(g, t):
        return vt_ref[t, g * MLA_V:(g + 1) * MLA_V, :]

    def logits(g, s, c, lo):
        return _tri_mask(s) if c >= 0 else s

    def finalize(g, i):
        acc = acc_ref[...]
        o_ref[g * MLA_V:(g + 1) * MLA_V, i * tq:(i + 1) * tq] = (acc[:MLA_V] / acc[MLA_V:MLA_V + 1]).astype(BF16)

    _flash_pipeline(q_fn, k_fn, vt_fn, logits, finalize, *scratch,
                    nstreams=heads, nq=qt_ref.shape[1] // tq, tq=tq, tk=tk)


def _mla_attn(qt, k, vt, tq, tk):
    bsz, nq, s = qt.shape
    nkt = s // tk
    hps = MLA_HEADS_PER_STEP
    assert tq == 2 * tk and s % tq == 0 and MLA_HEADS % hps == 0
    return pl.pallas_call(
        functools.partial(_mla_attn_kernel, heads=hps, tq=tq, tk=tk),
        grid=(MLA_HEADS // hps, bsz),
        in_specs=[
            pl.BlockSpec((None, hps * HEAD_PAD, s), lambda h, b: (b, h, 0)),
            pl.BlockSpec((None, s, hps * HEAD_PAD), lambda h, b: (b, 0, h)),
            pl.BlockSpec((None, nkt, hps * MLA_V, tk), lambda h, b: (b, 0, h, 0)),
        ],
        out_specs=pl.BlockSpec((None, hps * MLA_V, s), lambda h, b: (b, h, 0)),
        out_shape=jax.ShapeDtypeStruct((bsz, MLA_HEADS * MLA_V, s), BF16),
        scratch_shapes=_flash_scratch(MLA_V, tq, tk),
        compiler_params=_params(("arbitrary", "arbitrary")),
        name="mla_attn",
    )(qt, k, vt)


def _diff_attn_kernel(qt_ref, k_ref, vt_ref, bias_ref, lamv_ref, gn_ref, o_ref, o0_ref, qm_ref, *scratch, tq, tk):
    acc_ref = scratch[-1]
    dv = 2 * DIFF_D

    qt = qt_ref[...]
    row = lax.broadcasted_iota(jnp.int32, qt.shape, 0)
    for g in range(2):
        keep = jnp.logical_and(row >= g * DIFF_D, row < (g + 1) * DIFF_D)
        qm_ref[g] = jnp.where(keep, qt, jnp.zeros_like(qt))

    lv = lamv_ref[...]
    lam = (jnp.exp(jnp.sum(lv[0:1] * lv[1:2], axis=1, keepdims=True))
           - jnp.exp(jnp.sum(lv[2:3] * lv[3:4], axis=1, keepdims=True)) + LAM_INIT)

    def q_fn(g, i, lo):
        return qm_ref[g, :, i * tq + lo:(i + 1) * tq]

    def k_fn(g, t):
        return k_ref[t * tk:(t + 1) * tk, :]

    def vt_fn(g, t):
        return vt_ref[t]

    def logits(g, s, c, lo):
        if c >= -1:
            s = s + bias_ref[g, c + 1, :, lo:]
        return _tri_mask(s) if c >= 0 else s

    def finalize(g, i):
        acc = acc_ref[...]
        o = acc[:dv] / acc[dv:dv + 1]
        cols = slice(i * tq, (i + 1) * tq)
        if g == 0:
            o0_ref[:, cols] = o
        else:
            d = o0_ref[:, cols] - lam * o
            d = d * lax.rsqrt(jnp.mean(d * d, axis=0, keepdims=True) + RMS_EPS) * gn_ref[...] * (1.0 - LAM_INIT)
            o_ref[:, cols] = d.astype(BF16)

    _flash_pipeline(q_fn, k_fn, vt_fn, logits, finalize, *scratch,
                    nstreams=2, nq=qt_ref.shape[1] // tq, tq=tq, tk=tk)


def _diff_attn(qt, k, vt, bias, lamv, gn, tq, tk):
    bsz, nd, s = qt.shape
    nkt = s // tk
    dv = 2 * DIFF_D
    assert tq == 2 * tk and s % tq == 0
    return pl.pallas_call(
        functools.partial(_diff_attn_kernel, tq=tq, tk=tk),
        grid=(DIFF_HEADS, bsz),
        in_specs=[
            pl.BlockSpec((None, dv, s), lambda h, b: (b, h, 0)),
            pl.BlockSpec((None, s, dv), lambda h, b: (b, 0, h)),
            pl.BlockSpec((None, nkt, dv, tk), lambda h, b: (b, 0, h, 0)),
            pl.BlockSpec((2, BIAS_TILES, tk, tq), lambda h, b: (h, 0, 0, 0)),
            pl.BlockSpec(lamv.shape, lambda h, b: (0, 0)),
            pl.BlockSpec(gn.shape, lambda h, b: (0, 0)),
        ],
        out_specs=pl.BlockSpec((None, dv, s), lambda h, b: (b, h, 0)),
        out_shape=jax.ShapeDtypeStruct((bsz, nd, s), BF16),
        scratch_shapes=[pltpu.VMEM((dv, s), F32), pltpu.VMEM((2, dv, s), BF16)] + _flash_scratch(dv, tq, tk),
        compiler_params=_params(("arbitrary", "arbitrary")),
        name="diff_attn",
    )(qt, k, vt, bias, lamv, gn)


def _post_mixer_kernel(h_ref, omt_ref, odt_ref, wum_ref, wud_ref, wg_ref, bg_ref, wo_ref, g_ref, b_ref,
                       o_ref):
    d = h_ref.shape[1]
    for rows in _chains(h_ref.shape[0]):
        h = h_ref[rows, :]
        y_m = _dot_tn(omt_ref[:, rows], wum_ref[...])
        y_d = _dot_tn(odt_ref[:, rows], wud_ref[...])
        z = _dot(h.astype(BF16), wg_ref[...]) + bg_ref[...]
        gate = 1.0 / (1.0 + jnp.exp(-z))
        mixed = gate[:, :d] * y_m + gate[:, d:] * y_d
        mix = _dot(mixed.astype(BF16), wo_ref[...])
        o_ref[rows, :] = _layer_norm(ALPHA * h + mix, g_ref[...], b_ref[...])


def _post_mixer(h, omt, odt, bsz, s, w_up_mla, w_up_diff, w_gate, b_gate, w_o, g, b):
    n, d = h.shape
    tm = ROW_TILE
    spt = s // tm
    ws = [w_up_mla.astype(BF16), w_up_diff.astype(BF16), w_gate.astype(BF16), b_gate.reshape(1, -1),
          w_o.astype(BF16), g.reshape(1, d), b.reshape(1, d)]

    return pl.pallas_call(
        _post_mixer_kernel,
        grid=(bsz, spt),
        in_specs=[
            pl.BlockSpec((tm, d), lambda bb, i: (bb * spt + i, 0)),
            pl.BlockSpec((None, omt.shape[1], tm), lambda bb, i: (bb, 0, i)),
            pl.BlockSpec((None, odt.shape[1], tm), lambda bb, i: (bb, 0, i)),
        ] + [_resident(a) for a in ws],
        out_specs=pl.BlockSpec((tm, d), lambda bb, i: (bb * spt + i, 0)),
        out_shape=jax.ShapeDtypeStruct((n, d), F32),
        compiler_params=_params(("arbitrary", "arbitrary")),
        name="post_mixer",
    )(h, omt, odt, *ws)


def _mem_kv_kernel(mem_ref, w_ref, k_ref, v_ref):
    d = k_ref.shape[1]
    kv = _dot(mem_ref[...].astype(BF16), w_ref[...])
    k_ref[...] = kv[:, :d].astype(BF16)
    v_ref[...] = kv[:, d:].astype(BF16)


def _mem_kv(mem, w_kv):
    bsz, m, d = mem.shape
    w = w_kv.astype(BF16)
    spec = pl.BlockSpec((None, m, d), lambda b: (b, 0, 0))
    return pl.pallas_call(
        _mem_kv_kernel,
        grid=(bsz,),
        in_specs=[spec, pl.BlockSpec(w.shape, lambda b: (0, 0))],
        out_specs=(spec, spec),
        out_shape=(jax.ShapeDtypeStruct((bsz, m, d), BF16), jax.ShapeDtypeStruct((bsz, m, d), BF16)),
        compiler_params=_params(("arbitrary",)),
        name="mem_kv",
    )(mem, w)


def _mem_attn_kernel(h_ref, k_ref, v_ref, wq_ref, wo_ref, g_ref, b_ref, o_ref):
    d = h_ref.shape[1]
    hd = d // MEM_HEADS
    for rows in _chains(h_ref.shape[0], MEM_CHAIN_ROWS):
        h = h_ref[rows, :]
        q = _dot(h.astype(BF16), wq_ref[...]).astype(BF16)
        outs = []
        for hh in range(MEM_HEADS):
            lo, hi = hh * hd, (hh + 1) * hd
            s = _dot_nt(q[:, lo:hi], k_ref[:, lo:hi]) * (hd ** -0.5)
            p = jnp.exp(s - jnp.max(s, axis=-1, keepdims=True))
            p = p / jnp.sum(p, axis=-1, keepdims=True)
            outs.append(_dot(p.astype(BF16), v_ref[:, lo:hi]).astype(BF16))
        att = _dot(jnp.concatenate(outs, axis=1), wo_ref[...])
        o_ref[rows, :] = _layer_norm(ALPHA * h + att, g_ref[...], b_ref[...])


def _mem_attn(h, km, vm, bsz, s, w_q, w_o, g, b):
    n, d = h.shape
    tm = ROW_TILE
    spt = s // tm
    m = km.shape[1]
    ws = [w_q.astype(BF16), w_o.astype(BF16), g.reshape(1, d), b.reshape(1, d)]

    return pl.pallas_call(
        _mem_attn_kernel,
        grid=(bsz, spt),
        in_specs=[
            pl.BlockSpec((tm, d), lambda bb, i: (bb * spt + i, 0)),
            pl.BlockSpec((None, m, d), lambda bb, i: (bb, 0, 0)),
            pl.BlockSpec((None, m, d), lambda bb, i: (bb, 0, 0)),
        ] + [_resident(a) for a in ws],
        out_specs=pl.BlockSpec((tm, d), lambda bb, i: (bb * spt + i, 0)),
        out_shape=jax.ShapeDtypeStruct((n, d), F32),
        compiler_params=_params(("arbitrary", "arbitrary")),
        name="mem_attn",
    )(h, km, vm, *ws)


def kernel(x, mem, rel_bias, ffn1_w1, ffn1_w3, ffn1_w2, ln1_g, ln1_b, w_in, q_norm_g, kv_norm_g, w_q_b, w_kv_b, lam_q1, lam_k1, lam_q2, lam_k2, diff_norm_g, w_gate, b_gate, w_up_mla, w_up_diff, w_o, ln2_g, ln2_b, mem_w_q, mem_w_kv, mem_w_o, ln3_g, ln3_b, ffn2_w1, ffn2_w3, ffn2_w2, ln4_g, ln4_b):
    bsz, s, d = x.shape
    assert ffn1_w1.shape[0] == DEPTH
    l = 0
    h = x.reshape(bsz * s, d)

    h = _ffn_ln(h, ffn1_w1[l], ffn1_w3[l], ffn1_w2[l], ln1_g[l], ln1_b[l])

    qtm, km, vtm, qtd, kd, vtd = _in_proj(h, bsz, s, w_in[l], q_norm_g[l], kv_norm_g[l], w_q_b[l], w_kv_b[l])
    bias = _rel_bias_tiles(rel_bias, ATTN_TQ, ATTN_TK)
    omt = _mla_attn(qtm, km, vtm, ATTN_TQ, ATTN_TK)
    lamv = jnp.stack([lam_q1[l], lam_k1[l], lam_q2[l], lam_k2[l]], axis=0)
    odt = _diff_attn(qtd, kd, vtd, bias, lamv, diff_norm_g[l].reshape(-1, 1), ATTN_TQ, ATTN_TK)
    h = _post_mixer(h, omt, odt, bsz, s, w_up_mla[l], w_up_diff[l], w_gate[l], b_gate[l], w_o[l],
                    ln2_g[l], ln2_b[l])

    kmem, vmem = _mem_kv(mem, mem_w_kv[l])
    h = _mem_attn(h, kmem, vmem, bsz, s, mem_w_q[l], mem_w_o[l], ln3_g[l], ln3_b[l])

    h = _ffn_ln(h, ffn2_w1[l], ffn2_w3[l], ffn2_w2[l], ln4_g[l], ln4_b[l])
    return h.reshape(bsz, s, d)
```

```python
import functools
import math

import jax
import jax.numpy as jnp
from jax import lax
from jax.experimental import pallas as pl
from jax.experimental.pallas import tpu as pltpu

F32 = jnp.float32
BF16 = jnp.bfloat16

DEPTH = 1
MLA_HEADS = 8
MLA_Q_RANK = 384
MLA_KV_RANK = 256
MLA_NOPE = 64
MLA_ROPE = 32
MLA_V = 64
ROPE_BASE = 10000.0
DIFF_HEADS = 4
DIFF_D = 64
REL_BUCKETS = 32
REL_MAX_DIST = 128
MEM_HEADS = 4
ALPHA = (2.0 * DEPTH) ** 0.25
LN_EPS = 1e-5
RMS_EPS = 1e-6
LAM_INIT = 0.8 - 0.6 * math.exp(-0.3 * 0)
LOG2E = math.log2(math.e)

HEAD_PAD = 128
VMEM_LIMIT_BYTES = 56 * 1024 * 1024

ROW_TILE = 1024
WIDE_ROW_TILE = 2048
CHAIN_ROWS = 256
MEM_CHAIN_ROWS = 512
ATTN_TQ = 512
ATTN_TK = 256
MLA_HEADS_PER_STEP = 4


def _params(semantics):
    return pltpu.CompilerParams(dimension_semantics=semantics, vmem_limit_bytes=VMEM_LIMIT_BYTES)


def _dot(a, b):
    return jnp.dot(a, b, preferred_element_type=F32)


def _dot_nt(a, b):
    return lax.dot_general(a, b, (((1,), (1,)), ((), ())), preferred_element_type=F32)


def _dot_tn(a, b):
    return lax.dot_general(a, b, (((0,), (0,)), ((), ())), preferred_element_type=F32)


def _layer_norm(y, g, b):
    mu = jnp.mean(y, axis=-1, keepdims=True)
    d = y - mu
    var = jnp.mean(d * d, axis=-1, keepdims=True)
    return d * lax.rsqrt(var + LN_EPS) * g + b


def _rms_norm_rows(x, g):
    return x * lax.rsqrt(jnp.mean(x * x, axis=-1, keepdims=True) + RMS_EPS) * g


def _chains(rows, chain=CHAIN_ROWS):
    return [slice(r, r + chain) for r in range(0, rows, chain)]


def _resident(a):
    return pl.BlockSpec(a.shape, lambda *_: (0,) * a.ndim, pipeline_mode=pl.Buffered(1))


def _ffn_ln_kernel(x_ref, w1_ref, w3_ref, w2_ref, g_ref, b_ref, o_ref):
    for rows in _chains(x_ref.shape[0]):
        x = x_ref[rows, :]
        xb = x.astype(BF16)
        u = _dot(xb, w1_ref[...])
        v = _dot(xb, w3_ref[...])
        act = (u / (1.0 + jnp.exp(-u))) * v
        y = ALPHA * x + 0.5 * _dot(act.astype(BF16), w2_ref[...])
        o_ref[rows, :] = _layer_norm(y, g_ref[...], b_ref[...])


def _ffn_ln(x, w1, w3, w2, g, b):
    n, d = x.shape
    tm = WIDE_ROW_TILE
    assert n % tm == 0
    ws = [w1.astype(BF16), w3.astype(BF16), w2.astype(BF16), g.reshape(1, d), b.reshape(1, d)]
    return pl.pallas_call(
        _ffn_ln_kernel,
        grid=(n // tm,),
        in_specs=[pl.BlockSpec((tm, d), lambda i: (i, 0))] + [_resident(a) for a in ws],
        out_specs=pl.BlockSpec((tm, d), lambda i: (i, 0)),
        out_shape=jax.ShapeDtypeStruct((n, d), F32),
        compiler_params=_params(("arbitrary",)),
        name="ffn_ln",
    )(x, *ws)


def _in_proj_kernel(h_ref, wnat_ref, wt_ref, wq_ref, wkn_ref, wvt_ref, gq_ref, gkv_ref,
                    cosk_ref, sink_ref, cosq_ref, sinq_ref,
                    qtm_ref, km_ref, vtm_ref, qtd_ref, kd_ref, vtd_ref):
    tk = vtm_ref.shape[-1]
    half = MLA_ROPE // 2
    nqd = qtd_ref.shape[0]
    o1 = MLA_Q_RANK
    o2 = o1 + MLA_KV_RANK
    o3 = o2 + HEAD_PAD
    q_scale = (MLA_NOPE + MLA_ROPE) ** -0.5 * LOG2E
    for rows in _chains(h_ref.shape[0]):
        key_tiles = range(rows.start // tk, rows.stop // tk)
        hb = h_ref[rows, :].astype(BF16)

        pn = _dot(hb, wnat_ref[...])
        c_q, c_kv, krg, kd = pn[:, :o1], pn[:, o1:o2], pn[:, o2:o3], pn[:, o3:]
        kd_ref[rows, :] = kd.astype(BF16)

        pt = _dot_nt(wt_ref[...], hb)
        qtd_ref[:, rows] = (pt[:nqd] * (DIFF_D ** -0.5 * LOG2E)).astype(BF16)
        for c, t in enumerate(key_tiles):
            vtd_ref[t] = pt[nqd:, c * tk:(c + 1) * tk].astype(BF16)

        cqn = _rms_norm_rows(c_q, gq_ref[...]).astype(BF16)
        qt = _dot_nt(wq_ref[...], cqn)
        cos_q, sin_q = cosq_ref[:, rows], sinq_ref[:, rows]
        for hh in range(MLA_HEADS):
            lo = hh * HEAD_PAD
            r0 = lo + MLA_NOPE
            r1, r2 = r0 + half, r0 + MLA_ROPE
            x1, x2 = qt[r0:r1], qt[r1:r2]
            qtm_ref[lo:r0, rows] = (qt[lo:r0] * q_scale).astype(BF16)
            qtm_ref[r0:r1, rows] = ((x1 * cos_q - x2 * sin_q) * q_scale).astype(BF16)
            qtm_ref[r1:r2, rows] = ((x2 * cos_q + x1 * sin_q) * q_scale).astype(BF16)
            qtm_ref[r2:lo + HEAD_PAD, rows] = jnp.zeros((lo + HEAD_PAD - r2, CHAIN_ROWS), BF16)

        ckvn = _rms_norm_rows(c_kv, gkv_ref[...]).astype(BF16)
        kn = _dot(ckvn, wkn_ref[...])
        kr = krg * cosk_ref[rows, :] + pltpu.roll(krg, HEAD_PAD - MLA_ROPE, 1) * sink_ref[rows, :]
        for hh in range(MLA_HEADS):
            lo, hi = hh * HEAD_PAD, (hh + 1) * HEAD_PAD
            km_ref[rows, lo:hi] = (kn[:, lo:hi] + kr).astype(BF16)
        vt = _dot_nt(wvt_ref[...], ckvn)
        for c, t in enumerate(key_tiles):
            vtm_ref[t] = vt[:, c * tk:(c + 1) * tk].astype(BF16)


def _in_proj_weights(w_in, w_q_b, w_kv_b):
    d = w_in.shape[0]
    o_cq = MLA_Q_RANK
    o_ckv = o_cq + MLA_KV_RANK
    o_kr = o_ckv + MLA_ROPE
    n_d = 2 * DIFF_HEADS * DIFF_D
    o_qd = o_kr + n_d
    o_kd = o_qd + n_d
    half = MLA_ROPE // 2
    pad = HEAD_PAD - MLA_NOPE - MLA_ROPE
    assert pad == MLA_ROPE
    kr = w_in[:, o_ckv:o_kr]
    krg = jnp.concatenate([jnp.zeros((d, MLA_NOPE), F32), kr, -kr[:, half:], kr[:, :half]], axis=1)
    w_nat = jnp.concatenate([w_in[:, :o_ckv], krg, w_in[:, o_qd:o_kd]], axis=1).astype(BF16)
    w_t = jnp.concatenate([w_in[:, o_kr:o_qd], w_in[:, o_kd:]], axis=1).T.astype(BF16)

    r = w_q_b.shape[0]
    wq = w_q_b.reshape(r, MLA_HEADS, MLA_NOPE + MLA_ROPE)
    wq = jnp.concatenate([wq, jnp.zeros((r, MLA_HEADS, pad), F32)], axis=-1)
    wq_t = wq.reshape(r, MLA_HEADS * HEAD_PAD).T.astype(BF16)

    rk = w_kv_b.shape[0]
    wkv = w_kv_b.reshape(rk, MLA_HEADS, MLA_NOPE + MLA_V)
    wkn = jnp.concatenate([wkv[..., :MLA_NOPE], jnp.zeros((rk, MLA_HEADS, HEAD_PAD - MLA_NOPE), F32)],
                          axis=-1).reshape(rk, MLA_HEADS * HEAD_PAD).astype(BF16)
    wv_t = wkv[..., MLA_NOPE:].reshape(rk, MLA_HEADS * MLA_V).T.astype(BF16)
    return w_nat, w_t, wq_t, wkn, wv_t


def _rope_tables(s):
    inv_freq = ROPE_BASE ** (-jnp.arange(0, MLA_ROPE, 2, dtype=F32) / MLA_ROPE)
    ang = jnp.arange(s).astype(F32)[:, None] * inv_freq[None, :]
    cos, sin = jnp.cos(ang), jnp.sin(ang)
    z_lo = jnp.zeros((s, MLA_NOPE), F32)
    z_hi = jnp.zeros((s, HEAD_PAD - MLA_NOPE - MLA_ROPE), F32)
    cos_k = jnp.concatenate([z_lo, cos, cos, z_hi], axis=1)
    sin_k = jnp.concatenate([z_lo, sin, sin, z_hi], axis=1)
    return cos_k, sin_k, cos.T, sin.T


def _in_proj(h, bsz, s, w_in, q_norm_g, kv_norm_g, w_q_b, w_kv_b):
    n, d = h.shape
    tm, tk = ROW_TILE, ATTN_TK
    assert s % tm == 0 and CHAIN_ROWS % tk == 0
    w_nat, w_t, wq_t, wkn, wv_t = _in_proj_weights(w_in, w_q_b, w_kv_b)
    cos_k, sin_k, cos_q, sin_q = _rope_tables(s)
    half = MLA_ROPE // 2
    spt = s // tm
    nqm = MLA_HEADS * HEAD_PAD
    nvm = MLA_HEADS * MLA_V
    nd = 2 * DIFF_HEADS * DIFF_D

    gq = q_norm_g.reshape(1, -1)
    gkv = kv_norm_g.reshape(1, -1)
    out_shape = (
        jax.ShapeDtypeStruct((bsz, nqm, s), BF16),
        jax.ShapeDtypeStruct((bsz, s, nqm), BF16),
        jax.ShapeDtypeStruct((bsz, s // tk, nvm, tk), BF16),
        jax.ShapeDtypeStruct((bsz, nd, s), BF16),
        jax.ShapeDtypeStruct((bsz, s, nd), BF16),
        jax.ShapeDtypeStruct((bsz, s // tk, nd, tk), BF16),
    )
    out_specs = (
        pl.BlockSpec((None, nqm, tm), lambda b, i: (b, 0, i)),
        pl.BlockSpec((None, tm, nqm), lambda b, i: (b, i, 0)),
        pl.BlockSpec((None, tm // tk, nvm, tk), lambda b, i: (b, i, 0, 0)),
        pl.BlockSpec((None, nd, tm), lambda b, i: (b, 0, i)),
        pl.BlockSpec((None, tm, nd), lambda b, i: (b, i, 0)),
        pl.BlockSpec((None, tm // tk, nd, tk), lambda b, i: (b, i, 0, 0)),
    )
    return pl.pallas_call(
        _in_proj_kernel,
        grid=(bsz, spt),
        in_specs=[
            pl.BlockSpec((tm, d), lambda b, i: (b * spt + i, 0)),
            _resident(w_nat), _resident(w_t), _resident(wq_t), _resident(wkn), _resident(wv_t),
            _resident(gq), _resident(gkv),
            pl.BlockSpec((tm, HEAD_PAD), lambda b, i: (i, 0)),
            pl.BlockSpec((tm, HEAD_PAD), lambda b, i: (i, 0)),
            pl.BlockSpec((half, tm), lambda b, i: (0, i)),
            pl.BlockSpec((half, tm), lambda b, i: (0, i)),
        ],
        out_specs=out_specs,
        out_shape=out_shape,
        compiler_params=_params(("arbitrary", "arbitrary")),
        name="in_proj",
    )(h, w_nat, w_t, wq_t, wkn, wv_t, gq, gkv, cos_k, sin_k, cos_q, sin_q)


BIAS_TILES = 3


def _rel_bias_kernel(rb_ref, o_ref, *, tq, tk):
    c = pl.program_id(0) - 1
    kk = lax.broadcasted_iota(jnp.int32, (tk, tq), 0)
    qq = lax.broadcasted_iota(jnp.int32, (tk, tq), 1)
    n = jnp.maximum(qq - kk - c * tk, 0)
    max_exact = REL_BUCKETS // 2
    nf = jnp.maximum(n, 1).astype(F32)
    large = max_exact + (jnp.log(nf / max_exact) / math.log(REL_MAX_DIST / max_exact)
                         * (REL_BUCKETS - max_exact)).astype(jnp.int32)
    large = jnp.minimum(large, REL_BUCKETS - 1)
    bucket = jnp.where(n < max_exact, n, large)
    for m in range(o_ref.shape[0]):
        acc = jnp.zeros((tk, tq), F32)
        for bkt in range(REL_BUCKETS):
            acc = jnp.where(bucket == bkt, rb_ref[bkt, m], acc)
        o_ref[m] = (acc - rb_ref[REL_BUCKETS - 1, m]) * LOG2E


def _rel_bias_tiles(rel_bias, tq, tk):
    assert tq == 2 * tk and tk + 1 >= REL_MAX_DIST
    nmaps = rel_bias.shape[1]
    return pl.pallas_call(
        functools.partial(_rel_bias_kernel, tq=tq, tk=tk),
        grid=(BIAS_TILES,),
        in_specs=[pl.BlockSpec(memory_space=pltpu.SMEM)],
        out_specs=pl.BlockSpec((nmaps, None, tk, tq), lambda t: (0, t, 0, 0)),
        out_shape=jax.ShapeDtypeStruct((nmaps, BIAS_TILES, tk, tq), F32),
        compiler_params=_params(("arbitrary",)),
        name="rel_bias",
    )(rel_bias)


SCORES_AHEAD = 3
VALUES_BEHIND = 1
ONES_ROWS = 16


def _flash_pipeline(q_fn, k_fn, vt_fn, logits_fn, finalize_fn, s_ref, mt_ref, p_ref, al_ref, m_ref, acc_ref,
                    *, nstreams, nq, tq, tk):
    tiles = [(g, i, t, t - 2 * i) for g in range(nstreams) for i in range(nq) for t in range(2 * i + 2)]
    ones = jnp.ones((ONES_ROWS, tk), BF16)
    ahead = s_ref.shape[0] - 1
    behind = p_ref.shape[0] - 1

    def lane0(c):
        return tk if c == 1 else 0

    def scores(n):
        g, i, t, c = tiles[n]
        slot, lo = n % (ahead + 1), lane0(c)
        s = logits_fn(g, _dot(k_fn(g, t), q_fn(g, i, lo)), c, lo)
        s_ref[slot, :, lo:] = s
        mt_ref[slot, :, lo:] = jnp.max(s, axis=0, keepdims=True)

    def softmax(n):
        g, i, t, c = tiles[n]
        sslot, slot, lo = n % (ahead + 1), n % (behind + 1), lane0(c)
        m_new = mt_ref[sslot, :, lo:]
        if t > 0:
            m_prev = m_ref[:, lo:]
            m_new = jnp.maximum(m_prev, m_new)
            al_ref[slot, :, lo:] = jnp.exp2(m_prev - m_new)
        m_ref[:, lo:] = m_new
        p_ref[slot, :, lo:] = jnp.exp2(s_ref[sslot, :, lo:] - m_new).astype(BF16)

    def values(n):
        g, i, t, c = tiles[n]
        slot, lo = n % (behind + 1), lane0(c)
        pv = _dot(jnp.concatenate([vt_fn(g, t), ones], axis=0), p_ref[slot, :, lo:])
        if t > 0:
            pv = al_ref[slot, :, lo:] * acc_ref[:, lo:] + pv
        acc_ref[:, lo:] = pv
        if c == 1:
            finalize_fn(g, i)

    for n in range(ahead):
        scores(n)
    for n in range(len(tiles) + behind):
        if behind <= n:
            values(n - behind)
        if n + ahead < len(tiles):
            scores(n + ahead)
        if n < len(tiles):
            softmax(n)


def _tri_mask(s):
    kk = lax.broadcasted_iota(jnp.int32, s.shape, 0)
    qq = lax.broadcasted_iota(jnp.int32, s.shape, 1)
    return jnp.where(kk <= qq, s, -jnp.inf)


def _flash_scratch(dv, tq, tk):
    return [pltpu.VMEM((SCORES_AHEAD + 1, tk, tq), F32), pltpu.VMEM((SCORES_AHEAD + 1, 1, tq), F32),
            pltpu.VMEM((VALUES_BEHIND + 1, tk, tq), BF16), pltpu.VMEM((VALUES_BEHIND + 1, 1, tq), F32),
            pltpu.VMEM((1, tq), F32), pltpu.VMEM((dv + ONES_ROWS, tq), F32)]


def _mla_attn_kernel(qt_ref, k_ref, vt_ref, o_ref, *scratch, heads, tq, tk):
    acc_ref = scratch[-1]

    def q_fn(g, i, lo):
        return qt_ref[g * HEAD_PAD:(g + 1) * HEAD_PAD, i * tq + lo:(i + 1) * tq]

    def k_fn(g, t):
        return k_ref[t * tk:(t + 1) * tk, g * HEAD_PAD:(g + 1) * HEAD_PAD]

    def vt_fn(g, t):
        return vt_ref[t, g * MLA_V:(g + 1) * MLA_V, :]

    def logits(g, s, c, lo):
        return _tri_mask(s) if c >= 0 else s

    def finalize(g, i):
        acc = acc_ref[...]
        o_ref[g * MLA_V:(g + 1) * MLA_V, i * tq:(i + 1) * tq] = (acc[:MLA_V] / acc[MLA_V:MLA_V + 1]).astype(BF16)

    _flash_pipeline(q_fn, k_fn, vt_fn, logits, finalize, *scratch,
                    nstreams=heads, nq=qt_ref.shape[1] // tq, tq=tq, tk=tk)


def _mla_attn(qt, k, vt, tq, tk):
    bsz, nq, s = qt.shape
    nkt = s // tk
    hps = MLA_HEADS_PER_STEP
    assert tq == 2 * tk and s % tq == 0 and MLA_HEADS % hps == 0
    return pl.pallas_call(
        functools.partial(_mla_attn_kernel, heads=hps, tq=tq, tk=tk),
        grid=(MLA_HEADS // hps, bsz),
        in_specs=[
            pl.BlockSpec((None, hps * HEAD_PAD, s), lambda h, b: (b, h, 0)),
            pl.BlockSpec((None, s, hps * HEAD_PAD), lambda h, b: (b, 0, h)),
            pl.BlockSpec((None, nkt, hps * MLA_V, tk), lambda h, b: (b, 0, h, 0)),
        ],
        out_specs=pl.BlockSpec((None, hps * MLA_V, s), lambda h, b: (b, h, 0)),
        out_shape=jax.ShapeDtypeStruct((bsz, MLA_HEADS * MLA_V, s), BF16),
        scratch_shapes=_flash_scratch(MLA_V, tq, tk),
        compiler_params=_params(("arbitrary", "arbitrary")),
        name="mla_attn",
    )(qt, k, vt)


def _diff_attn_kernel(qt_ref, k_ref, vt_ref, bias_ref, lamv_ref, gn_ref, o_ref, o0_ref, qm_ref, *scratch, tq, tk):
    acc_ref = scratch[-1]
    dv = 2 * DIFF_D

    qt = qt_ref[...]
    row = lax.broadcasted_iota(jnp.int32, qt.shape, 0)
    for g in range(2):
        keep = jnp.logical_and(row >= g * DIFF_D, row < (g + 1) * DIFF_D)
        qm_ref[g] = jnp.where(keep, qt, jnp.zeros_like(qt))

    lv = lamv_ref[...]
    lam = (jnp.exp(jnp.sum(lv[0:1] * lv[1:2], axis=1, keepdims=True))
           - jnp.exp(jnp.sum(lv[2:3] * lv[3:4], axis=1, keepdims=True)) + LAM_INIT)

    def q_fn(g, i, lo):
        return qm_ref[g, :, i * tq + lo:(i + 1) * tq]

    def k_fn(g, t):
        return k_ref[t * tk:(t + 1) * tk, :]

    def vt_fn(g, t):
        return vt_ref[t]

    def logits(g, s, c, lo):
        if c >= -1:
            s = s + bias_ref[g, c + 1, :, lo:]
        return _tri_mask(s) if c >= 0 else s

    def finalize(g, i):
        acc = acc_ref[...]
        o = acc[:dv] / acc[dv:dv + 1]
        cols = slice(i * tq, (i + 1) * tq)
        if g == 0:
            o0_ref[:, cols] = o
        else:
            d = o0_ref[:, cols] - lam * o
            d = d * lax.rsqrt(jnp.mean(d * d, axis=0, keepdims=True) + RMS_EPS) * gn_ref[...] * (1.0 - LAM_INIT)
            o_ref[:, cols] = d.astype(BF16)

    _flash_pipeline(q_fn, k_fn, vt_fn, logits, finalize, *scratch,
                    nstreams=2, nq=qt_ref.shape[1] // tq, tq=tq, tk=tk)


def _diff_attn(qt, k, vt, bias, lamv, gn, tq, tk):
    bsz, nd, s = qt.shape
    nkt = s // tk
    dv = 2 * DIFF_D
    assert tq == 2 * tk and s % tq == 0
    return pl.pallas_call(
        functools.partial(_diff_attn_kernel, tq=tq, tk=tk),
        grid=(DIFF_HEADS, bsz),
        in_specs=[
            pl.BlockSpec((None, dv, s), lambda h, b: (b, h, 0)),
            pl.BlockSpec((None, s, dv), lambda h, b: (b, 0, h)),
            pl.BlockSpec((None, nkt, dv, tk), lambda h, b: (b, 0, h, 0)),
            pl.BlockSpec((2, BIAS_TILES, tk, tq), lambda h, b: (h, 0, 0, 0)),
            pl.BlockSpec(lamv.shape, lambda h, b: (0, 0)),
            pl.BlockSpec(gn.shape, lambda h, b: (0, 0)),
        ],
        out_specs=pl.BlockSpec((None, dv, s), lambda h, b: (b, h, 0)),
        out_shape=jax.ShapeDtypeStruct((bsz, nd, s), BF16),
        scratch_shapes=[pltpu.VMEM((dv, s), F32), pltpu.VMEM((2, dv, s), BF16)] + _flash_scratch(dv, tq, tk),
        compiler_params=_params(("arbitrary", "arbitrary")),
        name="diff_attn",
    )(qt, k, vt, bias, lamv, gn)


def _post_mixer_kernel(h_ref, omt_ref, odt_ref, wum_ref, wud_ref, wg_ref, bg_ref, wo_ref, g_ref, b_ref,
                       o_ref):
    d = h_ref.shape[1]
    for rows in _chains(h_ref.shape[0]):
        h = h_ref[rows, :]
        y_m = _dot_tn(omt_ref[:, rows], wum_ref[...])
        y_d = _dot_tn(odt_ref[:, rows], wud_ref[...])
        z = _dot(h.astype(BF16), wg_ref[...]) + bg_ref[...]
        gate = 1.0 / (1.0 + jnp.exp(-z))
        mixed = gate[:, :d] * y_m + gate[:, d:] * y_d
        mix = _dot(mixed.astype(BF16), wo_ref[...])
        o_ref[rows, :] = _layer_norm(ALPHA * h + mix, g_ref[...], b_ref[...])


def _post_mixer(h, omt, odt, bsz, s, w_up_mla, w_up_diff, w_gate, b_gate, w_o, g, b):
    n, d = h.shape
    tm = ROW_TILE
    spt = s // tm
    ws = [w_up_mla.astype(BF16), w_up_diff.astype(BF16), w_gate.astype(BF16), b_gate.reshape(1, -1),
          w_o.astype(BF16), g.reshape(1, d), b.reshape(1, d)]

    return pl.pallas_call(
        _post_mixer_kernel,
        grid=(bsz, spt),
        in_specs=[
            pl.BlockSpec((tm, d), lambda bb, i: (bb * spt + i, 0)),
            pl.BlockSpec((None, omt.shape[1], tm), lambda bb, i: (bb, 0, i)),
            pl.BlockSpec((None, odt.shape[1], tm), lambda bb, i: (bb, 0, i)),
        ] + [_resident(a) for a in ws],
        out_specs=pl.BlockSpec((tm, d), lambda bb, i: (bb * spt + i, 0)),
        out_shape=jax.ShapeDtypeStruct((n, d), F32),
        compiler_params=_params(("arbitrary", "arbitrary")),
        name="post_mixer",
    )(h, omt, odt, *ws)


def _mem_kv_kernel(mem_ref, w_ref, k_ref, v_ref):
    d = k_ref.shape[1]
    kv = _dot(mem_ref[...].astype(BF16), w_ref[...])
    k_ref[...] = kv[:, :d].astype(BF16)
    v_ref[...] = kv[:, d:].astype(BF16)


def _mem_kv(mem, w_kv):
    bsz, m, d = mem.shape
    w = w_kv.astype(BF16)
    spec = pl.BlockSpec((None, m, d), lambda b: (b, 0, 0))
    return pl.pallas_call(
        _mem_kv_kernel,
        grid=(bsz,),
        in_specs=[spec, pl.BlockSpec(w.shape, lambda b: (0, 0))],
        out_specs=(spec, spec),
        out_shape=(jax.ShapeDtypeStruct((bsz, m, d), BF16), jax.ShapeDtypeStruct((bsz, m, d), BF16)),
        compiler_params=_params(("arbitrary",)),
        name="mem_kv",
    )(mem, w)


def _mem_attn_kernel(h_ref, k_ref, v_ref, wq_ref, wo_ref, g_ref, b_ref, o_ref):
    d = h_ref.shape[1]
    hd = d // MEM_HEADS
    for rows in _chains(h_ref.shape[0], MEM_CHAIN_ROWS):
        h = h_ref[rows, :]
        q = _dot(h.astype(BF16), wq_ref[...]).astype(BF16)
        outs = []
        for hh in range(MEM_HEADS):
            lo, hi = hh * hd, (hh + 1) * hd
            s = _dot_nt(q[:, lo:hi], k_ref[:, lo:hi]) * (hd ** -0.5)
            p = jnp.exp(s - jnp.max(s, axis=-1, keepdims=True))
            p = p / jnp.sum(p, axis=-1, keepdims=True)
            outs.append(_dot(p.astype(BF16), v_ref[:, lo:hi]).astype(BF16))
        att = _dot(jnp.concatenate(outs, axis=1), wo_ref[...])
        o_ref[rows, :] = _layer_norm(ALPHA * h + att, g_ref[...], b_ref[...])


def _mem_attn(h, km, vm, bsz, s, w_q, w_o, g, b):
    n, d = h.shape
    tm = WIDE_ROW_TILE
    spt = s // tm
    m = km.shape[1]
    ws = [w_q.astype(BF16), w_o.astype(BF16), g.reshape(1, d), b.reshape(1, d)]

    return pl.pallas_call(
        _mem_attn_kernel,
        grid=(bsz, spt),
        in_specs=[
            pl.BlockSpec((tm, d), lambda bb, i: (bb * spt + i, 0)),
            pl.BlockSpec((None, m, d), lambda bb, i: (bb, 0, 0)),
            pl.BlockSpec((None, m, d), lambda bb, i: (bb, 0, 0)),
        ] + [_resident(a) for a in ws],
        out_specs=pl.BlockSpec((tm, d), lambda bb, i: (bb * spt + i, 0)),
        out_shape=jax.ShapeDtypeStruct((n, d), F32),
        compiler_params=_params(("arbitrary", "arbitrary")),
        name="mem_attn",
    )(h, km, vm, *ws)


def kernel(x, mem, rel_bias, ffn1_w1, ffn1_w3, ffn1_w2, ln1_g, ln1_b, w_in, q_norm_g, kv_norm_g, w_q_b, w_kv_b, lam_q1, lam_k1, lam_q2, lam_k2, diff_norm_g, w_gate, b_gate, w_up_mla, w_up_diff, w_o, ln2_g, ln2_b, mem_w_q, mem_w_kv, mem_w_o, ln3_g, ln3_b, ffn2_w1, ffn2_w3, ffn2_w2, ln4_g, ln4_b):
    bsz, s, d = x.shape
    assert ffn1_w1.shape[0] == DEPTH
    l = 0
    h = x.reshape(bsz * s, d)

    h = _ffn_ln(h, ffn1_w1[l], ffn1_w3[l], ffn1_w2[l], ln1_g[l], ln1_b[l])

    qtm, km, vtm, qtd, kd, vtd = _in_proj(h, bsz, s, w_in[l], q_norm_g[l], kv_norm_g[l], w_q_b[l], w_kv_b[l])
    bias = _rel_bias_tiles(rel_bias, ATTN_TQ, ATTN_TK)
    omt = _mla_attn(qtm, km, vtm, ATTN_TQ, ATTN_TK)
    lamv = jnp.stack([lam_q1[l], lam_k1[l], lam_q2[l], lam_k2[l]], axis=0)
    odt = _diff_attn(qtd, kd, vtd, bias, lamv, diff_norm_g[l].reshape(-1, 1), ATTN_TQ, ATTN_TK)
    h = _post_mixer(h, omt, odt, bsz, s, w_up_mla[l], w_up_diff[l], w_gate[l], b_gate[l], w_o[l],
                    ln2_g[l], ln2_b[l])

    kmem, vmem = _mem_kv(mem, mem_w_kv[l])
    h = _mem_attn(h, kmem, vmem, bsz, s, mem_w_q[l], mem_w_o[l], ln3_g[l], ln3_b[l])

    h = _ffn_ln(h, ffn2_w1[l], ffn2_w3[l], ffn2_w2[l], ln4_g[l], ln4_b[l])
    return h.reshape(bsz, s, d)
```

```python
import functools
import math

import jax
import jax.numpy as jnp
from jax import lax
from jax.experimental import pallas as pl
from jax.experimental.pallas import tpu as pltpu

F32 = jnp.float32
BF16 = jnp.bfloat16

DEPTH = 1
MLA_HEADS = 8
MLA_Q_RANK = 384
MLA_KV_RANK = 256
MLA_NOPE = 64
MLA_ROPE = 32
MLA_V = 64
ROPE_BASE = 10000.0
DIFF_HEADS = 4
DIFF_D = 64
REL_BUCKETS = 32
REL_MAX_DIST = 128
MEM_HEADS = 4
ALPHA = (2.0 * DEPTH) ** 0.25
LN_EPS = 1e-5
RMS_EPS = 1e-6
LAM_INIT = 0.8 - 0.6 * math.exp(-0.3 * 0)
LOG2E = math.log2(math.e)

HEAD_PAD = 128
VMEM_LIMIT_BYTES = 56 * 1024 * 1024

ROW_TILE = 1024
WIDE_ROW_TILE = 2048
CHAIN_ROWS = 256
MEM_CHAIN_ROWS = 512
ATTN_TQ = 512
ATTN_TK = 256
MLA_HEADS_PER_STEP = 4


def _params(semantics):
    return pltpu.CompilerParams(dimension_semantics=semantics, vmem_limit_bytes=VMEM_LIMIT_BYTES)


def _dot(a, b):
    return jnp.dot(a, b, preferred_element_type=F32)


def _dot_nt(a, b):
    return lax.dot_general(a, b, (((1,), (1,)), ((), ())), preferred_element_type=F32)


def _dot_tn(a, b):
    return lax.dot_general(a, b, (((0,), (0,)), ((), ())), preferred_element_type=F32)


def _layer_norm(y, g, b):
    mu = jnp.mean(y, axis=-1, keepdims=True)
    d = y - mu
    var = jnp.mean(d * d, axis=-1, keepdims=True)
    return d * lax.rsqrt(var + LN_EPS) * g + b


def _rms_norm_rows(x, g):
    return x * lax.rsqrt(jnp.mean(x * x, axis=-1, keepdims=True) + RMS_EPS) * g


def _chains(rows, chain=CHAIN_ROWS):
    return [slice(r, r + chain) for r in range(0, rows, chain)]


def _resident(a):
    return pl.BlockSpec(a.shape, lambda *_: (0,) * a.ndim, pipeline_mode=pl.Buffered(1))


def _ffn_ln_kernel(x_ref, w1_ref, w3_ref, w2_ref, g_ref, b_ref, o_ref):
    for rows in _chains(x_ref.shape[0]):
        x = x_ref[rows, :]
        xb = x.astype(BF16)
        u = _dot(xb, w1_ref[...])
        v = _dot(xb, w3_ref[...])
        act = (u / (1.0 + jnp.exp(-u))) * v
        y = ALPHA * x + 0.5 * _dot(act.astype(BF16), w2_ref[...])
        o_ref[rows, :] = _layer_norm(y, g_ref[...], b_ref[...])


def _ffn_ln(x, w1, w3, w2, g, b):
    n, d = x.shape
    tm = ROW_TILE
    assert n % tm == 0
    ws = [w1.astype(BF16), w3.astype(BF16), w2.astype(BF16), g.reshape(1, d), b.reshape(1, d)]
    return pl.pallas_call(
        _ffn_ln_kernel,
        grid=(n // tm,),
        in_specs=[pl.BlockSpec((tm, d), lambda i: (i, 0))] + [_resident(a) for a in ws],
        out_specs=pl.BlockSpec((tm, d), lambda i: (i, 0)),
        out_shape=jax.ShapeDtypeStruct((n, d), F32),
        compiler_params=_params(("arbitrary",)),
        name="ffn_ln",
    )(x, *ws)


def _in_proj_kernel(h_ref, wnat_ref, wt_ref, wq_ref, wkn_ref, wvt_ref, gq_ref, gkv_ref,
                    cosk_ref, sink_ref, cosq_ref, sinq_ref,
                    qtm_ref, km_ref, vtm_ref, qtd_ref, kd_ref, vtd_ref):
    tk = vtm_ref.shape[-1]
    half = MLA_ROPE // 2
    nqd = qtd_ref.shape[0]
    o1 = MLA_Q_RANK
    o2 = o1 + MLA_KV_RANK
    o3 = o2 + HEAD_PAD
    q_scale = (MLA_NOPE + MLA_ROPE) ** -0.5 * LOG2E
    for rows in _chains(h_ref.shape[0]):
        key_tiles = range(rows.start // tk, rows.stop // tk)
        hb = h_ref[rows, :].astype(BF16)

        pn = _dot(hb, wnat_ref[...])
        c_q, c_kv, krg, kd = pn[:, :o1], pn[:, o1:o2], pn[:, o2:o3], pn[:, o3:]
        for hh in range(DIFF_HEADS):
            kd_ref[hh, rows, :] = kd[:, hh * 2 * DIFF_D:(hh + 1) * 2 * DIFF_D].astype(BF16)

        pt = _dot_nt(wt_ref[...], hb)
        qtd_ref[:, rows] = (pt[:nqd] * (DIFF_D ** -0.5 * LOG2E)).astype(BF16)
        for c, t in enumerate(key_tiles):
            vtd_ref[t] = pt[nqd:, c * tk:(c + 1) * tk].astype(BF16)

        cqn = _rms_norm_rows(c_q, gq_ref[...]).astype(BF16)
        qt = _dot_nt(wq_ref[...], cqn)
        cos_q, sin_q = cosq_ref[:, rows], sinq_ref[:, rows]
        for hh in range(MLA_HEADS):
            lo = hh * HEAD_PAD
            r0 = lo + MLA_NOPE
            r1, r2 = r0 + half, r0 + MLA_ROPE
            x1, x2 = qt[r0:r1], qt[r1:r2]
            qtm_ref[lo:r0, rows] = (qt[lo:r0] * q_scale).astype(BF16)
            qtm_ref[r0:r1, rows] = ((x1 * cos_q - x2 * sin_q) * q_scale).astype(BF16)
            qtm_ref[r1:r2, rows] = ((x2 * cos_q + x1 * sin_q) * q_scale).astype(BF16)
            qtm_ref[r2:lo + HEAD_PAD, rows] = jnp.zeros((lo + HEAD_PAD - r2, CHAIN_ROWS), BF16)

        ckvn = _rms_norm_rows(c_kv, gkv_ref[...]).astype(BF16)
        kn = _dot(ckvn, wkn_ref[...])
        kr = krg * cosk_ref[rows, :] + pltpu.roll(krg, HEAD_PAD - MLA_ROPE, 1) * sink_ref[rows, :]
        for hh in range(MLA_HEADS):
            lo, hi = hh * HEAD_PAD, (hh + 1) * HEAD_PAD
            km_ref[hh, rows, :] = (kn[:, lo:hi] + kr).astype(BF16)
        vt = _dot_nt(wvt_ref[...], ckvn)
        for c, t in enumerate(key_tiles):
            vtm_ref[t] = vt[:, c * tk:(c + 1) * tk].astype(BF16)


def _in_proj_weights(w_in, w_q_b, w_kv_b):
    d = w_in.shape[0]
    o_cq = MLA_Q_RANK
    o_ckv = o_cq + MLA_KV_RANK
    o_kr = o_ckv + MLA_ROPE
    n_d = 2 * DIFF_HEADS * DIFF_D
    o_qd = o_kr + n_d
    o_kd = o_qd + n_d
    half = MLA_ROPE // 2
    pad = HEAD_PAD - MLA_NOPE - MLA_ROPE
    assert pad == MLA_ROPE
    kr = w_in[:, o_ckv:o_kr]
    krg = jnp.concatenate([jnp.zeros((d, MLA_NOPE), F32), kr, -kr[:, half:], kr[:, :half]], axis=1)
    w_nat = jnp.concatenate([w_in[:, :o_ckv], krg, w_in[:, o_qd:o_kd]], axis=1).astype(BF16)
    w_t = jnp.concatenate([w_in[:, o_kr:o_qd], w_in[:, o_kd:]], axis=1).T.astype(BF16)

    r = w_q_b.shape[0]
    wq = w_q_b.reshape(r, MLA_HEADS, MLA_NOPE + MLA_ROPE)
    wq = jnp.concatenate([wq, jnp.zeros((r, MLA_HEADS, pad), F32)], axis=-1)
    wq_t = wq.reshape(r, MLA_HEADS * HEAD_PAD).T.astype(BF16)

    rk = w_kv_b.shape[0]
    wkv = w_kv_b.reshape(rk, MLA_HEADS, MLA_NOPE + MLA_V)
    wkn = jnp.concatenate([wkv[..., :MLA_NOPE], jnp.zeros((rk, MLA_HEADS, HEAD_PAD - MLA_NOPE), F32)],
                          axis=-1).reshape(rk, MLA_HEADS * HEAD_PAD).astype(BF16)
    wv_t = wkv[..., MLA_NOPE:].reshape(rk, MLA_HEADS * MLA_V).T.astype(BF16)
    return w_nat, w_t, wq_t, wkn, wv_t


def _rope_tables(s):
    inv_freq = ROPE_BASE ** (-jnp.arange(0, MLA_ROPE, 2, dtype=F32) / MLA_ROPE)
    ang = jnp.arange(s).astype(F32)[:, None] * inv_freq[None, :]
    cos, sin = jnp.cos(ang), jnp.sin(ang)
    z_lo = jnp.zeros((s, MLA_NOPE), F32)
    z_hi = jnp.zeros((s, HEAD_PAD - MLA_NOPE - MLA_ROPE), F32)
    cos_k = jnp.concatenate([z_lo, cos, cos, z_hi], axis=1)
    sin_k = jnp.concatenate([z_lo, sin, sin, z_hi], axis=1)
    return cos_k, sin_k, cos.T, sin.T


def _in_proj(h, bsz, s, w_in, q_norm_g, kv_norm_g, w_q_b, w_kv_b):
    n, d = h.shape
    tm, tk = ROW_TILE, ATTN_TK
    assert s % tm == 0 and CHAIN_ROWS % tk == 0
    w_nat, w_t, wq_t, wkn, wv_t = _in_proj_weights(w_in, w_q_b, w_kv_b)
    cos_k, sin_k, cos_q, sin_q = _rope_tables(s)
    half = MLA_ROPE // 2
    spt = s // tm
    nqm = MLA_HEADS * HEAD_PAD
    nvm = MLA_HEADS * MLA_V
    nd = 2 * DIFF_HEADS * DIFF_D

    gq = q_norm_g.reshape(1, -1)
    gkv = kv_norm_g.reshape(1, -1)
    out_shape = (
        jax.ShapeDtypeStruct((bsz, nqm, s), BF16),
        jax.ShapeDtypeStruct((bsz, MLA_HEADS, s, HEAD_PAD), BF16),
        jax.ShapeDtypeStruct((bsz, s // tk, nvm, tk), BF16),
        jax.ShapeDtypeStruct((bsz, nd, s), BF16),
        jax.ShapeDtypeStruct((bsz, DIFF_HEADS, s, nd // DIFF_HEADS), BF16),
        jax.ShapeDtypeStruct((bsz, s // tk, nd, tk), BF16),
    )
    out_specs = (
        pl.BlockSpec((None, nqm, tm), lambda b, i: (b, 0, i)),
        pl.BlockSpec((None, MLA_HEADS, tm, HEAD_PAD), lambda b, i: (b, 0, i, 0)),
        pl.BlockSpec((None, tm // tk, nvm, tk), lambda b, i: (b, i, 0, 0)),
        pl.BlockSpec((None, nd, tm), lambda b, i: (b, 0, i)),
        pl.BlockSpec((None, DIFF_HEADS, tm, nd // DIFF_HEADS), lambda b, i: (b, 0, i, 0)),
        pl.BlockSpec((None, tm // tk, nd, tk), lambda b, i: (b, i, 0, 0)),
    )
    return pl.pallas_call(
        _in_proj_kernel,
        grid=(bsz, spt),
        in_specs=[
            pl.BlockSpec((tm, d), lambda b, i: (b * spt + i, 0)),
            _resident(w_nat), _resident(w_t), _resident(wq_t), _resident(wkn), _resident(wv_t),
            _resident(gq), _resident(gkv),
            pl.BlockSpec((tm, HEAD_PAD), lambda b, i: (i, 0)),
            pl.BlockSpec((tm, HEAD_PAD), lambda b, i: (i, 0)),
            pl.BlockSpec((half, tm), lambda b, i: (0, i)),
            pl.BlockSpec((half, tm), lambda b, i: (0, i)),
        ],
        out_specs=out_specs,
        out_shape=out_shape,
        compiler_params=_params(("arbitrary", "arbitrary")),
        name="in_proj",
    )(h, w_nat, w_t, wq_t, wkn, wv_t, gq, gkv, cos_k, sin_k, cos_q, sin_q)


BIAS_TILES = 3


def _rel_bias_kernel(rb_ref, o_ref, *, tq, tk):
    c = pl.program_id(0) - 1
    kk = lax.broadcasted_iota(jnp.int32, (tk, tq), 0)
    qq = lax.broadcasted_iota(jnp.int32, (tk, tq), 1)
    n = jnp.maximum(qq - kk - c * tk, 0)
    max_exact = REL_BUCKETS // 2
    nf = jnp.maximum(n, 1).astype(F32)
    large = max_exact + (jnp.log(nf / max_exact) / math.log(REL_MAX_DIST / max_exact)
                         * (REL_BUCKETS - max_exact)).astype(jnp.int32)
    large = jnp.minimum(large, REL_BUCKETS - 1)
    bucket = jnp.where(n < max_exact, n, large)
    for m in range(o_ref.shape[0]):
        acc = jnp.zeros((tk, tq), F32)
        for bkt in range(REL_BUCKETS):
            acc = jnp.where(bucket == bkt, rb_ref[bkt, m], acc)
        o_ref[m] = (acc - rb_ref[REL_BUCKETS - 1, m]) * LOG2E


def _rel_bias_tiles(rel_bias, tq, tk):
    assert tq == 2 * tk and tk + 1 >= REL_MAX_DIST
    nmaps = rel_bias.shape[1]
    return pl.pallas_call(
        functools.partial(_rel_bias_kernel, tq=tq, tk=tk),
        grid=(BIAS_TILES,),
        in_specs=[pl.BlockSpec(memory_space=pltpu.SMEM)],
        out_specs=pl.BlockSpec((nmaps, None, tk, tq), lambda t: (0, t, 0, 0)),
        out_shape=jax.ShapeDtypeStruct((nmaps, BIAS_TILES, tk, tq), F32),
        compiler_params=_params(("arbitrary",)),
        name="rel_bias",
    )(rel_bias)


SCORES_AHEAD = 3
VALUES_BEHIND = 1
ONES_ROWS = 16


def _flash_pipeline(q_fn, k_fn, vt_fn, logits_fn, finalize_fn, s_ref, mt_ref, p_ref, al_ref, m_ref, l_ref, acc_ref,
                    *, nstreams, nq, tq, tk):
    tiles = [(g, i, t, t - 2 * i) for g in range(nstreams) for i in range(nq) for t in range(2 * i + 2)]
    ones = jnp.ones((ONES_ROWS, tk), BF16)
    ahead = s_ref.shape[0] - 1
    behind = p_ref.shape[0] - 1

    def lane0(c):
        return tk if c == 1 else 0

    def scores(n):
        g, i, t, c = tiles[n]
        slot, lo = n % (ahead + 1), lane0(c)
        s = logits_fn(g, _dot(k_fn(g, t), q_fn(g, i, lo)), c, lo)
        s_ref[slot, :, lo:] = s
        mt_ref[slot, :, lo:] = jnp.max(s, axis=0, keepdims=True)

    def softmax(n):
        g, i, t, c = tiles[n]
        sslot, slot, lo = n % (ahead + 1), n % (behind + 1), lane0(c)
        m_new = mt_ref[sslot, :, lo:]
        if t > 0:
            m_prev = m_ref[:, lo:]
            m_new = jnp.maximum(m_prev, m_new)
            al_ref[slot, :, lo:] = jnp.exp2(m_prev - m_new)
        m_ref[:, lo:] = m_new
        p = jnp.exp2(s_ref[sslot, :, lo:] - m_new)
        p_ref[slot, :, lo:] = p.astype(BF16)
        if ONES_ROWS == 0:
            lslot = (g * nq + i) % 2
            l_new = jnp.sum(p, axis=0, keepdims=True)
            if t > 0:
                l_new = al_ref[slot, :, lo:] * l_ref[lslot, :, lo:] + l_new
            l_ref[lslot, :, lo:] = l_new

    def values(n):
        g, i, t, c = tiles[n]
        slot, lo = n % (behind + 1), lane0(c)
        vt = vt_fn(g, t)
        if ONES_ROWS:
            vt = jnp.concatenate([vt, ones], axis=0)
        pv = _dot(vt, p_ref[slot, :, lo:])
        if t > 0:
            pv = al_ref[slot, :, lo:] * acc_ref[:, lo:] + pv
        acc_ref[:, lo:] = pv
        if c == 1:
            acc = acc_ref[...]
            dv = acc.shape[0] - ONES_ROWS
            finalize_fn(g, i, acc[:dv] / (acc[dv:dv + 1] if ONES_ROWS else l_ref[(g * nq + i) % 2]))

    for n in range(ahead):
        scores(n)
    for n in range(len(tiles) + behind):
        if n + ahead < len(tiles):
            scores(n + ahead)
        if n < len(tiles):
            softmax(n)
        if behind <= n:
            values(n - behind)


def _tri_mask(s):
    kk = lax.broadcasted_iota(jnp.int32, s.shape, 0)
    qq = lax.broadcasted_iota(jnp.int32, s.shape, 1)
    return jnp.where(kk <= qq, s, -jnp.inf)


def _flash_scratch(dv, tq, tk):
    return [pltpu.VMEM((SCORES_AHEAD + 1, tk, tq), F32), pltpu.VMEM((SCORES_AHEAD + 1, 1, tq), F32),
            pltpu.VMEM((VALUES_BEHIND + 1, tk, tq), BF16), pltpu.VMEM((VALUES_BEHIND + 1, 1, tq), F32),
            pltpu.VMEM((1, tq), F32), pltpu.VMEM((2, 1, tq), F32), pltpu.VMEM((dv + ONES_ROWS, tq), F32)]


def _mla_attn_kernel(qt_ref, k_ref, vt_ref, o_ref, *scratch, heads, tq, tk):

    def q_fn(g, i, lo):
        return qt_ref[g * HEAD_PAD:(g + 1) * HEAD_PAD, i * tq + lo:(i + 1) * tq]

    def k_fn(g, t):
        return k_ref[g, t * tk:(t + 1) * tk, :]

    def vt_fn(g, t):
        return vt_ref[t, g * MLA_V:(g + 1) * MLA_V, :]

    def logits(g, s, c, lo):
        return _tri_mask(s) if c >= 0 else s

    def finalize(g, i, o):
        o_ref[g * MLA_V:(g + 1) * MLA_V, i * tq:(i + 1) * tq] = o.astype(BF16)

    _flash_pipeline(q_fn, k_fn, vt_fn, logits, finalize, *scratch,
                    nstreams=heads, nq=qt_ref.shape[1] // tq, tq=tq, tk=tk)


def _mla_attn(qt, k, vt, tq, tk):
    bsz, nq, s = qt.shape
    nkt = s // tk
    hps = MLA_HEADS_PER_STEP
    assert tq == 2 * tk and s % tq == 0 and MLA_HEADS % hps == 0
    return pl.pallas_call(
        functools.partial(_mla_attn_kernel, heads=hps, tq=tq, tk=tk),
        grid=(MLA_HEADS // hps, bsz),
        in_specs=[
            pl.BlockSpec((None, hps * HEAD_PAD, s), lambda h, b: (b, h, 0)),
            pl.BlockSpec((None, hps, s, HEAD_PAD), lambda h, b: (b, h, 0, 0)),
            pl.BlockSpec((None, nkt, hps * MLA_V, tk), lambda h, b: (b, 0, h, 0)),
        ],
        out_specs=pl.BlockSpec((None, hps * MLA_V, s), lambda h, b: (b, h, 0)),
        out_shape=jax.ShapeDtypeStruct((bsz, MLA_HEADS * MLA_V, s), BF16),
        scratch_shapes=_flash_scratch(MLA_V, tq, tk),
        compiler_params=_params(("arbitrary", "arbitrary")),
        name="mla_attn",
    )(qt, k, vt)


def _diff_attn_kernel(qt_ref, k_ref, vt_ref, bias_ref, lamv_ref, gn_ref, o_ref, o0_ref, qm_ref, *scratch, tq, tk):
    qt = qt_ref[...]
    row = lax.broadcasted_iota(jnp.int32, qt.shape, 0)
    for g in range(2):
        keep = jnp.logical_and(row >= g * DIFF_D, row < (g + 1) * DIFF_D)
        qm_ref[g] = jnp.where(keep, qt, jnp.zeros_like(qt))

    lv = lamv_ref[...]
    lam = (jnp.exp(jnp.sum(lv[0:1] * lv[1:2], axis=1, keepdims=True))
           - jnp.exp(jnp.sum(lv[2:3] * lv[3:4], axis=1, keepdims=True)) + LAM_INIT)

    def q_fn(g, i, lo):
        return qm_ref[g, :, i * tq + lo:(i + 1) * tq]

    def k_fn(g, t):
        return k_ref[t * tk:(t + 1) * tk, :]

    def vt_fn(g, t):
        return vt_ref[t]

    def logits(g, s, c, lo):
        if c >= -1:
            s = s + bias_ref[g, c + 1, :, lo:]
        return _tri_mask(s) if c >= 0 else s

    def finalize(g, i, o):
        cols = slice(i * tq, (i + 1) * tq)
        if g == 0:
            o0_ref[:, cols] = o
        else:
            d = o0_ref[:, cols] - lam * o
            d = d * lax.rsqrt(jnp.mean(d * d, axis=0, keepdims=True) + RMS_EPS) * gn_ref[...] * (1.0 - LAM_INIT)
            o_ref[:, cols] = d.astype(BF16)

    _flash_pipeline(q_fn, k_fn, vt_fn, logits, finalize, *scratch,
                    nstreams=2, nq=qt_ref.shape[1] // tq, tq=tq, tk=tk)


def _diff_attn(qt, k, vt, bias, lamv, gn, tq, tk):
    bsz, nd, s = qt.shape
    nkt = s // tk
    dv = 2 * DIFF_D
    assert tq == 2 * tk and s % tq == 0
    return pl.pallas_call(
        functools.partial(_diff_attn_kernel, tq=tq, tk=tk),
        grid=(DIFF_HEADS, bsz),
        in_specs=[
            pl.BlockSpec((None, dv, s), lambda h, b: (b, h, 0)),
            pl.BlockSpec((None, None, s, dv), lambda h, b: (b, h, 0, 0)),
            pl.BlockSpec((None, nkt, dv, tk), lambda h, b: (b, 0, h, 0)),
            pl.BlockSpec((2, BIAS_TILES, tk, tq), lambda h, b: (h, 0, 0, 0)),
            pl.BlockSpec(lamv.shape, lambda h, b: (0, 0)),
            pl.BlockSpec(gn.shape, lambda h, b: (0, 0)),
        ],
        out_specs=pl.BlockSpec((None, dv, s), lambda h, b: (b, h, 0)),
        out_shape=jax.ShapeDtypeStruct((bsz, nd, s), BF16),
        scratch_shapes=[pltpu.VMEM((dv, s), F32), pltpu.VMEM((2, dv, s), BF16)] + _flash_scratch(dv, tq, tk),
        compiler_params=_params(("arbitrary", "arbitrary")),
        name="diff_attn",
    )(qt, k, vt, bias, lamv, gn)


def _post_mixer_kernel(h_ref, omt_ref, odt_ref, wum_ref, wud_ref, wg_ref, bg_ref, wo_ref, g_ref, b_ref,
                       o_ref):
    d = h_ref.shape[1]
    for rows in _chains(h_ref.shape[0]):
        h = h_ref[rows, :]
        y_m = _dot_tn(omt_ref[:, rows], wum_ref[...])
        y_d = _dot_tn(odt_ref[:, rows], wud_ref[...])
        z = _dot(h.astype(BF16), wg_ref[...]) + bg_ref[...]
        gate = 1.0 / (1.0 + jnp.exp(-z))
        mixed = gate[:, :d] * y_m + gate[:, d:] * y_d
        mix = _dot(mixed.astype(BF16), wo_ref[...])
        o_ref[rows, :] = _layer_norm(ALPHA * h + mix, g_ref[...], b_ref[...])


def _post_mixer(h, omt, odt, bsz, s, w_up_mla, w_up_diff, w_gate, b_gate, w_o, g, b):
    n, d = h.shape
    tm = ROW_TILE
    spt = s // tm
    ws = [w_up_mla.astype(BF16), w_up_diff.astype(BF16), w_gate.astype(BF16), b_gate.reshape(1, -1),
          w_o.astype(BF16), g.reshape(1, d), b.reshape(1, d)]

    return pl.pallas_call(
        _post_mixer_kernel,
        grid=(bsz, spt),
        in_specs=[
            pl.BlockSpec((tm, d), lambda bb, i: (bb * spt + i, 0)),
            pl.BlockSpec((None, omt.shape[1], tm), lambda bb, i: (bb, 0, i)),
            pl.BlockSpec((None, odt.shape[1], tm), lambda bb, i: (bb, 0, i)),
        ] + [_resident(a) for a in ws],
        out_specs=pl.BlockSpec((tm, d), lambda bb, i: (bb * spt + i, 0)),
        out_shape=jax.ShapeDtypeStruct((n, d), F32),
        compiler_params=_params(("arbitrary", "arbitrary")),
        name="post_mixer",
    )(h, omt, odt, *ws)


def _mem_kv_kernel(mem_ref, w_ref, k_ref, v_ref):
    d = k_ref.shape[1]
    kv = _dot(mem_ref[...].astype(BF16), w_ref[...])
    k_ref[...] = kv[:, :d].astype(BF16)
    v_ref[...] = kv[:, d:].astype(BF16)


def _mem_kv(mem, w_kv):
    bsz, m, d = mem.shape
    w = w_kv.astype(BF16)
    spec = pl.BlockSpec((None, m, d), lambda b: (b, 0, 0))
    return pl.pallas_call(
        _mem_kv_kernel,
        grid=(bsz,),
        in_specs=[spec, pl.BlockSpec(w.shape, lambda b: (0, 0))],
        out_specs=(spec, spec),
        out_shape=(jax.ShapeDtypeStruct((bsz, m, d), BF16), jax.ShapeDtypeStruct((bsz, m, d), BF16)),
        compiler_params=_params(("arbitrary",)),
        name="mem_kv",
    )(mem, w)


def _mem_attn_kernel(h_ref, k_ref, v_ref, wq_ref, wo_ref, g_ref, b_ref, o_ref):
    d = h_ref.shape[1]
    hd = d // MEM_HEADS
    for rows in _chains(h_ref.shape[0], MEM_CHAIN_ROWS):
        h = h_ref[rows, :]
        q = _dot(h.astype(BF16), wq_ref[...]).astype(BF16)
        outs = []
        for hh in range(MEM_HEADS):
            lo, hi = hh * hd, (hh + 1) * hd
            s = _dot_nt(q[:, lo:hi], k_ref[:, lo:hi]) * (hd ** -0.5)
            p = jnp.exp(s - jnp.max(s, axis=-1, keepdims=True))
            p = p / jnp.sum(p, axis=-1, keepdims=True)
            outs.append(_dot(p.astype(BF16), v_ref[:, lo:hi]).astype(BF16))
        att = _dot(jnp.concatenate(outs, axis=1), wo_ref[...])
        o_ref[rows, :] = _layer_norm(ALPHA * h + att, g_ref[...], b_ref[...])


def _mem_attn(h, km, vm, bsz, s, w_q, w_o, g, b):
    n, d = h.shape
    tm = WIDE_ROW_TILE
    spt = s // tm
    m = km.shape[1]
    ws = [w_q.astype(BF16), w_o.astype(BF16), g.reshape(1, d), b.reshape(1, d)]

    return pl.pallas_call(
        _mem_attn_kernel,
        grid=(bsz, spt),
        in_specs=[
            pl.BlockSpec((tm, d), lambda bb, i: (bb * spt + i, 0)),
            pl.BlockSpec((None, m, d), lambda bb, i: (bb, 0, 0)),
            pl.BlockSpec((None, m, d), lambda bb, i: (bb, 0, 0)),
        ] + [_resident(a) for a in ws],
        out_specs=pl.BlockSpec((tm, d), lambda bb, i: (bb * spt + i, 0)),
        out_shape=jax.ShapeDtypeStruct((n, d), F32),
        compiler_params=_params(("arbitrary", "arbitrary")),
        name="mem_attn",
    )(h, km, vm, *ws)


def kernel(x, mem, rel_bias, ffn1_w1, ffn1_w3, ffn1_w2, ln1_g, ln1_b, w_in, q_norm_g, kv_norm_g, w_q_b, w_kv_b, lam_q1, lam_k1, lam_q2, lam_k2, diff_norm_g, w_gate, b_gate, w_up_mla, w_up_diff, w_o, ln2_g, ln2_b, mem_w_q, mem_w_kv, mem_w_o, ln3_g, ln3_b, ffn2_w1, ffn2_w3, ffn2_w2, ln4_g, ln4_b):
    bsz, s, d = x.shape
    assert ffn1_w1.shape[0] == DEPTH
    l = 0
    h = x.reshape(bsz * s, d)

    h = _ffn_ln(h, ffn1_w1[l], ffn1_w3[l], ffn1_w2[l], ln1_g[l], ln1_b[l])

    qtm, km, vtm, qtd, kd, vtd = _in_proj(h, bsz, s, w_in[l], q_norm_g[l], kv_norm_g[l], w_q_b[l], w_kv_b[l])
    bias = _rel_bias_tiles(rel_bias, ATTN_TQ, ATTN_TK)
    omt = _mla_attn(qtm, km, vtm, ATTN_TQ, ATTN_TK)
    lamv = jnp.stack([lam_q1[l], lam_k1[l], lam_q2[l], lam_k2[l]], axis=0)
    odt = _diff_attn(qtd, kd, vtd, bias, lamv, diff_norm_g[l].reshape(-1, 1), ATTN_TQ, ATTN_TK)
    h = _post_mixer(h, omt, odt, bsz, s, w_up_mla[l], w_up_diff[l], w_gate[l], b_gate[l], w_o[l],
                    ln2_g[l], ln2_b[l])

    kmem, vmem = _mem_kv(mem, mem_w_kv[l])
    h = _mem_attn(h, kmem, vmem, bsz, s, mem_w_q[l], mem_w_o[l], ln3_g[l], ln3_b[l])

    h = _ffn_ln(h, ffn2_w1[l], ffn2_w3[l], ffn2_w2[l], ln4_g[l], ln4_b[l])
    return h.reshape(bsz, s, d)
```

```python
import functools
import math

import jax
import jax.numpy as jnp
from jax import lax
from jax.experimental import pallas as pl
from jax.experimental.pallas import tpu as pltpu

F32 = jnp.float32
BF16 = jnp.bfloat16

DEPTH = 1
MLA_HEADS = 8
MLA_Q_RANK = 384
MLA_KV_RANK = 256
MLA_NOPE = 64
MLA_ROPE = 32
MLA_V = 64
ROPE_BASE = 10000.0
DIFF_HEADS = 4
DIFF_D = 64
REL_BUCKETS = 32
REL_MAX_DIST = 128
MEM_HEADS = 4
ALPHA = (2.0 * DEPTH) ** 0.25
LN_EPS = 1e-5
RMS_EPS = 1e-6
LAM_INIT = 0.8 - 0.6 * math.exp(-0.3 * 0)
LOG2E = math.log2(math.e)

HEAD_PAD = 128
VMEM_LIMIT_BYTES = 56 * 1024 * 1024

ROW_TILE = 1024
WIDE_ROW_TILE = 2048
CHAIN_ROWS = 256
MEM_CHAIN_ROWS = 512
ATTN_TQ = 512
ATTN_TK = 256
MLA_HEADS_PER_STEP = 4


def _params(semantics):
    return pltpu.CompilerParams(dimension_semantics=semantics, vmem_limit_bytes=VMEM_LIMIT_BYTES)


def _dot(a, b):
    return jnp.dot(a, b, preferred_element_type=F32)


def _dot_nt(a, b):
    return lax.dot_general(a, b, (((1,), (1,)), ((), ())), preferred_element_type=F32)


def _dot_tn(a, b):
    return lax.dot_general(a, b, (((0,), (0,)), ((), ())), preferred_element_type=F32)


def _layer_norm(y, g, b):
    mu = jnp.mean(y, axis=-1, keepdims=True)
    d = y - mu
    var = jnp.mean(d * d, axis=-1, keepdims=True)
    return d * lax.rsqrt(var + LN_EPS) * g + b


def _rms_norm_rows(x, g):
    return x * lax.rsqrt(jnp.mean(x * x, axis=-1, keepdims=True) + RMS_EPS) * g


def _chains(rows, chain=CHAIN_ROWS):
    return [slice(r, r + chain) for r in range(0, rows, chain)]


def _resident(a):
    return pl.BlockSpec(a.shape, lambda *_: (0,) * a.ndim, pipeline_mode=pl.Buffered(1))


def _ffn_ln_kernel(x_ref, w1_ref, w3_ref, w2_ref, g_ref, b_ref, o_ref):
    for rows in _chains(x_ref.shape[0]):
        x = x_ref[rows, :]
        xb = x.astype(BF16)
        u = _dot(xb, w1_ref[...])
        v = _dot(xb, w3_ref[...])
        act = (u / (1.0 + jnp.exp(-u))) * v
        y = ALPHA * x + 0.5 * _dot(act.astype(BF16), w2_ref[...])
        o_ref[rows, :] = _layer_norm(y, g_ref[...], b_ref[...])


def _ffn_ln(x, w1, w3, w2, g, b):
    n, d = x.shape
    tm = ROW_TILE
    assert n % tm == 0
    ws = [w1.astype(BF16), w3.astype(BF16), w2.astype(BF16), g.reshape(1, d), b.reshape(1, d)]
    return pl.pallas_call(
        _ffn_ln_kernel,
        grid=(n // tm,),
        in_specs=[pl.BlockSpec((tm, d), lambda i: (i, 0))] + [_resident(a) for a in ws],
        out_specs=pl.BlockSpec((tm, d), lambda i: (i, 0)),
        out_shape=jax.ShapeDtypeStruct((n, d), F32),
        compiler_params=_params(("arbitrary",)),
        name="ffn_ln",
    )(x, *ws)


def _in_proj_kernel(h_ref, wnat_ref, wt_ref, wq_ref, wkn_ref, wvt_ref, gq_ref, gkv_ref,
                    cosk_ref, sink_ref, cosq_ref, sinq_ref,
                    qtm_ref, km_ref, vtm_ref, qtd_ref, kd_ref, vtd_ref):
    tk = vtm_ref.shape[-1]
    half = MLA_ROPE // 2
    nqd = qtd_ref.shape[0]
    o1 = MLA_Q_RANK
    o2 = o1 + MLA_KV_RANK
    o3 = o2 + HEAD_PAD
    q_scale = (MLA_NOPE + MLA_ROPE) ** -0.5 * LOG2E
    for rows in _chains(h_ref.shape[0]):
        key_tiles = range(rows.start // tk, rows.stop // tk)
        hb = h_ref[rows, :].astype(BF16)

        pn = _dot(hb, wnat_ref[...])
        c_q, c_kv, krg, kd = pn[:, :o1], pn[:, o1:o2], pn[:, o2:o3], pn[:, o3:]
        for hh in range(DIFF_HEADS):
            kd_ref[hh, rows, :] = kd[:, hh * 2 * DIFF_D:(hh + 1) * 2 * DIFF_D].astype(BF16)

        pt = _dot_nt(wt_ref[...], hb)
        qtd_ref[:, rows] = (pt[:nqd] * (DIFF_D ** -0.5 * LOG2E)).astype(BF16)
        for c, t in enumerate(key_tiles):
            vtd_ref[t] = pt[nqd:, c * tk:(c + 1) * tk].astype(BF16)

        cqn = _rms_norm_rows(c_q, gq_ref[...]).astype(BF16)
        qt = _dot_nt(wq_ref[...], cqn)
        cos_q, sin_q = cosq_ref[:, rows], sinq_ref[:, rows]
        for hh in range(MLA_HEADS):
            lo = hh * HEAD_PAD
            r0 = lo + MLA_NOPE
            r1, r2 = r0 + half, r0 + MLA_ROPE
            x1, x2 = qt[r0:r1], qt[r1:r2]
            qtm_ref[lo:r0, rows] = (qt[lo:r0] * q_scale).astype(BF16)
            qtm_ref[r0:r1, rows] = ((x1 * cos_q - x2 * sin_q) * q_scale).astype(BF16)
            qtm_ref[r1:r2, rows] = ((x2 * cos_q + x1 * sin_q) * q_scale).astype(BF16)
            qtm_ref[r2:lo + HEAD_PAD, rows] = jnp.zeros((lo + HEAD_PAD - r2, CHAIN_ROWS), BF16)

        ckvn = _rms_norm_rows(c_kv, gkv_ref[...]).astype(BF16)
        kn = _dot(ckvn, wkn_ref[...])
        kr = krg * cosk_ref[rows, :] + pltpu.roll(krg, HEAD_PAD - MLA_ROPE, 1) * sink_ref[rows, :]
        for hh in range(MLA_HEADS):
            lo, hi = hh * HEAD_PAD, (hh + 1) * HEAD_PAD
            km_ref[rows, lo:hi] = (kn[:, lo:hi] + kr).astype(BF16)
        vt = _dot_nt(wvt_ref[...], ckvn)
        for c, t in enumerate(key_tiles):
            vtm_ref[t] = vt[:, c * tk:(c + 1) * tk].astype(BF16)


def _in_proj_weights(w_in, w_q_b, w_kv_b):
    d = w_in.shape[0]
    o_cq = MLA_Q_RANK
    o_ckv = o_cq + MLA_KV_RANK
    o_kr = o_ckv + MLA_ROPE
    n_d = 2 * DIFF_HEADS * DIFF_D
    o_qd = o_kr + n_d
    o_kd = o_qd + n_d
    half = MLA_ROPE // 2
    pad = HEAD_PAD - MLA_NOPE - MLA_ROPE
    assert pad == MLA_ROPE
    kr = w_in[:, o_ckv:o_kr]
    krg = jnp.concatenate([jnp.zeros((d, MLA_NOPE), F32), kr, -kr[:, half:], kr[:, :half]], axis=1)
    w_nat = jnp.concatenate([w_in[:, :o_ckv], krg, w_in[:, o_qd:o_kd]], axis=1).astype(BF16)
    w_t = jnp.concatenate([w_in[:, o_kr:o_qd], w_in[:, o_kd:]], axis=1).T.astype(BF16)

    r = w_q_b.shape[0]
    wq = w_q_b.reshape(r, MLA_HEADS, MLA_NOPE + MLA_ROPE)
    wq = jnp.concatenate([wq, jnp.zeros((r, MLA_HEADS, pad), F32)], axis=-1)
    wq_t = wq.reshape(r, MLA_HEADS * HEAD_PAD).T.astype(BF16)

    rk = w_kv_b.shape[0]
    wkv = w_kv_b.reshape(rk, MLA_HEADS, MLA_NOPE + MLA_V)
    wkn = jnp.concatenate([wkv[..., :MLA_NOPE], jnp.zeros((rk, MLA_HEADS, HEAD_PAD - MLA_NOPE), F32)],
                          axis=-1).reshape(rk, MLA_HEADS * HEAD_PAD).astype(BF16)
    wv_t = wkv[..., MLA_NOPE:].reshape(rk, MLA_HEADS * MLA_V).T.astype(BF16)
    return w_nat, w_t, wq_t, wkn, wv_t


def _rope_tables(s):
    inv_freq = ROPE_BASE ** (-jnp.arange(0, MLA_ROPE, 2, dtype=F32) / MLA_ROPE)
    ang = jnp.arange(s).astype(F32)[:, None] * inv_freq[None, :]
    cos, sin = jnp.cos(ang), jnp.sin(ang)
    z_lo = jnp.zeros((s, MLA_NOPE), F32)
    z_hi = jnp.zeros((s, HEAD_PAD - MLA_NOPE - MLA_ROPE), F32)
    cos_k = jnp.concatenate([z_lo, cos, cos, z_hi], axis=1)
    sin_k = jnp.concatenate([z_lo, sin, sin, z_hi], axis=1)
    return cos_k, sin_k, cos.T, sin.T


def _in_proj(h, bsz, s, w_in, q_norm_g, kv_norm_g, w_q_b, w_kv_b):
    n, d = h.shape
    tm, tk = ROW_TILE, ATTN_TK
    assert s % tm == 0 and CHAIN_ROWS % tk == 0
    w_nat, w_t, wq_t, wkn, wv_t = _in_proj_weights(w_in, w_q_b, w_kv_b)
    cos_k, sin_k, cos_q, sin_q = _rope_tables(s)
    half = MLA_ROPE // 2
    spt = s // tm
    nqm = MLA_HEADS * HEAD_PAD
    nvm = MLA_HEADS * MLA_V
    nd = 2 * DIFF_HEADS * DIFF_D

    gq = q_norm_g.reshape(1, -1)
    gkv = kv_norm_g.reshape(1, -1)
    out_shape = (
        jax.ShapeDtypeStruct((bsz, nqm, s), BF16),
        jax.ShapeDtypeStruct((bsz, s, nqm), BF16),
        jax.ShapeDtypeStruct((bsz, s // tk, nvm, tk), BF16),
        jax.ShapeDtypeStruct((bsz, nd, s), BF16),
        jax.ShapeDtypeStruct((bsz, DIFF_HEADS, s, nd // DIFF_HEADS), BF16),
        jax.ShapeDtypeStruct((bsz, s // tk, nd, tk), BF16),
    )
    out_specs = (
        pl.BlockSpec((None, nqm, tm), lambda b, i: (b, 0, i)),
        pl.BlockSpec((None, tm, nqm), lambda b, i: (b, i, 0)),
        pl.BlockSpec((None, tm // tk, nvm, tk), lambda b, i: (b, i, 0, 0)),
        pl.BlockSpec((None, nd, tm), lambda b, i: (b, 0, i)),
        pl.BlockSpec((None, DIFF_HEADS, tm, nd // DIFF_HEADS), lambda b, i: (b, 0, i, 0)),
        pl.BlockSpec((None, tm // tk, nd, tk), lambda b, i: (b, i, 0, 0)),
    )
    return pl.pallas_call(
        _in_proj_kernel,
        grid=(bsz, spt),
        in_specs=[
            pl.BlockSpec((tm, d), lambda b, i: (b * spt + i, 0)),
            _resident(w_nat), _resident(w_t), _resident(wq_t), _resident(wkn), _resident(wv_t),
            _resident(gq), _resident(gkv),
            pl.BlockSpec((tm, HEAD_PAD), lambda b, i: (i, 0)),
            pl.BlockSpec((tm, HEAD_PAD), lambda b, i: (i, 0)),
            pl.BlockSpec((half, tm), lambda b, i: (0, i)),
            pl.BlockSpec((half, tm), lambda b, i: (0, i)),
        ],
        out_specs=out_specs,
        out_shape=out_shape,
        compiler_params=_params(("arbitrary", "arbitrary")),
        name="in_proj",
    )(h, w_nat, w_t, wq_t, wkn, wv_t, gq, gkv, cos_k, sin_k, cos_q, sin_q)


BIAS_TILES = 3


def _rel_bias_kernel(rb_ref, o_ref, *, tq, tk):
    c = pl.program_id(0) - 1
    kk = lax.broadcasted_iota(jnp.int32, (tk, tq), 0)
    qq = lax.broadcasted_iota(jnp.int32, (tk, tq), 1)
    n = jnp.maximum(qq - kk - c * tk, 0)
    max_exact = REL_BUCKETS // 2
    nf = jnp.maximum(n, 1).astype(F32)
    large = max_exact + (jnp.log(nf / max_exact) / math.log(REL_MAX_DIST / max_exact)
                         * (REL_BUCKETS - max_exact)).astype(jnp.int32)
    large = jnp.minimum(large, REL_BUCKETS - 1)
    bucket = jnp.where(n < max_exact, n, large)
    for m in range(o_ref.shape[0]):
        acc = jnp.zeros((tk, tq), F32)
        for bkt in range(REL_BUCKETS):
            acc = jnp.where(bucket == bkt, rb_ref[bkt, m], acc)
        o_ref[m] = (acc - rb_ref[REL_BUCKETS - 1, m]) * LOG2E


def _rel_bias_tiles(rel_bias, tq, tk):
    assert tq == 2 * tk and tk + 1 >= REL_MAX_DIST
    nmaps = rel_bias.shape[1]
    return pl.pallas_call(
        functools.partial(_rel_bias_kernel, tq=tq, tk=tk),
        grid=(BIAS_TILES,),
        in_specs=[pl.BlockSpec(memory_space=pltpu.SMEM)],
        out_specs=pl.BlockSpec((nmaps, None, tk, tq), lambda t: (0, t, 0, 0)),
        out_shape=jax.ShapeDtypeStruct((nmaps, BIAS_TILES, tk, tq), F32),
        compiler_params=_params(("arbitrary",)),
        name="rel_bias",
    )(rel_bias)


SCORES_AHEAD = 3
VALUES_BEHIND = 1
ONES_ROWS = 16


def _flash_pipeline(q_fn, k_fn, vt_fn, logits_fn, finalize_fn, s_ref, mt_ref, p_ref, al_ref, m_ref, l_ref, acc_ref,
                    *, nstreams, nq, tq, tk):
    tiles = [(g, i, t, t - 2 * i) for g in range(nstreams) for i in range(nq) for t in range(2 * i + 2)]
    ones = jnp.ones((ONES_ROWS, tk), BF16)
    ahead = s_ref.shape[0] - 1
    behind = p_ref.shape[0] - 1

    def lane0(c):
        return tk if c == 1 else 0

    def scores(n):
        g, i, t, c = tiles[n]
        slot, lo = n % (ahead + 1), lane0(c)
        s = logits_fn(g, _dot(k_fn(g, t), q_fn(g, i, lo)), c, lo)
        s_ref[slot, :, lo:] = s
        mt_ref[slot, :, lo:] = jnp.max(s, axis=0, keepdims=True)

    def softmax(n):
        g, i, t, c = tiles[n]
        sslot, slot, lo = n % (ahead + 1), n % (behind + 1), lane0(c)
        m_new = mt_ref[sslot, :, lo:]
        if t > 0:
            m_prev = m_ref[:, lo:]
            m_new = jnp.maximum(m_prev, m_new)
            al_ref[slot, :, lo:] = jnp.exp2(m_prev - m_new)
        m_ref[:, lo:] = m_new
        p = jnp.exp2(s_ref[sslot, :, lo:] - m_new)
        p_ref[slot, :, lo:] = p.astype(BF16)
        if ONES_ROWS == 0:
            lslot = (g * nq + i) % 2
            l_new = jnp.sum(p, axis=0, keepdims=True)
            if t > 0:
                l_new = al_ref[slot, :, lo:] * l_ref[lslot, :, lo:] + l_new
            l_ref[lslot, :, lo:] = l_new

    def values(n):
        g, i, t, c = tiles[n]
        slot, lo = n % (behind + 1), lane0(c)
        vt = vt_fn(g, t)
        if ONES_ROWS:
            vt = jnp.concatenate([vt, ones], axis=0)
        pv = _dot(vt, p_ref[slot, :, lo:])
        if t > 0:
            pv = al_ref[slot, :, lo:] * acc_ref[:, lo:] + pv
        acc_ref[:, lo:] = pv
        if c == 1:
            acc = acc_ref[...]
            dv = acc.shape[0] - ONES_ROWS
            finalize_fn(g, i, acc[:dv] / (acc[dv:dv + 1] if ONES_ROWS else l_ref[(g * nq + i) % 2]))

    for n in range(ahead):
        scores(n)
    for n in range(len(tiles) + behind):
        if n + ahead < len(tiles):
            scores(n + ahead)
        if n < len(tiles):
            softmax(n)
        if behind <= n:
            values(n - behind)


def _tri_mask(s):
    kk = lax.broadcasted_iota(jnp.int32, s.shape, 0)
    qq = lax.broadcasted_iota(jnp.int32, s.shape, 1)
    return jnp.where(kk <= qq, s, -jnp.inf)


def _flash_scratch(dv, tq, tk):
    return [pltpu.VMEM((SCORES_AHEAD + 1, tk, tq), F32), pltpu.VMEM((SCORES_AHEAD + 1, 1, tq), F32),
            pltpu.VMEM((VALUES_BEHIND + 1, tk, tq), BF16), pltpu.VMEM((VALUES_BEHIND + 1, 1, tq), F32),
            pltpu.VMEM((1, tq), F32), pltpu.VMEM((2, 1, tq), F32), pltpu.VMEM((dv + ONES_ROWS, tq), F32)]


def _mla_attn_kernel(qt_ref, k_ref, vt_ref, o_ref, *scratch, heads, tq, tk):

    def q_fn(g, i, lo):
        return qt_ref[g * HEAD_PAD:(g + 1) * HEAD_PAD, i * tq + lo:(i + 1) * tq]

    def k_fn(g, t):
        return k_ref[t * tk:(t + 1) * tk, g * HEAD_PAD:(g + 1) * HEAD_PAD]

    def vt_fn(g, t):
        return vt_ref[t, g * MLA_V:(g + 1) * MLA_V, :]

    def logits(g, s, c, lo):
        return _tri_mask(s) if c >= 0 else s

    def finalize(g, i, o):
        o_ref[g * MLA_V:(g + 1) * MLA_V, i * tq:(i + 1) * tq] = o.astype(BF16)

    _flash_pipeline(q_fn, k_fn, vt_fn, logits, finalize, *scratch,
                    nstreams=heads, nq=qt_ref.shape[1] // tq, tq=tq, tk=tk)


def _mla_attn(qt, k, vt, tq, tk):
    bsz, nq, s = qt.shape
    nkt = s // tk
    hps = MLA_HEADS_PER_STEP
    assert tq == 2 * tk and s % tq == 0 and MLA_HEADS % hps == 0
    return pl.pallas_call(
        functools.partial(_mla_attn_kernel, heads=hps, tq=tq, tk=tk),
        grid=(MLA_HEADS // hps, bsz),
        in_specs=[
            pl.BlockSpec((None, hps * HEAD_PAD, s), lambda h, b: (b, h, 0)),
            pl.BlockSpec((None, s, hps * HEAD_PAD), lambda h, b: (b, 0, h)),
            pl.BlockSpec((None, nkt, hps * MLA_V, tk), lambda h, b: (b, 0, h, 0)),
        ],
        out_specs=pl.BlockSpec((None, hps * MLA_V, s), lambda h, b: (b, h, 0)),
        out_shape=jax.ShapeDtypeStruct((bsz, MLA_HEADS * MLA_V, s), BF16),
        scratch_shapes=_flash_scratch(MLA_V, tq, tk),
        compiler_params=_params(("arbitrary", "arbitrary")),
        name="mla_attn",
    )(qt, k, vt)


def _diff_attn_kernel(qt_ref, k_ref, vt_ref, bias_ref, lamv_ref, gn_ref, o_ref, o0_ref, qm_ref, *scratch, tq, tk):
    qt = qt_ref[...]
    row = lax.broadcasted_iota(jnp.int32, qt.shape, 0)
    for g in range(2):
        keep = jnp.logical_and(row >= g * DIFF_D, row < (g + 1) * DIFF_D)
        qm_ref[g] = jnp.where(keep, qt, jnp.zeros_like(qt))

    lv = lamv_ref[...]
    lam = (jnp.exp(jnp.sum(lv[0:1] * lv[1:2], axis=1, keepdims=True))
           - jnp.exp(jnp.sum(lv[2:3] * lv[3:4], axis=1, keepdims=True)) + LAM_INIT)

    def q_fn(g, i, lo):
        return qm_ref[g, :, i * tq + lo:(i + 1) * tq]

    def k_fn(g, t):
        return k_ref[t * tk:(t + 1) * tk, :]

    def vt_fn(g, t):
        return vt_ref[t]

    def logits(g, s, c, lo):
        if c >= -1:
            s = s + bias_ref[g, c + 1, :, lo:]
        return _tri_mask(s) if c >= 0 else s

    def finalize(g, i, o):
        cols = slice(i * tq, (i + 1) * tq)
        if g == 0:
            o0_ref[:, cols] = o
        else:
            d = o0_ref[:, cols] - lam * o
            d = d * lax.rsqrt(jnp.mean(d * d, axis=0, keepdims=True) + RMS_EPS) * gn_ref[...] * (1.0 - LAM_INIT)
            o_ref[:, cols] = d.astype(BF16)

    _flash_pipeline(q_fn, k_fn, vt_fn, logits, finalize, *scratch,
                    nstreams=2, nq=qt_ref.shape[1] // tq, tq=tq, tk=tk)


def _diff_attn(qt, k, vt, bias, lamv, gn, tq, tk):
    bsz, nd, s = qt.shape
    nkt = s // tk
    dv = 2 * DIFF_D
    assert tq == 2 * tk and s % tq == 0
    return pl.pallas_call(
        functools.partial(_diff_attn_kernel, tq=tq, tk=tk),
        grid=(DIFF_HEADS, bsz),
        in_specs=[
            pl.BlockSpec((None, dv, s), lambda h, b: (b, h, 0)),
            pl.BlockSpec((None, None, s, dv), lambda h, b: (b, h, 0, 0)),
            pl.BlockSpec((None, nkt, dv, tk), lambda h, b: (b, 0, h, 0)),
            pl.BlockSpec((2, BIAS_TILES, tk, tq), lambda h, b: (h, 0, 0, 0)),
            pl.BlockSpec(lamv.shape, lambda h, b: (0, 0)),
            pl.BlockSpec(gn.shape, lambda h, b: (0, 0)),
        ],
        out_specs=pl.BlockSpec((None, dv, s), lambda h, b: (b, h, 0)),
        out_shape=jax.ShapeDtypeStruct((bsz, nd, s), BF16),
        scratch_shapes=[pltpu.VMEM((dv, s), F32), pltpu.VMEM((2, dv, s), BF16)] + _flash_scratch(dv, tq, tk),
        compiler_params=_params(("arbitrary", "arbitrary")),
        name="diff_attn",
    )(qt, k, vt, bias, lamv, gn)


def _post_mixer_kernel(h_ref, omt_ref, odt_ref, wum_ref, wud_ref, wg_ref, bg_ref, wo_ref, g_ref, b_ref,
                       o_ref):
    d = h_ref.shape[1]
    for rows in _chains(h_ref.shape[0]):
        h = h_ref[rows, :]
        y_m = _dot_tn(omt_ref[:, rows], wum_ref[...])
        y_d = _dot_tn(odt_ref[:, rows], wud_ref[...])
        z = _dot(h.astype(BF16), wg_ref[...]) + bg_ref[...]
        gate = 1.0 / (1.0 + jnp.exp(-z))
        mixed = gate[:, :d] * y_m + gate[:, d:] * y_d
        mix = _dot(mixed.astype(BF16), wo_ref[...])
        o_ref[rows, :] = _layer_norm(ALPHA * h + mix, g_ref[...], b_ref[...])


def _post_mixer(h, omt, odt, bsz, s, w_up_mla, w_up_diff, w_gate, b_gate, w_o, g, b):
    n, d = h.shape
    tm = ROW_TILE
    spt = s // tm
    ws = [w_up_mla.astype(BF16), w_up_diff.astype(BF16), w_gate.astype(BF16), b_gate.reshape(1, -1),
          w_o.astype(BF16), g.reshape(1, d), b.reshape(1, d)]

    return pl.pallas_call(
        _post_mixer_kernel,
        grid=(bsz, spt),
        in_specs=[
            pl.BlockSpec((tm, d), lambda bb, i: (bb * spt + i, 0)),
            pl.BlockSpec((None, omt.shape[1], tm), lambda bb, i: (bb, 0, i)),
            pl.BlockSpec((None, odt.shape[1], tm), lambda bb, i: (bb, 0, i)),
        ] + [_resident(a) for a in ws],
        out_specs=pl.BlockSpec((tm, d), lambda bb, i: (bb * spt + i, 0)),
        out_shape=jax.ShapeDtypeStruct((n, d), F32),
        compiler_params=_params(("arbitrary", "arbitrary")),
        name="post_mixer",
    )(h, omt, odt, *ws)


def _mem_kv_kernel(mem_ref, w_ref, k_ref, v_ref):
    d = k_ref.shape[1]
    kv = _dot(mem_ref[...].astype(BF16), w_ref[...])
    k_ref[...] = kv[:, :d].astype(BF16)
    v_ref[...] = kv[:, d:].astype(BF16)


def _mem_kv(mem, w_kv):
    bsz, m, d = mem.shape
    w = w_kv.astype(BF16)
    spec = pl.BlockSpec((None, m, d), lambda b: (b, 0, 0))
    return pl.pallas_call(
        _mem_kv_kernel,
        grid=(bsz,),
        in_specs=[spec, pl.BlockSpec(w.shape, lambda b: (0, 0))],
        out_specs=(spec, spec),
        out_shape=(jax.ShapeDtypeStruct((bsz, m, d), BF16), jax.ShapeDtypeStruct((bsz, m, d), BF16)),
        compiler_params=_params(("arbitrary",)),
        name="mem_kv",
    )(mem, w)


def _mem_attn_kernel(h_ref, k_ref, v_ref, wq_ref, wo_ref, g_ref, b_ref, o_ref):
    d = h_ref.shape[1]
    hd = d // MEM_HEADS
    for rows in _chains(h_ref.shape[0], MEM_CHAIN_ROWS):
        h = h_ref[rows, :]
        q = _dot(h.astype(BF16), wq_ref[...]).astype(BF16)
        outs = []
        for hh in range(MEM_HEADS):
            lo, hi = hh * hd, (hh + 1) * hd
            s = _dot_nt(q[:, lo:hi], k_ref[:, lo:hi]) * (hd ** -0.5)
            p = jnp.exp(s - jnp.max(s, axis=-1, keepdims=True))
            p = p / jnp.sum(p, axis=-1, keepdims=True)
            outs.append(_dot(p.astype(BF16), v_ref[:, lo:hi]).astype(BF16))
        att = _dot(jnp.concatenate(outs, axis=1), wo_ref[...])
        o_ref[rows, :] = _layer_norm(ALPHA * h + att, g_ref[...], b_ref[...])


def _mem_attn(h, km, vm, bsz, s, w_q, w_o, g, b):
    n, d = h.shape
    tm = WIDE_ROW_TILE
    spt = s // tm
    m = km.shape[1]
    ws = [w_q.astype(BF16), w_o.astype(BF16), g.reshape(1, d), b.reshape(1, d)]

    return pl.pallas_call(
        _mem_attn_kernel,
        grid=(bsz, spt),
        in_specs=[
            pl.BlockSpec((tm, d), lambda bb, i: (bb * spt + i, 0)),
            pl.BlockSpec((None, m, d), lambda bb, i: (bb, 0, 0)),
            pl.BlockSpec((None, m, d), lambda bb, i: (bb, 0, 0)),
        ] + [_resident(a) for a in ws],
        out_specs=pl.BlockSpec((tm, d), lambda bb, i: (bb * spt + i, 0)),
        out_shape=jax.ShapeDtypeStruct((n, d), F32),
        compiler_params=_params(("arbitrary", "arbitrary")),
        name="mem_attn",
    )(h, km, vm, *ws)


def kernel(x, mem, rel_bias, ffn1_w1, ffn1_w3, ffn1_w2, ln1_g, ln1_b, w_in, q_norm_g, kv_norm_g, w_q_b, w_kv_b, lam_q1, lam_k1, lam_q2, lam_k2, diff_norm_g, w_gate, b_gate, w_up_mla, w_up_diff, w_o, ln2_g, ln2_b, mem_w_q, mem_w_kv, mem_w_o, ln3_g, ln3_b, ffn2_w1, ffn2_w3, ffn2_w2, ln4_g, ln4_b):
    bsz, s, d = x.shape
    assert ffn1_w1.shape[0] == DEPTH
    l = 0
    h = x.reshape(bsz * s, d)

    h = _ffn_ln(h, ffn1_w1[l], ffn1_w3[l], ffn1_w2[l], ln1_g[l], ln1_b[l])

    qtm, km, vtm, qtd, kd, vtd = _in_proj(h, bsz, s, w_in[l], q_norm_g[l], kv_norm_g[l], w_q_b[l], w_kv_b[l])
    bias = _rel_bias_tiles(rel_bias, ATTN_TQ, ATTN_TK)
    omt = _mla_attn(qtm, km, vtm, ATTN_TQ, ATTN_TK)
    lamv = jnp.stack([lam_q1[l], lam_k1[l], lam_q2[l], lam_k2[l]], axis=0)
    odt = _diff_attn(qtd, kd, vtd, bias, lamv, diff_norm_g[l].reshape(-1, 1), ATTN_TQ, ATTN_TK)
    h = _post_mixer(h, omt, odt, bsz, s, w_up_mla[l], w_up_diff[l], w_gate[l], b_gate[l], w_o[l],
                    ln2_g[l], ln2_b[l])

    kmem, vmem = _mem_kv(mem, mem_w_kv[l])
    h = _mem_attn(h, kmem, vmem, bsz, s, mem_w_q[l], mem_w_o[l], ln3_g[l], ln3_b[l])

    h = _ffn_ln(h, ffn2_w1[l], ffn2_w3[l], ffn2_w2[l], ln4_g[l], ln4_b[l])
    return h.reshape(bsz, s, d)
```

```python
import functools
import math

import jax
import jax.numpy as jnp
from jax import lax
from jax.experimental import pallas as pl
from jax.experimental.pallas import tpu as pltpu

F32 = jnp.float32
BF16 = jnp.bfloat16

DEPTH = 1
MLA_HEADS = 8
MLA_Q_RANK = 384
MLA_KV_RANK = 256
MLA_NOPE = 64
MLA_ROPE = 32
MLA_V = 64
ROPE_BASE = 10000.0
DIFF_HEADS = 4
DIFF_D = 64
REL_BUCKETS = 32
REL_MAX_DIST = 128
MEM_HEADS = 4
ALPHA = (2.0 * DEPTH) ** 0.25
LN_EPS = 1e-5
RMS_EPS = 1e-6
LAM_INIT = 0.8 - 0.6 * math.exp(-0.3 * 0)
LOG2E = math.log2(math.e)

HEAD_PAD = 128
VMEM_LIMIT_BYTES = 56 * 1024 * 1024

ROW_TILE = 1024
WIDE_ROW_TILE = 2048
CHAIN_ROWS = 256
MEM_CHAIN_ROWS = 512
ATTN_TQ = 512
ATTN_TK = 256
MLA_HEADS_PER_STEP = 4


def _params(semantics):
    return pltpu.CompilerParams(dimension_semantics=semantics, vmem_limit_bytes=VMEM_LIMIT_BYTES)


def _dot(a, b):
    return jnp.dot(a, b, preferred_element_type=F32)


def _dot_nt(a, b):
    return lax.dot_general(a, b, (((1,), (1,)), ((), ())), preferred_element_type=F32)


def _dot_tn(a, b):
    return lax.dot_general(a, b, (((0,), (0,)), ((), ())), preferred_element_type=F32)


def _layer_norm(y, g, b):
    mu = jnp.mean(y, axis=-1, keepdims=True)
    d = y - mu
    var = jnp.mean(d * d, axis=-1, keepdims=True)
    return d * lax.rsqrt(var + LN_EPS) * g + b


def _rms_norm_rows(x, g):
    return x * lax.rsqrt(jnp.mean(x * x, axis=-1, keepdims=True) + RMS_EPS) * g


def _chains(rows, chain=CHAIN_ROWS):
    return [slice(r, r + chain) for r in range(0, rows, chain)]


def _resident(a):
    return pl.BlockSpec(a.shape, lambda *_: (0,) * a.ndim, pipeline_mode=pl.Buffered(1))


def _ffn_ln_kernel(x_ref, w1_ref, w3_ref, w2_ref, g_ref, b_ref, o_ref):
    for rows in _chains(x_ref.shape[0]):
        x = x_ref[rows, :]
        xb = x.astype(BF16)
        u = _dot(xb, w1_ref[...])
        v = _dot(xb, w3_ref[...])
        act = (u / (1.0 + jnp.exp(-u))) * v
        y = ALPHA * x + 0.5 * _dot(act.astype(BF16), w2_ref[...])
        o_ref[rows, :] = _layer_norm(y, g_ref[...], b_ref[...])


def _ffn_ln(x, w1, w3, w2, g, b):
    n, d = x.shape
    tm = ROW_TILE
    assert n % tm == 0
    ws = [w1.astype(BF16), w3.astype(BF16), w2.astype(BF16), g.reshape(1, d), b.reshape(1, d)]
    return pl.pallas_call(
        _ffn_ln_kernel,
        grid=(n // tm,),
        in_specs=[pl.BlockSpec((tm, d), lambda i: (i, 0))] + [_resident(a) for a in ws],
        out_specs=pl.BlockSpec((tm, d), lambda i: (i, 0)),
        out_shape=jax.ShapeDtypeStruct((n, d), F32),
        compiler_params=_params(("arbitrary",)),
        name="ffn_ln",
    )(x, *ws)


def _in_proj_kernel(h_ref, wnat_ref, wt_ref, wq_ref, wkn_ref, wvt_ref, gq_ref, gkv_ref,
                    cosk_ref, sink_ref, cosq_ref, sinq_ref,
                    qtm_ref, km_ref, vtm_ref, qtd_ref, kd_ref, vtd_ref):
    tk = vtm_ref.shape[-1]
    half = MLA_ROPE // 2
    nqd = qtd_ref.shape[0]
    o1 = MLA_Q_RANK
    o2 = o1 + MLA_KV_RANK
    o3 = o2 + HEAD_PAD
    q_scale = (MLA_NOPE + MLA_ROPE) ** -0.5 * LOG2E
    for rows in _chains(h_ref.shape[0]):
        key_tiles = range(rows.start // tk, rows.stop // tk)
        hb = h_ref[rows, :].astype(BF16)

        pn = _dot(hb, wnat_ref[...])
        c_q, c_kv, krg, kd = pn[:, :o1], pn[:, o1:o2], pn[:, o2:o3], pn[:, o3:]
        for hh in range(DIFF_HEADS):
            kd_ref[hh, rows, :] = kd[:, hh * 2 * DIFF_D:(hh + 1) * 2 * DIFF_D].astype(BF16)

        pt = _dot_nt(wt_ref[...], hb)
        qtd_ref[:, rows] = (pt[:nqd] * (DIFF_D ** -0.5 * LOG2E)).astype(BF16)
        for c, t in enumerate(key_tiles):
            vtd_ref[t] = pt[nqd:, c * tk:(c + 1) * tk].astype(BF16)

        cqn = _rms_norm_rows(c_q, gq_ref[...]).astype(BF16)
        qt = _dot_nt(wq_ref[...], cqn)
        cos_q, sin_q = cosq_ref[:, rows], sinq_ref[:, rows]
        for hh in range(MLA_HEADS):
            lo = hh * HEAD_PAD
            r0 = lo + MLA_NOPE
            r1, r2 = r0 + half, r0 + MLA_ROPE
            x1, x2 = qt[r0:r1], qt[r1:r2]
            qtm_ref[lo:r0, rows] = (qt[lo:r0] * q_scale).astype(BF16)
            qtm_ref[r0:r1, rows] = ((x1 * cos_q - x2 * sin_q) * q_scale).astype(BF16)
            qtm_ref[r1:r2, rows] = ((x2 * cos_q + x1 * sin_q) * q_scale).astype(BF16)
            qtm_ref[r2:lo + HEAD_PAD, rows] = jnp.zeros((lo + HEAD_PAD - r2, CHAIN_ROWS), BF16)

        ckvn = _rms_norm_rows(c_kv, gkv_ref[...]).astype(BF16)
        kn = _dot(ckvn, wkn_ref[...])
        kr = krg * cosk_ref[rows, :] + pltpu.roll(krg, HEAD_PAD - MLA_ROPE, 1) * sink_ref[rows, :]
        for hh in range(MLA_HEADS):
            lo, hi = hh * HEAD_PAD, (hh + 1) * HEAD_PAD
            km_ref[rows, lo:hi] = (kn[:, lo:hi] + kr).astype(BF16)
        vt = _dot_nt(wvt_ref[...], ckvn)
        for c, t in enumerate(key_tiles):
            vtm_ref[t] = vt[:, c * tk:(c + 1) * tk].astype(BF16)


def _in_proj_weights(w_in, w_q_b, w_kv_b):
    d = w_in.shape[0]
    o_cq = MLA_Q_RANK
    o_ckv = o_cq + MLA_KV_RANK
    o_kr = o_ckv + MLA_ROPE
    n_d = 2 * DIFF_HEADS * DIFF_D
    o_qd = o_kr + n_d
    o_kd = o_qd + n_d
    half = MLA_ROPE // 2
    pad = HEAD_PAD - MLA_NOPE - MLA_ROPE
    assert pad == MLA_ROPE
    kr = w_in[:, o_ckv:o_kr]
    krg = jnp.concatenate([jnp.zeros((d, MLA_NOPE), F32), kr, -kr[:, half:], kr[:, :half]], axis=1)
    w_nat = jnp.concatenate([w_in[:, :o_ckv], krg, w_in[:, o_qd:o_kd]], axis=1).astype(BF16)
    w_t = jnp.concatenate([w_in[:, o_kr:o_qd], w_in[:, o_kd:]], axis=1).T.astype(BF16)

    r = w_q_b.shape[0]
    wq = w_q_b.reshape(r, MLA_HEADS, MLA_NOPE + MLA_ROPE)
    wq = jnp.concatenate([wq, jnp.zeros((r, MLA_HEADS, pad), F32)], axis=-1)
    wq_t = wq.reshape(r, MLA_HEADS * HEAD_PAD).T.astype(BF16)

    rk = w_kv_b.shape[0]
    wkv = w_kv_b.reshape(rk, MLA_HEADS, MLA_NOPE + MLA_V)
    wkn = jnp.concatenate([wkv[..., :MLA_NOPE], jnp.zeros((rk, MLA_HEADS, HEAD_PAD - MLA_NOPE), F32)],
                          axis=-1).reshape(rk, MLA_HEADS * HEAD_PAD).astype(BF16)
    wv_t = wkv[..., MLA_NOPE:].reshape(rk, MLA_HEADS * MLA_V).T.astype(BF16)
    return w_nat, w_t, wq_t, wkn, wv_t


def _rope_tables(s):
    inv_freq = ROPE_BASE ** (-jnp.arange(0, MLA_ROPE, 2, dtype=F32) / MLA_ROPE)
    ang = jnp.arange(s).astype(F32)[:, None] * inv_freq[None, :]
    cos, sin = jnp.cos(ang), jnp.sin(ang)
    z_lo = jnp.zeros((s, MLA_NOPE), F32)
    z_hi = jnp.zeros((s, HEAD_PAD - MLA_NOPE - MLA_ROPE), F32)
    cos_k = jnp.concatenate([z_lo, cos, cos, z_hi], axis=1)
    sin_k = jnp.concatenate([z_lo, sin, sin, z_hi], axis=1)
    return cos_k, sin_k, cos.T, sin.T


def _in_proj(h, bsz, s, w_in, q_norm_g, kv_norm_g, w_q_b, w_kv_b):
    n, d = h.shape
    tm, tk = ROW_TILE, ATTN_TK
    assert s % tm == 0 and CHAIN_ROWS % tk == 0
    w_nat, w_t, wq_t, wkn, wv_t = _in_proj_weights(w_in, w_q_b, w_kv_b)
    cos_k, sin_k, cos_q, sin_q = _rope_tables(s)
    half = MLA_ROPE // 2
    spt = s // tm
    nqm = MLA_HEADS * HEAD_PAD
    nvm = MLA_HEADS * MLA_V
    nd = 2 * DIFF_HEADS * DIFF_D

    gq = q_norm_g.reshape(1, -1)
    gkv = kv_norm_g.reshape(1, -1)
    out_shape = (
        jax.ShapeDtypeStruct((bsz, nqm, s), BF16),
        jax.ShapeDtypeStruct((bsz, s, nqm), BF16),
        jax.ShapeDtypeStruct((bsz, s // tk, nvm, tk), BF16),
        jax.ShapeDtypeStruct((bsz, nd, s), BF16),
        jax.ShapeDtypeStruct((bsz, DIFF_HEADS, s, nd // DIFF_HEADS), BF16),
        jax.ShapeDtypeStruct((bsz, s // tk, nd, tk), BF16),
    )
    out_specs = (
        pl.BlockSpec((None, nqm, tm), lambda b, i: (b, 0, i)),
        pl.BlockSpec((None, tm, nqm), lambda b, i: (b, i, 0)),
        pl.BlockSpec((None, tm // tk, nvm, tk), lambda b, i: (b, i, 0, 0)),
        pl.BlockSpec((None, nd, tm), lambda b, i: (b, 0, i)),
        pl.BlockSpec((None, DIFF_HEADS, tm, nd // DIFF_HEADS), lambda b, i: (b, 0, i, 0)),
        pl.BlockSpec((None, tm // tk, nd, tk), lambda b, i: (b, i, 0, 0)),
    )
    return pl.pallas_call(
        _in_proj_kernel,
        grid=(bsz, spt),
        in_specs=[
            pl.BlockSpec((tm, d), lambda b, i: (b * spt + i, 0)),
            _resident(w_nat), _resident(w_t), _resident(wq_t), _resident(wkn), _resident(wv_t),
            _resident(gq), _resident(gkv),
            pl.BlockSpec((tm, HEAD_PAD), lambda b, i: (i, 0)),
            pl.BlockSpec((tm, HEAD_PAD), lambda b, i: (i, 0)),
            pl.BlockSpec((half, tm), lambda b, i: (0, i)),
            pl.BlockSpec((half, tm), lambda b, i: (0, i)),
        ],
        out_specs=out_specs,
        out_shape=out_shape,
        compiler_params=_params(("arbitrary", "arbitrary")),
        name="in_proj",
    )(h, w_nat, w_t, wq_t, wkn, wv_t, gq, gkv, cos_k, sin_k, cos_q, sin_q)


BIAS_TILES = 3


def _rel_bias_kernel(rb_ref, o_ref, *, tq, tk):
    c = pl.program_id(0) - 1
    kk = lax.broadcasted_iota(jnp.int32, (tk, tq), 0)
    qq = lax.broadcasted_iota(jnp.int32, (tk, tq), 1)
    n = jnp.maximum(qq - kk - c * tk, 0)
    max_exact = REL_BUCKETS // 2
    nf = jnp.maximum(n, 1).astype(F32)
    large = max_exact + (jnp.log(nf / max_exact) / math.log(REL_MAX_DIST / max_exact)
                         * (REL_BUCKETS - max_exact)).astype(jnp.int32)
    large = jnp.minimum(large, REL_BUCKETS - 1)
    bucket = jnp.where(n < max_exact, n, large)
    for m in range(o_ref.shape[0]):
        acc = jnp.zeros((tk, tq), F32)
        for bkt in range(REL_BUCKETS):
            acc = jnp.where(bucket == bkt, rb_ref[bkt, m], acc)
        o_ref[m] = (acc - rb_ref[REL_BUCKETS - 1, m]) * LOG2E


def _rel_bias_tiles(rel_bias, tq, tk):
    assert tq == 2 * tk and tk + 1 >= REL_MAX_DIST
    nmaps = rel_bias.shape[1]
    return pl.pallas_call(
        functools.partial(_rel_bias_kernel, tq=tq, tk=tk),
        grid=(BIAS_TILES,),
        in_specs=[pl.BlockSpec(memory_space=pltpu.SMEM)],
        out_specs=pl.BlockSpec((nmaps, None, tk, tq), lambda t: (0, t, 0, 0)),
        out_shape=jax.ShapeDtypeStruct((nmaps, BIAS_TILES, tk, tq), F32),
        compiler_params=_params(("arbitrary",)),
        name="rel_bias",
    )(rel_bias)


SCORES_AHEAD = 3
VALUES_BEHIND = 1
ONES_ROWS = 16


def _flash_pipeline(q_fn, k_fn, vt_fn, logits_fn, finalize_fn, s_ref, mt_ref, p_ref, al_ref, m_ref, l_ref, acc_ref,
                    *, nstreams, nq, tq, tk, values_first):
    tiles = [(g, i, t, t - 2 * i) for g in range(nstreams) for i in range(nq) for t in range(2 * i + 2)]
    ones = jnp.ones((ONES_ROWS, tk), BF16)
    ahead = s_ref.shape[0] - 1
    behind = p_ref.shape[0] - 1

    def lane0(c):
        return tk if c == 1 else 0

    def scores(n):
        g, i, t, c = tiles[n]
        slot, lo = n % (ahead + 1), lane0(c)
        s = logits_fn(g, _dot(k_fn(g, t), q_fn(g, i, lo)), c, lo)
        s_ref[slot, :, lo:] = s
        mt_ref[slot, :, lo:] = jnp.max(s, axis=0, keepdims=True)

    def softmax(n):
        g, i, t, c = tiles[n]
        sslot, slot, lo = n % (ahead + 1), n % (behind + 1), lane0(c)
        m_new = mt_ref[sslot, :, lo:]
        if t > 0:
            m_prev = m_ref[:, lo:]
            m_new = jnp.maximum(m_prev, m_new)
            al_ref[slot, :, lo:] = jnp.exp2(m_prev - m_new)
        m_ref[:, lo:] = m_new
        p = jnp.exp2(s_ref[sslot, :, lo:] - m_new)
        p_ref[slot, :, lo:] = p.astype(BF16)
        if ONES_ROWS == 0:
            lslot = (g * nq + i) % 2
            l_new = jnp.sum(p, axis=0, keepdims=True)
            if t > 0:
                l_new = al_ref[slot, :, lo:] * l_ref[lslot, :, lo:] + l_new
            l_ref[lslot, :, lo:] = l_new

    def values(n):
        g, i, t, c = tiles[n]
        slot, lo = n % (behind + 1), lane0(c)
        vt = vt_fn(g, t)
        if ONES_ROWS:
            vt = jnp.concatenate([vt, ones], axis=0)
        pv = _dot(vt, p_ref[slot, :, lo:])
        if t > 0:
            pv = al_ref[slot, :, lo:] * acc_ref[:, lo:] + pv
        acc_ref[:, lo:] = pv
        if c == 1:
            acc = acc_ref[...]
            dv = acc.shape[0] - ONES_ROWS
            finalize_fn(g, i, acc[:dv] / (acc[dv:dv + 1] if ONES_ROWS else l_ref[(g * nq + i) % 2]))

    for n in range(ahead):
        scores(n)
    for n in range(len(tiles) + behind):
        if values_first and behind <= n:
            values(n - behind)
        if n + ahead < len(tiles):
            scores(n + ahead)
        if n < len(tiles):
            softmax(n)
        if not values_first and behind <= n:
            values(n - behind)


def _tri_mask(s):
    kk = lax.broadcasted_iota(jnp.int32, s.shape, 0)
    qq = lax.broadcasted_iota(jnp.int32, s.shape, 1)
    return jnp.where(kk <= qq, s, -jnp.inf)


def _flash_scratch(dv, tq, tk):
    return [pltpu.VMEM((SCORES_AHEAD + 1, tk, tq), F32), pltpu.VMEM((SCORES_AHEAD + 1, 1, tq), F32),
            pltpu.VMEM((VALUES_BEHIND + 1, tk, tq), BF16), pltpu.VMEM((VALUES_BEHIND + 1, 1, tq), F32),
            pltpu.VMEM((1, tq), F32), pltpu.VMEM((2, 1, tq), F32), pltpu.VMEM((dv + ONES_ROWS, tq), F32)]


def _mla_attn_kernel(qt_ref, k_ref, vt_ref, o_ref, *scratch, heads, tq, tk):

    def q_fn(g, i, lo):
        return qt_ref[g * HEAD_PAD:(g + 1) * HEAD_PAD, i * tq + lo:(i + 1) * tq]

    def k_fn(g, t):
        return k_ref[t * tk:(t + 1) * tk, g * HEAD_PAD:(g + 1) * HEAD_PAD]

    def vt_fn(g, t):
        return vt_ref[t, g * MLA_V:(g + 1) * MLA_V, :]

    def logits(g, s, c, lo):
        return _tri_mask(s) if c >= 0 else s

    def finalize(g, i, o):
        o_ref[g * MLA_V:(g + 1) * MLA_V, i * tq:(i + 1) * tq] = o.astype(BF16)

    _flash_pipeline(q_fn, k_fn, vt_fn, logits, finalize, *scratch,
                    nstreams=heads, nq=qt_ref.shape[1] // tq, tq=tq, tk=tk, values_first=True)


def _mla_attn(qt, k, vt, tq, tk):
    bsz, nq, s = qt.shape
    nkt = s // tk
    hps = MLA_HEADS_PER_STEP
    assert tq == 2 * tk and s % tq == 0 and MLA_HEADS % hps == 0
    return pl.pallas_call(
        functools.partial(_mla_attn_kernel, heads=hps, tq=tq, tk=tk),
        grid=(MLA_HEADS // hps, bsz),
        in_specs=[
            pl.BlockSpec((None, hps * HEAD_PAD, s), lambda h, b: (b, h, 0)),
            pl.BlockSpec((None, s, hps * HEAD_PAD), lambda h, b: (b, 0, h)),
            pl.BlockSpec((None, nkt, hps * MLA_V, tk), lambda h, b: (b, 0, h, 0)),
        ],
        out_specs=pl.BlockSpec((None, hps * MLA_V, s), lambda h, b: (b, h, 0)),
        out_shape=jax.ShapeDtypeStruct((bsz, MLA_HEADS * MLA_V, s), BF16),
        scratch_shapes=_flash_scratch(MLA_V, tq, tk),
        compiler_params=_params(("arbitrary", "arbitrary")),
        name="mla_attn",
    )(qt, k, vt)


def _diff_attn_kernel(qt_ref, k_ref, vt_ref, bias_ref, lamv_ref, gn_ref, o_ref, o0_ref, qm_ref, *scratch, tq, tk):
    qt = qt_ref[...]
    row = lax.broadcasted_iota(jnp.int32, qt.shape, 0)
    for g in range(2):
        keep = jnp.logical_and(row >= g * DIFF_D, row < (g + 1) * DIFF_D)
        qm_ref[g] = jnp.where(keep, qt, jnp.zeros_like(qt))

    lv = lamv_ref[...]
    lam = (jnp.exp(jnp.sum(lv[0:1] * lv[1:2], axis=1, keepdims=True))
           - jnp.exp(jnp.sum(lv[2:3] * lv[3:4], axis=1, keepdims=True)) + LAM_INIT)

    def q_fn(g, i, lo):
        return qm_ref[g, :, i * tq + lo:(i + 1) * tq]

    def k_fn(g, t):
        return k_ref[t * tk:(t + 1) * tk, :]

    def vt_fn(g, t):
        return vt_ref[t]

    def logits(g, s, c, lo):
        if c >= -1:
            s = s + bias_ref[g, c + 1, :, lo:]
        return _tri_mask(s) if c >= 0 else s

    def finalize(g, i, o):
        cols = slice(i * tq, (i + 1) * tq)
        if g == 0:
            o0_ref[:, cols] = o
        else:
            d = o0_ref[:, cols] - lam * o
            d = d * lax.rsqrt(jnp.mean(d * d, axis=0, keepdims=True) + RMS_EPS) * gn_ref[...] * (1.0 - LAM_INIT)
            o_ref[:, cols] = d.astype(BF16)

    _flash_pipeline(q_fn, k_fn, vt_fn, logits, finalize, *scratch,
                    nstreams=2, nq=qt_ref.shape[1] // tq, tq=tq, tk=tk, values_first=False)


def _diff_attn(qt, k, vt, bias, lamv, gn, tq, tk):
    bsz, nd, s = qt.shape
    nkt = s // tk
    dv = 2 * DIFF_D
    assert tq == 2 * tk and s % tq == 0
    return pl.pallas_call(
        functools.partial(_diff_attn_kernel, tq=tq, tk=tk),
        grid=(DIFF_HEADS, bsz),
        in_specs=[
            pl.BlockSpec((None, dv, s), lambda h, b: (b, h, 0)),
            pl.BlockSpec((None, None, s, dv), lambda h, b: (b, h, 0, 0)),
            pl.BlockSpec((None, nkt, dv, tk), lambda h, b: (b, 0, h, 0)),
            pl.BlockSpec((2, BIAS_TILES, tk, tq), lambda h, b: (h, 0, 0, 0)),
            pl.BlockSpec(lamv.shape, lambda h, b: (0, 0)),
            pl.BlockSpec(gn.shape, lambda h, b: (0, 0)),
        ],
        out_specs=pl.BlockSpec((None, dv, s), lambda h, b: (b, h, 0)),
        out_shape=jax.ShapeDtypeStruct((bsz, nd, s), BF16),
        scratch_shapes=[pltpu.VMEM((dv, s), F32), pltpu.VMEM((2, dv, s), BF16)] + _flash_scratch(dv, tq, tk),
        compiler_params=_params(("arbitrary", "arbitrary")),
        name="diff_attn",
    )(qt, k, vt, bias, lamv, gn)


def _post_mixer_kernel(h_ref, omt_ref, odt_ref, wum_ref, wud_ref, wg_ref, bg_ref, wo_ref, g_ref, b_ref,
                       o_ref):
    d = h_ref.shape[1]
    for rows in _chains(h_ref.shape[0]):
        h = h_ref[rows, :]
        y_m = _dot_tn(omt_ref[:, rows], wum_ref[...])
        y_d = _dot_tn(odt_ref[:, rows], wud_ref[...])
        z = _dot(h.astype(BF16), wg_ref[...]) + bg_ref[...]
        gate = 1.0 / (1.0 + jnp.exp(-z))
        mixed = gate[:, :d] * y_m + gate[:, d:] * y_d
        mix = _dot(mixed.astype(BF16), wo_ref[...])
        o_ref[rows, :] = _layer_norm(ALPHA * h + mix, g_ref[...], b_ref[...])


def _post_mixer(h, omt, odt, bsz, s, w_up_mla, w_up_diff, w_gate, b_gate, w_o, g, b):
    n, d = h.shape
    tm = ROW_TILE
    spt = s // tm
    ws = [w_up_mla.astype(BF16), w_up_diff.astype(BF16), w_gate.astype(BF16), b_gate.reshape(1, -1),
          w_o.astype(BF16), g.reshape(1, d), b.reshape(1, d)]

    return pl.pallas_call(
        _post_mixer_kernel,
        grid=(bsz, spt),
        in_specs=[
            pl.BlockSpec((tm, d), lambda bb, i: (bb * spt + i, 0)),
            pl.BlockSpec((None, omt.shape[1], tm), lambda bb, i: (bb, 0, i)),
            pl.BlockSpec((None, odt.shape[1], tm), lambda bb, i: (bb, 0, i)),
        ] + [_resident(a) for a in ws],
        out_specs=pl.BlockSpec((tm, d), lambda bb, i: (bb * spt + i, 0)),
        out_shape=jax.ShapeDtypeStruct((n, d), F32),
        compiler_params=_params(("arbitrary", "arbitrary")),
        name="post_mixer",
    )(h, omt, odt, *ws)


def _mem_kv_kernel(mem_ref, w_ref, k_ref, v_ref):
    d = k_ref.shape[1]
    kv = _dot(mem_ref[...].astype(BF16), w_ref[...])
    k_ref[...] = kv[:, :d].astype(BF16)
    v_ref[...] = kv[:, d:].astype(BF16)


def _mem_kv(mem, w_kv):
    bsz, m, d = mem.shape
    w = w_kv.astype(BF16)
    spec = pl.BlockSpec((None, m, d), lambda b: (b, 0, 0))
    return pl.pallas_call(
        _mem_kv_kernel,
        grid=(bsz,),
        in_specs=[spec, pl.BlockSpec(w.shape, lambda b: (0, 0))],
        out_specs=(spec, spec),
        out_shape=(jax.ShapeDtypeStruct((bsz, m, d), BF16), jax.ShapeDtypeStruct((bsz, m, d), BF16)),
        compiler_params=_params(("arbitrary",)),
        name="mem_kv",
    )(mem, w)


def _mem_attn_kernel(h_ref, k_ref, v_ref, wq_ref, wo_ref, g_ref, b_ref, o_ref):
    d = h_ref.shape[1]
    hd = d // MEM_HEADS
    for rows in _chains(h_ref.shape[0], MEM_CHAIN_ROWS):
        h = h_ref[rows, :]
        q = _dot(h.astype(BF16), wq_ref[...]).astype(BF16)
        outs = []
        for hh in range(MEM_HEADS):
            lo, hi = hh * hd, (hh + 1) * hd
            s = _dot_nt(q[:, lo:hi], k_ref[:, lo:hi]) * (hd ** -0.5)
            p = jnp.exp(s - jnp.max(s, axis=-1, keepdims=True))
            p = p / jnp.sum(p, axis=-1, keepdims=True)
            outs.append(_dot(p.astype(BF16), v_ref[:, lo:hi]).astype(BF16))
        att = _dot(jnp.concatenate(outs, axis=1), wo_ref[...])
        o_ref[rows, :] = _layer_norm(ALPHA * h + att, g_ref[...], b_ref[...])


def _mem_attn(h, km, vm, bsz, s, w_q, w_o, g, b):
    n, d = h.shape
    tm = WIDE_ROW_TILE
    spt = s // tm
    m = km.shape[1]
    ws = [w_q.astype(BF16), w_o.astype(BF16), g.reshape(1, d), b.reshape(1, d)]

    return pl.pallas_call(
        _mem_attn_kernel,
        grid=(bsz, spt),
        in_specs=[
            pl.BlockSpec((tm, d), lambda bb, i: (bb * spt + i, 0)),
            pl.BlockSpec((None, m, d), lambda bb, i: (bb, 0, 0)),
            pl.BlockSpec((None, m, d), lambda bb, i: (bb, 0, 0)),
        ] + [_resident(a) for a in ws],
        out_specs=pl.BlockSpec((tm, d), lambda bb, i: (bb * spt + i, 0)),
        out_shape=jax.ShapeDtypeStruct((n, d), F32),
        compiler_params=_params(("arbitrary", "arbitrary")),
        name="mem_attn",
    )(h, km, vm, *ws)


def kernel(x, mem, rel_bias, ffn1_w1, ffn1_w3, ffn1_w2, ln1_g, ln1_b, w_in, q_norm_g, kv_norm_g, w_q_b, w_kv_b, lam_q1, lam_k1, lam_q2, lam_k2, diff_norm_g, w_gate, b_gate, w_up_mla, w_up_diff, w_o, ln2_g, ln2_b, mem_w_q, mem_w_kv, mem_w_o, ln3_g, ln3_b, ffn2_w1, ffn2_w3, ffn2_w2, ln4_g, ln4_b):
    bsz, s, d = x.shape
    assert ffn1_w1.shape[0] == DEPTH
    l = 0
    h = x.reshape(bsz * s, d)

    h = _ffn_ln(h, ffn1_w1[l], ffn1_w3[l], ffn1_w2[l], ln1_g[l], ln1_b[l])

    qtm, km, vtm, qtd, kd, vtd = _in_proj(h, bsz, s, w_in[l], q_norm_g[l], kv_norm_g[l], w_q_b[l], w_kv_b[l])
    bias = _rel_bias_tiles(rel_bias, ATTN_TQ, ATTN_TK)
    omt = _mla_attn(qtm, km, vtm, ATTN_TQ, ATTN_TK)
    lamv = jnp.stack([lam_q1[l], lam_k1[l], lam_q2[l], lam_k2[l]], axis=0)
    odt = _diff_attn(qtd, kd, vtd, bias, lamv, diff_norm_g[l].reshape(-1, 1), ATTN_TQ, ATTN_TK)
    h = _post_mixer(h, omt, odt, bsz, s, w_up_mla[l], w_up_diff[l], w_gate[l], b_gate[l], w_o[l],
                    ln2_g[l], ln2_b[l])

    kmem, vmem = _mem_kv(mem, mem_w_kv[l])
    h = _mem_attn(h, kmem, vmem, bsz, s, mem_w_q[l], mem_w_o[l], ln3_g[l], ln3_b[l])

    h = _ffn_ln(h, ffn2_w1[l], ffn2_w3[l], ffn2_w2[l], ln4_g[l], ln4_b[l])
    return h.reshape(bsz, s, d)
```

```python
import functools
import math

import jax
import jax.numpy as jnp
from jax import lax
from jax.experimental import pallas as pl
from jax.experimental.pallas import tpu as pltpu

F32 = jnp.float32
BF16 = jnp.bfloat16

DEPTH = 1
MLA_HEADS = 8
MLA_Q_RANK = 384
MLA_KV_RANK = 256
MLA_NOPE = 64
MLA_ROPE = 32
MLA_V = 64
ROPE_BASE = 10000.0
DIFF_HEADS = 4
DIFF_D = 64
REL_BUCKETS = 32
REL_MAX_DIST = 128
MEM_HEADS = 4
ALPHA = (2.0 * DEPTH) ** 0.25
LN_EPS = 1e-5
RMS_EPS = 1e-6
LAM_INIT = 0.8 - 0.6 * math.exp(-0.3 * 0)
LOG2E = math.log2(math.e)

HEAD_PAD = 128
VMEM_LIMIT_BYTES = 56 * 1024 * 1024

ROW_TILE = 1024
WIDE_ROW_TILE = 2048
CHAIN_ROWS = 256
MEM_CHAIN_ROWS = 512
ATTN_TQ = 512
ATTN_TK = 256
MLA_HEADS_PER_STEP = 4


def _params(semantics):
    return pltpu.CompilerParams(dimension_semantics=semantics, vmem_limit_bytes=VMEM_LIMIT_BYTES)


def _dot(a, b):
    return jnp.dot(a, b, preferred_element_type=F32)


def _dot_nt(a, b):
    return lax.dot_general(a, b, (((1,), (1,)), ((), ())), preferred_element_type=F32)


def _dot_tn(a, b):
    return lax.dot_general(a, b, (((0,), (0,)), ((), ())), preferred_element_type=F32)


def _layer_norm(y, g, b):
    mu = jnp.mean(y, axis=-1, keepdims=True)
    d = y - mu
    var = jnp.mean(d * d, axis=-1, keepdims=True)
    return d * lax.rsqrt(var + LN_EPS) * g + b


def _rms_norm_rows(x, g):
    return x * lax.rsqrt(jnp.mean(x * x, axis=-1, keepdims=True) + RMS_EPS) * g


def _chains(rows, chain=CHAIN_ROWS):
    return [slice(r, r + chain) for r in range(0, rows, chain)]


def _resident(a):
    return pl.BlockSpec(a.shape, lambda *_: (0,) * a.ndim, pipeline_mode=pl.Buffered(1))


def _ffn_ln_kernel(x_ref, w1_ref, w3_ref, w2_ref, g_ref, b_ref, o_ref):
    for rows in _chains(x_ref.shape[0]):
        x = x_ref[rows, :]
        xb = x.astype(BF16)
        u = _dot(xb, w1_ref[...])
        v = _dot(xb, w3_ref[...])
        act = (u / (1.0 + jnp.exp(-u))) * v
        y = ALPHA * x + 0.5 * _dot(act.astype(BF16), w2_ref[...])
        o_ref[rows, :] = _layer_norm(y, g_ref[...], b_ref[...])


def _ffn_ln(x, w1, w3, w2, g, b):
    n, d = x.shape
    tm = ROW_TILE
    assert n % tm == 0
    ws = [w1.astype(BF16), w3.astype(BF16), w2.astype(BF16), g.reshape(1, d), b.reshape(1, d)]
    return pl.pallas_call(
        _ffn_ln_kernel,
        grid=(n // tm,),
        in_specs=[pl.BlockSpec((tm, d), lambda i: (i, 0))] + [_resident(a) for a in ws],
        out_specs=pl.BlockSpec((tm, d), lambda i: (i, 0)),
        out_shape=jax.ShapeDtypeStruct((n, d), F32),
        compiler_params=_params(("arbitrary",)),
        name="ffn_ln",
    )(x, *ws)


def _in_proj_kernel(h_ref, wnat_ref, wt_ref, wq_ref, wkn_ref, wvt_ref, gq_ref, gkv_ref,
                    cosk_ref, sink_ref, cosq_ref, sinq_ref,
                    qtm_ref, km_ref, vtm_ref, qtd_ref, kd_ref, vtd_ref):
    tk = vtm_ref.shape[-1]
    half = MLA_ROPE // 2
    nqd = qtd_ref.shape[0]
    o1 = MLA_Q_RANK
    o2 = o1 + MLA_KV_RANK
    o3 = o2 + HEAD_PAD
    q_scale = (MLA_NOPE + MLA_ROPE) ** -0.5 * LOG2E
    for rows in _chains(h_ref.shape[0]):
        key_tiles = range(rows.start // tk, rows.stop // tk)
        hb = h_ref[rows, :].astype(BF16)

        pn = _dot(hb, wnat_ref[...])
        c_q, c_kv, krg, kd = pn[:, :o1], pn[:, o1:o2], pn[:, o2:o3], pn[:, o3:]
        for hh in range(DIFF_HEADS):
            kd_ref[hh, rows, :] = kd[:, hh * 2 * DIFF_D:(hh + 1) * 2 * DIFF_D].astype(BF16)

        pt = _dot_nt(wt_ref[...], hb)
        qtd_ref[:, rows] = (pt[:nqd] * (DIFF_D ** -0.5 * LOG2E)).astype(BF16)
        for c, t in enumerate(key_tiles):
            vtd_ref[t] = pt[nqd:, c * tk:(c + 1) * tk].astype(BF16)

        cqn = _rms_norm_rows(c_q, gq_ref[...]).astype(BF16)
        qt = _dot_nt(wq_ref[...], cqn)
        cos_q, sin_q = cosq_ref[:, rows], sinq_ref[:, rows]
        for hh in range(MLA_HEADS):
            lo = hh * HEAD_PAD
            r0 = lo + MLA_NOPE
            r1, r2 = r0 + half, r0 + MLA_ROPE
            x1, x2 = qt[r0:r1], qt[r1:r2]
            qtm_ref[lo:r0, rows] = (qt[lo:r0] * q_scale).astype(BF16)
            qtm_ref[r0:r1, rows] = ((x1 * cos_q - x2 * sin_q) * q_scale).astype(BF16)
            qtm_ref[r1:r2, rows] = ((x2 * cos_q + x1 * sin_q) * q_scale).astype(BF16)
            qtm_ref[r2:lo + HEAD_PAD, rows] = jnp.zeros((lo + HEAD_PAD - r2, CHAIN_ROWS), BF16)

        ckvn = _rms_norm_rows(c_kv, gkv_ref[...]).astype(BF16)
        kn = _dot(ckvn, wkn_ref[...])
        kr = krg * cosk_ref[rows, :] + pltpu.roll(krg, HEAD_PAD - MLA_ROPE, 1) * sink_ref[rows, :]
        for hh in range(MLA_HEADS):
            lo, hi = hh * HEAD_PAD, (hh + 1) * HEAD_PAD
            km_ref[rows, lo:hi] = (kn[:, lo:hi] + kr).astype(BF16)
        vt = _dot_nt(wvt_ref[...], ckvn)
        for c, t in enumerate(key_tiles):
            vtm_ref[t] = vt[:, c * tk:(c + 1) * tk].astype(BF16)


def _in_proj_weights(w_in, w_q_b, w_kv_b):
    d = w_in.shape[0]
    o_cq = MLA_Q_RANK
    o_ckv = o_cq + MLA_KV_RANK
    o_kr = o_ckv + MLA_ROPE
    n_d = 2 * DIFF_HEADS * DIFF_D
    o_qd = o_kr + n_d
    o_kd = o_qd + n_d
    half = MLA_ROPE // 2
    pad = HEAD_PAD - MLA_NOPE - MLA_ROPE
    assert pad == MLA_ROPE
    kr = w_in[:, o_ckv:o_kr]
    krg = jnp.concatenate([jnp.zeros((d, MLA_NOPE), F32), kr, -kr[:, half:], kr[:, :half]], axis=1)
    w_nat = jnp.concatenate([w_in[:, :o_ckv], krg, w_in[:, o_qd:o_kd]], axis=1).astype(BF16)
    w_t = jnp.concatenate([w_in[:, o_kr:o_qd], w_in[:, o_kd:]], axis=1).T.astype(BF16)

    r = w_q_b.shape[0]
    wq = w_q_b.reshape(r, MLA_HEADS, MLA_NOPE + MLA_ROPE)
    wq = jnp.concatenate([wq, jnp.zeros((r, MLA_HEADS, pad), F32)], axis=-1)
    wq_t = wq.reshape(r, MLA_HEADS * HEAD_PAD).T.astype(BF16)

    rk = w_kv_b.shape[0]
    wkv = w_kv_b.reshape(rk, MLA_HEADS, MLA_NOPE + MLA_V)
    wkn = jnp.concatenate([wkv[..., :MLA_NOPE], jnp.zeros((rk, MLA_HEADS, HEAD_PAD - MLA_NOPE), F32)],
                          axis=-1).reshape(rk, MLA_HEADS * HEAD_PAD).astype(BF16)
    wv_t = wkv[..., MLA_NOPE:].reshape(rk, MLA_HEADS * MLA_V).T.astype(BF16)
    return w_nat, w_t, wq_t, wkn, wv_t


def _rope_tables(s):
    inv_freq = ROPE_BASE ** (-jnp.arange(0, MLA_ROPE, 2, dtype=F32) / MLA_ROPE)
    ang = jnp.arange(s).astype(F32)[:, None] * inv_freq[None, :]
    cos, sin = jnp.cos(ang), jnp.sin(ang)
    z_lo = jnp.zeros((s, MLA_NOPE), F32)
    z_hi = jnp.zeros((s, HEAD_PAD - MLA_NOPE - MLA_ROPE), F32)
    cos_k = jnp.concatenate([z_lo, cos, cos, z_hi], axis=1)
    sin_k = jnp.concatenate([z_lo, sin, sin, z_hi], axis=1)
    return cos_k, sin_k, cos.T, sin.T


def _in_proj(h, bsz, s, w_in, q_norm_g, kv_norm_g, w_q_b, w_kv_b):
    n, d = h.shape
    tm, tk = ROW_TILE, ATTN_TK
    assert s % tm == 0 and CHAIN_ROWS % tk == 0
    w_nat, w_t, wq_t, wkn, wv_t = _in_proj_weights(w_in, w_q_b, w_kv_b)
    cos_k, sin_k, cos_q, sin_q = _rope_tables(s)
    half = MLA_ROPE // 2
    spt = s // tm
    nqm = MLA_HEADS * HEAD_PAD
    nvm = MLA_HEADS * MLA_V
    nd = 2 * DIFF_HEADS * DIFF_D

    gq = q_norm_g.reshape(1, -1)
    gkv = kv_norm_g.reshape(1, -1)
    out_shape = (
        jax.ShapeDtypeStruct((bsz, nqm, s), BF16),
        jax.ShapeDtypeStruct((bsz, s, nqm), BF16),
        jax.ShapeDtypeStruct((bsz, s // tk, nvm, tk), BF16),
        jax.ShapeDtypeStruct((bsz, nd, s), BF16),
        jax.ShapeDtypeStruct((bsz, DIFF_HEADS, s, nd // DIFF_HEADS), BF16),
        jax.ShapeDtypeStruct((bsz, s // tk, nd, tk), BF16),
    )
    out_specs = (
        pl.BlockSpec((None, nqm, tm), lambda b, i: (b, 0, i)),
        pl.BlockSpec((None, tm, nqm), lambda b, i: (b, i, 0)),
        pl.BlockSpec((None, tm // tk, nvm, tk), lambda b, i: (b, i, 0, 0)),
        pl.BlockSpec((None, nd, tm), lambda b, i: (b, 0, i)),
        pl.BlockSpec((None, DIFF_HEADS, tm, nd // DIFF_HEADS), lambda b, i: (b, 0, i, 0)),
        pl.BlockSpec((None, tm // tk, nd, tk), lambda b, i: (b, i, 0, 0)),
    )
    return pl.pallas_call(
        _in_proj_kernel,
        grid=(bsz, spt),
        in_specs=[
            pl.BlockSpec((tm, d), lambda b, i: (b * spt + i, 0)),
            _resident(w_nat), _resident(w_t), _resident(wq_t), _resident(wkn), _resident(wv_t),
            _resident(gq), _resident(gkv),
            pl.BlockSpec((tm, HEAD_PAD), lambda b, i: (i, 0)),
            pl.BlockSpec((tm, HEAD_PAD), lambda b, i: (i, 0)),
            pl.BlockSpec((half, tm), lambda b, i: (0, i)),
            pl.BlockSpec((half, tm), lambda b, i: (0, i)),
        ],
        out_specs=out_specs,
        out_shape=out_shape,
        compiler_params=_params(("arbitrary", "arbitrary")),
        name="in_proj",
    )(h, w_nat, w_t, wq_t, wkn, wv_t, gq, gkv, cos_k, sin_k, cos_q, sin_q)


BIAS_TILES = 3


def _rel_bias_kernel(rb_ref, o_ref, *, tq, tk):
    c = pl.program_id(0) - 1
    kk = lax.broadcasted_iota(jnp.int32, (tk, tq), 0)
    qq = lax.broadcasted_iota(jnp.int32, (tk, tq), 1)
    n = jnp.maximum(qq - kk - c * tk, 0)
    max_exact = REL_BUCKETS // 2
    nf = jnp.maximum(n, 1).astype(F32)
    large = max_exact + (jnp.log(nf / max_exact) / math.log(REL_MAX_DIST / max_exact)
                         * (REL_BUCKETS - max_exact)).astype(jnp.int32)
    large = jnp.minimum(large, REL_BUCKETS - 1)
    bucket = jnp.where(n < max_exact, n, large)
    for m in range(o_ref.shape[0]):
        acc = jnp.zeros((tk, tq), F32)
        for bkt in range(REL_BUCKETS):
            acc = jnp.where(bucket == bkt, rb_ref[bkt, m], acc)
        o_ref[m] = (acc - rb_ref[REL_BUCKETS - 1, m]) * LOG2E


def _rel_bias_tiles(rel_bias, tq, tk):
    assert tq == 2 * tk and tk + 1 >= REL_MAX_DIST
    nmaps = rel_bias.shape[1]
    return pl.pallas_call(
        functools.partial(_rel_bias_kernel, tq=tq, tk=tk),
        grid=(BIAS_TILES,),
        in_specs=[pl.BlockSpec(memory_space=pltpu.SMEM)],
        out_specs=pl.BlockSpec((nmaps, None, tk, tq), lambda t: (0, t, 0, 0)),
        out_shape=jax.ShapeDtypeStruct((nmaps, BIAS_TILES, tk, tq), F32),
        compiler_params=_params(("arbitrary",)),
        name="rel_bias",
    )(rel_bias)


SCORES_AHEAD = 3
VALUES_BEHIND = 1
ONES_ROWS = 16


def _flash_pipeline(q_fn, k_fn, vt_fn, logits_fn, finalize_fn, s_ref, mt_ref, p_ref, al_ref, m_ref, l_ref, acc_ref,
                    *, nstreams, nq, tq, tk, order):
    tiles = [(g, i, t, t - 2 * i) for g in range(nstreams) for i in range(nq) for t in range(2 * i + 2)]
    ones = jnp.ones((ONES_ROWS, tk), BF16)
    ahead = s_ref.shape[0] - 1
    behind = p_ref.shape[0] - 1

    def lane0(c):
        return tk if c == 1 else 0

    def scores(n):
        g, i, t, c = tiles[n]
        slot, lo = n % (ahead + 1), lane0(c)
        s = logits_fn(g, _dot(k_fn(g, t), q_fn(g, i, lo)), c, lo)
        s_ref[slot, :, lo:] = s
        mt_ref[slot, :, lo:] = jnp.max(s, axis=0, keepdims=True)

    def softmax(n):
        g, i, t, c = tiles[n]
        sslot, slot, lo = n % (ahead + 1), n % (behind + 1), lane0(c)
        m_new = mt_ref[sslot, :, lo:]
        if t > 0:
            m_prev = m_ref[:, lo:]
            m_new = jnp.maximum(m_prev, m_new)
            al_ref[slot, :, lo:] = jnp.exp2(m_prev - m_new)
        m_ref[:, lo:] = m_new
        p = jnp.exp2(s_ref[sslot, :, lo:] - m_new)
        p_ref[slot, :, lo:] = p.astype(BF16)
        if ONES_ROWS == 0:
            lslot = (g * nq + i) % 2
            l_new = jnp.sum(p, axis=0, keepdims=True)
            if t > 0:
                l_new = al_ref[slot, :, lo:] * l_ref[lslot, :, lo:] + l_new
            l_ref[lslot, :, lo:] = l_new

    def values(n):
        g, i, t, c = tiles[n]
        slot, lo = n % (behind + 1), lane0(c)
        vt = vt_fn(g, t)
        if ONES_ROWS:
            vt = jnp.concatenate([vt, ones], axis=0)
        pv = _dot(vt, p_ref[slot, :, lo:])
        if t > 0:
            pv = al_ref[slot, :, lo:] * acc_ref[:, lo:] + pv
        acc_ref[:, lo:] = pv
        if c == 1:
            acc = acc_ref[...]
            dv = acc.shape[0] - ONES_ROWS
            finalize_fn(g, i, acc[:dv] / (acc[dv:dv + 1] if ONES_ROWS else l_ref[(g * nq + i) % 2]))

    for n in range(ahead):
        scores(n)
    assert sorted(order) == ["c", "s", "v"] and (behind > 0 or order.index("s") < order.index("v"))
    for n in range(len(tiles) + behind):
        for stage in order:
            if stage == "v" and behind <= n:
                values(n - behind)
            if stage == "c" and n + ahead < len(tiles):
                scores(n + ahead)
            if stage == "s" and n < len(tiles):
                softmax(n)


def _tri_mask(s):
    kk = lax.broadcasted_iota(jnp.int32, s.shape, 0)
    qq = lax.broadcasted_iota(jnp.int32, s.shape, 1)
    return jnp.where(kk <= qq, s, -jnp.inf)


def _flash_scratch(dv, tq, tk):
    return [pltpu.VMEM((SCORES_AHEAD + 1, tk, tq), F32), pltpu.VMEM((SCORES_AHEAD + 1, 1, tq), F32),
            pltpu.VMEM((VALUES_BEHIND + 1, tk, tq), BF16), pltpu.VMEM((VALUES_BEHIND + 1, 1, tq), F32),
            pltpu.VMEM((1, tq), F32), pltpu.VMEM((2, 1, tq), F32), pltpu.VMEM((dv + ONES_ROWS, tq), F32)]


def _mla_attn_kernel(qt_ref, k_ref, vt_ref, o_ref, *scratch, heads, tq, tk):

    def q_fn(g, i, lo):
        return qt_ref[g * HEAD_PAD:(g + 1) * HEAD_PAD, i * tq + lo:(i + 1) * tq]

    def k_fn(g, t):
        return k_ref[t * tk:(t + 1) * tk, g * HEAD_PAD:(g + 1) * HEAD_PAD]

    def vt_fn(g, t):
        return vt_ref[t, g * MLA_V:(g + 1) * MLA_V, :]

    def logits(g, s, c, lo):
        return _tri_mask(s) if c >= 0 else s

    def finalize(g, i, o):
        o_ref[g * MLA_V:(g + 1) * MLA_V, i * tq:(i + 1) * tq] = o.astype(BF16)

    _flash_pipeline(q_fn, k_fn, vt_fn, logits, finalize, *scratch,
                    nstreams=heads, nq=qt_ref.shape[1] // tq, tq=tq, tk=tk, order="svc")


def _mla_attn(qt, k, vt, tq, tk):
    bsz, nq, s = qt.shape
    nkt = s // tk
    hps = MLA_HEADS_PER_STEP
    assert tq == 2 * tk and s % tq == 0 and MLA_HEADS % hps == 0
    return pl.pallas_call(
        functools.partial(_mla_attn_kernel, heads=hps, tq=tq, tk=tk),
        grid=(MLA_HEADS // hps, bsz),
        in_specs=[
            pl.BlockSpec((None, hps * HEAD_PAD, s), lambda h, b: (b, h, 0)),
            pl.BlockSpec((None, s, hps * HEAD_PAD), lambda h, b: (b, 0, h)),
            pl.BlockSpec((None, nkt, hps * MLA_V, tk), lambda h, b: (b, 0, h, 0)),
        ],
        out_specs=pl.BlockSpec((None, hps * MLA_V, s), lambda h, b: (b, h, 0)),
        out_shape=jax.ShapeDtypeStruct((bsz, MLA_HEADS * MLA_V, s), BF16),
        scratch_shapes=_flash_scratch(MLA_V, tq, tk),
        compiler_params=_params(("arbitrary", "arbitrary")),
        name="mla_attn",
    )(qt, k, vt)


def _diff_attn_kernel(qt_ref, k_ref, vt_ref, bias_ref, lamv_ref, gn_ref, o_ref, o0_ref, qm_ref, *scratch, tq, tk):
    qt = qt_ref[...]
    row = lax.broadcasted_iota(jnp.int32, qt.shape, 0)
    for g in range(2):
        keep = jnp.logical_and(row >= g * DIFF_D, row < (g + 1) * DIFF_D)
        qm_ref[g] = jnp.where(keep, qt, jnp.zeros_like(qt))

    lv = lamv_ref[...]
    lam = (jnp.exp(jnp.sum(lv[0:1] * lv[1:2], axis=1, keepdims=True))
           - jnp.exp(jnp.sum(lv[2:3] * lv[3:4], axis=1, keepdims=True)) + LAM_INIT)

    def q_fn(g, i, lo):
        return qm_ref[g, :, i * tq + lo:(i + 1) * tq]

    def k_fn(g, t):
        return k_ref[t * tk:(t + 1) * tk, :]

    def vt_fn(g, t):
        return vt_ref[t]

    def logits(g, s, c, lo):
        if c >= -1:
            s = s + bias_ref[g, c + 1, :, lo:]
        return _tri_mask(s) if c >= 0 else s

    def finalize(g, i, o):
        cols = slice(i * tq, (i + 1) * tq)
        if g == 0:
            o0_ref[:, cols] = o
        else:
            d = o0_ref[:, cols] - lam * o
            d = d * lax.rsqrt(jnp.mean(d * d, axis=0, keepdims=True) + RMS_EPS) * gn_ref[...] * (1.0 - LAM_INIT)
            o_ref[:, cols] = d.astype(BF16)

    _flash_pipeline(q_fn, k_fn, vt_fn, logits, finalize, *scratch,
                    nstreams=2, nq=qt_ref.shape[1] // tq, tq=tq, tk=tk, order="svc")


def _diff_attn(qt, k, vt, bias, lamv, gn, tq, tk):
    bsz, nd, s = qt.shape
    nkt = s // tk
    dv = 2 * DIFF_D
    assert tq == 2 * tk and s % tq == 0
    return pl.pallas_call(
        functools.partial(_diff_attn_kernel, tq=tq, tk=tk),
        grid=(DIFF_HEADS, bsz),
        in_specs=[
            pl.BlockSpec((None, dv, s), lambda h, b: (b, h, 0)),
            pl.BlockSpec((None, None, s, dv), lambda h, b: (b, h, 0, 0)),
            pl.BlockSpec((None, nkt, dv, tk), lambda h, b: (b, 0, h, 0)),
            pl.BlockSpec((2, BIAS_TILES, tk, tq), lambda h, b: (h, 0, 0, 0)),
            pl.BlockSpec(lamv.shape, lambda h, b: (0, 0)),
            pl.BlockSpec(gn.shape, lambda h, b: (0, 0)),
        ],
        out_specs=pl.BlockSpec((None, dv, s), lambda h, b: (b, h, 0)),
        out_shape=jax.ShapeDtypeStruct((bsz, nd, s), BF16),
        scratch_shapes=[pltpu.VMEM((dv, s), F32), pltpu.VMEM((2, dv, s), BF16)] + _flash_scratch(dv, tq, tk),
        compiler_params=_params(("arbitrary", "arbitrary")),
        name="diff_attn",
    )(qt, k, vt, bias, lamv, gn)


def _post_mixer_kernel(h_ref, omt_ref, odt_ref, wum_ref, wud_ref, wg_ref, bg_ref, wo_ref, g_ref, b_ref,
                       o_ref):
    d = h_ref.shape[1]
    for rows in _chains(h_ref.shape[0]):
        h = h_ref[rows, :]
        y_m = _dot_tn(omt_ref[:, rows], wum_ref[...])
        y_d = _dot_tn(odt_ref[:, rows], wud_ref[...])
        z = _dot(h.astype(BF16), wg_ref[...]) + bg_ref[...]
        gate = 1.0 / (1.0 + jnp.exp(-z))
        mixed = gate[:, :d] * y_m + gate[:, d:] * y_d
        mix = _dot(mixed.astype(BF16), wo_ref[...])
        o_ref[rows, :] = _layer_norm(ALPHA * h + mix, g_ref[...], b_ref[...])


def _post_mixer(h, omt, odt, bsz, s, w_up_mla, w_up_diff, w_gate, b_gate, w_o, g, b):
    n, d = h.shape
    tm = ROW_TILE
    spt = s // tm
    ws = [w_up_mla.astype(BF16), w_up_diff.astype(BF16), w_gate.astype(BF16), b_gate.reshape(1, -1),
          w_o.astype(BF16), g.reshape(1, d), b.reshape(1, d)]

    return pl.pallas_call(
        _post_mixer_kernel,
        grid=(bsz, spt),
        in_specs=[
            pl.BlockSpec((tm, d), lambda bb, i: (bb * spt + i, 0)),
            pl.BlockSpec((None, omt.shape[1], tm), lambda bb, i: (bb, 0, i)),
            pl.BlockSpec((None, odt.shape[1], tm), lambda bb, i: (bb, 0, i)),
        ] + [_resident(a) for a in ws],
        out_specs=pl.BlockSpec((tm, d), lambda bb, i: (bb * spt + i, 0)),
        out_shape=jax.ShapeDtypeStruct((n, d), F32),
        compiler_params=_params(("arbitrary", "arbitrary")),
        name="post_mixer",
    )(h, omt, odt, *ws)


def _mem_kv_kernel(mem_ref, w_ref, k_ref, v_ref):
    d = k_ref.shape[1]
    kv = _dot(mem_ref[...].astype(BF16), w_ref[...])
    k_ref[...] = kv[:, :d].astype(BF16)
    v_ref[...] = kv[:, d:].astype(BF16)


def _mem_kv(mem, w_kv):
    bsz, m, d = mem.shape
    w = w_kv.astype(BF16)
    spec = pl.BlockSpec((None, m, d), lambda b: (b, 0, 0))
    return pl.pallas_call(
        _mem_kv_kernel,
        grid=(bsz,),
        in_specs=[spec, pl.BlockSpec(w.shape, lambda b: (0, 0))],
        out_specs=(spec, spec),
        out_shape=(jax.ShapeDtypeStruct((bsz, m, d), BF16), jax.ShapeDtypeStruct((bsz, m, d), BF16)),
        compiler_params=_params(("arbitrary",)),
        name="mem_kv",
    )(mem, w)


def _mem_attn_kernel(h_ref, k_ref, v_ref, wq_ref, wo_ref, g_ref, b_ref, o_ref):
    d = h_ref.shape[1]
    hd = d // MEM_HEADS
    for rows in _chains(h_ref.shape[0], MEM_CHAIN_ROWS):
        h = h_ref[rows, :]
        q = _dot(h.astype(BF16), wq_ref[...]).astype(BF16)
        outs = []
        for hh in range(MEM_HEADS):
            lo, hi = hh * hd, (hh + 1) * hd
            s = _dot_nt(q[:, lo:hi], k_ref[:, lo:hi]) * (hd ** -0.5)
            p = jnp.exp(s - jnp.max(s, axis=-1, keepdims=True))
            p = p / jnp.sum(p, axis=-1, keepdims=True)
            outs.append(_dot(p.astype(BF16), v_ref[:, lo:hi]).astype(BF16))
        att = _dot(jnp.concatenate(outs, axis=1), wo_ref[...])
        o_ref[rows, :] = _layer_norm(ALPHA * h + att, g_ref[...], b_ref[...])


def _mem_attn(h, km, vm, bsz, s, w_q, w_o, g, b):
    n, d = h.shape
    tm = WIDE_ROW_TILE
    spt = s // tm
    m = km.shape[1]
    ws = [w_q.astype(BF16), w_o.astype(BF16), g.reshape(1, d), b.reshape(1, d)]

    return pl.pallas_call(
        _mem_attn_kernel,
        grid=(bsz, spt),
        in_specs=[
            pl.BlockSpec((tm, d), lambda bb, i: (bb * spt + i, 0)),
            pl.BlockSpec((None, m, d), lambda bb, i: (bb, 0, 0)),
            pl.BlockSpec((None, m, d), lambda bb, i: (bb, 0, 0)),
        ] + [_resident(a) for a in ws],
        out_specs=pl.BlockSpec((tm, d), lambda bb, i: (bb * spt + i, 0)),
        out_shape=jax.ShapeDtypeStruct((n, d), F32),
        compiler_params=_params(("arbitrary", "arbitrary")),
        name="mem_attn",
    )(h, km, vm, *ws)


def kernel(x, mem, rel_bias, ffn1_w1, ffn1_w3, ffn1_w2, ln1_g, ln1_b, w_in, q_norm_g, kv_norm_g, w_q_b, w_kv_b, lam_q1, lam_k1, lam_q2, lam_k2, diff_norm_g, w_gate, b_gate, w_up_mla, w_up_diff, w_o, ln2_g, ln2_b, mem_w_q, mem_w_kv, mem_w_o, ln3_g, ln3_b, ffn2_w1, ffn2_w3, ffn2_w2, ln4_g, ln4_b):
    bsz, s, d = x.shape
    assert ffn1_w1.shape[0] == DEPTH
    l = 0
    h = x.reshape(bsz * s, d)

    h = _ffn_ln(h, ffn1_w1[l], ffn1_w3[l], ffn1_w2[l], ln1_g[l], ln1_b[l])

    qtm, km, vtm, qtd, kd, vtd = _in_proj(h, bsz, s, w_in[l], q_norm_g[l], kv_norm_g[l], w_q_b[l], w_kv_b[l])
    bias = _rel_bias_tiles(rel_bias, ATTN_TQ, ATTN_TK)
    omt = _mla_attn(qtm, km, vtm, ATTN_TQ, ATTN_TK)
    lamv = jnp.stack([lam_q1[l], lam_k1[l], lam_q2[l], lam_k2[l]], axis=0)
    odt = _diff_attn(qtd, kd, vtd, bias, lamv, diff_norm_g[l].reshape(-1, 1), ATTN_TQ, ATTN_TK)
    h = _post_mixer(h, omt, odt, bsz, s, w_up_mla[l], w_up_diff[l], w_gate[l], b_gate[l], w_o[l],
                    ln2_g[l], ln2_b[l])

    kmem, vmem = _mem_kv(mem, mem_w_kv[l])
    h = _mem_attn(h, kmem, vmem, bsz, s, mem_w_q[l], mem_w_o[l], ln3_g[l], ln3_b[l])

    h = _ffn_ln(h, ffn2_w1[l], ffn2_w3[l], ffn2_w2[l], ln4_g[l], ln4_b[l])
    return h.reshape(bsz, s, d)
```

```python
import functools
import math

import jax
import jax.numpy as jnp
from jax import lax
from jax.experimental import pallas as pl
from jax.experimental.pallas import tpu as pltpu

F32 = jnp.float32
BF16 = jnp.bfloat16

DEPTH = 1
MLA_HEADS = 8
MLA_Q_RANK = 384
MLA_KV_RANK = 256
MLA_NOPE = 64
MLA_ROPE = 32
MLA_V = 64
ROPE_BASE = 10000.0
DIFF_HEADS = 4
DIFF_D = 64
REL_BUCKETS = 32
REL_MAX_DIST = 128
MEM_HEADS = 4
ALPHA = (2.0 * DEPTH) ** 0.25
LN_EPS = 1e-5
RMS_EPS = 1e-6
LAM_INIT = 0.8 - 0.6 * math.exp(-0.3 * 0)
LOG2E = math.log2(math.e)

HEAD_PAD = 128
VMEM_LIMIT_BYTES = 56 * 1024 * 1024

ROW_TILE = 1024
WIDE_ROW_TILE = 2048
CHAIN_ROWS = 256
MEM_CHAIN_ROWS = 512
ATTN_TQ = 512
ATTN_TK = 256
MLA_HEADS_PER_STEP = 4


def _params(semantics):
    return pltpu.CompilerParams(dimension_semantics=semantics, vmem_limit_bytes=VMEM_LIMIT_BYTES)


def _dot(a, b):
    return jnp.dot(a, b, preferred_element_type=F32)


def _dot_nt(a, b):
    return lax.dot_general(a, b, (((1,), (1,)), ((), ())), preferred_element_type=F32)


def _dot_tn(a, b):
    return lax.dot_general(a, b, (((0,), (0,)), ((), ())), preferred_element_type=F32)


def _layer_norm(y, g, b):
    mu = jnp.mean(y, axis=-1, keepdims=True)
    d = y - mu
    var = jnp.mean(d * d, axis=-1, keepdims=True)
    return d * lax.rsqrt(var + LN_EPS) * g + b


def _rms_norm_rows(x, g):
    return x * lax.rsqrt(jnp.mean(x * x, axis=-1, keepdims=True) + RMS_EPS) * g


def _chains(rows, chain=CHAIN_ROWS):
    return [slice(r, r + chain) for r in range(0, rows, chain)]


def _resident(a):
    return pl.BlockSpec(a.shape, lambda *_: (0,) * a.ndim, pipeline_mode=pl.Buffered(1))


def _ffn_ln_kernel(x_ref, w1_ref, w3_ref, w2_ref, g_ref, b_ref, o_ref):
    for rows in _chains(x_ref.shape[0]):
        x = x_ref[rows, :]
        xb = x.astype(BF16)
        u = _dot(xb, w1_ref[...])
        v = _dot(xb, w3_ref[...])
        act = (u / (1.0 + jnp.exp(-u))) * v
        y = ALPHA * x + 0.5 * _dot(act.astype(BF16), w2_ref[...])
        o_ref[rows, :] = _layer_norm(y, g_ref[...], b_ref[...])


def _ffn_ln(x, w1, w3, w2, g, b):
    n, d = x.shape
    tm = ROW_TILE
    assert n % tm == 0
    ws = [w1.astype(BF16), w3.astype(BF16), w2.astype(BF16), g.reshape(1, d), b.reshape(1, d)]
    return pl.pallas_call(
        _ffn_ln_kernel,
        grid=(n // tm,),
        in_specs=[pl.BlockSpec((tm, d), lambda i: (i, 0))] + [_resident(a) for a in ws],
        out_specs=pl.BlockSpec((tm, d), lambda i: (i, 0)),
        out_shape=jax.ShapeDtypeStruct((n, d), F32),
        compiler_params=_params(("arbitrary",)),
        name="ffn_ln",
    )(x, *ws)


def _in_proj_kernel(h_ref, wnat_ref, wt_ref, wq_ref, wkn_ref, wvt_ref, gq_ref, gkv_ref,
                    cosk_ref, sink_ref, cosq_ref, sinq_ref,
                    qtm_ref, km_ref, vtm_ref, qtd_ref, kd_ref, vtd_ref):
    tk = vtm_ref.shape[-1]
    half = MLA_ROPE // 2
    nqd = qtd_ref.shape[0]
    o1 = MLA_Q_RANK
    o2 = o1 + MLA_KV_RANK
    o3 = o2 + HEAD_PAD
    q_scale = (MLA_NOPE + MLA_ROPE) ** -0.5 * LOG2E
    for rows in _chains(h_ref.shape[0]):
        key_tiles = range(rows.start // tk, rows.stop // tk)
        hb = h_ref[rows, :].astype(BF16)

        pn = _dot(hb, wnat_ref[...])
        c_q, c_kv, krg, kd = pn[:, :o1], pn[:, o1:o2], pn[:, o2:o3], pn[:, o3:]
        for hh in range(DIFF_HEADS):
            kd_ref[hh, rows, :] = kd[:, hh * 2 * DIFF_D:(hh + 1) * 2 * DIFF_D].astype(BF16)

        pt = _dot_nt(wt_ref[...], hb)
        qtd_ref[:, rows] = (pt[:nqd] * (DIFF_D ** -0.5 * LOG2E)).astype(BF16)
        for c, t in enumerate(key_tiles):
            vtd_ref[t] = pt[nqd:, c * tk:(c + 1) * tk].astype(BF16)

        cqn = _rms_norm_rows(c_q, gq_ref[...]).astype(BF16)
        qt = _dot_nt(wq_ref[...], cqn)
        cos_q, sin_q = cosq_ref[:, rows], sinq_ref[:, rows]
        for hh in range(MLA_HEADS):
            lo = hh * HEAD_PAD
            r0 = lo + MLA_NOPE
            r1, r2 = r0 + half, r0 + MLA_ROPE
            x1, x2 = qt[r0:r1], qt[r1:r2]
            qtm_ref[lo:r0, rows] = (qt[lo:r0] * q_scale).astype(BF16)
            qtm_ref[r0:r1, rows] = ((x1 * cos_q - x2 * sin_q) * q_scale).astype(BF16)
            qtm_ref[r1:r2, rows] = ((x2 * cos_q + x1 * sin_q) * q_scale).astype(BF16)
            qtm_ref[r2:lo + HEAD_PAD, rows] = jnp.zeros((lo + HEAD_PAD - r2, CHAIN_ROWS), BF16)

        ckvn = _rms_norm_rows(c_kv, gkv_ref[...]).astype(BF16)
        kn = _dot(ckvn, wkn_ref[...])
        kr = krg * cosk_ref[rows, :] + pltpu.roll(krg, HEAD_PAD - MLA_ROPE, 1) * sink_ref[rows, :]
        for hh in range(MLA_HEADS):
            lo, hi = hh * HEAD_PAD, (hh + 1) * HEAD_PAD
            km_ref[rows, lo:hi] = (kn[:, lo:hi] + kr).astype(BF16)
        vt = _dot_nt(wvt_ref[...], ckvn)
        for c, t in enumerate(key_tiles):
            vtm_ref[t] = vt[:, c * tk:(c + 1) * tk].astype(BF16)


def _in_proj_weights(w_in, w_q_b, w_kv_b):
    d = w_in.shape[0]
    o_cq = MLA_Q_RANK
    o_ckv = o_cq + MLA_KV_RANK
    o_kr = o_ckv + MLA_ROPE
    n_d = 2 * DIFF_HEADS * DIFF_D
    o_qd = o_kr + n_d
    o_kd = o_qd + n_d
    half = MLA_ROPE // 2
    pad = HEAD_PAD - MLA_NOPE - MLA_ROPE
    assert pad == MLA_ROPE
    kr = w_in[:, o_ckv:o_kr]
    krg = jnp.concatenate([jnp.zeros((d, MLA_NOPE), F32), kr, -kr[:, half:], kr[:, :half]], axis=1)
    w_nat = jnp.concatenate([w_in[:, :o_ckv], krg, w_in[:, o_qd:o_kd]], axis=1).astype(BF16)
    w_t = jnp.concatenate([w_in[:, o_kr:o_qd], w_in[:, o_kd:]], axis=1).T.astype(BF16)

    r = w_q_b.shape[0]
    wq = w_q_b.reshape(r, MLA_HEADS, MLA_NOPE + MLA_ROPE)
    wq = jnp.concatenate([wq, jnp.zeros((r, MLA_HEADS, pad), F32)], axis=-1)
    wq_t = wq.reshape(r, MLA_HEADS * HEAD_PAD).T.astype(BF16)

    rk = w_kv_b.shape[0]
    wkv = w_kv_b.reshape(rk, MLA_HEADS, MLA_NOPE + MLA_V)
    wkn = jnp.concatenate([wkv[..., :MLA_NOPE], jnp.zeros((rk, MLA_HEADS, HEAD_PAD - MLA_NOPE), F32)],
                          axis=-1).reshape(rk, MLA_HEADS * HEAD_PAD).astype(BF16)
    wv_t = wkv[..., MLA_NOPE:].reshape(rk, MLA_HEADS * MLA_V).T.astype(BF16)
    return w_nat, w_t, wq_t, wkn, wv_t


def _rope_tables(s):
    inv_freq = ROPE_BASE ** (-jnp.arange(0, MLA_ROPE, 2, dtype=F32) / MLA_ROPE)
    ang = jnp.arange(s).astype(F32)[:, None] * inv_freq[None, :]
    cos, sin = jnp.cos(ang), jnp.sin(ang)
    z_lo = jnp.zeros((s, MLA_NOPE), F32)
    z_hi = jnp.zeros((s, HEAD_PAD - MLA_NOPE - MLA_ROPE), F32)
    cos_k = jnp.concatenate([z_lo, cos, cos, z_hi], axis=1)
    sin_k = jnp.concatenate([z_lo, sin, sin, z_hi], axis=1)
    return cos_k, sin_k, cos.T, sin.T


def _in_proj(h, bsz, s, w_in, q_norm_g, kv_norm_g, w_q_b, w_kv_b):
    n, d = h.shape
    tm, tk = ROW_TILE, ATTN_TK
    assert s % tm == 0 and CHAIN_ROWS % tk == 0
    w_nat, w_t, wq_t, wkn, wv_t = _in_proj_weights(w_in, w_q_b, w_kv_b)
    cos_k, sin_k, cos_q, sin_q = _rope_tables(s)
    half = MLA_ROPE // 2
    spt = s // tm
    nqm = MLA_HEADS * HEAD_PAD
    nvm = MLA_HEADS * MLA_V
    nd = 2 * DIFF_HEADS * DIFF_D

    gq = q_norm_g.reshape(1, -1)
    gkv = kv_norm_g.reshape(1, -1)
    out_shape = (
        jax.ShapeDtypeStruct((bsz, nqm, s), BF16),
        jax.ShapeDtypeStruct((bsz, s, nqm), BF16),
        jax.ShapeDtypeStruct((bsz, s // tk, nvm, tk), BF16),
        jax.ShapeDtypeStruct((bsz, nd, s), BF16),
        jax.ShapeDtypeStruct((bsz, DIFF_HEADS, s, nd // DIFF_HEADS), BF16),
        jax.ShapeDtypeStruct((bsz, s // tk, nd, tk), BF16),
    )
    out_specs = (
        pl.BlockSpec((None, nqm, tm), lambda b, i: (b, 0, i)),
        pl.BlockSpec((None, tm, nqm), lambda b, i: (b, i, 0)),
        pl.BlockSpec((None, tm // tk, nvm, tk), lambda b, i: (b, i, 0, 0)),
        pl.BlockSpec((None, nd, tm), lambda b, i: (b, 0, i)),
        pl.BlockSpec((None, DIFF_HEADS, tm, nd // DIFF_HEADS), lambda b, i: (b, 0, i, 0)),
        pl.BlockSpec((None, tm // tk, nd, tk), lambda b, i: (b, i, 0, 0)),
    )
    return pl.pallas_call(
        _in_proj_kernel,
        grid=(bsz, spt),
        in_specs=[
            pl.BlockSpec((tm, d), lambda b, i: (b * spt + i, 0)),
            _resident(w_nat), _resident(w_t), _resident(wq_t), _resident(wkn), _resident(wv_t),
            _resident(gq), _resident(gkv),
            pl.BlockSpec((tm, HEAD_PAD), lambda b, i: (i, 0)),
            pl.BlockSpec((tm, HEAD_PAD), lambda b, i: (i, 0)),
            pl.BlockSpec((half, tm), lambda b, i: (0, i)),
            pl.BlockSpec((half, tm), lambda b, i: (0, i)),
        ],
        out_specs=out_specs,
        out_shape=out_shape,
        compiler_params=_params(("arbitrary", "arbitrary")),
        name="in_proj",
    )(h, w_nat, w_t, wq_t, wkn, wv_t, gq, gkv, cos_k, sin_k, cos_q, sin_q)


BIAS_TILES = 3


def _rel_bias_kernel(rb_ref, o_ref, *, tq, tk):
    c = pl.program_id(0) - 1
    kk = lax.broadcasted_iota(jnp.int32, (tk, tq), 0)
    qq = lax.broadcasted_iota(jnp.int32, (tk, tq), 1)
    n = jnp.maximum(qq - kk - c * tk, 0)
    max_exact = REL_BUCKETS // 2
    nf = jnp.maximum(n, 1).astype(F32)
    large = max_exact + (jnp.log(nf / max_exact) / math.log(REL_MAX_DIST / max_exact)
                         * (REL_BUCKETS - max_exact)).astype(jnp.int32)
    large = jnp.minimum(large, REL_BUCKETS - 1)
    bucket = jnp.where(n < max_exact, n, large)
    for m in range(o_ref.shape[0]):
        acc = jnp.zeros((tk, tq), F32)
        for bkt in range(REL_BUCKETS):
            acc = jnp.where(bucket == bkt, rb_ref[bkt, m], acc)
        o_ref[m] = (acc - rb_ref[REL_BUCKETS - 1, m]) * LOG2E


def _rel_bias_tiles(rel_bias, tq, tk):
    assert tq == 2 * tk and tk + 1 >= REL_MAX_DIST
    nmaps = rel_bias.shape[1]
    return pl.pallas_call(
        functools.partial(_rel_bias_kernel, tq=tq, tk=tk),
        grid=(BIAS_TILES,),
        in_specs=[pl.BlockSpec(memory_space=pltpu.SMEM)],
        out_specs=pl.BlockSpec((nmaps, None, tk, tq), lambda t: (0, t, 0, 0)),
        out_shape=jax.ShapeDtypeStruct((nmaps, BIAS_TILES, tk, tq), F32),
        compiler_params=_params(("arbitrary",)),
        name="rel_bias",
    )(rel_bias)


SCORES_AHEAD = 3
VALUES_BEHIND = 1
ONES_ROWS = 16


def _flash_pipeline(q_fn, k_fn, vt_fn, logits_fn, finalize_fn, s_ref, mt_ref, p_ref, al_ref, m_ref, l_ref, acc_ref,
                    *, nstreams, nq, tq, tk, order):
    tiles = [(g, i, t, t - 2 * i) for g in range(nstreams) for i in range(nq) for t in range(2 * i + 2)]
    ones = jnp.ones((ONES_ROWS, tk), BF16)
    ahead = s_ref.shape[0] - 1
    behind = p_ref.shape[0] - 1

    def lane0(c):
        return tk if c == 1 else 0

    def scores(n):
        g, i, t, c = tiles[n]
        slot, lo = n % (ahead + 1), lane0(c)
        s = logits_fn(g, _dot(k_fn(g, t), q_fn(g, i, lo)), c, lo)
        s_ref[slot, :, lo:] = s
        mt_ref[slot, :, lo:] = jnp.max(s, axis=0, keepdims=True)

    def softmax(n):
        g, i, t, c = tiles[n]
        sslot, slot, lo = n % (ahead + 1), n % (behind + 1), lane0(c)
        m_new = mt_ref[sslot, :, lo:]
        if t > 0:
            m_prev = m_ref[:, lo:]
            m_new = jnp.maximum(m_prev, m_new)
            al_ref[slot, :, lo:] = jnp.exp2(m_prev - m_new)
        m_ref[:, lo:] = m_new
        p = jnp.exp2(s_ref[sslot, :, lo:] - m_new)
        p_ref[slot, :, lo:] = p.astype(BF16)
        if ONES_ROWS == 0:
            lslot = (g * nq + i) % 2
            l_new = jnp.sum(p, axis=0, keepdims=True)
            if t > 0:
                l_new = al_ref[slot, :, lo:] * l_ref[lslot, :, lo:] + l_new
            l_ref[lslot, :, lo:] = l_new

    def values(n):
        g, i, t, c = tiles[n]
        slot, lo = n % (behind + 1), lane0(c)
        vt = vt_fn(g, t)
        if ONES_ROWS:
            vt = jnp.concatenate([vt, ones], axis=0)
        pv = _dot(vt, p_ref[slot, :, lo:])
        if t > 0:
            pv = al_ref[slot, :, lo:] * acc_ref[:, lo:] + pv
        acc_ref[:, lo:] = pv
        if c == 1:
            acc = acc_ref[...]
            dv = acc.shape[0] - ONES_ROWS
            finalize_fn(g, i, acc[:dv] / (acc[dv:dv + 1] if ONES_ROWS else l_ref[(g * nq + i) % 2]))

    for n in range(ahead):
        scores(n)
    assert sorted(order) == ["c", "s", "v"] and (behind > 0 or order.index("s") < order.index("v"))
    for n in range(len(tiles) + behind):
        for stage in order:
            if stage == "v" and behind <= n:
                values(n - behind)
            if stage == "c" and n + ahead < len(tiles):
                scores(n + ahead)
            if stage == "s" and n < len(tiles):
                softmax(n)


def _tri_mask(s):
    kk = lax.broadcasted_iota(jnp.int32, s.shape, 0)
    qq = lax.broadcasted_iota(jnp.int32, s.shape, 1)
    return jnp.where(kk <= qq, s, -jnp.inf)


def _flash_scratch(dv, tq, tk):
    return [pltpu.VMEM((SCORES_AHEAD + 1, tk, tq), F32), pltpu.VMEM((SCORES_AHEAD + 1, 1, tq), F32),
            pltpu.VMEM((VALUES_BEHIND + 1, tk, tq), BF16), pltpu.VMEM((VALUES_BEHIND + 1, 1, tq), F32),
            pltpu.VMEM((1, tq), F32), pltpu.VMEM((2, 1, tq), F32), pltpu.VMEM((dv + ONES_ROWS, tq), F32)]


def _mla_attn_kernel(qt_ref, k_ref, vt_ref, o_ref, *scratch, heads, tq, tk):

    def q_fn(g, i, lo):
        return qt_ref[g * HEAD_PAD:(g + 1) * HEAD_PAD, i * tq + lo:(i + 1) * tq]

    def k_fn(g, t):
        return k_ref[t * tk:(t + 1) * tk, g * HEAD_PAD:(g + 1) * HEAD_PAD]

    def vt_fn(g, t):
        return vt_ref[t, g * MLA_V:(g + 1) * MLA_V, :]

    def logits(g, s, c, lo):
        return _tri_mask(s) if c >= 0 else s

    def finalize(g, i, o):
        o_ref[g * MLA_V:(g + 1) * MLA_V, i * tq:(i + 1) * tq] = o.astype(BF16)

    _flash_pipeline(q_fn, k_fn, vt_fn, logits, finalize, *scratch,
                    nstreams=heads, nq=qt_ref.shape[1] // tq, tq=tq, tk=tk, order="cvs")


def _mla_attn(qt, k, vt, tq, tk):
    bsz, nq, s = qt.shape
    nkt = s // tk
    hps = MLA_HEADS_PER_STEP
    assert tq == 2 * tk and s % tq == 0 and MLA_HEADS % hps == 0
    return pl.pallas_call(
        functools.partial(_mla_attn_kernel, heads=hps, tq=tq, tk=tk),
        grid=(MLA_HEADS // hps, bsz),
        in_specs=[
            pl.BlockSpec((None, hps * HEAD_PAD, s), lambda h, b: (b, h, 0)),
            pl.BlockSpec((None, s, hps * HEAD_PAD), lambda h, b: (b, 0, h)),
            pl.BlockSpec((None, nkt, hps * MLA_V, tk), lambda h, b: (b, 0, h, 0)),
        ],
        out_specs=pl.BlockSpec((None, hps * MLA_V, s), lambda h, b: (b, h, 0)),
        out_shape=jax.ShapeDtypeStruct((bsz, MLA_HEADS * MLA_V, s), BF16),
        scratch_shapes=_flash_scratch(MLA_V, tq, tk),
        compiler_params=_params(("arbitrary", "arbitrary")),
        name="mla_attn",
    )(qt, k, vt)


def _diff_attn_kernel(qt_ref, k_ref, vt_ref, bias_ref, lamv_ref, gn_ref, o_ref, o0_ref, qm_ref, *scratch, tq, tk):
    qt = qt_ref[...]
    row = lax.broadcasted_iota(jnp.int32, qt.shape, 0)
    for g in range(2):
        keep = jnp.logical_and(row >= g * DIFF_D, row < (g + 1) * DIFF_D)
        qm_ref[g] = jnp.where(keep, qt, jnp.zeros_like(qt))

    lv = lamv_ref[...]
    lam = (jnp.exp(jnp.sum(lv[0:1] * lv[1:2], axis=1, keepdims=True))
           - jnp.exp(jnp.sum(lv[2:3] * lv[3:4], axis=1, keepdims=True)) + LAM_INIT)

    def q_fn(g, i, lo):
        return qm_ref[g, :, i * tq + lo:(i + 1) * tq]

    def k_fn(g, t):
        return k_ref[t * tk:(t + 1) * tk, :]

    def vt_fn(g, t):
        return vt_ref[t]

    def logits(g, s, c, lo):
        if c >= -1:
            s = s + bias_ref[g, c + 1, :, lo:]
        return _tri_mask(s) if c >= 0 else s

    def finalize(g, i, o):
        cols = slice(i * tq, (i + 1) * tq)
        if g == 0:
            o0_ref[:, cols] = o
        else:
            d = o0_ref[:, cols] - lam * o
            d = d * lax.rsqrt(jnp.mean(d * d, axis=0, keepdims=True) + RMS_EPS) * gn_ref[...] * (1.0 - LAM_INIT)
            o_ref[:, cols] = d.astype(BF16)

    _flash_pipeline(q_fn, k_fn, vt_fn, logits, finalize, *scratch,
                    nstreams=2, nq=qt_ref.shape[1] // tq, tq=tq, tk=tk, order="cvs")


def _diff_attn(qt, k, vt, bias, lamv, gn, tq, tk):
    bsz, nd, s = qt.shape
    nkt = s // tk
    dv = 2 * DIFF_D
    assert tq == 2 * tk and s % tq == 0
    return pl.pallas_call(
        functools.partial(_diff_attn_kernel, tq=tq, tk=tk),
        grid=(DIFF_HEADS, bsz),
        in_specs=[
            pl.BlockSpec((None, dv, s), lambda h, b: (b, h, 0)),
            pl.BlockSpec((None, None, s, dv), lambda h, b: (b, h, 0, 0)),
            pl.BlockSpec((None, nkt, dv, tk), lambda h, b: (b, 0, h, 0)),
            pl.BlockSpec((2, BIAS_TILES, tk, tq), lambda h, b: (h, 0, 0, 0)),
            pl.BlockSpec(lamv.shape, lambda h, b: (0, 0)),
            pl.BlockSpec(gn.shape, lambda h, b: (0, 0)),
        ],
        out_specs=pl.BlockSpec((None, dv, s), lambda h, b: (b, h, 0)),
        out_shape=jax.ShapeDtypeStruct((bsz, nd, s), BF16),
        scratch_shapes=[pltpu.VMEM((dv, s), F32), pltpu.VMEM((2, dv, s), BF16)] + _flash_scratch(dv, tq, tk),
        compiler_params=_params(("arbitrary", "arbitrary")),
        name="diff_attn",
    )(qt, k, vt, bias, lamv, gn)


def _post_mixer_kernel(h_ref, omt_ref, odt_ref, wum_ref, wud_ref, wg_ref, bg_ref, wo_ref, g_ref, b_ref,
                       o_ref):
    d = h_ref.shape[1]
    for rows in _chains(h_ref.shape[0]):
        h = h_ref[rows, :]
        y_m = _dot_tn(omt_ref[:, rows], wum_ref[...])
        y_d = _dot_tn(odt_ref[:, rows], wud_ref[...])
        z = _dot(h.astype(BF16), wg_ref[...]) + bg_ref[...]
        gate = 1.0 / (1.0 + jnp.exp(-z))
        mixed = gate[:, :d] * y_m + gate[:, d:] * y_d
        mix = _dot(mixed.astype(BF16), wo_ref[...])
        o_ref[rows, :] = _layer_norm(ALPHA * h + mix, g_ref[...], b_ref[...])


def _post_mixer(h, omt, odt, bsz, s, w_up_mla, w_up_diff, w_gate, b_gate, w_o, g, b):
    n, d = h.shape
    tm = ROW_TILE
    spt = s // tm
    ws = [w_up_mla.astype(BF16), w_up_diff.astype(BF16), w_gate.astype(BF16), b_gate.reshape(1, -1),
          w_o.astype(BF16), g.reshape(1, d), b.reshape(1, d)]

    return pl.pallas_call(
        _post_mixer_kernel,
        grid=(bsz, spt),
        in_specs=[
            pl.BlockSpec((tm, d), lambda bb, i: (bb * spt + i, 0)),
            pl.BlockSpec((None, omt.shape[1], tm), lambda bb, i: (bb, 0, i)),
            pl.BlockSpec((None, odt.shape[1], tm), lambda bb, i: (bb, 0, i)),
        ] + [_resident(a) for a in ws],
        out_specs=pl.BlockSpec((tm, d), lambda bb, i: (bb * spt + i, 0)),
        out_shape=jax.ShapeDtypeStruct((n, d), F32),
        compiler_params=_params(("arbitrary", "arbitrary")),
        name="post_mixer",
    )(h, omt, odt, *ws)


def _mem_kv_kernel(mem_ref, w_ref, k_ref, v_ref):
    d = k_ref.shape[1]
    kv = _dot(mem_ref[...].astype(BF16), w_ref[...])
    k_ref[...] = kv[:, :d].astype(BF16)
    v_ref[...] = kv[:, d:].astype(BF16)


def _mem_kv(mem, w_kv):
    bsz, m, d = mem.shape
    w = w_kv.astype(BF16)
    spec = pl.BlockSpec((None, m, d), lambda b: (b, 0, 0))
    return pl.pallas_call(
        _mem_kv_kernel,
        grid=(bsz,),
        in_specs=[spec, pl.BlockSpec(w.shape, lambda b: (0, 0))],
        out_specs=(spec, spec),
        out_shape=(jax.ShapeDtypeStruct((bsz, m, d), BF16), jax.ShapeDtypeStruct((bsz, m, d), BF16)),
        compiler_params=_params(("arbitrary",)),
        name="mem_kv",
    )(mem, w)


def _mem_attn_kernel(h_ref, k_ref, v_ref, wq_ref, wo_ref, g_ref, b_ref, o_ref):
    d = h_ref.shape[1]
    hd = d // MEM_HEADS
    for rows in _chains(h_ref.shape[0], MEM_CHAIN_ROWS):
        h = h_ref[rows, :]
        q = _dot(h.astype(BF16), wq_ref[...]).astype(BF16)
        outs = []
        for hh in range(MEM_HEADS):
            lo, hi = hh * hd, (hh + 1) * hd
            s = _dot_nt(q[:, lo:hi], k_ref[:, lo:hi]) * (hd ** -0.5)
            p = jnp.exp(s - jnp.max(s, axis=-1, keepdims=True))
            p = p / jnp.sum(p, axis=-1, keepdims=True)
            outs.append(_dot(p.astype(BF16), v_ref[:, lo:hi]).astype(BF16))
        att = _dot(jnp.concatenate(outs, axis=1), wo_ref[...])
        o_ref[rows, :] = _layer_norm(ALPHA * h + att, g_ref[...], b_ref[...])


def _mem_attn(h, km, vm, bsz, s, w_q, w_o, g, b):
    n, d = h.shape
    tm = WIDE_ROW_TILE
    spt = s // tm
    m = km.shape[1]
    ws = [w_q.astype(BF16), w_o.astype(BF16), g.reshape(1, d), b.reshape(1, d)]

    return pl.pallas_call(
        _mem_attn_kernel,
        grid=(bsz, spt),
        in_specs=[
            pl.BlockSpec((tm, d), lambda bb, i: (bb * spt + i, 0)),
            pl.BlockSpec((None, m, d), lambda bb, i: (bb, 0, 0)),
            pl.BlockSpec((None, m, d), lambda bb, i: (bb, 0, 0)),
        ] + [_resident(a) for a in ws],
        out_specs=pl.BlockSpec((tm, d), lambda bb, i: (bb * spt + i, 0)),
        out_shape=jax.ShapeDtypeStruct((n, d), F32),
        compiler_params=_params(("arbitrary", "arbitrary")),
        name="mem_attn",
    )(h, km, vm, *ws)


def kernel(x, mem, rel_bias, ffn1_w1, ffn1_w3, ffn1_w2, ln1_g, ln1_b, w_in, q_norm_g, kv_norm_g, w_q_b, w_kv_b, lam_q1, lam_k1, lam_q2, lam_k2, diff_norm_g, w_gate, b_gate, w_up_mla, w_up_diff, w_o, ln2_g, ln2_b, mem_w_q, mem_w_kv, mem_w_o, ln3_g, ln3_b, ffn2_w1, ffn2_w3, ffn2_w2, ln4_g, ln4_b):
    bsz, s, d = x.shape
    assert ffn1_w1.shape[0] == DEPTH
    l = 0
    h = x.reshape(bsz * s, d)

    h = _ffn_ln(h, ffn1_w1[l], ffn1_w3[l], ffn1_w2[l], ln1_g[l], ln1_b[l])

    qtm, km, vtm, qtd, kd, vtd = _in_proj(h, bsz, s, w_in[l], q_norm_g[l], kv_norm_g[l], w_q_b[l], w_kv_b[l])
    bias = _rel_bias_tiles(rel_bias, ATTN_TQ, ATTN_TK)
    omt = _mla_attn(qtm, km, vtm, ATTN_TQ, ATTN_TK)
    lamv = jnp.stack([lam_q1[l], lam_k1[l], lam_q2[l], lam_k2[l]], axis=0)
    odt = _diff_attn(qtd, kd, vtd, bias, lamv, diff_norm_g[l].reshape(-1, 1), ATTN_TQ, ATTN_TK)
    h = _post_mixer(h, omt, odt, bsz, s, w_up_mla[l], w_up_diff[l], w_gate[l], b_gate[l], w_o[l],
                    ln2_g[l], ln2_b[l])

    kmem, vmem = _mem_kv(mem, mem_w_kv[l])
    h = _mem_attn(h, kmem, vmem, bsz, s, mem_w_q[l], mem_w_o[l], ln3_g[l], ln3_b[l])

    h = _ffn_ln(h, ffn2_w1[l], ffn2_w3[l], ffn2_w2[l], ln4_g[l], ln4_b[l])
    return h.reshape(bsz, s, d)
```

```python
import functools
import math

import jax
import jax.numpy as jnp
from jax import lax
from jax.experimental import pallas as pl
from jax.experimental.pallas import tpu as pltpu

F32 = jnp.float32
BF16 = jnp.bfloat16

DEPTH = 1
MLA_HEADS = 8
MLA_Q_RANK = 384
MLA_KV_RANK = 256
MLA_NOPE = 64
MLA_ROPE = 32
MLA_V = 64
ROPE_BASE = 10000.0
DIFF_HEADS = 4
DIFF_D = 64
REL_BUCKETS = 32
REL_MAX_DIST = 128
MEM_HEADS = 4
ALPHA = (2.0 * DEPTH) ** 0.25
LN_EPS = 1e-5
RMS_EPS = 1e-6
LAM_INIT = 0.8 - 0.6 * math.exp(-0.3 * 0)
LOG2E = math.log2(math.e)

HEAD_PAD = 128
VMEM_LIMIT_BYTES = 56 * 1024 * 1024

ROW_TILE = 1024
WIDE_ROW_TILE = 2048
CHAIN_ROWS = 256
MEM_CHAIN_ROWS = 512
ATTN_TQ = 512
ATTN_TK = 256
MLA_HEADS_PER_STEP = 4
MLA_STAGE_ORDER = "vcs"
DIFF_STAGE_ORDER = "csv"


def _params(semantics):
    return pltpu.CompilerParams(dimension_semantics=semantics, vmem_limit_bytes=VMEM_LIMIT_BYTES)


def _dot(a, b):
    return jnp.dot(a, b, preferred_element_type=F32)


def _dot_nt(a, b):
    return lax.dot_general(a, b, (((1,), (1,)), ((), ())), preferred_element_type=F32)


def _dot_tn(a, b):
    return lax.dot_general(a, b, (((0,), (0,)), ((), ())), preferred_element_type=F32)


def _layer_norm(y, g, b):
    mu = jnp.mean(y, axis=-1, keepdims=True)
    d = y - mu
    var = jnp.mean(d * d, axis=-1, keepdims=True)
    return d * lax.rsqrt(var + LN_EPS) * g + b


def _rms_norm_rows(x, g):
    return x * lax.rsqrt(jnp.mean(x * x, axis=-1, keepdims=True) + RMS_EPS) * g


def _chains(rows, chain=CHAIN_ROWS):
    return [slice(r, r + chain) for r in range(0, rows, chain)]


def _resident(a):
    return pl.BlockSpec(a.shape, lambda *_: (0,) * a.ndim, pipeline_mode=pl.Buffered(1))


def _ffn_ln_kernel(x_ref, w1_ref, w3_ref, w2_ref, g_ref, b_ref, o_ref):
    for rows in _chains(x_ref.shape[0]):
        x = x_ref[rows, :]
        xb = x.astype(BF16)
        u = _dot(xb, w1_ref[...])
        v = _dot(xb, w3_ref[...])
        act = (u / (1.0 + jnp.exp(-u))) * v
        y = ALPHA * x + 0.5 * _dot(act.astype(BF16), w2_ref[...])
        o_ref[rows, :] = _layer_norm(y, g_ref[...], b_ref[...])


def _ffn_ln(x, w1, w3, w2, g, b):
    n, d = x.shape
    tm = ROW_TILE
    assert n % tm == 0
    ws = [w1.astype(BF16), w3.astype(BF16), w2.astype(BF16), g.reshape(1, d), b.reshape(1, d)]
    return pl.pallas_call(
        _ffn_ln_kernel,
        grid=(n // tm,),
        in_specs=[pl.BlockSpec((tm, d), lambda i: (i, 0))] + [_resident(a) for a in ws],
        out_specs=pl.BlockSpec((tm, d), lambda i: (i, 0)),
        out_shape=jax.ShapeDtypeStruct((n, d), F32),
        compiler_params=_params(("arbitrary",)),
        name="ffn_ln",
    )(x, *ws)


def _in_proj_kernel(h_ref, wnat_ref, wt_ref, wq_ref, wkn_ref, wvt_ref, gq_ref, gkv_ref,
                    cosk_ref, sink_ref, cosq_ref, sinq_ref,
                    qtm_ref, km_ref, vtm_ref, qtd_ref, kd_ref, vtd_ref):
    tk = vtm_ref.shape[-1]
    half = MLA_ROPE // 2
    nqd = qtd_ref.shape[0]
    o1 = MLA_Q_RANK
    o2 = o1 + MLA_KV_RANK
    o3 = o2 + HEAD_PAD
    q_scale = (MLA_NOPE + MLA_ROPE) ** -0.5 * LOG2E
    for rows in _chains(h_ref.shape[0]):
        key_tiles = range(rows.start // tk, rows.stop // tk)
        hb = h_ref[rows, :].astype(BF16)

        pn = _dot(hb, wnat_ref[...])
        c_q, c_kv, krg, kd = pn[:, :o1], pn[:, o1:o2], pn[:, o2:o3], pn[:, o3:]
        for hh in range(DIFF_HEADS):
            kd_ref[hh, rows, :] = kd[:, hh * 2 * DIFF_D:(hh + 1) * 2 * DIFF_D].astype(BF16)

        pt = _dot_nt(wt_ref[...], hb)
        qtd_ref[:, rows] = (pt[:nqd] * (DIFF_D ** -0.5 * LOG2E)).astype(BF16)
        for c, t in enumerate(key_tiles):
            vtd_ref[t] = pt[nqd:, c * tk:(c + 1) * tk].astype(BF16)

        cqn = _rms_norm_rows(c_q, gq_ref[...]).astype(BF16)
        qt = _dot_nt(wq_ref[...], cqn)
        cos_q, sin_q = cosq_ref[:, rows], sinq_ref[:, rows]
        for hh in range(MLA_HEADS):
            lo = hh * HEAD_PAD
            r0 = lo + MLA_NOPE
            r1, r2 = r0 + half, r0 + MLA_ROPE
            x1, x2 = qt[r0:r1], qt[r1:r2]
            qtm_ref[lo:r0, rows] = (qt[lo:r0] * q_scale).astype(BF16)
            qtm_ref[r0:r1, rows] = ((x1 * cos_q - x2 * sin_q) * q_scale).astype(BF16)
            qtm_ref[r1:r2, rows] = ((x2 * cos_q + x1 * sin_q) * q_scale).astype(BF16)
            qtm_ref[r2:lo + HEAD_PAD, rows] = jnp.zeros((lo + HEAD_PAD - r2, CHAIN_ROWS), BF16)

        ckvn = _rms_norm_rows(c_kv, gkv_ref[...]).astype(BF16)
        kn = _dot(ckvn, wkn_ref[...])
        krh = jnp.concatenate([krg[:, MLA_ROPE:], krg[:, :MLA_ROPE]], axis=1)
        kr = krg * cosk_ref[rows, :] + krh * sink_ref[rows, :]
        for hh in range(MLA_HEADS):
            lo, hi = hh * HEAD_PAD, (hh + 1) * HEAD_PAD
            km_ref[rows, lo:hi] = (kn[:, lo:hi] + kr).astype(BF16)
        vt = _dot_nt(wvt_ref[...], ckvn)
        for c, t in enumerate(key_tiles):
            vtm_ref[t] = vt[:, c * tk:(c + 1) * tk].astype(BF16)


def _in_proj_weights(w_in, w_q_b, w_kv_b):
    d = w_in.shape[0]
    o_cq = MLA_Q_RANK
    o_ckv = o_cq + MLA_KV_RANK
    o_kr = o_ckv + MLA_ROPE
    n_d = 2 * DIFF_HEADS * DIFF_D
    o_qd = o_kr + n_d
    o_kd = o_qd + n_d
    half = MLA_ROPE // 2
    pad = HEAD_PAD - MLA_NOPE - MLA_ROPE
    assert pad == MLA_ROPE
    kr = w_in[:, o_ckv:o_kr]
    krg = jnp.concatenate([jnp.zeros((d, MLA_NOPE), F32), kr, -kr[:, half:], kr[:, :half]], axis=1)
    w_nat = jnp.concatenate([w_in[:, :o_ckv], krg, w_in[:, o_qd:o_kd]], axis=1).astype(BF16)
    w_t = jnp.concatenate([w_in[:, o_kr:o_qd], w_in[:, o_kd:]], axis=1).T.astype(BF16)

    r = w_q_b.shape[0]
    wq = w_q_b.reshape(r, MLA_HEADS, MLA_NOPE + MLA_ROPE)
    wq = jnp.concatenate([wq, jnp.zeros((r, MLA_HEADS, pad), F32)], axis=-1)
    wq_t = wq.reshape(r, MLA_HEADS * HEAD_PAD).T.astype(BF16)

    rk = w_kv_b.shape[0]
    wkv = w_kv_b.reshape(rk, MLA_HEADS, MLA_NOPE + MLA_V)
    wkn = jnp.concatenate([wkv[..., :MLA_NOPE], jnp.zeros((rk, MLA_HEADS, HEAD_PAD - MLA_NOPE), F32)],
                          axis=-1).reshape(rk, MLA_HEADS * HEAD_PAD).astype(BF16)
    wv_t = wkv[..., MLA_NOPE:].reshape(rk, MLA_HEADS * MLA_V).T.astype(BF16)
    return w_nat, w_t, wq_t, wkn, wv_t


def _rope_tables(s):
    inv_freq = ROPE_BASE ** (-jnp.arange(0, MLA_ROPE, 2, dtype=F32) / MLA_ROPE)
    ang = jnp.arange(s).astype(F32)[:, None] * inv_freq[None, :]
    cos, sin = jnp.cos(ang), jnp.sin(ang)
    z_lo = jnp.zeros((s, MLA_NOPE), F32)
    z_hi = jnp.zeros((s, HEAD_PAD - MLA_NOPE - MLA_ROPE), F32)
    cos_k = jnp.concatenate([z_lo, cos, cos, z_hi], axis=1)
    sin_k = jnp.concatenate([z_lo, sin, sin, z_hi], axis=1)
    return cos_k, sin_k, cos.T, sin.T


def _in_proj(h, bsz, s, w_in, q_norm_g, kv_norm_g, w_q_b, w_kv_b):
    n, d = h.shape
    tm, tk = ROW_TILE, ATTN_TK
    assert s % tm == 0 and CHAIN_ROWS % tk == 0
    w_nat, w_t, wq_t, wkn, wv_t = _in_proj_weights(w_in, w_q_b, w_kv_b)
    cos_k, sin_k, cos_q, sin_q = _rope_tables(s)
    half = MLA_ROPE // 2
    spt = s // tm
    nqm = MLA_HEADS * HEAD_PAD
    nvm = MLA_HEADS * MLA_V
    nd = 2 * DIFF_HEADS * DIFF_D

    gq = q_norm_g.reshape(1, -1)
    gkv = kv_norm_g.reshape(1, -1)
    out_shape = (
        jax.ShapeDtypeStruct((bsz, nqm, s), BF16),
        jax.ShapeDtypeStruct((bsz, s, nqm), BF16),
        jax.ShapeDtypeStruct((bsz, s // tk, nvm, tk), BF16),
        jax.ShapeDtypeStruct((bsz, nd, s), BF16),
        jax.ShapeDtypeStruct((bsz, DIFF_HEADS, s, nd // DIFF_HEADS), BF16),
        jax.ShapeDtypeStruct((bsz, s // tk, nd, tk), BF16),
    )
    out_specs = (
        pl.BlockSpec((None, nqm, tm), lambda b, i: (b, 0, i)),
        pl.BlockSpec((None, tm, nqm), lambda b, i: (b, i, 0)),
        pl.BlockSpec((None, tm // tk, nvm, tk), lambda b, i: (b, i, 0, 0)),
        pl.BlockSpec((None, nd, tm), lambda b, i: (b, 0, i)),
        pl.BlockSpec((None, DIFF_HEADS, tm, nd // DIFF_HEADS), lambda b, i: (b, 0, i, 0)),
        pl.BlockSpec((None, tm // tk, nd, tk), lambda b, i: (b, i, 0, 0)),
    )
    return pl.pallas_call(
        _in_proj_kernel,
        grid=(bsz, spt),
        in_specs=[
            pl.BlockSpec((tm, d), lambda b, i: (b * spt + i, 0)),
            _resident(w_nat), _resident(w_t), _resident(wq_t), _resident(wkn), _resident(wv_t),
            _resident(gq), _resident(gkv),
            pl.BlockSpec((tm, HEAD_PAD), lambda b, i: (i, 0)),
            pl.BlockSpec((tm, HEAD_PAD), lambda b, i: (i, 0)),
            pl.BlockSpec((half, tm), lambda b, i: (0, i)),
            pl.BlockSpec((half, tm), lambda b, i: (0, i)),
        ],
        out_specs=out_specs,
        out_shape=out_shape,
        compiler_params=_params(("arbitrary", "arbitrary")),
        name="in_proj",
    )(h, w_nat, w_t, wq_t, wkn, wv_t, gq, gkv, cos_k, sin_k, cos_q, sin_q)


BIAS_TILES = 3


def _rel_bias_kernel(rb_ref, o_ref, *, tq, tk):
    c = pl.program_id(0) - 1
    kk = lax.broadcasted_iota(jnp.int32, (tk, tq), 0)
    qq = lax.broadcasted_iota(jnp.int32, (tk, tq), 1)
    n = jnp.maximum(qq - kk - c * tk, 0)
    max_exact = REL_BUCKETS // 2
    nf = jnp.maximum(n, 1).astype(F32)
    large = max_exact + (jnp.log(nf / max_exact) / math.log(REL_MAX_DIST / max_exact)
                         * (REL_BUCKETS - max_exact)).astype(jnp.int32)
    large = jnp.minimum(large, REL_BUCKETS - 1)
    bucket = jnp.where(n < max_exact, n, large)
    for m in range(o_ref.shape[0]):
        acc = jnp.full((tk, tq), rb_ref[0, m], F32)
        for bkt in range(1, REL_BUCKETS):
            acc = jnp.where(bucket >= bkt, rb_ref[bkt, m], acc)
        o_ref[m] = (acc - rb_ref[REL_BUCKETS - 1, m]) * LOG2E


def _rel_bias_tiles(rel_bias, tq, tk):
    assert tq == 2 * tk and tk + 1 >= REL_MAX_DIST
    nmaps = rel_bias.shape[1]
    return pl.pallas_call(
        functools.partial(_rel_bias_kernel, tq=tq, tk=tk),
        grid=(BIAS_TILES,),
        in_specs=[pl.BlockSpec(memory_space=pltpu.SMEM)],
        out_specs=pl.BlockSpec((nmaps, None, tk, tq), lambda t: (0, t, 0, 0)),
        out_shape=jax.ShapeDtypeStruct((nmaps, BIAS_TILES, tk, tq), F32),
        compiler_params=_params(("arbitrary",)),
        name="rel_bias",
    )(rel_bias)


SCORES_AHEAD = 3
VALUES_BEHIND = 1
ONES_ROWS = 16


def _flash_pipeline(q_fn, k_fn, vt_fn, logits_fn, finalize_fn, s_ref, mt_ref, p_ref, al_ref, m_ref, l_ref, acc_ref,
                    *, nstreams, nq, tq, tk, order):
    tiles = [(g, i, t, t - 2 * i) for g in range(nstreams) for i in range(nq) for t in range(2 * i + 2)]
    ones = jnp.ones((ONES_ROWS, tk), BF16)
    ahead = s_ref.shape[0] - 1
    behind = p_ref.shape[0] - 1

    def lane0(c):
        return tk if c == 1 else 0

    def scores(n):
        g, i, t, c = tiles[n]
        slot, lo = n % (ahead + 1), lane0(c)
        s = logits_fn(g, _dot(k_fn(g, t), q_fn(g, i, lo)), c, lo)
        s_ref[slot, :, lo:] = s
        mt_ref[slot, :, lo:] = jnp.max(s, axis=0, keepdims=True)

    def softmax(n):
        g, i, t, c = tiles[n]
        sslot, slot, lo = n % (ahead + 1), n % (behind + 1), lane0(c)
        m_new = mt_ref[sslot, :, lo:]
        if t > 0:
            m_prev = m_ref[:, lo:]
            m_new = jnp.maximum(m_prev, m_new)
            al_ref[slot, :, lo:] = jnp.exp2(m_prev - m_new)
        m_ref[:, lo:] = m_new
        p = jnp.exp2(s_ref[sslot, :, lo:] - m_new)
        p_ref[slot, :, lo:] = p.astype(BF16)
        if ONES_ROWS == 0:
            lslot = (g * nq + i) % 2
            l_new = jnp.sum(p, axis=0, keepdims=True)
            if t > 0:
                l_new = al_ref[slot, :, lo:] * l_ref[lslot, :, lo:] + l_new
            l_ref[lslot, :, lo:] = l_new

    def values(n):
        g, i, t, c = tiles[n]
        slot, lo = n % (behind + 1), lane0(c)
        vt = vt_fn(g, t)
        if ONES_ROWS:
            vt = jnp.concatenate([vt, ones], axis=0)
        pv = _dot(vt, p_ref[slot, :, lo:])
        if t > 0:
            pv = al_ref[slot, :, lo:] * acc_ref[:, lo:] + pv
        acc_ref[:, lo:] = pv
        if c == 1:
            acc = acc_ref[...]
            dv = acc.shape[0] - ONES_ROWS
            finalize_fn(g, i, acc[:dv] / (acc[dv:dv + 1] if ONES_ROWS else l_ref[(g * nq + i) % 2]))

    for n in range(ahead):
        scores(n)
    assert sorted(order) == ["c", "s", "v"] and (behind > 0 or order.index("s") < order.index("v"))
    for n in range(len(tiles) + behind):
        for stage in order:
            if stage == "v" and behind <= n:
                values(n - behind)
            if stage == "c" and n + ahead < len(tiles):
                scores(n + ahead)
            if stage == "s" and n < len(tiles):
                softmax(n)


def _tri_mask(s):
    kk = lax.broadcasted_iota(jnp.int32, s.shape, 0)
    qq = lax.broadcasted_iota(jnp.int32, s.shape, 1)
    return jnp.where(kk <= qq, s, -jnp.inf)


def _flash_scratch(dv, tq, tk, ahead=SCORES_AHEAD):
    return [pltpu.VMEM((ahead + 1, tk, tq), F32), pltpu.VMEM((ahead + 1, 1, tq), F32),
            pltpu.VMEM((VALUES_BEHIND + 1, tk, tq), BF16), pltpu.VMEM((VALUES_BEHIND + 1, 1, tq), F32),
            pltpu.VMEM((1, tq), F32), pltpu.VMEM((2, 1, tq), F32), pltpu.VMEM((dv + ONES_ROWS, tq), F32)]


def _mla_attn_kernel(qt_ref, k_ref, vt_ref, o_ref, *scratch, heads, tq, tk):

    def q_fn(g, i, lo):
        return qt_ref[g * HEAD_PAD:(g + 1) * HEAD_PAD, i * tq + lo:(i + 1) * tq]

    def k_fn(g, t):
        return k_ref[t * tk:(t + 1) * tk, g * HEAD_PAD:(g + 1) * HEAD_PAD]

    def vt_fn(g, t):
        return vt_ref[t, g * MLA_V:(g + 1) * MLA_V, :]

    def logits(g, s, c, lo):
        return _tri_mask(s) if c >= 0 else s

    def finalize(g, i, o):
        o_ref[g * MLA_V:(g + 1) * MLA_V, i * tq:(i + 1) * tq] = o.astype(BF16)

    _flash_pipeline(q_fn, k_fn, vt_fn, logits, finalize, *scratch,
                    nstreams=heads, nq=qt_ref.shape[1] // tq, tq=tq, tk=tk, order=MLA_STAGE_ORDER)


def _mla_attn(qt, k, vt, tq, tk):
    bsz, nq, s = qt.shape
    nkt = s // tk
    hps = MLA_HEADS_PER_STEP
    assert tq == 2 * tk and s % tq == 0 and MLA_HEADS % hps == 0
    return pl.pallas_call(
        functools.partial(_mla_attn_kernel, heads=hps, tq=tq, tk=tk),
        grid=(MLA_HEADS // hps, bsz),
        in_specs=[
            pl.BlockSpec((None, hps * HEAD_PAD, s), lambda h, b: (b, h, 0)),
            pl.BlockSpec((None, s, hps * HEAD_PAD), lambda h, b: (b, 0, h)),
            pl.BlockSpec((None, nkt, hps * MLA_V, tk), lambda h, b: (b, 0, h, 0)),
        ],
        out_specs=pl.BlockSpec((None, hps * MLA_V, s), lambda h, b: (b, h, 0)),
        out_shape=jax.ShapeDtypeStruct((bsz, MLA_HEADS * MLA_V, s), BF16),
        scratch_shapes=_flash_scratch(MLA_V, tq, tk),
        compiler_params=_params(("arbitrary", "arbitrary")),
        name="mla_attn",
    )(qt, k, vt)


def _diff_attn_kernel(qt_ref, k_ref, vt_ref, bias_ref, lamv_ref, gn_ref, o_ref, o0_ref, qm_ref, *scratch, tq, tk):
    qt = qt_ref[...]
    row = lax.broadcasted_iota(jnp.int32, qt.shape, 0)
    for g in range(2):
        keep = jnp.logical_and(row >= g * DIFF_D, row < (g + 1) * DIFF_D)
        qm_ref[g] = jnp.where(keep, qt, jnp.zeros_like(qt))

    lv = lamv_ref[...]
    lam = (jnp.exp(jnp.sum(lv[0:1] * lv[1:2], axis=1, keepdims=True))
           - jnp.exp(jnp.sum(lv[2:3] * lv[3:4], axis=1, keepdims=True)) + LAM_INIT)

    def q_fn(g, i, lo):
        return qm_ref[g, :, i * tq + lo:(i + 1) * tq]

    def k_fn(g, t):
        return k_ref[t * tk:(t + 1) * tk, :]

    def vt_fn(g, t):
        return vt_ref[t]

    def logits(g, s, c, lo):
        if c >= -1:
            s = s + bias_ref[g, c + 1, :, lo:]
        return _tri_mask(s) if c >= 0 else s

    def finalize(g, i, o):
        cols = slice(i * tq, (i + 1) * tq)
        if g == 0:
            o0_ref[:, cols] = o
        else:
            d = o0_ref[:, cols] - lam * o
            d = d * lax.rsqrt(jnp.mean(d * d, axis=0, keepdims=True) + RMS_EPS) * gn_ref[...] * (1.0 - LAM_INIT)
            o_ref[:, cols] = d.astype(BF16)

    _flash_pipeline(q_fn, k_fn, vt_fn, logits, finalize, *scratch,
                    nstreams=2, nq=qt_ref.shape[1] // tq, tq=tq, tk=tk, order=DIFF_STAGE_ORDER)


def _diff_attn(qt, k, vt, bias, lamv, gn, tq, tk):
    bsz, nd, s = qt.shape
    nkt = s // tk
    dv = 2 * DIFF_D
    assert tq == 2 * tk and s % tq == 0
    return pl.pallas_call(
        functools.partial(_diff_attn_kernel, tq=tq, tk=tk),
        grid=(DIFF_HEADS, bsz),
        in_specs=[
            pl.BlockSpec((None, dv, s), lambda h, b: (b, h, 0)),
            pl.BlockSpec((None, None, s, dv), lambda h, b: (b, h, 0, 0)),
            pl.BlockSpec((None, nkt, dv, tk), lambda h, b: (b, 0, h, 0)),
            pl.BlockSpec((2, BIAS_TILES, tk, tq), lambda h, b: (h, 0, 0, 0)),
            pl.BlockSpec(lamv.shape, lambda h, b: (0, 0)),
            pl.BlockSpec(gn.shape, lambda h, b: (0, 0)),
        ],
        out_specs=pl.BlockSpec((None, dv, s), lambda h, b: (b, h, 0)),
        out_shape=jax.ShapeDtypeStruct((bsz, nd, s), BF16),
        scratch_shapes=[pltpu.VMEM((dv, s), F32), pltpu.VMEM((2, dv, s), BF16)] + _flash_scratch(dv, tq, tk, 2),
        compiler_params=_params(("arbitrary", "arbitrary")),
        name="diff_attn",
    )(qt, k, vt, bias, lamv, gn)


def _post_mixer_kernel(h_ref, omt_ref, odt_ref, wum_ref, wud_ref, wg_ref, bg_ref, wo_ref, g_ref, b_ref,
                       o_ref):
    d = h_ref.shape[1]
    for rows in _chains(h_ref.shape[0]):
        h = h_ref[rows, :]
        y_m = _dot_tn(omt_ref[:, rows], wum_ref[...])
        y_d = _dot_tn(odt_ref[:, rows], wud_ref[...])
        z = _dot(h.astype(BF16), wg_ref[...]) + bg_ref[...]
        gate = 1.0 / (1.0 + jnp.exp(-z))
        mixed = gate[:, :d] * y_m + gate[:, d:] * y_d
        mix = _dot(mixed.astype(BF16), wo_ref[...])
        o_ref[rows, :] = _layer_norm(ALPHA * h + mix, g_ref[...], b_ref[...])


def _post_mixer(h, omt, odt, bsz, s, w_up_mla, w_up_diff, w_gate, b_gate, w_o, g, b):
    n, d = h.shape
    tm = ROW_TILE
    spt = s // tm
    ws = [w_up_mla.astype(BF16), w_up_diff.astype(BF16), w_gate.astype(BF16), b_gate.reshape(1, -1),
          w_o.astype(BF16), g.reshape(1, d), b.reshape(1, d)]

    return pl.pallas_call(
        _post_mixer_kernel,
        grid=(bsz, spt),
        in_specs=[
            pl.BlockSpec((tm, d), lambda bb, i: (bb * spt + i, 0)),
            pl.BlockSpec((None, omt.shape[1], tm), lambda bb, i: (bb, 0, i)),
            pl.BlockSpec((None, odt.shape[1], tm), lambda bb, i: (bb, 0, i)),
        ] + [_resident(a) for a in ws],
        out_specs=pl.BlockSpec((tm, d), lambda bb, i: (bb * spt + i, 0)),
        out_shape=jax.ShapeDtypeStruct((n, d), F32),
        compiler_params=_params(("arbitrary", "arbitrary")),
        name="post_mixer",
    )(h, omt, odt, *ws)


def _mem_kv_kernel(mem_ref, w_ref, k_ref, v_ref):
    d = k_ref.shape[1]
    kv = _dot(mem_ref[...].astype(BF16), w_ref[...])
    k_ref[...] = kv[:, :d].astype(BF16)
    v_ref[...] = kv[:, d:].astype(BF16)


def _mem_kv(mem, w_kv):
    bsz, m, d = mem.shape
    w = w_kv.astype(BF16)
    spec = pl.BlockSpec((None, m, d), lambda b: (b, 0, 0))
    return pl.pallas_call(
        _mem_kv_kernel,
        grid=(bsz,),
        in_specs=[spec, pl.BlockSpec(w.shape, lambda b: (0, 0))],
        out_specs=(spec, spec),
        out_shape=(jax.ShapeDtypeStruct((bsz, m, d), BF16), jax.ShapeDtypeStruct((bsz, m, d), BF16)),
        compiler_params=_params(("arbitrary",)),
        name="mem_kv",
    )(mem, w)


def _mem_attn_kernel(h_ref, k_ref, v_ref, wq_ref, wo_ref, g_ref, b_ref, o_ref):
    d = h_ref.shape[1]
    hd = d // MEM_HEADS
    for rows in _chains(h_ref.shape[0], MEM_CHAIN_ROWS):
        h = h_ref[rows, :]
        q = _dot(h.astype(BF16), wq_ref[...]).astype(BF16)
        outs = []
        for hh in range(MEM_HEADS):
            lo, hi = hh * hd, (hh + 1) * hd
            s = _dot_nt(q[:, lo:hi], k_ref[:, lo:hi]) * (hd ** -0.5)
            p = jnp.exp(s - jnp.max(s, axis=-1, keepdims=True))
            p = p / jnp.sum(p, axis=-1, keepdims=True)
            outs.append(_dot(p.astype(BF16), v_ref[:, lo:hi]).astype(BF16))
        att = _dot(jnp.concatenate(outs, axis=1), wo_ref[...])
        o_ref[rows, :] = _layer_norm(ALPHA * h + att, g_ref[...], b_ref[...])


def _mem_attn(h, km, vm, bsz, s, w_q, w_o, g, b):
    n, d = h.shape
    tm = WIDE_ROW_TILE
    spt = s // tm
    m = km.shape[1]
    ws = [w_q.astype(BF16), w_o.astype(BF16), g.reshape(1, d), b.reshape(1, d)]

    return pl.pallas_call(
        _mem_attn_kernel,
        grid=(bsz, spt),
        in_specs=[
            pl.BlockSpec((tm, d), lambda bb, i: (bb * spt + i, 0)),
            pl.BlockSpec((None, m, d), lambda bb, i: (bb, 0, 0)),
            pl.BlockSpec((None, m, d), lambda bb, i: (bb, 0, 0)),
        ] + [_resident(a) for a in ws],
        out_specs=pl.BlockSpec((tm, d), lambda bb, i: (bb * spt + i, 0)),
        out_shape=jax.ShapeDtypeStruct((n, d), F32),
        compiler_params=_params(("arbitrary", "arbitrary")),
        name="mem_attn",
    )(h, km, vm, *ws)


def kernel(x, mem, rel_bias, ffn1_w1, ffn1_w3, ffn1_w2, ln1_g, ln1_b, w_in, q_norm_g, kv_norm_g, w_q_b, w_kv_b, lam_q1, lam_k1, lam_q2, lam_k2, diff_norm_g, w_gate, b_gate, w_up_mla, w_up_diff, w_o, ln2_g, ln2_b, mem_w_q, mem_w_kv, mem_w_o, ln3_g, ln3_b, ffn2_w1, ffn2_w3, ffn2_w2, ln4_g, ln4_b):
    bsz, s, d = x.shape
    assert ffn1_w1.shape[0] == DEPTH
    l = 0
    h = x.reshape(bsz * s, d)

    h = _ffn_ln(h, ffn1_w1[l], ffn1_w3[l], ffn1_w2[l], ln1_g[l], ln1_b[l])

    qtm, km, vtm, qtd, kd, vtd = _in_proj(h, bsz, s, w_in[l], q_norm_g[l], kv_norm_g[l], w_q_b[l], w_kv_b[l])
    bias = _rel_bias_tiles(rel_bias, ATTN_TQ, ATTN_TK)
    omt = _mla_attn(qtm, km, vtm, ATTN_TQ, ATTN_TK)
    lamv = jnp.stack([lam_q1[l], lam_k1[l], lam_q2[l], lam_k2[l]], axis=0)
    odt = _diff_attn(qtd, kd, vtd, bias, lamv, diff_norm_g[l].reshape(-1, 1), ATTN_TQ, ATTN_TK)
    h = _post_mixer(h, omt, odt, bsz, s, w_up_mla[l], w_up_diff[l], w_gate[l], b_gate[l], w_o[l],
                    ln2_g[l], ln2_b[l])

    kmem, vmem = _mem_kv(mem, mem_w_kv[l])
    h = _mem_attn(h, kmem, vmem, bsz, s, mem_w_q[l], mem_w_o[l], ln3_g[l], ln3_b[l])

    h = _ffn_ln(h, ffn2_w1[l], ffn2_w3[l], ffn2_w2[l], ln4_g[l], ln4_b[l])
    return h.reshape(bsz, s, d)
```

```python
import functools
import math

import jax
import jax.numpy as jnp
from jax import lax
from jax.experimental import pallas as pl
from jax.experimental.pallas import tpu as pltpu

F32 = jnp.float32
BF16 = jnp.bfloat16

DEPTH = 1
MLA_HEADS = 8
MLA_Q_RANK = 384
MLA_KV_RANK = 256
MLA_NOPE = 64
MLA_ROPE = 32
MLA_V = 64
ROPE_BASE = 10000.0
DIFF_HEADS = 4
DIFF_D = 64
REL_BUCKETS = 32
REL_MAX_DIST = 128
MEM_HEADS = 4
ALPHA = (2.0 * DEPTH) ** 0.25
LN_EPS = 1e-5
RMS_EPS = 1e-6
LAM_INIT = 0.8 - 0.6 * math.exp(-0.3 * 0)
LOG2E = math.log2(math.e)

HEAD_PAD = 128
VMEM_LIMIT_BYTES = 56 * 1024 * 1024

ROW_TILE = 1024
WIDE_ROW_TILE = 2048
CHAIN_ROWS = 256
MEM_CHAIN_ROWS = 512
ATTN_TQ = 512
ATTN_TK = 256
MLA_HEADS_PER_STEP = 4
MLA_STAGE_ORDER = "vcs"
DIFF_STAGE_ORDER = "csv"


def _params(semantics):
    return pltpu.CompilerParams(dimension_semantics=semantics, vmem_limit_bytes=VMEM_LIMIT_BYTES)


def _dot(a, b):
    return jnp.dot(a, b, preferred_element_type=F32)


def _dot_nt(a, b):
    return lax.dot_general(a, b, (((1,), (1,)), ((), ())), preferred_element_type=F32)


def _dot_tn(a, b):
    return lax.dot_general(a, b, (((0,), (0,)), ((), ())), preferred_element_type=F32)


def _layer_norm(y, g, b):
    mu = jnp.mean(y, axis=-1, keepdims=True)
    d = y - mu
    var = jnp.mean(d * d, axis=-1, keepdims=True)
    return d * lax.rsqrt(var + LN_EPS) * g + b


def _rms_norm_rows(x, g):
    return x * lax.rsqrt(jnp.mean(x * x, axis=-1, keepdims=True) + RMS_EPS) * g


def _chains(rows, chain=CHAIN_ROWS):
    return [slice(r, r + chain) for r in range(0, rows, chain)]


def _resident(a):
    return pl.BlockSpec(a.shape, lambda *_: (0,) * a.ndim, pipeline_mode=pl.Buffered(1))


def _ffn_ln_kernel(x_ref, w1_ref, w3_ref, w2_ref, g_ref, b_ref, o_ref):
    for rows in _chains(x_ref.shape[0]):
        x = x_ref[rows, :]
        xb = x.astype(BF16)
        u = _dot(xb, w1_ref[...])
        v = _dot(xb, w3_ref[...])
        act = (u / (1.0 + jnp.exp(-u))) * v
        y = ALPHA * x + 0.5 * _dot(act.astype(BF16), w2_ref[...])
        o_ref[rows, :] = _layer_norm(y, g_ref[...], b_ref[...])


def _ffn_ln(x, w1, w3, w2, g, b):
    n, d = x.shape
    tm = ROW_TILE
    assert n % tm == 0
    ws = [w1.astype(BF16), w3.astype(BF16), w2.astype(BF16), g.reshape(1, d), b.reshape(1, d)]
    return pl.pallas_call(
        _ffn_ln_kernel,
        grid=(n // tm,),
        in_specs=[pl.BlockSpec((tm, d), lambda i: (i, 0))] + [_resident(a) for a in ws],
        out_specs=pl.BlockSpec((tm, d), lambda i: (i, 0)),
        out_shape=jax.ShapeDtypeStruct((n, d), F32),
        compiler_params=_params(("arbitrary",)),
        name="ffn_ln",
    )(x, *ws)


def _in_proj_kernel(h_ref, wnat_ref, wt_ref, wq_ref, wkn_ref, wvt_ref, gq_ref, gkv_ref,
                    cosk_ref, sink_ref, cosq_ref, sinq_ref,
                    qtm_ref, km_ref, vtm_ref, qtd_ref, kd_ref, vtd_ref):
    tk = vtm_ref.shape[-1]
    half = MLA_ROPE // 2
    nqd = qtd_ref.shape[0]
    o1 = MLA_Q_RANK
    o2 = o1 + MLA_KV_RANK
    o3 = o2 + HEAD_PAD
    q_scale = (MLA_NOPE + MLA_ROPE) ** -0.5 * LOG2E
    for rows in _chains(h_ref.shape[0]):
        key_tiles = range(rows.start // tk, rows.stop // tk)
        hb = h_ref[rows, :].astype(BF16)

        pt = _dot_nt(wt_ref[...], hb)
        qtd_ref[:, rows] = (pt[:nqd] * (DIFF_D ** -0.5 * LOG2E)).astype(BF16)
        for c, t in enumerate(key_tiles):
            vtd_ref[t] = pt[nqd:, c * tk:(c + 1) * tk].astype(BF16)

        pn = _dot(hb, wnat_ref[...])
        c_q, c_kv, krg, kd = pn[:, :o1], pn[:, o1:o2], pn[:, o2:o3], pn[:, o3:]
        for hh in range(DIFF_HEADS):
            kd_ref[hh, rows, :] = kd[:, hh * 2 * DIFF_D:(hh + 1) * 2 * DIFF_D].astype(BF16)

        cqn = _rms_norm_rows(c_q, gq_ref[...]).astype(BF16)
        qt = _dot_nt(wq_ref[...], cqn)
        cos_q, sin_q = cosq_ref[:, rows], sinq_ref[:, rows]
        for hh in range(MLA_HEADS):
            lo = hh * HEAD_PAD
            r0 = lo + MLA_NOPE
            r1, r2 = r0 + half, r0 + MLA_ROPE
            x1, x2 = qt[r0:r1], qt[r1:r2]
            qtm_ref[lo:r0, rows] = (qt[lo:r0] * q_scale).astype(BF16)
            qtm_ref[r0:r1, rows] = ((x1 * cos_q - x2 * sin_q) * q_scale).astype(BF16)
            qtm_ref[r1:r2, rows] = ((x2 * cos_q + x1 * sin_q) * q_scale).astype(BF16)
            qtm_ref[r2:lo + HEAD_PAD, rows] = jnp.zeros((lo + HEAD_PAD - r2, CHAIN_ROWS), BF16)

        ckvn = _rms_norm_rows(c_kv, gkv_ref[...]).astype(BF16)
        kn = _dot(ckvn, wkn_ref[...])
        krh = jnp.concatenate([krg[:, MLA_ROPE:], krg[:, :MLA_ROPE]], axis=1)
        kr = krg * cosk_ref[rows, :] + krh * sink_ref[rows, :]
        for hh in range(MLA_HEADS):
            lo, hi = hh * HEAD_PAD, (hh + 1) * HEAD_PAD
            km_ref[rows, lo:hi] = (kn[:, lo:hi] + kr).astype(BF16)
        vt = _dot_nt(wvt_ref[...], ckvn)
        for c, t in enumerate(key_tiles):
            vtm_ref[t] = vt[:, c * tk:(c + 1) * tk].astype(BF16)


def _in_proj_weights(w_in, w_q_b, w_kv_b):
    d = w_in.shape[0]
    o_cq = MLA_Q_RANK
    o_ckv = o_cq + MLA_KV_RANK
    o_kr = o_ckv + MLA_ROPE
    n_d = 2 * DIFF_HEADS * DIFF_D
    o_qd = o_kr + n_d
    o_kd = o_qd + n_d
    half = MLA_ROPE // 2
    pad = HEAD_PAD - MLA_NOPE - MLA_ROPE
    assert pad == MLA_ROPE
    kr = w_in[:, o_ckv:o_kr]
    krg = jnp.concatenate([jnp.zeros((d, MLA_NOPE), F32), kr, -kr[:, half:], kr[:, :half]], axis=1)
    w_nat = jnp.concatenate([w_in[:, :o_ckv], krg, w_in[:, o_qd:o_kd]], axis=1).astype(BF16)
    w_t = jnp.concatenate([w_in[:, o_kr:o_qd], w_in[:, o_kd:]], axis=1).T.astype(BF16)

    r = w_q_b.shape[0]
    wq = w_q_b.reshape(r, MLA_HEADS, MLA_NOPE + MLA_ROPE)
    wq = jnp.concatenate([wq, jnp.zeros((r, MLA_HEADS, pad), F32)], axis=-1)
    wq_t = wq.reshape(r, MLA_HEADS * HEAD_PAD).T.astype(BF16)

    rk = w_kv_b.shape[0]
    wkv = w_kv_b.reshape(rk, MLA_HEADS, MLA_NOPE + MLA_V)
    wkn = jnp.concatenate([wkv[..., :MLA_NOPE], jnp.zeros((rk, MLA_HEADS, HEAD_PAD - MLA_NOPE), F32)],
                          axis=-1).reshape(rk, MLA_HEADS * HEAD_PAD).astype(BF16)
    wv_t = wkv[..., MLA_NOPE:].reshape(rk, MLA_HEADS * MLA_V).T.astype(BF16)
    return w_nat, w_t, wq_t, wkn, wv_t


def _rope_tables(s):
    inv_freq = ROPE_BASE ** (-jnp.arange(0, MLA_ROPE, 2, dtype=F32) / MLA_ROPE)
    ang = jnp.arange(s).astype(F32)[:, None] * inv_freq[None, :]
    cos, sin = jnp.cos(ang), jnp.sin(ang)
    z_lo = jnp.zeros((s, MLA_NOPE), F32)
    z_hi = jnp.zeros((s, HEAD_PAD - MLA_NOPE - MLA_ROPE), F32)
    cos_k = jnp.concatenate([z_lo, cos, cos, z_hi], axis=1)
    sin_k = jnp.concatenate([z_lo, sin, sin, z_hi], axis=1)
    return cos_k, sin_k, cos.T, sin.T


def _in_proj(h, bsz, s, w_in, q_norm_g, kv_norm_g, w_q_b, w_kv_b):
    n, d = h.shape
    tm, tk = ROW_TILE, ATTN_TK
    assert s % tm == 0 and CHAIN_ROWS % tk == 0
    w_nat, w_t, wq_t, wkn, wv_t = _in_proj_weights(w_in, w_q_b, w_kv_b)
    cos_k, sin_k, cos_q, sin_q = _rope_tables(s)
    half = MLA_ROPE // 2
    spt = s // tm
    nqm = MLA_HEADS * HEAD_PAD
    nvm = MLA_HEADS * MLA_V
    nd = 2 * DIFF_HEADS * DIFF_D

    gq = q_norm_g.reshape(1, -1)
    gkv = kv_norm_g.reshape(1, -1)
    out_shape = (
        jax.ShapeDtypeStruct((bsz, nqm, s), BF16),
        jax.ShapeDtypeStruct((bsz, s, nqm), BF16),
        jax.ShapeDtypeStruct((bsz, s // tk, nvm, tk), BF16),
        jax.ShapeDtypeStruct((bsz, nd, s), BF16),
        jax.ShapeDtypeStruct((bsz, DIFF_HEADS, s, nd // DIFF_HEADS), BF16),
        jax.ShapeDtypeStruct((bsz, s // tk, nd, tk), BF16),
    )
    out_specs = (
        pl.BlockSpec((None, nqm, tm), lambda b, i: (b, 0, i)),
        pl.BlockSpec((None, tm, nqm), lambda b, i: (b, i, 0)),
        pl.BlockSpec((None, tm // tk, nvm, tk), lambda b, i: (b, i, 0, 0)),
        pl.BlockSpec((None, nd, tm), lambda b, i: (b, 0, i)),
        pl.BlockSpec((None, DIFF_HEADS, tm, nd // DIFF_HEADS), lambda b, i: (b, 0, i, 0)),
        pl.BlockSpec((None, tm // tk, nd, tk), lambda b, i: (b, i, 0, 0)),
    )
    return pl.pallas_call(
        _in_proj_kernel,
        grid=(bsz, spt),
        in_specs=[
            pl.BlockSpec((tm, d), lambda b, i: (b * spt + i, 0)),
            _resident(w_nat), _resident(w_t), _resident(wq_t), _resident(wkn), _resident(wv_t),
            _resident(gq), _resident(gkv),
            pl.BlockSpec((tm, HEAD_PAD), lambda b, i: (i, 0)),
            pl.BlockSpec((tm, HEAD_PAD), lambda b, i: (i, 0)),
            pl.BlockSpec((half, tm), lambda b, i: (0, i)),
            pl.BlockSpec((half, tm), lambda b, i: (0, i)),
        ],
        out_specs=out_specs,
        out_shape=out_shape,
        compiler_params=_params(("arbitrary", "arbitrary")),
        name="in_proj",
    )(h, w_nat, w_t, wq_t, wkn, wv_t, gq, gkv, cos_k, sin_k, cos_q, sin_q)


BIAS_TILES = 3


def _rel_bias_kernel(rb_ref, o_ref, *, tq, tk):
    c = pl.program_id(0) - 1
    kk = lax.broadcasted_iota(jnp.int32, (tk, tq), 0)
    qq = lax.broadcasted_iota(jnp.int32, (tk, tq), 1)
    n = jnp.maximum(qq - kk - c * tk, 0)
    max_exact = REL_BUCKETS // 2
    nf = jnp.maximum(n, 1).astype(F32)
    large = max_exact + (jnp.log(nf / max_exact) / math.log(REL_MAX_DIST / max_exact)
                         * (REL_BUCKETS - max_exact)).astype(jnp.int32)
    large = jnp.minimum(large, REL_BUCKETS - 1)
    bucket = jnp.where(n < max_exact, n, large)
    for m in range(o_ref.shape[0]):
        acc = jnp.full((tk, tq), rb_ref[0, m], F32)
        for bkt in range(1, REL_BUCKETS):
            acc = jnp.where(bucket >= bkt, rb_ref[bkt, m], acc)
        o_ref[m] = (acc - rb_ref[REL_BUCKETS - 1, m]) * LOG2E


def _rel_bias_tiles(rel_bias, tq, tk):
    assert tq == 2 * tk and tk + 1 >= REL_MAX_DIST
    nmaps = rel_bias.shape[1]
    return pl.pallas_call(
        functools.partial(_rel_bias_kernel, tq=tq, tk=tk),
        grid=(BIAS_TILES,),
        in_specs=[pl.BlockSpec(memory_space=pltpu.SMEM)],
        out_specs=pl.BlockSpec((nmaps, None, tk, tq), lambda t: (0, t, 0, 0)),
        out_shape=jax.ShapeDtypeStruct((nmaps, BIAS_TILES, tk, tq), F32),
        compiler_params=_params(("arbitrary",)),
        name="rel_bias",
    )(rel_bias)


SCORES_AHEAD = 3
VALUES_BEHIND = 1
ONES_ROWS = 16


def _flash_pipeline(q_fn, k_fn, vt_fn, logits_fn, finalize_fn, s_ref, mt_ref, p_ref, al_ref, m_ref, l_ref, acc_ref,
                    *, nstreams, nq, tq, tk, order):
    tiles = [(g, i, t, t - 2 * i) for g in range(nstreams) for i in range(nq) for t in range(2 * i + 2)]
    ones = jnp.ones((ONES_ROWS, tk), BF16)
    ahead = s_ref.shape[0] - 1
    behind = p_ref.shape[0] - 1

    def lane0(c):
        return tk if c == 1 else 0

    def scores(n):
        g, i, t, c = tiles[n]
        slot, lo = n % (ahead + 1), lane0(c)
        s = logits_fn(g, _dot(k_fn(g, t), q_fn(g, i, lo)), c, lo)
        s_ref[slot, :, lo:] = s
        mt_ref[slot, :, lo:] = jnp.max(s, axis=0, keepdims=True)

    def softmax(n):
        g, i, t, c = tiles[n]
        sslot, slot, lo = n % (ahead + 1), n % (behind + 1), lane0(c)
        m_new = mt_ref[sslot, :, lo:]
        if t > 0:
            m_prev = m_ref[:, lo:]
            m_new = jnp.maximum(m_prev, m_new)
            al_ref[slot, :, lo:] = jnp.exp2(m_prev - m_new)
        m_ref[:, lo:] = m_new
        p = jnp.exp2(s_ref[sslot, :, lo:] - m_new)
        p_ref[slot, :, lo:] = p.astype(BF16)
        if ONES_ROWS == 0:
            lslot = (g * nq + i) % 2
            l_new = jnp.sum(p, axis=0, keepdims=True)
            if t > 0:
                l_new = al_ref[slot, :, lo:] * l_ref[lslot, :, lo:] + l_new
            l_ref[lslot, :, lo:] = l_new

    def values(n):
        g, i, t, c = tiles[n]
        slot, lo = n % (behind + 1), lane0(c)
        vt = vt_fn(g, t)
        if ONES_ROWS:
            vt = jnp.concatenate([vt, ones], axis=0)
        pv = _dot(vt, p_ref[slot, :, lo:])
        if t > 0:
            pv = al_ref[slot, :, lo:] * acc_ref[:, lo:] + pv
        acc_ref[:, lo:] = pv
        if c == 1:
            acc = acc_ref[...]
            dv = acc.shape[0] - ONES_ROWS
            finalize_fn(g, i, acc[:dv] / (acc[dv:dv + 1] if ONES_ROWS else l_ref[(g * nq + i) % 2]))

    for n in range(ahead):
        scores(n)
    assert sorted(order) == ["c", "s", "v"] and (behind > 0 or order.index("s") < order.index("v"))
    for n in range(len(tiles) + behind):
        for stage in order:
            if stage == "v" and behind <= n:
                values(n - behind)
            if stage == "c" and n + ahead < len(tiles):
                scores(n + ahead)
            if stage == "s" and n < len(tiles):
                softmax(n)


def _tri_mask(s):
    kk = lax.broadcasted_iota(jnp.int32, s.shape, 0)
    qq = lax.broadcasted_iota(jnp.int32, s.shape, 1)
    return jnp.where(kk <= qq, s, -jnp.inf)


def _flash_scratch(dv, tq, tk, ahead=SCORES_AHEAD):
    return [pltpu.VMEM((ahead + 1, tk, tq), F32), pltpu.VMEM((ahead + 1, 1, tq), F32),
            pltpu.VMEM((VALUES_BEHIND + 1, tk, tq), BF16), pltpu.VMEM((VALUES_BEHIND + 1, 1, tq), F32),
            pltpu.VMEM((1, tq), F32), pltpu.VMEM((2, 1, tq), F32), pltpu.VMEM((dv + ONES_ROWS, tq), F32)]


def _mla_attn_kernel(qt_ref, k_ref, vt_ref, o_ref, *scratch, heads, tq, tk):

    def q_fn(g, i, lo):
        return qt_ref[g * HEAD_PAD:(g + 1) * HEAD_PAD, i * tq + lo:(i + 1) * tq]

    def k_fn(g, t):
        return k_ref[t * tk:(t + 1) * tk, g * HEAD_PAD:(g + 1) * HEAD_PAD]

    def vt_fn(g, t):
        return vt_ref[t, g * MLA_V:(g + 1) * MLA_V, :]

    def logits(g, s, c, lo):
        return _tri_mask(s) if c >= 0 else s

    def finalize(g, i, o):
        o_ref[g * MLA_V:(g + 1) * MLA_V, i * tq:(i + 1) * tq] = o.astype(BF16)

    _flash_pipeline(q_fn, k_fn, vt_fn, logits, finalize, *scratch,
                    nstreams=heads, nq=qt_ref.shape[1] // tq, tq=tq, tk=tk, order=MLA_STAGE_ORDER)


def _mla_attn(qt, k, vt, tq, tk):
    bsz, nq, s = qt.shape
    nkt = s // tk
    hps = MLA_HEADS_PER_STEP
    assert tq == 2 * tk and s % tq == 0 and MLA_HEADS % hps == 0
    return pl.pallas_call(
        functools.partial(_mla_attn_kernel, heads=hps, tq=tq, tk=tk),
        grid=(MLA_HEADS // hps, bsz),
        in_specs=[
            pl.BlockSpec((None, hps * HEAD_PAD, s), lambda h, b: (b, h, 0)),
            pl.BlockSpec((None, s, hps * HEAD_PAD), lambda h, b: (b, 0, h)),
            pl.BlockSpec((None, nkt, hps * MLA_V, tk), lambda h, b: (b, 0, h, 0)),
        ],
        out_specs=pl.BlockSpec((None, hps * MLA_V, s), lambda h, b: (b, h, 0)),
        out_shape=jax.ShapeDtypeStruct((bsz, MLA_HEADS * MLA_V, s), BF16),
        scratch_shapes=_flash_scratch(MLA_V, tq, tk),
        compiler_params=_params(("arbitrary", "arbitrary")),
        name="mla_attn",
    )(qt, k, vt)


def _diff_attn_kernel(qt_ref, k_ref, vt_ref, bias_ref, lamv_ref, gn_ref, o_ref, o0_ref, qm_ref, *scratch, tq, tk):
    qt = qt_ref[...]
    row = lax.broadcasted_iota(jnp.int32, qt.shape, 0)
    for g in range(2):
        keep = jnp.logical_and(row >= g * DIFF_D, row < (g + 1) * DIFF_D)
        qm_ref[g] = jnp.where(keep, qt, jnp.zeros_like(qt))

    lv = lamv_ref[...]
    lam = (jnp.exp(jnp.sum(lv[0:1] * lv[1:2], axis=1, keepdims=True))
           - jnp.exp(jnp.sum(lv[2:3] * lv[3:4], axis=1, keepdims=True)) + LAM_INIT)

    def q_fn(g, i, lo):
        return qm_ref[g, :, i * tq + lo:(i + 1) * tq]

    def k_fn(g, t):
        return k_ref[t * tk:(t + 1) * tk, :]

    def vt_fn(g, t):
        return vt_ref[t]

    def logits(g, s, c, lo):
        if c >= -1:
            s = s + bias_ref[g, c + 1, :, lo:]
        return _tri_mask(s) if c >= 0 else s

    def finalize(g, i, o):
        cols = slice(i * tq, (i + 1) * tq)
        if g == 0:
            o0_ref[:, cols] = o
        else:
            d = o0_ref[:, cols] - lam * o
            d = d * lax.rsqrt(jnp.mean(d * d, axis=0, keepdims=True) + RMS_EPS) * gn_ref[...] * (1.0 - LAM_INIT)
            o_ref[:, cols] = d.astype(BF16)

    _flash_pipeline(q_fn, k_fn, vt_fn, logits, finalize, *scratch,
                    nstreams=2, nq=qt_ref.shape[1] // tq, tq=tq, tk=tk, order=DIFF_STAGE_ORDER)


def _diff_attn(qt, k, vt, bias, lamv, gn, tq, tk):
    bsz, nd, s = qt.shape
    nkt = s // tk
    dv = 2 * DIFF_D
    assert tq == 2 * tk and s % tq == 0
    return pl.pallas_call(
        functools.partial(_diff_attn_kernel, tq=tq, tk=tk),
        grid=(DIFF_HEADS, bsz),
        in_specs=[
            pl.BlockSpec((None, dv, s), lambda h, b: (b, h, 0)),
            pl.BlockSpec((None, None, s, dv), lambda h, b: (b, h, 0, 0)),
            pl.BlockSpec((None, nkt, dv, tk), lambda h, b: (b, 0, h, 0)),
            pl.BlockSpec((2, BIAS_TILES, tk, tq), lambda h, b: (h, 0, 0, 0)),
            pl.BlockSpec(lamv.shape, lambda h, b: (0, 0)),
            pl.BlockSpec(gn.shape, lambda h, b: (0, 0)),
        ],
        out_specs=pl.BlockSpec((None, dv, s), lambda h, b: (b, h, 0)),
        out_shape=jax.ShapeDtypeStruct((bsz, nd, s), BF16),
        scratch_shapes=[pltpu.VMEM((dv, s), F32), pltpu.VMEM((2, dv, s), BF16)] + _flash_scratch(dv, tq, tk),
        compiler_params=_params(("arbitrary", "arbitrary")),
        name="diff_attn",
    )(qt, k, vt, bias, lamv, gn)


def _post_mixer_kernel(h_ref, omt_ref, odt_ref, wum_ref, wud_ref, wg_ref, bg_ref, wo_ref, g_ref, b_ref,
                       o_ref):
    d = h_ref.shape[1]
    for rows in _chains(h_ref.shape[0]):
        h = h_ref[rows, :]
        z = _dot(h.astype(BF16), wg_ref[...]) + bg_ref[...]
        gate = 1.0 / (1.0 + jnp.exp(-z))
        y_m = _dot_tn(omt_ref[:, rows], wum_ref[...])
        y_d = _dot_tn(odt_ref[:, rows], wud_ref[...])
        mixed = gate[:, :d] * y_m + gate[:, d:] * y_d
        mix = _dot(mixed.astype(BF16), wo_ref[...])
        o_ref[rows, :] = _layer_norm(ALPHA * h + mix, g_ref[...], b_ref[...])


def _post_mixer(h, omt, odt, bsz, s, w_up_mla, w_up_diff, w_gate, b_gate, w_o, g, b):
    n, d = h.shape
    tm = ROW_TILE
    spt = s // tm
    ws = [w_up_mla.astype(BF16), w_up_diff.astype(BF16), w_gate.astype(BF16), b_gate.reshape(1, -1),
          w_o.astype(BF16), g.reshape(1, d), b.reshape(1, d)]

    return pl.pallas_call(
        _post_mixer_kernel,
        grid=(bsz, spt),
        in_specs=[
            pl.BlockSpec((tm, d), lambda bb, i: (bb * spt + i, 0)),
            pl.BlockSpec((None, omt.shape[1], tm), lambda bb, i: (bb, 0, i)),
            pl.BlockSpec((None, odt.shape[1], tm), lambda bb, i: (bb, 0, i)),
        ] + [_resident(a) for a in ws],
        out_specs=pl.BlockSpec((tm, d), lambda bb, i: (bb * spt + i, 0)),
        out_shape=jax.ShapeDtypeStruct((n, d), F32),
        compiler_params=_params(("arbitrary", "arbitrary")),
        name="post_mixer",
    )(h, omt, odt, *ws)


def _mem_kv_kernel(mem_ref, w_ref, k_ref, v_ref):
    d = k_ref.shape[1]
    kv = _dot(mem_ref[...].astype(BF16), w_ref[...])
    k_ref[...] = kv[:, :d].astype(BF16)
    v_ref[...] = kv[:, d:].astype(BF16)


def _mem_kv(mem, w_kv):
    bsz, m, d = mem.shape
    w = w_kv.astype(BF16)
    spec = pl.BlockSpec((None, m, d), lambda b: (b, 0, 0))
    return pl.pallas_call(
        _mem_kv_kernel,
        grid=(bsz,),
        in_specs=[spec, pl.BlockSpec(w.shape, lambda b: (0, 0))],
        out_specs=(spec, spec),
        out_shape=(jax.ShapeDtypeStruct((bsz, m, d), BF16), jax.ShapeDtypeStruct((bsz, m, d), BF16)),
        compiler_params=_params(("arbitrary",)),
        name="mem_kv",
    )(mem, w)


def _mem_attn_kernel(h_ref, k_ref, v_ref, wq_ref, wo_ref, g_ref, b_ref, o_ref):
    d = h_ref.shape[1]
    hd = d // MEM_HEADS
    for rows in _chains(h_ref.shape[0], MEM_CHAIN_ROWS):
        h = h_ref[rows, :]
        q = _dot(h.astype(BF16), wq_ref[...]).astype(BF16)
        outs = []
        for hh in range(MEM_HEADS):
            lo, hi = hh * hd, (hh + 1) * hd
            s = _dot_nt(q[:, lo:hi], k_ref[:, lo:hi]) * (hd ** -0.5)
            p = jnp.exp(s - jnp.max(s, axis=-1, keepdims=True))
            p = p / jnp.sum(p, axis=-1, keepdims=True)
            outs.append(_dot(p.astype(BF16), v_ref[:, lo:hi]).astype(BF16))
        att = _dot(jnp.concatenate(outs, axis=1), wo_ref[...])
        o_ref[rows, :] = _layer_norm(ALPHA * h + att, g_ref[...], b_ref[...])


def _mem_attn(h, km, vm, bsz, s, w_q, w_o, g, b):
    n, d = h.shape
    tm = WIDE_ROW_TILE
    spt = s // tm
    m = km.shape[1]
    ws = [w_q.astype(BF16), w_o.astype(BF16), g.reshape(1, d), b.reshape(1, d)]

    return pl.pallas_call(
        _mem_attn_kernel,
        grid=(bsz, spt),
        in_specs=[
            pl.BlockSpec((tm, d), lambda bb, i: (bb * spt + i, 0)),
            pl.BlockSpec((None, m, d), lambda bb, i: (bb, 0, 0)),
            pl.BlockSpec((None, m, d), lambda bb, i: (bb, 0, 0)),
        ] + [_resident(a) for a in ws],
        out_specs=pl.BlockSpec((tm, d), lambda bb, i: (bb * spt + i, 0)),
        out_shape=jax.ShapeDtypeStruct((n, d), F32),
        compiler_params=_params(("arbitrary", "arbitrary")),
        name="mem_attn",
    )(h, km, vm, *ws)


def kernel(x, mem, rel_bias, ffn1_w1, ffn1_w3, ffn1_w2, ln1_g, ln1_b, w_in, q_norm_g, kv_norm_g, w_q_b, w_kv_b, lam_q1, lam_k1, lam_q2, lam_k2, diff_norm_g, w_gate, b_gate, w_up_mla, w_up_diff, w_o, ln2_g, ln2_b, mem_w_q, mem_w_kv, mem_w_o, ln3_g, ln3_b, ffn2_w1, ffn2_w3, ffn2_w2, ln4_g, ln4_b):
    bsz, s, d = x.shape
    assert ffn1_w1.shape[0] == DEPTH
    l = 0
    h = x.reshape(bsz * s, d)

    h = _ffn_ln(h, ffn1_w1[l], ffn1_w3[l], ffn1_w2[l], ln1_g[l], ln1_b[l])

    qtm, km, vtm, qtd, kd, vtd = _in_proj(h, bsz, s, w_in[l], q_norm_g[l], kv_norm_g[l], w_q_b[l], w_kv_b[l])
    bias = _rel_bias_tiles(rel_bias, ATTN_TQ, ATTN_TK)
    omt = _mla_attn(qtm, km, vtm, ATTN_TQ, ATTN_TK)
    lamv = jnp.stack([lam_q1[l], lam_k1[l], lam_q2[l], lam_k2[l]], axis=0)
    odt = _diff_attn(qtd, kd, vtd, bias, lamv, diff_norm_g[l].reshape(-1, 1), ATTN_TQ, ATTN_TK)
    h = _post_mixer(h, omt, odt, bsz, s, w_up_mla[l], w_up_diff[l], w_gate[l], b_gate[l], w_o[l],
                    ln2_g[l], ln2_b[l])

    kmem, vmem = _mem_kv(mem, mem_w_kv[l])
    h = _mem_attn(h, kmem, vmem, bsz, s, mem_w_q[l], mem_w_o[l], ln3_g[l], ln3_b[l])

    h = _ffn_ln(h, ffn2_w1[l], ffn2_w3[l], ffn2_w2[l], ln4_g[l], ln4_b[l])
    return h.reshape(bsz, s, d)
```

```python
import functools
import math

import jax
import jax.numpy as jnp
from jax import lax
from jax.experimental import pallas as pl
from jax.experimental.pallas import tpu as pltpu

F32 = jnp.float32
BF16 = jnp.bfloat16

DEPTH = 1
MLA_HEADS = 8
MLA_Q_RANK = 384
MLA_KV_RANK = 256
MLA_NOPE = 64
MLA_ROPE = 32
MLA_V = 64
ROPE_BASE = 10000.0
DIFF_HEADS = 4
DIFF_D = 64
REL_BUCKETS = 32
REL_MAX_DIST = 128
MEM_HEADS = 4
ALPHA = (2.0 * DEPTH) ** 0.25
LN_EPS = 1e-5
RMS_EPS = 1e-6
LAM_INIT = 0.8 - 0.6 * math.exp(-0.3 * 0)
LOG2E = math.log2(math.e)

HEAD_PAD = 128
VMEM_LIMIT_BYTES = 56 * 1024 * 1024

ROW_TILE = 1024
WIDE_ROW_TILE = 2048
CHAIN_ROWS = 256
MEM_CHAIN_ROWS = 512
ATTN_TQ = 512
ATTN_TK = 256
MLA_HEADS_PER_STEP = 4
MLA_STAGE_ORDER = "vcs"
DIFF_STAGE_ORDER = "csv"


def _params(semantics):
    return pltpu.CompilerParams(dimension_semantics=semantics, vmem_limit_bytes=VMEM_LIMIT_BYTES)


def _dot(a, b):
    return jnp.dot(a, b, preferred_element_type=F32)


def _dot_nt(a, b):
    return lax.dot_general(a, b, (((1,), (1,)), ((), ())), preferred_element_type=F32)


def _dot_tn(a, b):
    return lax.dot_general(a, b, (((0,), (0,)), ((), ())), preferred_element_type=F32)


def _layer_norm(y, g, b):
    mu = jnp.mean(y, axis=-1, keepdims=True)
    d = y - mu
    var = jnp.mean(d * d, axis=-1, keepdims=True)
    return d * lax.rsqrt(var + LN_EPS) * g + b


def _rms_norm_rows(x, g):
    return x * lax.rsqrt(jnp.mean(x * x, axis=-1, keepdims=True) + RMS_EPS) * g


def _chains(rows, chain=CHAIN_ROWS):
    return [slice(r, r + chain) for r in range(0, rows, chain)]


def _resident(a):
    return pl.BlockSpec(a.shape, lambda *_: (0,) * a.ndim, pipeline_mode=pl.Buffered(1))


def _ffn_ln_kernel(x_ref, w1_ref, w3_ref, w2_ref, g_ref, b_ref, o_ref):
    for rows in _chains(x_ref.shape[0]):
        x = x_ref[rows, :]
        xb = x.astype(BF16)
        v = _dot(xb, w3_ref[...])
        u = _dot(xb, w1_ref[...])
        act = (u / (1.0 + jnp.exp(-u))) * v
        y = ALPHA * x + 0.5 * _dot(act.astype(BF16), w2_ref[...])
        o_ref[rows, :] = _layer_norm(y, g_ref[...], b_ref[...])


def _ffn_ln(x, w1, w3, w2, g, b):
    n, d = x.shape
    tm = ROW_TILE
    assert n % tm == 0
    ws = [w1.astype(BF16), w3.astype(BF16), w2.astype(BF16), g.reshape(1, d), b.reshape(1, d)]
    return pl.pallas_call(
        _ffn_ln_kernel,
        grid=(n // tm,),
        in_specs=[pl.BlockSpec((tm, d), lambda i: (i, 0))] + [_resident(a) for a in ws],
        out_specs=pl.BlockSpec((tm, d), lambda i: (i, 0)),
        out_shape=jax.ShapeDtypeStruct((n, d), F32),
        compiler_params=_params(("arbitrary",)),
        name="ffn_ln",
    )(x, *ws)


def _in_proj_kernel(h_ref, wnat_ref, wt_ref, wq_ref, wkn_ref, wvt_ref, gq_ref, gkv_ref,
                    cosk_ref, sink_ref, cosq_ref, sinq_ref,
                    qtm_ref, km_ref, vtm_ref, qtd_ref, kd_ref, vtd_ref):
    tk = vtm_ref.shape[-1]
    half = MLA_ROPE // 2
    nqd = qtd_ref.shape[0]
    o1 = MLA_Q_RANK
    o2 = o1 + MLA_KV_RANK
    o3 = o2 + HEAD_PAD
    q_scale = (MLA_NOPE + MLA_ROPE) ** -0.5 * LOG2E
    for rows in _chains(h_ref.shape[0]):
        key_tiles = range(rows.start // tk, rows.stop // tk)
        hb = h_ref[rows, :].astype(BF16)

        pt = _dot_nt(wt_ref[...], hb)
        qtd_ref[:, rows] = (pt[:nqd] * (DIFF_D ** -0.5 * LOG2E)).astype(BF16)
        for c, t in enumerate(key_tiles):
            vtd_ref[t] = pt[nqd:, c * tk:(c + 1) * tk].astype(BF16)

        pn = _dot(hb, wnat_ref[...])
        c_q, c_kv, krg, kd = pn[:, :o1], pn[:, o1:o2], pn[:, o2:o3], pn[:, o3:]
        for hh in range(DIFF_HEADS):
            kd_ref[hh, rows, :] = kd[:, hh * 2 * DIFF_D:(hh + 1) * 2 * DIFF_D].astype(BF16)

        cqn = _rms_norm_rows(c_q, gq_ref[...]).astype(BF16)
        qt = _dot_nt(wq_ref[...], cqn)
        cos_q, sin_q = cosq_ref[:, rows], sinq_ref[:, rows]
        for hh in range(MLA_HEADS):
            lo = hh * HEAD_PAD
            r0 = lo + MLA_NOPE
            r1, r2 = r0 + half, r0 + MLA_ROPE
            x1, x2 = qt[r0:r1], qt[r1:r2]
            qtm_ref[lo:r0, rows] = (qt[lo:r0] * q_scale).astype(BF16)
            qtm_ref[r0:r1, rows] = ((x1 * cos_q - x2 * sin_q) * q_scale).astype(BF16)
            qtm_ref[r1:r2, rows] = ((x2 * cos_q + x1 * sin_q) * q_scale).astype(BF16)
            qtm_ref[r2:lo + HEAD_PAD, rows] = jnp.zeros((lo + HEAD_PAD - r2, CHAIN_ROWS), BF16)

        ckvn = _rms_norm_rows(c_kv, gkv_ref[...]).astype(BF16)
        kn = _dot(ckvn, wkn_ref[...])
        krh = jnp.concatenate([krg[:, MLA_ROPE:], krg[:, :MLA_ROPE]], axis=1)
        kr = krg * cosk_ref[rows, :] + krh * sink_ref[rows, :]
        for hh in range(MLA_HEADS):
            lo, hi = hh * HEAD_PAD, (hh + 1) * HEAD_PAD
            km_ref[rows, lo:hi] = (kn[:, lo:hi] + kr).astype(BF16)
        vt = _dot_nt(wvt_ref[...], ckvn)
        for c, t in enumerate(key_tiles):
            vtm_ref[t] = vt[:, c * tk:(c + 1) * tk].astype(BF16)


def _in_proj_weights(w_in, w_q_b, w_kv_b):
    d = w_in.shape[0]
    o_cq = MLA_Q_RANK
    o_ckv = o_cq + MLA_KV_RANK
    o_kr = o_ckv + MLA_ROPE
    n_d = 2 * DIFF_HEADS * DIFF_D
    o_qd = o_kr + n_d
    o_kd = o_qd + n_d
    half = MLA_ROPE // 2
    pad = HEAD_PAD - MLA_NOPE - MLA_ROPE
    assert pad == MLA_ROPE
    kr = w_in[:, o_ckv:o_kr]
    krg = jnp.concatenate([jnp.zeros((d, MLA_NOPE), F32), kr, -kr[:, half:], kr[:, :half]], axis=1)
    w_nat = jnp.concatenate([w_in[:, :o_ckv], krg, w_in[:, o_qd:o_kd]], axis=1).astype(BF16)
    w_t = jnp.concatenate([w_in[:, o_kr:o_qd], w_in[:, o_kd:]], axis=1).T.astype(BF16)

    r = w_q_b.shape[0]
    wq = w_q_b.reshape(r, MLA_HEADS, MLA_NOPE + MLA_ROPE)
    wq = jnp.concatenate([wq, jnp.zeros((r, MLA_HEADS, pad), F32)], axis=-1)
    wq_t = wq.reshape(r, MLA_HEADS * HEAD_PAD).T.astype(BF16)

    rk = w_kv_b.shape[0]
    wkv = w_kv_b.reshape(rk, MLA_HEADS, MLA_NOPE + MLA_V)
    wkn = jnp.concatenate([wkv[..., :MLA_NOPE], jnp.zeros((rk, MLA_HEADS, HEAD_PAD - MLA_NOPE), F32)],
                          axis=-1).reshape(rk, MLA_HEADS * HEAD_PAD).astype(BF16)
    wv_t = wkv[..., MLA_NOPE:].reshape(rk, MLA_HEADS * MLA_V).T.astype(BF16)
    return w_nat, w_t, wq_t, wkn, wv_t


def _rope_tables(s):
    inv_freq = ROPE_BASE ** (-jnp.arange(0, MLA_ROPE, 2, dtype=F32) / MLA_ROPE)
    ang = jnp.arange(s).astype(F32)[:, None] * inv_freq[None, :]
    cos, sin = jnp.cos(ang), jnp.sin(ang)
    z_lo = jnp.zeros((s, MLA_NOPE), F32)
    z_hi = jnp.zeros((s, HEAD_PAD - MLA_NOPE - MLA_ROPE), F32)
    cos_k = jnp.concatenate([z_lo, cos, cos, z_hi], axis=1)
    sin_k = jnp.concatenate([z_lo, sin, sin, z_hi], axis=1)
    return cos_k, sin_k, cos.T, sin.T


def _in_proj(h, bsz, s, w_in, q_norm_g, kv_norm_g, w_q_b, w_kv_b):
    n, d = h.shape
    tm, tk = ROW_TILE, ATTN_TK
    assert s % tm == 0 and CHAIN_ROWS % tk == 0
    w_nat, w_t, wq_t, wkn, wv_t = _in_proj_weights(w_in, w_q_b, w_kv_b)
    cos_k, sin_k, cos_q, sin_q = _rope_tables(s)
    half = MLA_ROPE // 2
    spt = s // tm
    nqm = MLA_HEADS * HEAD_PAD
    nvm = MLA_HEADS * MLA_V
    nd = 2 * DIFF_HEADS * DIFF_D

    gq = q_norm_g.reshape(1, -1)
    gkv = kv_norm_g.reshape(1, -1)
    out_shape = (
        jax.ShapeDtypeStruct((bsz, nqm, s), BF16),
        jax.ShapeDtypeStruct((bsz, s, nqm), BF16),
        jax.ShapeDtypeStruct((bsz, s // tk, nvm, tk), BF16),
        jax.ShapeDtypeStruct((bsz, nd, s), BF16),
        jax.ShapeDtypeStruct((bsz, DIFF_HEADS, s, nd // DIFF_HEADS), BF16),
        jax.ShapeDtypeStruct((bsz, s // tk, nd, tk), BF16),
    )
    out_specs = (
        pl.BlockSpec((None, nqm, tm), lambda b, i: (b, 0, i)),
        pl.BlockSpec((None, tm, nqm), lambda b, i: (b, i, 0)),
        pl.BlockSpec((None, tm // tk, nvm, tk), lambda b, i: (b, i, 0, 0)),
        pl.BlockSpec((None, nd, tm), lambda b, i: (b, 0, i)),
        pl.BlockSpec((None, DIFF_HEADS, tm, nd // DIFF_HEADS), lambda b, i: (b, 0, i, 0)),
        pl.BlockSpec((None, tm // tk, nd, tk), lambda b, i: (b, i, 0, 0)),
    )
    return pl.pallas_call(
        _in_proj_kernel,
        grid=(bsz, spt),
        in_specs=[
            pl.BlockSpec((tm, d), lambda b, i: (b * spt + i, 0)),
            _resident(w_nat), _resident(w_t), _resident(wq_t), _resident(wkn), _resident(wv_t),
            _resident(gq), _resident(gkv),
            pl.BlockSpec((tm, HEAD_PAD), lambda b, i: (i, 0)),
            pl.BlockSpec((tm, HEAD_PAD), lambda b, i: (i, 0)),
            pl.BlockSpec((half, tm), lambda b, i: (0, i)),
            pl.BlockSpec((half, tm), lambda b, i: (0, i)),
        ],
        out_specs=out_specs,
        out_shape=out_shape,
        compiler_params=_params(("arbitrary", "arbitrary")),
        name="in_proj",
    )(h, w_nat, w_t, wq_t, wkn, wv_t, gq, gkv, cos_k, sin_k, cos_q, sin_q)


BIAS_TILES = 3


def _rel_bias_kernel(rb_ref, o_ref, *, tq, tk):
    c = pl.program_id(0) - 1
    kk = lax.broadcasted_iota(jnp.int32, (tk, tq), 0)
    qq = lax.broadcasted_iota(jnp.int32, (tk, tq), 1)
    n = jnp.maximum(qq - kk - c * tk, 0)
    max_exact = REL_BUCKETS // 2
    nf = jnp.maximum(n, 1).astype(F32)
    large = max_exact + (jnp.log(nf / max_exact) / math.log(REL_MAX_DIST / max_exact)
                         * (REL_BUCKETS - max_exact)).astype(jnp.int32)
    large = jnp.minimum(large, REL_BUCKETS - 1)
    bucket = jnp.where(n < max_exact, n, large)
    for m in range(o_ref.shape[0]):
        acc = jnp.full((tk, tq), rb_ref[0, m], F32)
        for bkt in range(1, REL_BUCKETS):
            acc = jnp.where(bucket >= bkt, rb_ref[bkt, m], acc)
        o_ref[m] = (acc - rb_ref[REL_BUCKETS - 1, m]) * LOG2E


def _rel_bias_tiles(rel_bias, tq, tk):
    assert tq == 2 * tk and tk + 1 >= REL_MAX_DIST
    nmaps = rel_bias.shape[1]
    return pl.pallas_call(
        functools.partial(_rel_bias_kernel, tq=tq, tk=tk),
        grid=(BIAS_TILES,),
        in_specs=[pl.BlockSpec(memory_space=pltpu.SMEM)],
        out_specs=pl.BlockSpec((nmaps, None, tk, tq), lambda t: (0, t, 0, 0)),
        out_shape=jax.ShapeDtypeStruct((nmaps, BIAS_TILES, tk, tq), F32),
        compiler_params=_params(("arbitrary",)),
        name="rel_bias",
    )(rel_bias)


SCORES_AHEAD = 3
VALUES_BEHIND = 1
ONES_ROWS = 16


def _flash_pipeline(q_fn, k_fn, vt_fn, logits_fn, finalize_fn, s_ref, mt_ref, p_ref, al_ref, m_ref, l_ref, acc_ref,
                    *, nstreams, nq, tq, tk, order):
    tiles = [(g, i, t, t - 2 * i) for g in range(nstreams) for i in range(nq) for t in range(2 * i + 2)]
    ones = jnp.ones((ONES_ROWS, tk), BF16)
    ahead = s_ref.shape[0] - 1
    behind = p_ref.shape[0] - 1

    def lane0(c):
        return tk if c == 1 else 0

    def scores(n):
        g, i, t, c = tiles[n]
        slot, lo = n % (ahead + 1), lane0(c)
        s = logits_fn(g, _dot(k_fn(g, t), q_fn(g, i, lo)), c, lo)
        s_ref[slot, :, lo:] = s
        mt_ref[slot, :, lo:] = jnp.max(s, axis=0, keepdims=True)

    def softmax(n):
        g, i, t, c = tiles[n]
        sslot, slot, lo = n % (ahead + 1), n % (behind + 1), lane0(c)
        m_new = mt_ref[sslot, :, lo:]
        if t > 0:
            m_prev = m_ref[:, lo:]
            m_new = jnp.maximum(m_prev, m_new)
            al_ref[slot, :, lo:] = jnp.exp2(m_prev - m_new)
        m_ref[:, lo:] = m_new
        p = jnp.exp2(s_ref[sslot, :, lo:] - m_new)
        p_ref[slot, :, lo:] = p.astype(BF16)
        if ONES_ROWS == 0:
            lslot = (g * nq + i) % 2
            l_new = jnp.sum(p, axis=0, keepdims=True)
            if t > 0:
                l_new = al_ref[slot, :, lo:] * l_ref[lslot, :, lo:] + l_new
            l_ref[lslot, :, lo:] = l_new

    def values(n):
        g, i, t, c = tiles[n]
        slot, lo = n % (behind + 1), lane0(c)
        vt = vt_fn(g, t)
        if ONES_ROWS:
            vt = jnp.concatenate([vt, ones], axis=0)
        pv = _dot(vt, p_ref[slot, :, lo:])
        if t > 0:
            pv = al_ref[slot, :, lo:] * acc_ref[:, lo:] + pv
        acc_ref[:, lo:] = pv
        if c == 1:
            acc = acc_ref[...]
            dv = acc.shape[0] - ONES_ROWS
            finalize_fn(g, i, acc[:dv] / (acc[dv:dv + 1] if ONES_ROWS else l_ref[(g * nq + i) % 2]))

    for n in range(ahead):
        scores(n)
    assert sorted(order) == ["c", "s", "v"] and (behind > 0 or order.index("s") < order.index("v"))
    for n in range(len(tiles) + behind):
        for stage in order:
            if stage == "v" and behind <= n:
                values(n - behind)
            if stage == "c" and n + ahead < len(tiles):
                scores(n + ahead)
            if stage == "s" and n < len(tiles):
                softmax(n)


def _tri_mask(s):
    kk = lax.broadcasted_iota(jnp.int32, s.shape, 0)
    qq = lax.broadcasted_iota(jnp.int32, s.shape, 1)
    return jnp.where(kk <= qq, s, -jnp.inf)


def _flash_scratch(dv, tq, tk, ahead=SCORES_AHEAD):
    return [pltpu.VMEM((ahead + 1, tk, tq), F32), pltpu.VMEM((ahead + 1, 1, tq), F32),
            pltpu.VMEM((VALUES_BEHIND + 1, tk, tq), BF16), pltpu.VMEM((VALUES_BEHIND + 1, 1, tq), F32),
            pltpu.VMEM((1, tq), F32), pltpu.VMEM((2, 1, tq), F32), pltpu.VMEM((dv + ONES_ROWS, tq), F32)]


def _mla_attn_kernel(qt_ref, k_ref, vt_ref, o_ref, *scratch, heads, tq, tk):

    def q_fn(g, i, lo):
        return qt_ref[g * HEAD_PAD:(g + 1) * HEAD_PAD, i * tq + lo:(i + 1) * tq]

    def k_fn(g, t):
        return k_ref[t * tk:(t + 1) * tk, g * HEAD_PAD:(g + 1) * HEAD_PAD]

    def vt_fn(g, t):
        return vt_ref[t, g * MLA_V:(g + 1) * MLA_V, :]

    def logits(g, s, c, lo):
        return _tri_mask(s) if c >= 0 else s

    def finalize(g, i, o):
        o_ref[g * MLA_V:(g + 1) * MLA_V, i * tq:(i + 1) * tq] = o.astype(BF16)

    _flash_pipeline(q_fn, k_fn, vt_fn, logits, finalize, *scratch,
                    nstreams=heads, nq=qt_ref.shape[1] // tq, tq=tq, tk=tk, order=MLA_STAGE_ORDER)


def _mla_attn(qt, k, vt, tq, tk):
    bsz, nq, s = qt.shape
    nkt = s // tk
    hps = MLA_HEADS_PER_STEP
    assert tq == 2 * tk and s % tq == 0 and MLA_HEADS % hps == 0
    return pl.pallas_call(
        functools.partial(_mla_attn_kernel, heads=hps, tq=tq, tk=tk),
        grid=(MLA_HEADS // hps, bsz),
        in_specs=[
            pl.BlockSpec((None, hps * HEAD_PAD, s), lambda h, b: (b, h, 0)),
            pl.BlockSpec((None, s, hps * HEAD_PAD), lambda h, b: (b, 0, h)),
            pl.BlockSpec((None, nkt, hps * MLA_V, tk), lambda h, b: (b, 0, h, 0)),
        ],
        out_specs=pl.BlockSpec((None, hps * MLA_V, s), lambda h, b: (b, h, 0)),
        out_shape=jax.ShapeDtypeStruct((bsz, MLA_HEADS * MLA_V, s), BF16),
        scratch_shapes=_flash_scratch(MLA_V, tq, tk),
        compiler_params=_params(("arbitrary", "arbitrary")),
        name="mla_attn",
    )(qt, k, vt)


def _diff_attn_kernel(qt_ref, k_ref, vt_ref, bias_ref, lamv_ref, gn_ref, o_ref, o0_ref, qm_ref, *scratch, tq, tk):
    qt = qt_ref[...]
    row = lax.broadcasted_iota(jnp.int32, qt.shape, 0)
    for g in range(2):
        keep = jnp.logical_and(row >= g * DIFF_D, row < (g + 1) * DIFF_D)
        qm_ref[g] = jnp.where(keep, qt, jnp.zeros_like(qt))

    lv = lamv_ref[...]
    lam = (jnp.exp(jnp.sum(lv[0:1] * lv[1:2], axis=1, keepdims=True))
           - jnp.exp(jnp.sum(lv[2:3] * lv[3:4], axis=1, keepdims=True)) + LAM_INIT)

    def q_fn(g, i, lo):
        return qm_ref[g, :, i * tq + lo:(i + 1) * tq]

    def k_fn(g, t):
        return k_ref[t * tk:(t + 1) * tk, :]

    def vt_fn(g, t):
        return vt_ref[t]

    def logits(g, s, c, lo):
        if c >= -1:
            s = s + bias_ref[g, c + 1, :, lo:]
        return _tri_mask(s) if c >= 0 else s

    def finalize(g, i, o):
        cols = slice(i * tq, (i + 1) * tq)
        if g == 0:
            o0_ref[:, cols] = o
        else:
            d = o0_ref[:, cols] - lam * o
            d = d * lax.rsqrt(jnp.mean(d * d, axis=0, keepdims=True) + RMS_EPS) * gn_ref[...] * (1.0 - LAM_INIT)
            o_ref[:, cols] = d.astype(BF16)

    _flash_pipeline(q_fn, k_fn, vt_fn, logits, finalize, *scratch,
                    nstreams=2, nq=qt_ref.shape[1] // tq, tq=tq, tk=tk, order=DIFF_STAGE_ORDER)


def _diff_attn(qt, k, vt, bias, lamv, gn, tq, tk):
    bsz, nd, s = qt.shape
    nkt = s // tk
    dv = 2 * DIFF_D
    assert tq == 2 * tk and s % tq == 0
    return pl.pallas_call(
        functools.partial(_diff_attn_kernel, tq=tq, tk=tk),
        grid=(DIFF_HEADS, bsz),
        in_specs=[
            pl.BlockSpec((None, dv, s), lambda h, b: (b, h, 0)),
            pl.BlockSpec((None, None, s, dv), lambda h, b: (b, h, 0, 0)),
            pl.BlockSpec((None, nkt, dv, tk), lambda h, b: (b, 0, h, 0)),
            pl.BlockSpec((2, BIAS_TILES, tk, tq), lambda h, b: (h, 0, 0, 0)),
            pl.BlockSpec(lamv.shape, lambda h, b: (0, 0)),
            pl.BlockSpec(gn.shape, lambda h, b: (0, 0)),
        ],
        out_specs=pl.BlockSpec((None, dv, s), lambda h, b: (b, h, 0)),
        out_shape=jax.ShapeDtypeStruct((bsz, nd, s), BF16),
        scratch_shapes=[pltpu.VMEM((dv, s), F32), pltpu.VMEM((2, dv, s), BF16)] + _flash_scratch(dv, tq, tk),
        compiler_params=_params(("arbitrary", "arbitrary")),
        name="diff_attn",
    )(qt, k, vt, bias, lamv, gn)


def _post_mixer_kernel(h_ref, omt_ref, odt_ref, wum_ref, wud_ref, wg_ref, bg_ref, wo_ref, g_ref, b_ref,
                       o_ref):
    d = h_ref.shape[1]
    for rows in _chains(h_ref.shape[0]):
        h = h_ref[rows, :]
        z = _dot(h.astype(BF16), wg_ref[...]) + bg_ref[...]
        y_m = _dot_tn(omt_ref[:, rows], wum_ref[...])
        y_d = _dot_tn(odt_ref[:, rows], wud_ref[...])
        gate = 1.0 / (1.0 + jnp.exp(-z))
        mixed = gate[:, :d] * y_m + gate[:, d:] * y_d
        mix = _dot(mixed.astype(BF16), wo_ref[...])
        o_ref[rows, :] = _layer_norm(ALPHA * h + mix, g_ref[...], b_ref[...])


def _post_mixer(h, omt, odt, bsz, s, w_up_mla, w_up_diff, w_gate, b_gate, w_o, g, b):
    n, d = h.shape
    tm = ROW_TILE
    spt = s // tm
    ws = [w_up_mla.astype(BF16), w_up_diff.astype(BF16), w_gate.astype(BF16), b_gate.reshape(1, -1),
          w_o.astype(BF16), g.reshape(1, d), b.reshape(1, d)]

    return pl.pallas_call(
        _post_mixer_kernel,
        grid=(bsz, spt),
        in_specs=[
            pl.BlockSpec((tm, d), lambda bb, i: (bb * spt + i, 0)),
            pl.BlockSpec((None, omt.shape[1], tm), lambda bb, i: (bb, 0, i)),
            pl.BlockSpec((None, odt.shape[1], tm), lambda bb, i: (bb, 0, i)),
        ] + [_resident(a) for a in ws],
        out_specs=pl.BlockSpec((tm, d), lambda bb, i: (bb * spt + i, 0)),
        out_shape=jax.ShapeDtypeStruct((n, d), F32),
        compiler_params=_params(("arbitrary", "arbitrary")),
        name="post_mixer",
    )(h, omt, odt, *ws)


def _mem_kv_kernel(mem_ref, w_ref, k_ref, v_ref):
    d = k_ref.shape[1]
    kv = _dot(mem_ref[...].astype(BF16), w_ref[...])
    k_ref[...] = kv[:, :d].astype(BF16)
    v_ref[...] = kv[:, d:].astype(BF16)


def _mem_kv(mem, w_kv):
    bsz, m, d = mem.shape
    w = w_kv.astype(BF16)
    spec = pl.BlockSpec((None, m, d), lambda b: (b, 0, 0))
    return pl.pallas_call(
        _mem_kv_kernel,
        grid=(bsz,),
        in_specs=[spec, pl.BlockSpec(w.shape, lambda b: (0, 0))],
        out_specs=(spec, spec),
        out_shape=(jax.ShapeDtypeStruct((bsz, m, d), BF16), jax.ShapeDtypeStruct((bsz, m, d), BF16)),
        compiler_params=_params(("arbitrary",)),
        name="mem_kv",
    )(mem, w)


def _mem_attn_kernel(h_ref, k_ref, v_ref, wq_ref, wo_ref, g_ref, b_ref, o_ref):
    d = h_ref.shape[1]
    hd = d // MEM_HEADS
    for rows in _chains(h_ref.shape[0], MEM_CHAIN_ROWS):
        h = h_ref[rows, :]
        q = _dot(h.astype(BF16), wq_ref[...]).astype(BF16)
        heads = [slice(hh * hd, (hh + 1) * hd) for hh in range(MEM_HEADS)]
        logits = [_dot_nt(q[:, cols], k_ref[:, cols]) * (hd ** -0.5) for cols in heads]
        outs = []
        for cols, s in zip(heads, logits):
            p = jnp.exp(s - jnp.max(s, axis=-1, keepdims=True))
            p = p / jnp.sum(p, axis=-1, keepdims=True)
            outs.append(_dot(p.astype(BF16), v_ref[:, cols]).astype(BF16))
        att = _dot(jnp.concatenate(outs, axis=1), wo_ref[...])
        o_ref[rows, :] = _layer_norm(ALPHA * h + att, g_ref[...], b_ref[...])


def _mem_attn(h, km, vm, bsz, s, w_q, w_o, g, b):
    n, d = h.shape
    tm = WIDE_ROW_TILE
    spt = s // tm
    m = km.shape[1]
    ws = [w_q.astype(BF16), w_o.astype(BF16), g.reshape(1, d), b.reshape(1, d)]

    return pl.pallas_call(
        _mem_attn_kernel,
        grid=(bsz, spt),
        in_specs=[
            pl.BlockSpec((tm, d), lambda bb, i: (bb * spt + i, 0)),
            pl.BlockSpec((None, m, d), lambda bb, i: (bb, 0, 0)),
            pl.BlockSpec((None, m, d), lambda bb, i: (bb, 0, 0)),
        ] + [_resident(a) for a in ws],
        out_specs=pl.BlockSpec((tm, d), lambda bb, i: (bb * spt + i, 0)),
        out_shape=jax.ShapeDtypeStruct((n, d), F32),
        compiler_params=_params(("arbitrary", "arbitrary")),
        name="mem_attn",
    )(h, km, vm, *ws)


def kernel(x, mem, rel_bias, ffn1_w1, ffn1_w3, ffn1_w2, ln1_g, ln1_b, w_in, q_norm_g, kv_norm_g, w_q_b, w_kv_b, lam_q1, lam_k1, lam_q2, lam_k2, diff_norm_g, w_gate, b_gate, w_up_mla, w_up_diff, w_o, ln2_g, ln2_b, mem_w_q, mem_w_kv, mem_w_o, ln3_g, ln3_b, ffn2_w1, ffn2_w3, ffn2_w2, ln4_g, ln4_b):
    bsz, s, d = x.shape
    assert ffn1_w1.shape[0] == DEPTH
    l = 0
    h = x.reshape(bsz * s, d)

    h = _ffn_ln(h, ffn1_w1[l], ffn1_w3[l], ffn1_w2[l], ln1_g[l], ln1_b[l])

    qtm, km, vtm, qtd, kd, vtd = _in_proj(h, bsz, s, w_in[l], q_norm_g[l], kv_norm_g[l], w_q_b[l], w_kv_b[l])
    bias = _rel_bias_tiles(rel_bias, ATTN_TQ, ATTN_TK)
    omt = _mla_attn(qtm, km, vtm, ATTN_TQ, ATTN_TK)
    lamv = jnp.stack([lam_q1[l], lam_k1[l], lam_q2[l], lam_k2[l]], axis=0)
    odt = _diff_attn(qtd, kd, vtd, bias, lamv, diff_norm_g[l].reshape(-1, 1), ATTN_TQ, ATTN_TK)
    h = _post_mixer(h, omt, odt, bsz, s, w_up_mla[l], w_up_diff[l], w_gate[l], b_gate[l], w_o[l],
                    ln2_g[l], ln2_b[l])

    kmem, vmem = _mem_kv(mem, mem_w_kv[l])
    h = _mem_attn(h, kmem, vmem, bsz, s, mem_w_q[l], mem_w_o[l], ln3_g[l], ln3_b[l])

    h = _ffn_ln(h, ffn2_w1[l], ffn2_w3[l], ffn2_w2[l], ln4_g[l], ln4_b[l])
    return h.reshape(bsz, s, d)
```

```python
import functools
import math

import jax
import jax.numpy as jnp
from jax import lax
from jax.experimental import pallas as pl
from jax.experimental.pallas import tpu as pltpu

F32 = jnp.float32
BF16 = jnp.bfloat16

DEPTH = 1
MLA_HEADS = 8
MLA_Q_RANK = 384
MLA_KV_RANK = 256
MLA_NOPE = 64
MLA_ROPE = 32
MLA_V = 64
ROPE_BASE = 10000.0
DIFF_HEADS = 4
DIFF_D = 64
REL_BUCKETS = 32
REL_MAX_DIST = 128
MEM_HEADS = 4
ALPHA = (2.0 * DEPTH) ** 0.25
LN_EPS = 1e-5
RMS_EPS = 1e-6
LAM_INIT = 0.8 - 0.6 * math.exp(-0.3 * 0)
LOG2E = math.log2(math.e)

HEAD_PAD = 128
VMEM_LIMIT_BYTES = 56 * 1024 * 1024

ROW_TILE = 1024
WIDE_ROW_TILE = 2048
CHAIN_ROWS = 256
MEM_CHAIN_ROWS = 512
ATTN_TQ = 512
ATTN_TK = 256
MLA_HEADS_PER_STEP = 4
MLA_STAGE_ORDER = "vcs"
DIFF_STAGE_ORDER = "csv"


def _params(semantics):
    return pltpu.CompilerParams(dimension_semantics=semantics, vmem_limit_bytes=VMEM_LIMIT_BYTES)


def _dot(a, b):
    return jnp.dot(a, b, preferred_element_type=F32)


def _dot_nt(a, b):
    return lax.dot_general(a, b, (((1,), (1,)), ((), ())), preferred_element_type=F32)


def _dot_tn(a, b):
    return lax.dot_general(a, b, (((0,), (0,)), ((), ())), preferred_element_type=F32)


def _layer_norm(y, g, b):
    mu = jnp.mean(y, axis=-1, keepdims=True)
    d = y - mu
    var = jnp.mean(d * d, axis=-1, keepdims=True)
    return d * lax.rsqrt(var + LN_EPS) * g + b


def _rms_norm_rows(x, g):
    return x * lax.rsqrt(jnp.mean(x * x, axis=-1, keepdims=True) + RMS_EPS) * g


def _chains(rows, chain=CHAIN_ROWS):
    return [slice(r, r + chain) for r in range(0, rows, chain)]


def _resident(a):
    return pl.BlockSpec(a.shape, lambda *_: (0,) * a.ndim, pipeline_mode=pl.Buffered(1))


def _ffn_ln_kernel(x_ref, w1_ref, w3_ref, w2_ref, g_ref, b_ref, o_ref):
    for rows in _chains(x_ref.shape[0]):
        x = x_ref[rows, :]
        xb = x.astype(BF16)
        u = _dot(xb, w1_ref[...])
        silu = u / (1.0 + jnp.exp(-u))
        act = silu * _dot(xb, w3_ref[...])
        y = ALPHA * x + 0.5 * _dot(act.astype(BF16), w2_ref[...])
        o_ref[rows, :] = _layer_norm(y, g_ref[...], b_ref[...])


def _ffn_ln(x, w1, w3, w2, g, b):
    n, d = x.shape
    tm = ROW_TILE
    assert n % tm == 0
    ws = [w1.astype(BF16), w3.astype(BF16), w2.astype(BF16), g.reshape(1, d), b.reshape(1, d)]
    return pl.pallas_call(
        _ffn_ln_kernel,
        grid=(n // tm,),
        in_specs=[pl.BlockSpec((tm, d), lambda i: (i, 0))] + [_resident(a) for a in ws],
        out_specs=pl.BlockSpec((tm, d), lambda i: (i, 0)),
        out_shape=jax.ShapeDtypeStruct((n, d), F32),
        compiler_params=_params(("arbitrary",)),
        name="ffn_ln",
    )(x, *ws)


def _in_proj_kernel(h_ref, wnat_ref, wt_ref, wq_ref, wkn_ref, wvt_ref, gq_ref, gkv_ref,
                    cosk_ref, sink_ref, cosq_ref, sinq_ref,
                    qtm_ref, km_ref, vtm_ref, qtd_ref, kd_ref, vtd_ref):
    tk = vtm_ref.shape[-1]
    half = MLA_ROPE // 2
    nqd = qtd_ref.shape[0]
    o1 = MLA_Q_RANK
    o2 = o1 + MLA_KV_RANK
    o3 = o2 + HEAD_PAD
    q_scale = (MLA_NOPE + MLA_ROPE) ** -0.5 * LOG2E
    for rows in _chains(h_ref.shape[0]):
        key_tiles = range(rows.start // tk, rows.stop // tk)
        hb = h_ref[rows, :].astype(BF16)

        pt = _dot_nt(wt_ref[...], hb)
        qtd_ref[:, rows] = (pt[:nqd] * (DIFF_D ** -0.5 * LOG2E)).astype(BF16)
        for c, t in enumerate(key_tiles):
            vtd_ref[t] = pt[nqd:, c * tk:(c + 1) * tk].astype(BF16)

        pn = _dot(hb, wnat_ref[...])
        c_q, c_kv, krg, kd = pn[:, :o1], pn[:, o1:o2], pn[:, o2:o3], pn[:, o3:]
        for hh in range(DIFF_HEADS):
            kd_ref[hh, rows, :] = kd[:, hh * 2 * DIFF_D:(hh + 1) * 2 * DIFF_D].astype(BF16)

        cqn = _rms_norm_rows(c_q, gq_ref[...]).astype(BF16)
        qt = _dot_nt(wq_ref[...], cqn)
        cos_q, sin_q = cosq_ref[:, rows], sinq_ref[:, rows]
        for hh in range(MLA_HEADS):
            lo = hh * HEAD_PAD
            r0 = lo + MLA_NOPE
            r1, r2 = r0 + half, r0 + MLA_ROPE
            x1, x2 = qt[r0:r1], qt[r1:r2]
            qtm_ref[lo:r0, rows] = (qt[lo:r0] * q_scale).astype(BF16)
            qtm_ref[r0:r1, rows] = ((x1 * cos_q - x2 * sin_q) * q_scale).astype(BF16)
            qtm_ref[r1:r2, rows] = ((x2 * cos_q + x1 * sin_q) * q_scale).astype(BF16)
            qtm_ref[r2:lo + HEAD_PAD, rows] = jnp.zeros((lo + HEAD_PAD - r2, CHAIN_ROWS), BF16)

        ckvn = _rms_norm_rows(c_kv, gkv_ref[...]).astype(BF16)
        kn = _dot(ckvn, wkn_ref[...])
        krh = jnp.concatenate([krg[:, MLA_ROPE:], krg[:, :MLA_ROPE]], axis=1)
        kr = krg * cosk_ref[rows, :] + krh * sink_ref[rows, :]
        for hh in range(MLA_HEADS):
            lo, hi = hh * HEAD_PAD, (hh + 1) * HEAD_PAD
            km_ref[rows, lo:hi] = (kn[:, lo:hi] + kr).astype(BF16)
        vt = _dot_nt(wvt_ref[...], ckvn)
        for c, t in enumerate(key_tiles):
            vtm_ref[t] = vt[:, c * tk:(c + 1) * tk].astype(BF16)


def _in_proj_weights(w_in, w_q_b, w_kv_b):
    d = w_in.shape[0]
    o_cq = MLA_Q_RANK
    o_ckv = o_cq + MLA_KV_RANK
    o_kr = o_ckv + MLA_ROPE
    n_d = 2 * DIFF_HEADS * DIFF_D
    o_qd = o_kr + n_d
    o_kd = o_qd + n_d
    half = MLA_ROPE // 2
    pad = HEAD_PAD - MLA_NOPE - MLA_ROPE
    assert pad == MLA_ROPE
    kr = w_in[:, o_ckv:o_kr]
    krg = jnp.concatenate([jnp.zeros((d, MLA_NOPE), F32), kr, -kr[:, half:], kr[:, :half]], axis=1)
    w_nat = jnp.concatenate([w_in[:, :o_ckv], krg, w_in[:, o_qd:o_kd]], axis=1).astype(BF16)
    w_t = jnp.concatenate([w_in[:, o_kr:o_qd], w_in[:, o_kd:]], axis=1).T.astype(BF16)

    r = w_q_b.shape[0]
    wq = w_q_b.reshape(r, MLA_HEADS, MLA_NOPE + MLA_ROPE)
    wq = jnp.concatenate([wq, jnp.zeros((r, MLA_HEADS, pad), F32)], axis=-1)
    wq_t = wq.reshape(r, MLA_HEADS * HEAD_PAD).T.astype(BF16)

    rk = w_kv_b.shape[0]
    wkv = w_kv_b.reshape(rk, MLA_HEADS, MLA_NOPE + MLA_V)
    wkn = jnp.concatenate([wkv[..., :MLA_NOPE], jnp.zeros((rk, MLA_HEADS, HEAD_PAD - MLA_NOPE), F32)],
                          axis=-1).reshape(rk, MLA_HEADS * HEAD_PAD).astype(BF16)
    wv_t = wkv[..., MLA_NOPE:].reshape(rk, MLA_HEADS * MLA_V).T.astype(BF16)
    return w_nat, w_t, wq_t, wkn, wv_t


def _rope_tables(s):
    inv_freq = ROPE_BASE ** (-jnp.arange(0, MLA_ROPE, 2, dtype=F32) / MLA_ROPE)
    ang = jnp.arange(s).astype(F32)[:, None] * inv_freq[None, :]
    cos, sin = jnp.cos(ang), jnp.sin(ang)
    z_lo = jnp.zeros((s, MLA_NOPE), F32)
    z_hi = jnp.zeros((s, HEAD_PAD - MLA_NOPE - MLA_ROPE), F32)
    cos_k = jnp.concatenate([z_lo, cos, cos, z_hi], axis=1)
    sin_k = jnp.concatenate([z_lo, sin, sin, z_hi], axis=1)
    return cos_k, sin_k, cos.T, sin.T


def _in_proj(h, bsz, s, w_in, q_norm_g, kv_norm_g, w_q_b, w_kv_b):
    n, d = h.shape
    tm, tk = ROW_TILE, ATTN_TK
    assert s % tm == 0 and CHAIN_ROWS % tk == 0
    w_nat, w_t, wq_t, wkn, wv_t = _in_proj_weights(w_in, w_q_b, w_kv_b)
    cos_k, sin_k, cos_q, sin_q = _rope_tables(s)
    half = MLA_ROPE // 2
    spt = s // tm
    nqm = MLA_HEADS * HEAD_PAD
    nvm = MLA_HEADS * MLA_V
    nd = 2 * DIFF_HEADS * DIFF_D

    gq = q_norm_g.reshape(1, -1)
    gkv = kv_norm_g.reshape(1, -1)
    out_shape = (
        jax.ShapeDtypeStruct((bsz, nqm, s), BF16),
        jax.ShapeDtypeStruct((bsz, s, nqm), BF16),
        jax.ShapeDtypeStruct((bsz, s // tk, nvm, tk), BF16),
        jax.ShapeDtypeStruct((bsz, nd, s), BF16),
        jax.ShapeDtypeStruct((bsz, DIFF_HEADS, s, nd // DIFF_HEADS), BF16),
        jax.ShapeDtypeStruct((bsz, s // tk, nd, tk), BF16),
    )
    out_specs = (
        pl.BlockSpec((None, nqm, tm), lambda b, i: (b, 0, i)),
        pl.BlockSpec((None, tm, nqm), lambda b, i: (b, i, 0)),
        pl.BlockSpec((None, tm // tk, nvm, tk), lambda b, i: (b, i, 0, 0)),
        pl.BlockSpec((None, nd, tm), lambda b, i: (b, 0, i)),
        pl.BlockSpec((None, DIFF_HEADS, tm, nd // DIFF_HEADS), lambda b, i: (b, 0, i, 0)),
        pl.BlockSpec((None, tm // tk, nd, tk), lambda b, i: (b, i, 0, 0)),
    )
    return pl.pallas_call(
        _in_proj_kernel,
        grid=(bsz, spt),
        in_specs=[
            pl.BlockSpec((tm, d), lambda b, i: (b * spt + i, 0)),
            _resident(w_nat), _resident(w_t), _resident(wq_t), _resident(wkn), _resident(wv_t),
            _resident(gq), _resident(gkv),
            pl.BlockSpec((tm, HEAD_PAD), lambda b, i: (i, 0)),
            pl.BlockSpec((tm, HEAD_PAD), lambda b, i: (i, 0)),
            pl.BlockSpec((half, tm), lambda b, i: (0, i)),
            pl.BlockSpec((half, tm), lambda b, i: (0, i)),
        ],
        out_specs=out_specs,
        out_shape=out_shape,
        compiler_params=_params(("arbitrary", "arbitrary")),
        name="in_proj",
    )(h, w_nat, w_t, wq_t, wkn, wv_t, gq, gkv, cos_k, sin_k, cos_q, sin_q)


BIAS_TILES = 3


def _rel_bias_kernel(rb_ref, o_ref, *, tq, tk):
    c = pl.program_id(0) - 1
    kk = lax.broadcasted_iota(jnp.int32, (tk, tq), 0)
    qq = lax.broadcasted_iota(jnp.int32, (tk, tq), 1)
    n = jnp.maximum(qq - kk - c * tk, 0)
    max_exact = REL_BUCKETS // 2
    nf = jnp.maximum(n, 1).astype(F32)
    large = max_exact + (jnp.log(nf / max_exact) / math.log(REL_MAX_DIST / max_exact)
                         * (REL_BUCKETS - max_exact)).astype(jnp.int32)
    large = jnp.minimum(large, REL_BUCKETS - 1)
    bucket = jnp.where(n < max_exact, n, large)
    for m in range(o_ref.shape[0]):
        acc = jnp.full((tk, tq), rb_ref[0, m], F32)
        for bkt in range(1, REL_BUCKETS):
            acc = jnp.where(bucket >= bkt, rb_ref[bkt, m], acc)
        o_ref[m] = (acc - rb_ref[REL_BUCKETS - 1, m]) * LOG2E


def _rel_bias_tiles(rel_bias, tq, tk):
    assert tq == 2 * tk and tk + 1 >= REL_MAX_DIST
    nmaps = rel_bias.shape[1]
    return pl.pallas_call(
        functools.partial(_rel_bias_kernel, tq=tq, tk=tk),
        grid=(BIAS_TILES,),
        in_specs=[pl.BlockSpec(memory_space=pltpu.SMEM)],
        out_specs=pl.BlockSpec((nmaps, None, tk, tq), lambda t: (0, t, 0, 0)),
        out_shape=jax.ShapeDtypeStruct((nmaps, BIAS_TILES, tk, tq), F32),
        compiler_params=_params(("arbitrary",)),
        name="rel_bias",
    )(rel_bias)


SCORES_AHEAD = 3
VALUES_BEHIND = 1
ONES_ROWS = 16


def _flash_pipeline(q_fn, k_fn, vt_fn, logits_fn, finalize_fn, s_ref, mt_ref, p_ref, al_ref, m_ref, l_ref, acc_ref,
                    *, nstreams, nq, tq, tk, order):
    tiles = [(g, i, t, t - 2 * i) for g in range(nstreams) for i in range(nq) for t in range(2 * i + 2)]
    ones = jnp.ones((ONES_ROWS, tk), BF16)
    ahead = s_ref.shape[0] - 1
    behind = p_ref.shape[0] - 1

    def lane0(c):
        return tk if c == 1 else 0

    def scores(n):
        g, i, t, c = tiles[n]
        slot, lo = n % (ahead + 1), lane0(c)
        s = logits_fn(g, _dot(k_fn(g, t), q_fn(g, i, lo)), c, lo)
        s_ref[slot, :, lo:] = s
        mt_ref[slot, :, lo:] = jnp.max(s, axis=0, keepdims=True)

    def softmax(n):
        g, i, t, c = tiles[n]
        sslot, slot, lo = n % (ahead + 1), n % (behind + 1), lane0(c)
        m_new = mt_ref[sslot, :, lo:]
        if t > 0:
            m_prev = m_ref[:, lo:]
            m_new = jnp.maximum(m_prev, m_new)
            al_ref[slot, :, lo:] = jnp.exp2(m_prev - m_new)
        m_ref[:, lo:] = m_new
        p = jnp.exp2(s_ref[sslot, :, lo:] - m_new)
        p_ref[slot, :, lo:] = p.astype(BF16)
        if ONES_ROWS == 0:
            lslot = (g * nq + i) % 2
            l_new = jnp.sum(p, axis=0, keepdims=True)
            if t > 0:
                l_new = al_ref[slot, :, lo:] * l_ref[lslot, :, lo:] + l_new
            l_ref[lslot, :, lo:] = l_new

    def values(n):
        g, i, t, c = tiles[n]
        slot, lo = n % (behind + 1), lane0(c)
        vt = vt_fn(g, t)
        if ONES_ROWS:
            vt = jnp.concatenate([vt, ones], axis=0)
        pv = _dot(vt, p_ref[slot, :, lo:])
        if t > 0:
            pv = al_ref[slot, :, lo:] * acc_ref[:, lo:] + pv
        acc_ref[:, lo:] = pv
        if c == 1:
            acc = acc_ref[...]
            dv = acc.shape[0] - ONES_ROWS
            finalize_fn(g, i, acc[:dv] / (acc[dv:dv + 1] if ONES_ROWS else l_ref[(g * nq + i) % 2]))

    for n in range(ahead):
        scores(n)
    assert sorted(order) == ["c", "s", "v"] and (behind > 0 or order.index("s") < order.index("v"))
    for n in range(len(tiles) + behind):
        for stage in order:
            if stage == "v" and behind <= n:
                values(n - behind)
            if stage == "c" and n + ahead < len(tiles):
                scores(n + ahead)
            if stage == "s" and n < len(tiles):
                softmax(n)


def _tri_mask(s):
    kk = lax.broadcasted_iota(jnp.int32, s.shape, 0)
    qq = lax.broadcasted_iota(jnp.int32, s.shape, 1)
    return jnp.where(kk <= qq, s, -jnp.inf)


def _flash_scratch(dv, tq, tk, ahead=SCORES_AHEAD):
    return [pltpu.VMEM((ahead + 1, tk, tq), F32), pltpu.VMEM((ahead + 1, 1, tq), F32),
            pltpu.VMEM((VALUES_BEHIND + 1, tk, tq), BF16), pltpu.VMEM((VALUES_BEHIND + 1, 1, tq), F32),
            pltpu.VMEM((1, tq), F32), pltpu.VMEM((2, 1, tq), F32), pltpu.VMEM((dv + ONES_ROWS, tq), F32)]


def _mla_attn_kernel(qt_ref, k_ref, vt_ref, o_ref, *scratch, heads, tq, tk):

    def q_fn(g, i, lo):
        return qt_ref[g * HEAD_PAD:(g + 1) * HEAD_PAD, i * tq + lo:(i + 1) * tq]

    def k_fn(g, t):
        return k_ref[t * tk:(t + 1) * tk, g * HEAD_PAD:(g + 1) * HEAD_PAD]

    def vt_fn(g, t):
        return vt_ref[t, g * MLA_V:(g + 1) * MLA_V, :]

    def logits(g, s, c, lo):
        return _tri_mask(s) if c >= 0 else s

    def finalize(g, i, o):
        o_ref[g * MLA_V:(g + 1) * MLA_V, i * tq:(i + 1) * tq] = o.astype(BF16)

    _flash_pipeline(q_fn, k_fn, vt_fn, logits, finalize, *scratch,
                    nstreams=heads, nq=qt_ref.shape[1] // tq, tq=tq, tk=tk, order=MLA_STAGE_ORDER)


def _mla_attn(qt, k, vt, tq, tk):
    bsz, nq, s = qt.shape
    nkt = s // tk
    hps = MLA_HEADS_PER_STEP
    assert tq == 2 * tk and s % tq == 0 and MLA_HEADS % hps == 0
    return pl.pallas_call(
        functools.partial(_mla_attn_kernel, heads=hps, tq=tq, tk=tk),
        grid=(MLA_HEADS // hps, bsz),
        in_specs=[
            pl.BlockSpec((None, hps * HEAD_PAD, s), lambda h, b: (b, h, 0)),
            pl.BlockSpec((None, s, hps * HEAD_PAD), lambda h, b: (b, 0, h)),
            pl.BlockSpec((None, nkt, hps * MLA_V, tk), lambda h, b: (b, 0, h, 0)),
        ],
        out_specs=pl.BlockSpec((None, hps * MLA_V, s), lambda h, b: (b, h, 0)),
        out_shape=jax.ShapeDtypeStruct((bsz, MLA_HEADS * MLA_V, s), BF16),
        scratch_shapes=_flash_scratch(MLA_V, tq, tk),
        compiler_params=_params(("arbitrary", "arbitrary")),
        name="mla_attn",
    )(qt, k, vt)


def _diff_attn_kernel(qt_ref, k_ref, vt_ref, bias_ref, lamv_ref, gn_ref, o_ref, o0_ref, qm_ref, *scratch, tq, tk):
    qt = qt_ref[...]
    row = lax.broadcasted_iota(jnp.int32, qt.shape, 0)
    for g in range(2):
        keep = jnp.logical_and(row >= g * DIFF_D, row < (g + 1) * DIFF_D)
        qm_ref[g] = jnp.where(keep, qt, jnp.zeros_like(qt))

    lv = lamv_ref[...]
    lam = (jnp.exp(jnp.sum(lv[0:1] * lv[1:2], axis=1, keepdims=True))
           - jnp.exp(jnp.sum(lv[2:3] * lv[3:4], axis=1, keepdims=True)) + LAM_INIT)

    def q_fn(g, i, lo):
        return qm_ref[g, :, i * tq + lo:(i + 1) * tq]

    def k_fn(g, t):
        return k_ref[t * tk:(t + 1) * tk, :]

    def vt_fn(g, t):
        return vt_ref[t]

    def logits(g, s, c, lo):
        if c >= -1:
            s = s + bias_ref[g, c + 1, :, lo:]
        return _tri_mask(s) if c >= 0 else s

    def finalize(g, i, o):
        cols = slice(i * tq, (i + 1) * tq)
        if g == 0:
            o0_ref[:, cols] = o
        else:
            d = o0_ref[:, cols] - lam * o
            d = d * lax.rsqrt(jnp.mean(d * d, axis=0, keepdims=True) + RMS_EPS) * gn_ref[...] * (1.0 - LAM_INIT)
            o_ref[:, cols] = d.astype(BF16)

    _flash_pipeline(q_fn, k_fn, vt_fn, logits, finalize, *scratch,
                    nstreams=2, nq=qt_ref.shape[1] // tq, tq=tq, tk=tk, order=DIFF_STAGE_ORDER)


def _diff_attn(qt, k, vt, bias, lamv, gn, tq, tk):
    bsz, nd, s = qt.shape
    nkt = s // tk
    dv = 2 * DIFF_D
    assert tq == 2 * tk and s % tq == 0
    return pl.pallas_call(
        functools.partial(_diff_attn_kernel, tq=tq, tk=tk),
        grid=(DIFF_HEADS, bsz),
        in_specs=[
            pl.BlockSpec((None, dv, s), lambda h, b: (b, h, 0)),
            pl.BlockSpec((None, None, s, dv), lambda h, b: (b, h, 0, 0)),
            pl.BlockSpec((None, nkt, dv, tk), lambda h, b: (b, 0, h, 0)),
            pl.BlockSpec((2, BIAS_TILES, tk, tq), lambda h, b: (h, 0, 0, 0)),
            pl.BlockSpec(lamv.shape, lambda h, b: (0, 0)),
            pl.BlockSpec(gn.shape, lambda h, b: (0, 0)),
        ],
        out_specs=pl.BlockSpec((None, dv, s), lambda h, b: (b, h, 0)),
        out_shape=jax.ShapeDtypeStruct((bsz, nd, s), BF16),
        scratch_shapes=[pltpu.VMEM((dv, s), F32), pltpu.VMEM((2, dv, s), BF16)] + _flash_scratch(dv, tq, tk),
        compiler_params=_params(("arbitrary", "arbitrary")),
        name="diff_attn",
    )(qt, k, vt, bias, lamv, gn)


def _post_mixer_kernel(h_ref, omt_ref, odt_ref, wum_ref, wud_ref, wg_ref, bg_ref, wo_ref, g_ref, b_ref,
                       o_ref):
    d = h_ref.shape[1]
    for rows in _chains(h_ref.shape[0]):
        h = h_ref[rows, :]
        z = _dot(h.astype(BF16), wg_ref[...]) + bg_ref[...]
        gate = 1.0 / (1.0 + jnp.exp(-z))
        y_m = _dot_tn(omt_ref[:, rows], wum_ref[...])
        y_d = _dot_tn(odt_ref[:, rows], wud_ref[...])
        mixed = gate[:, :d] * y_m + gate[:, d:] * y_d
        mix = _dot(mixed.astype(BF16), wo_ref[...])
        o_ref[rows, :] = _layer_norm(ALPHA * h + mix, g_ref[...], b_ref[...])


def _post_mixer(h, omt, odt, bsz, s, w_up_mla, w_up_diff, w_gate, b_gate, w_o, g, b):
    n, d = h.shape
    tm = ROW_TILE
    spt = s // tm
    ws = [w_up_mla.astype(BF16), w_up_diff.astype(BF16), w_gate.astype(BF16), b_gate.reshape(1, -1),
          w_o.astype(BF16), g.reshape(1, d), b.reshape(1, d)]

    return pl.pallas_call(
        _post_mixer_kernel,
        grid=(bsz, spt),
        in_specs=[
            pl.BlockSpec((tm, d), lambda bb, i: (bb * spt + i, 0)),
            pl.BlockSpec((None, omt.shape[1], tm), lambda bb, i: (bb, 0, i)),
            pl.BlockSpec((None, odt.shape[1], tm), lambda bb, i: (bb, 0, i)),
        ] + [_resident(a) for a in ws],
        out_specs=pl.BlockSpec((tm, d), lambda bb, i: (bb * spt + i, 0)),
        out_shape=jax.ShapeDtypeStruct((n, d), F32),
        compiler_params=_params(("arbitrary", "arbitrary")),
        name="post_mixer",
    )(h, omt, odt, *ws)


def _mem_kv_kernel(mem_ref, w_ref, k_ref, v_ref):
    d = k_ref.shape[1]
    kv = _dot(mem_ref[...].astype(BF16), w_ref[...])
    k_ref[...] = kv[:, :d].astype(BF16)
    v_ref[...] = kv[:, d:].astype(BF16)


def _mem_kv(mem, w_kv):
    bsz, m, d = mem.shape
    w = w_kv.astype(BF16)
    spec = pl.BlockSpec((None, m, d), lambda b: (b, 0, 0))
    return pl.pallas_call(
        _mem_kv_kernel,
        grid=(bsz,),
        in_specs=[spec, pl.BlockSpec(w.shape, lambda b: (0, 0))],
        out_specs=(spec, spec),
        out_shape=(jax.ShapeDtypeStruct((bsz, m, d), BF16), jax.ShapeDtypeStruct((bsz, m, d), BF16)),
        compiler_params=_params(("arbitrary",)),
        name="mem_kv",
    )(mem, w)


def _mem_attn_kernel(h_ref, k_ref, v_ref, wq_ref, wo_ref, g_ref, b_ref, o_ref):
    d = h_ref.shape[1]
    hd = d // MEM_HEADS
    for rows in _chains(h_ref.shape[0], MEM_CHAIN_ROWS):
        h = h_ref[rows, :]
        q = _dot(h.astype(BF16), wq_ref[...]).astype(BF16)
        heads = [slice(hh * hd, (hh + 1) * hd) for hh in range(MEM_HEADS)]
        logits = [_dot_nt(q[:, cols], k_ref[:, cols]) * (hd ** -0.5) for cols in heads]
        probs = []
        for s in logits:
            p = jnp.exp(s - jnp.max(s, axis=-1, keepdims=True))
            probs.append((p / jnp.sum(p, axis=-1, keepdims=True)).astype(BF16))
        outs = [_dot(p, v_ref[:, cols]).astype(BF16) for cols, p in zip(heads, probs)]
        att = _dot(jnp.concatenate(outs, axis=1), wo_ref[...])
        o_ref[rows, :] = _layer_norm(ALPHA * h + att, g_ref[...], b_ref[...])


def _mem_attn(h, km, vm, bsz, s, w_q, w_o, g, b):
    n, d = h.shape
    tm = WIDE_ROW_TILE
    spt = s // tm
    m = km.shape[1]
    ws = [w_q.astype(BF16), w_o.astype(BF16), g.reshape(1, d), b.reshape(1, d)]

    return pl.pallas_call(
        _mem_attn_kernel,
        grid=(bsz, spt),
        in_specs=[
            pl.BlockSpec((tm, d), lambda bb, i: (bb * spt + i, 0)),
            pl.BlockSpec((None, m, d), lambda bb, i: (bb, 0, 0)),
            pl.BlockSpec((None, m, d), lambda bb, i: (bb, 0, 0)),
        ] + [_resident(a) for a in ws],
        out_specs=pl.BlockSpec((tm, d), lambda bb, i: (bb * spt + i, 0)),
        out_shape=jax.ShapeDtypeStruct((n, d), F32),
        compiler_params=_params(("arbitrary", "arbitrary")),
        name="mem_attn",
    )(h, km, vm, *ws)


def kernel(x, mem, rel_bias, ffn1_w1, ffn1_w3, ffn1_w2, ln1_g, ln1_b, w_in, q_norm_g, kv_norm_g, w_q_b, w_kv_b, lam_q1, lam_k1, lam_q2, lam_k2, diff_norm_g, w_gate, b_gate, w_up_mla, w_up_diff, w_o, ln2_g, ln2_b, mem_w_q, mem_w_kv, mem_w_o, ln3_g, ln3_b, ffn2_w1, ffn2_w3, ffn2_w2, ln4_g, ln4_b):
    bsz, s, d = x.shape
    assert ffn1_w1.shape[0] == DEPTH
    l = 0
    h = x.reshape(bsz * s, d)

    h = _ffn_ln(h, ffn1_w1[l], ffn1_w3[l], ffn1_w2[l], ln1_g[l], ln1_b[l])

    qtm, km, vtm, qtd, kd, vtd = _in_proj(h, bsz, s, w_in[l], q_norm_g[l], kv_norm_g[l], w_q_b[l], w_kv_b[l])
    bias = _rel_bias_tiles(rel_bias, ATTN_TQ, ATTN_TK)
    omt = _mla_attn(qtm, km, vtm, ATTN_TQ, ATTN_TK)
    lamv = jnp.stack([lam_q1[l], lam_k1[l], lam_q2[l], lam_k2[l]], axis=0)
    odt = _diff_attn(qtd, kd, vtd, bias, lamv, diff_norm_g[l].reshape(-1, 1), ATTN_TQ, ATTN_TK)
    h = _post_mixer(h, omt, odt, bsz, s, w_up_mla[l], w_up_diff[l], w_gate[l], b_gate[l], w_o[l],
                    ln2_g[l], ln2_b[l])

    kmem, vmem = _mem_kv(mem, mem_w_kv[l])
    h = _mem_attn(h, kmem, vmem, bsz, s, mem_w_q[l], mem_w_o[l], ln3_g[l], ln3_b[l])

    h = _ffn_ln(h, ffn2_w1[l], ffn2_w3[l], ffn2_w2[l], ln4_g[l], ln4_b[l])
    return h.reshape(bsz, s, d)
```

```python
import functools
import math

import jax
import jax.numpy as jnp
from jax import lax
from jax.experimental import pallas as pl
from jax.experimental.pallas import tpu as pltpu

F32 = jnp.float32
BF16 = jnp.bfloat16

DEPTH = 1
MLA_HEADS = 8
MLA_Q_RANK = 384
MLA_KV_RANK = 256
MLA_NOPE = 64
MLA_ROPE = 32
MLA_V = 64
ROPE_BASE = 10000.0
DIFF_HEADS = 4
DIFF_D = 64
REL_BUCKETS = 32
REL_MAX_DIST = 128
MEM_HEADS = 4
ALPHA = (2.0 * DEPTH) ** 0.25
LN_EPS = 1e-5
RMS_EPS = 1e-6
LAM_INIT = 0.8 - 0.6 * math.exp(-0.3 * 0)
LOG2E = math.log2(math.e)

HEAD_PAD = 128
VMEM_LIMIT_BYTES = 56 * 1024 * 1024

ROW_TILE = 1024
WIDE_ROW_TILE = 2048
CHAIN_ROWS = 256
MEM_CHAIN_ROWS = 512
ATTN_TQ = 512
ATTN_TK = 256
MLA_HEADS_PER_STEP = 4
MLA_STAGE_ORDER = "vcs"
DIFF_STAGE_ORDER = "csv"


def _params(semantics):
    return pltpu.CompilerParams(dimension_semantics=semantics, vmem_limit_bytes=VMEM_LIMIT_BYTES)


def _dot(a, b):
    return jnp.dot(a, b, preferred_element_type=F32)


def _dot_nt(a, b):
    return lax.dot_general(a, b, (((1,), (1,)), ((), ())), preferred_element_type=F32)


def _dot_tn(a, b):
    return lax.dot_general(a, b, (((0,), (0,)), ((), ())), preferred_element_type=F32)


def _layer_norm(y, g, b):
    mu = jnp.mean(y, axis=-1, keepdims=True)
    d = y - mu
    var = jnp.mean(d * d, axis=-1, keepdims=True)
    return d * lax.rsqrt(var + LN_EPS) * g + b


def _rms_norm_rows(x, g):
    return x * lax.rsqrt(jnp.mean(x * x, axis=-1, keepdims=True) + RMS_EPS) * g


def _chains(rows, chain=CHAIN_ROWS):
    return [slice(r, r + chain) for r in range(0, rows, chain)]


def _resident(a):
    return pl.BlockSpec(a.shape, lambda *_: (0,) * a.ndim, pipeline_mode=pl.Buffered(1))


def _ffn_ln_kernel(x_ref, w1_ref, w3_ref, w2_ref, g_ref, b_ref, o_ref):
    for rows in _chains(x_ref.shape[0]):
        x = x_ref[rows, :]
        xb = x.astype(BF16)
        u = _dot(xb, w1_ref[...])
        silu = u / (1.0 + jnp.exp(-u))
        act = silu * _dot(xb, w3_ref[...])
        y = ALPHA * x + 0.5 * _dot(act.astype(BF16), w2_ref[...])
        o_ref[rows, :] = _layer_norm(y, g_ref[...], b_ref[...])


def _ffn_ln(x, w1, w3, w2, g, b):
    n, d = x.shape
    tm = ROW_TILE
    assert n % tm == 0
    ws = [w1.astype(BF16), w3.astype(BF16), w2.astype(BF16), g.reshape(1, d), b.reshape(1, d)]
    return pl.pallas_call(
        _ffn_ln_kernel,
        grid=(n // tm,),
        in_specs=[pl.BlockSpec((tm, d), lambda i: (i, 0))] + [_resident(a) for a in ws],
        out_specs=pl.BlockSpec((tm, d), lambda i: (i, 0)),
        out_shape=jax.ShapeDtypeStruct((n, d), F32),
        compiler_params=_params(("arbitrary",)),
        name="ffn_ln",
    )(x, *ws)


def _in_proj_kernel(h_ref, wnat_ref, wt_ref, wq_ref, wkn_ref, wvt_ref, gq_ref, gkv_ref,
                    cosk_ref, sink_ref, cosq_ref, sinq_ref,
                    qtm_ref, km_ref, vtm_ref, qtd_ref, kd_ref, vtd_ref):
    tk = vtm_ref.shape[-1]
    half = MLA_ROPE // 2
    nqd = qtd_ref.shape[0]
    o1 = MLA_Q_RANK
    o2 = o1 + MLA_KV_RANK
    o3 = o2 + HEAD_PAD
    q_scale = (MLA_NOPE + MLA_ROPE) ** -0.5 * LOG2E
    for rows in _chains(h_ref.shape[0]):
        key_tiles = range(rows.start // tk, rows.stop // tk)
        hb = h_ref[rows, :].astype(BF16)

        pt = _dot_nt(wt_ref[...], hb)
        qtd_ref[:, rows] = (pt[:nqd] * (DIFF_D ** -0.5 * LOG2E)).astype(BF16)
        for c, t in enumerate(key_tiles):
            vtd_ref[t] = pt[nqd:, c * tk:(c + 1) * tk].astype(BF16)

        pn = _dot(hb, wnat_ref[...])
        c_q, c_kv, krg, kd = pn[:, :o1], pn[:, o1:o2], pn[:, o2:o3], pn[:, o3:]

        cqn = _rms_norm_rows(c_q, gq_ref[...]).astype(BF16)
        ckvn = _rms_norm_rows(c_kv, gkv_ref[...]).astype(BF16)
        qt = _dot_nt(wq_ref[...], cqn)
        kn = _dot(ckvn, wkn_ref[...])
        vt = _dot_nt(wvt_ref[...], ckvn)

        for hh in range(DIFF_HEADS):
            kd_ref[hh, rows, :] = kd[:, hh * 2 * DIFF_D:(hh + 1) * 2 * DIFF_D].astype(BF16)

        cos_q, sin_q = cosq_ref[:, rows], sinq_ref[:, rows]
        for hh in range(MLA_HEADS):
            lo = hh * HEAD_PAD
            r0 = lo + MLA_NOPE
            r1, r2 = r0 + half, r0 + MLA_ROPE
            x1, x2 = qt[r0:r1], qt[r1:r2]
            qtm_ref[lo:r0, rows] = (qt[lo:r0] * q_scale).astype(BF16)
            qtm_ref[r0:r1, rows] = ((x1 * cos_q - x2 * sin_q) * q_scale).astype(BF16)
            qtm_ref[r1:r2, rows] = ((x2 * cos_q + x1 * sin_q) * q_scale).astype(BF16)
            qtm_ref[r2:lo + HEAD_PAD, rows] = jnp.zeros((lo + HEAD_PAD - r2, CHAIN_ROWS), BF16)

        krh = jnp.concatenate([krg[:, MLA_ROPE:], krg[:, :MLA_ROPE]], axis=1)
        kr = krg * cosk_ref[rows, :] + krh * sink_ref[rows, :]
        for hh in range(MLA_HEADS):
            lo, hi = hh * HEAD_PAD, (hh + 1) * HEAD_PAD
            km_ref[rows, lo:hi] = (kn[:, lo:hi] + kr).astype(BF16)
        for c, t in enumerate(key_tiles):
            vtm_ref[t] = vt[:, c * tk:(c + 1) * tk].astype(BF16)


def _in_proj_weights(w_in, w_q_b, w_kv_b):
    d = w_in.shape[0]
    o_cq = MLA_Q_RANK
    o_ckv = o_cq + MLA_KV_RANK
    o_kr = o_ckv + MLA_ROPE
    n_d = 2 * DIFF_HEADS * DIFF_D
    o_qd = o_kr + n_d
    o_kd = o_qd + n_d
    half = MLA_ROPE // 2
    pad = HEAD_PAD - MLA_NOPE - MLA_ROPE
    assert pad == MLA_ROPE
    kr = w_in[:, o_ckv:o_kr]
    krg = jnp.concatenate([jnp.zeros((d, MLA_NOPE), F32), kr, -kr[:, half:], kr[:, :half]], axis=1)
    w_nat = jnp.concatenate([w_in[:, :o_ckv], krg, w_in[:, o_qd:o_kd]], axis=1).astype(BF16)
    w_t = jnp.concatenate([w_in[:, o_kr:o_qd], w_in[:, o_kd:]], axis=1).T.astype(BF16)

    r = w_q_b.shape[0]
    wq = w_q_b.reshape(r, MLA_HEADS, MLA_NOPE + MLA_ROPE)
    wq = jnp.concatenate([wq, jnp.zeros((r, MLA_HEADS, pad), F32)], axis=-1)
    wq_t = wq.reshape(r, MLA_HEADS * HEAD_PAD).T.astype(BF16)

    rk = w_kv_b.shape[0]
    wkv = w_kv_b.reshape(rk, MLA_HEADS, MLA_NOPE + MLA_V)
    wkn = jnp.concatenate([wkv[..., :MLA_NOPE], jnp.zeros((rk, MLA_HEADS, HEAD_PAD - MLA_NOPE), F32)],
                          axis=-1).reshape(rk, MLA_HEADS * HEAD_PAD).astype(BF16)
    wv_t = wkv[..., MLA_NOPE:].reshape(rk, MLA_HEADS * MLA_V).T.astype(BF16)
    return w_nat, w_t, wq_t, wkn, wv_t


def _rope_tables(s):
    inv_freq = ROPE_BASE ** (-jnp.arange(0, MLA_ROPE, 2, dtype=F32) / MLA_ROPE)
    ang = jnp.arange(s).astype(F32)[:, None] * inv_freq[None, :]
    cos, sin = jnp.cos(ang), jnp.sin(ang)
    z_lo = jnp.zeros((s, MLA_NOPE), F32)
    z_hi = jnp.zeros((s, HEAD_PAD - MLA_NOPE - MLA_ROPE), F32)
    cos_k = jnp.concatenate([z_lo, cos, cos, z_hi], axis=1)
    sin_k = jnp.concatenate([z_lo, sin, sin, z_hi], axis=1)
    return cos_k, sin_k, cos.T, sin.T


def _in_proj(h, bsz, s, w_in, q_norm_g, kv_norm_g, w_q_b, w_kv_b):
    n, d = h.shape
    tm, tk = ROW_TILE, ATTN_TK
    assert s % tm == 0 and CHAIN_ROWS % tk == 0
    w_nat, w_t, wq_t, wkn, wv_t = _in_proj_weights(w_in, w_q_b, w_kv_b)
    cos_k, sin_k, cos_q, sin_q = _rope_tables(s)
    half = MLA_ROPE // 2
    spt = s // tm
    nqm = MLA_HEADS * HEAD_PAD
    nvm = MLA_HEADS * MLA_V
    nd = 2 * DIFF_HEADS * DIFF_D

    gq = q_norm_g.reshape(1, -1)
    gkv = kv_norm_g.reshape(1, -1)
    out_shape = (
        jax.ShapeDtypeStruct((bsz, nqm, s), BF16),
        jax.ShapeDtypeStruct((bsz, s, nqm), BF16),
        jax.ShapeDtypeStruct((bsz, s // tk, nvm, tk), BF16),
        jax.ShapeDtypeStruct((bsz, nd, s), BF16),
        jax.ShapeDtypeStruct((bsz, DIFF_HEADS, s, nd // DIFF_HEADS), BF16),
        jax.ShapeDtypeStruct((bsz, s // tk, nd, tk), BF16),
    )
    out_specs = (
        pl.BlockSpec((None, nqm, tm), lambda b, i: (b, 0, i)),
        pl.BlockSpec((None, tm, nqm), lambda b, i: (b, i, 0)),
        pl.BlockSpec((None, tm // tk, nvm, tk), lambda b, i: (b, i, 0, 0)),
        pl.BlockSpec((None, nd, tm), lambda b, i: (b, 0, i)),
        pl.BlockSpec((None, DIFF_HEADS, tm, nd // DIFF_HEADS), lambda b, i: (b, 0, i, 0)),
        pl.BlockSpec((None, tm // tk, nd, tk), lambda b, i: (b, i, 0, 0)),
    )
    return pl.pallas_call(
        _in_proj_kernel,
        grid=(bsz, spt),
        in_specs=[
            pl.BlockSpec((tm, d), lambda b, i: (b * spt + i, 0)),
            _resident(w_nat), _resident(w_t), _resident(wq_t), _resident(wkn), _resident(wv_t),
            _resident(gq), _resident(gkv),
            pl.BlockSpec((tm, HEAD_PAD), lambda b, i: (i, 0)),
            pl.BlockSpec((tm, HEAD_PAD), lambda b, i: (i, 0)),
            pl.BlockSpec((half, tm), lambda b, i: (0, i)),
            pl.BlockSpec((half, tm), lambda b, i: (0, i)),
        ],
        out_specs=out_specs,
        out_shape=out_shape,
        compiler_params=_params(("arbitrary", "arbitrary")),
        name="in_proj",
    )(h, w_nat, w_t, wq_t, wkn, wv_t, gq, gkv, cos_k, sin_k, cos_q, sin_q)


BIAS_TILES = 3


def _rel_bias_kernel(rb_ref, o_ref, *, tq, tk):
    c = pl.program_id(0) - 1
    kk = lax.broadcasted_iota(jnp.int32, (tk, tq), 0)
    qq = lax.broadcasted_iota(jnp.int32, (tk, tq), 1)
    n = jnp.maximum(qq - kk - c * tk, 0)
    max_exact = REL_BUCKETS // 2
    nf = jnp.maximum(n, 1).astype(F32)
    large = max_exact + (jnp.log(nf / max_exact) / math.log(REL_MAX_DIST / max_exact)
                         * (REL_BUCKETS - max_exact)).astype(jnp.int32)
    large = jnp.minimum(large, REL_BUCKETS - 1)
    bucket = jnp.where(n < max_exact, n, large)
    for m in range(o_ref.shape[0]):
        acc = jnp.full((tk, tq), rb_ref[0, m], F32)
        for bkt in range(1, REL_BUCKETS):
            acc = jnp.where(bucket >= bkt, rb_ref[bkt, m], acc)
        o_ref[m] = (acc - rb_ref[REL_BUCKETS - 1, m]) * LOG2E


def _rel_bias_tiles(rel_bias, tq, tk):
    assert tq == 2 * tk and tk + 1 >= REL_MAX_DIST
    nmaps = rel_bias.shape[1]
    return pl.pallas_call(
        functools.partial(_rel_bias_kernel, tq=tq, tk=tk),
        grid=(BIAS_TILES,),
        in_specs=[pl.BlockSpec(memory_space=pltpu.SMEM)],
        out_specs=pl.BlockSpec((nmaps, None, tk, tq), lambda t: (0, t, 0, 0)),
        out_shape=jax.ShapeDtypeStruct((nmaps, BIAS_TILES, tk, tq), F32),
        compiler_params=_params(("arbitrary",)),
        name="rel_bias",
    )(rel_bias)


SCORES_AHEAD = 3
VALUES_BEHIND = 1
ONES_ROWS = 16


def _flash_pipeline(q_fn, k_fn, vt_fn, logits_fn, finalize_fn, s_ref, mt_ref, p_ref, al_ref, m_ref, l_ref, acc_ref,
                    *, nstreams, nq, tq, tk, order):
    tiles = [(g, i, t, t - 2 * i) for g in range(nstreams) for i in range(nq) for t in range(2 * i + 2)]
    ones = jnp.ones((ONES_ROWS, tk), BF16)
    ahead = s_ref.shape[0] - 1
    behind = p_ref.shape[0] - 1

    def lane0(c):
        return tk if c == 1 else 0

    def scores(n):
        g, i, t, c = tiles[n]
        slot, lo = n % (ahead + 1), lane0(c)
        s = logits_fn(g, _dot(k_fn(g, t), q_fn(g, i, lo)), c, lo)
        s_ref[slot, :, lo:] = s
        mt_ref[slot, :, lo:] = jnp.max(s, axis=0, keepdims=True)

    def softmax(n):
        g, i, t, c = tiles[n]
        sslot, slot, lo = n % (ahead + 1), n % (behind + 1), lane0(c)
        m_new = mt_ref[sslot, :, lo:]
        if t > 0:
            m_prev = m_ref[:, lo:]
            m_new = jnp.maximum(m_prev, m_new)
            al_ref[slot, :, lo:] = jnp.exp2(m_prev - m_new)
        m_ref[:, lo:] = m_new
        p = jnp.exp2(s_ref[sslot, :, lo:] - m_new)
        p_ref[slot, :, lo:] = p.astype(BF16)
        if ONES_ROWS == 0:
            lslot = (g * nq + i) % 2
            l_new = jnp.sum(p, axis=0, keepdims=True)
            if t > 0:
                l_new = al_ref[slot, :, lo:] * l_ref[lslot, :, lo:] + l_new
            l_ref[lslot, :, lo:] = l_new

    def values(n):
        g, i, t, c = tiles[n]
        slot, lo = n % (behind + 1), lane0(c)
        vt = vt_fn(g, t)
        if ONES_ROWS:
            vt = jnp.concatenate([vt, ones], axis=0)
        pv = _dot(vt, p_ref[slot, :, lo:])
        if t > 0:
            pv = al_ref[slot, :, lo:] * acc_ref[:, lo:] + pv
        acc_ref[:, lo:] = pv
        if c == 1:
            acc = acc_ref[...]
            dv = acc.shape[0] - ONES_ROWS
            finalize_fn(g, i, acc[:dv] / (acc[dv:dv + 1] if ONES_ROWS else l_ref[(g * nq + i) % 2]))

    for n in range(ahead):
        scores(n)
    assert sorted(order) == ["c", "s", "v"] and (behind > 0 or order.index("s") < order.index("v"))
    for n in range(len(tiles) + behind):
        for stage in order:
            if stage == "v" and behind <= n:
                values(n - behind)
            if stage == "c" and n + ahead < len(tiles):
                scores(n + ahead)
            if stage == "s" and n < len(tiles):
                softmax(n)


def _tri_mask(s):
    kk = lax.broadcasted_iota(jnp.int32, s.shape, 0)
    qq = lax.broadcasted_iota(jnp.int32, s.shape, 1)
    return jnp.where(kk <= qq, s, -jnp.inf)


def _flash_scratch(dv, tq, tk, ahead=SCORES_AHEAD):
    return [pltpu.VMEM((ahead + 1, tk, tq), F32), pltpu.VMEM((ahead + 1, 1, tq), F32),
            pltpu.VMEM((VALUES_BEHIND + 1, tk, tq), BF16), pltpu.VMEM((VALUES_BEHIND + 1, 1, tq), F32),
            pltpu.VMEM((1, tq), F32), pltpu.VMEM((2, 1, tq), F32), pltpu.VMEM((dv + ONES_ROWS, tq), F32)]


def _mla_attn_kernel(qt_ref, k_ref, vt_ref, o_ref, *scratch, heads, tq, tk):

    def q_fn(g, i, lo):
        return qt_ref[g * HEAD_PAD:(g + 1) * HEAD_PAD, i * tq + lo:(i + 1) * tq]

    def k_fn(g, t):
        return k_ref[t * tk:(t + 1) * tk, g * HEAD_PAD:(g + 1) * HEAD_PAD]

    def vt_fn(g, t):
        return vt_ref[t, g * MLA_V:(g + 1) * MLA_V, :]

    def logits(g, s, c, lo):
        return _tri_mask(s) if c >= 0 else s

    def finalize(g, i, o):
        o_ref[g * MLA_V:(g + 1) * MLA_V, i * tq:(i + 1) * tq] = o.astype(BF16)

    _flash_pipeline(q_fn, k_fn, vt_fn, logits, finalize, *scratch,
                    nstreams=heads, nq=qt_ref.shape[1] // tq, tq=tq, tk=tk, order=MLA_STAGE_ORDER)


def _mla_attn(qt, k, vt, tq, tk):
    bsz, nq, s = qt.shape
    nkt = s // tk
    hps = MLA_HEADS_PER_STEP
    assert tq == 2 * tk and s % tq == 0 and MLA_HEADS % hps == 0
    return pl.pallas_call(
        functools.partial(_mla_attn_kernel, heads=hps, tq=tq, tk=tk),
        grid=(MLA_HEADS // hps, bsz),
        in_specs=[
            pl.BlockSpec((None, hps * HEAD_PAD, s), lambda h, b: (b, h, 0)),
            pl.BlockSpec((None, s, hps * HEAD_PAD), lambda h, b: (b, 0, h)),
            pl.BlockSpec((None, nkt, hps * MLA_V, tk), lambda h, b: (b, 0, h, 0)),
        ],
        out_specs=pl.BlockSpec((None, hps * MLA_V, s), lambda h, b: (b, h, 0)),
        out_shape=jax.ShapeDtypeStruct((bsz, MLA_HEADS * MLA_V, s), BF16),
        scratch_shapes=_flash_scratch(MLA_V, tq, tk),
        compiler_params=_params(("arbitrary", "arbitrary")),
        name="mla_attn",
    )(qt, k, vt)


def _diff_attn_kernel(qt_ref, k_ref, vt_ref, bias_ref, lamv_ref, gn_ref, o_ref, o0_ref, qm_ref, *scratch, tq, tk):
    qt = qt_ref[...]
    row = lax.broadcasted_iota(jnp.int32, qt.shape, 0)
    for g in range(2):
        keep = jnp.logical_and(row >= g * DIFF_D, row < (g + 1) * DIFF_D)
        qm_ref[g] = jnp.where(keep, qt, jnp.zeros_like(qt))

    lv = lamv_ref[...]
    lam = (jnp.exp(jnp.sum(lv[0:1] * lv[1:2], axis=1, keepdims=True))
           - jnp.exp(jnp.sum(lv[2:3] * lv[3:4], axis=1, keepdims=True)) + LAM_INIT)

    def q_fn(g, i, lo):
        return qm_ref[g, :, i * tq + lo:(i + 1) * tq]

    def k_fn(g, t):
        return k_ref[t * tk:(t + 1) * tk, :]

    def vt_fn(g, t):
        return vt_ref[t]

    def logits(g, s, c, lo):
        if c >= -1:
            s = s + bias_ref[g, c + 1, :, lo:]
        return _tri_mask(s) if c >= 0 else s

    def finalize(g, i, o):
        cols = slice(i * tq, (i + 1) * tq)
        if g == 0:
            o0_ref[:, cols] = o
        else:
            d = o0_ref[:, cols] - lam * o
            d = d * lax.rsqrt(jnp.mean(d * d, axis=0, keepdims=True) + RMS_EPS) * gn_ref[...] * (1.0 - LAM_INIT)
            o_ref[:, cols] = d.astype(BF16)

    _flash_pipeline(q_fn, k_fn, vt_fn, logits, finalize, *scratch,
                    nstreams=2, nq=qt_ref.shape[1] // tq, tq=tq, tk=tk, order=DIFF_STAGE_ORDER)


def _diff_attn(qt, k, vt, bias, lamv, gn, tq, tk):
    bsz, nd, s = qt.shape
    nkt = s // tk
    dv = 2 * DIFF_D
    assert tq == 2 * tk and s % tq == 0
    return pl.pallas_call(
        functools.partial(_diff_attn_kernel, tq=tq, tk=tk),
        grid=(DIFF_HEADS, bsz),
        in_specs=[
            pl.BlockSpec((None, dv, s), lambda h, b: (b, h, 0)),
            pl.BlockSpec((None, None, s, dv), lambda h, b: (b, h, 0, 0)),
            pl.BlockSpec((None, nkt, dv, tk), lambda h, b: (b, 0, h, 0)),
            pl.BlockSpec((2, BIAS_TILES, tk, tq), lambda h, b: (h, 0, 0, 0)),
            pl.BlockSpec(lamv.shape, lambda h, b: (0, 0)),
            pl.BlockSpec(gn.shape, lambda h, b: (0, 0)),
        ],
        out_specs=pl.BlockSpec((None, dv, s), lambda h, b: (b, h, 0)),
        out_shape=jax.ShapeDtypeStruct((bsz, nd, s), BF16),
        scratch_shapes=[pltpu.VMEM((dv, s), F32), pltpu.VMEM((2, dv, s), BF16)] + _flash_scratch(dv, tq, tk),
        compiler_params=_params(("arbitrary", "arbitrary")),
        name="diff_attn",
    )(qt, k, vt, bias, lamv, gn)


def _post_mixer_kernel(h_ref, omt_ref, odt_ref, wum_ref, wud_ref, wg_ref, bg_ref, wo_ref, g_ref, b_ref,
                       o_ref):
    d = h_ref.shape[1]
    for rows in _chains(h_ref.shape[0]):
        h = h_ref[rows, :]
        z = _dot(h.astype(BF16), wg_ref[...]) + bg_ref[...]
        gate = 1.0 / (1.0 + jnp.exp(-z))
        y_m = _dot_tn(omt_ref[:, rows], wum_ref[...])
        y_d = _dot_tn(odt_ref[:, rows], wud_ref[...])
        mixed = gate[:, :d] * y_m + gate[:, d:] * y_d
        mix = _dot(mixed.astype(BF16), wo_ref[...])
        o_ref[rows, :] = _layer_norm(ALPHA * h + mix, g_ref[...], b_ref[...])


def _post_mixer(h, omt, odt, bsz, s, w_up_mla, w_up_diff, w_gate, b_gate, w_o, g, b):
    n, d = h.shape
    tm = ROW_TILE
    spt = s // tm
    ws = [w_up_mla.astype(BF16), w_up_diff.astype(BF16), w_gate.astype(BF16), b_gate.reshape(1, -1),
          w_o.astype(BF16), g.reshape(1, d), b.reshape(1, d)]

    return pl.pallas_call(
        _post_mixer_kernel,
        grid=(bsz, spt),
        in_specs=[
            pl.BlockSpec((tm, d), lambda bb, i: (bb * spt + i, 0)),
            pl.BlockSpec((None, omt.shape[1], tm), lambda bb, i: (bb, 0, i)),
            pl.BlockSpec((None, odt.shape[1], tm), lambda bb, i: (bb, 0, i)),
        ] + [_resident(a) for a in ws],
        out_specs=pl.BlockSpec((tm, d), lambda bb, i: (bb * spt + i, 0)),
        out_shape=jax.ShapeDtypeStruct((n, d), F32),
        compiler_params=_params(("arbitrary", "arbitrary")),
        name="post_mixer",
    )(h, omt, odt, *ws)


def _mem_kv_kernel(mem_ref, w_ref, k_ref, v_ref):
    d = k_ref.shape[1]
    kv = _dot(mem_ref[...].astype(BF16), w_ref[...])
    k_ref[...] = kv[:, :d].astype(BF16)
    v_ref[...] = kv[:, d:].astype(BF16)


def _mem_kv(mem, w_kv):
    bsz, m, d = mem.shape
    w = w_kv.astype(BF16)
    spec = pl.BlockSpec((None, m, d), lambda b: (b, 0, 0))
    return pl.pallas_call(
        _mem_kv_kernel,
        grid=(bsz,),
        in_specs=[spec, pl.BlockSpec(w.shape, lambda b: (0, 0))],
        out_specs=(spec, spec),
        out_shape=(jax.ShapeDtypeStruct((bsz, m, d), BF16), jax.ShapeDtypeStruct((bsz, m, d), BF16)),
        compiler_params=_params(("arbitrary",)),
        name="mem_kv",
    )(mem, w)


def _mem_attn_kernel(h_ref, k_ref, v_ref, wq_ref, wo_ref, g_ref, b_ref, o_ref):
    d = h_ref.shape[1]
    hd = d // MEM_HEADS
    for rows in _chains(h_ref.shape[0], MEM_CHAIN_ROWS):
        h = h_ref[rows, :]
        q = _dot(h.astype(BF16), wq_ref[...]).astype(BF16)
        heads = [slice(hh * hd, (hh + 1) * hd) for hh in range(MEM_HEADS)]
        logits = [_dot_nt(q[:, cols], k_ref[:, cols]) * (hd ** -0.5) for cols in heads]
        probs = []
        for s in logits:
            p = jnp.exp(s - jnp.max(s, axis=-1, keepdims=True))
            probs.append((p / jnp.sum(p, axis=-1, keepdims=True)).astype(BF16))
        outs = [_dot(p, v_ref[:, cols]).astype(BF16) for cols, p in zip(heads, probs)]
        att = _dot(jnp.concatenate(outs, axis=1), wo_ref[...])
        o_ref[rows, :] = _layer_norm(ALPHA * h + att, g_ref[...], b_ref[...])


def _mem_attn(h, km, vm, bsz, s, w_q, w_o, g, b):
    n, d = h.shape
    tm = WIDE_ROW_TILE
    spt = s // tm
    m = km.shape[1]
    ws = [w_q.astype(BF16), w_o.astype(BF16), g.reshape(1, d), b.reshape(1, d)]

    return pl.pallas_call(
        _mem_attn_kernel,
        grid=(bsz, spt),
        in_specs=[
            pl.BlockSpec((tm, d), lambda bb, i: (bb * spt + i, 0)),
            pl.BlockSpec((None, m, d), lambda bb, i: (bb, 0, 0)),
            pl.BlockSpec((None, m, d), lambda bb, i: (bb, 0, 0)),
        ] + [_resident(a) for a in ws],
        out_specs=pl.BlockSpec((tm, d), lambda bb, i: (bb * spt + i, 0)),
        out_shape=jax.ShapeDtypeStruct((n, d), F32),
        compiler_params=_params(("arbitrary", "arbitrary")),
        name="mem_attn",
    )(h, km, vm, *ws)


def kernel(x, mem, rel_bias, ffn1_w1, ffn1_w3, ffn1_w2, ln1_g, ln1_b, w_in, q_norm_g, kv_norm_g, w_q_b, w_kv_b, lam_q1, lam_k1, lam_q2, lam_k2, diff_norm_g, w_gate, b_gate, w_up_mla, w_up_diff, w_o, ln2_g, ln2_b, mem_w_q, mem_w_kv, mem_w_o, ln3_g, ln3_b, ffn2_w1, ffn2_w3, ffn2_w2, ln4_g, ln4_b):
    bsz, s, d = x.shape
    assert ffn1_w1.shape[0] == DEPTH
    l = 0
    h = x.reshape(bsz * s, d)

    h = _ffn_ln(h, ffn1_w1[l], ffn1_w3[l], ffn1_w2[l], ln1_g[l], ln1_b[l])

    qtm, km, vtm, qtd, kd, vtd = _in_proj(h, bsz, s, w_in[l], q_norm_g[l], kv_norm_g[l], w_q_b[l], w_kv_b[l])
    bias = _rel_bias_tiles(rel_bias, ATTN_TQ, ATTN_TK)
    omt = _mla_attn(qtm, km, vtm, ATTN_TQ, ATTN_TK)
    lamv = jnp.stack([lam_q1[l], lam_k1[l], lam_q2[l], lam_k2[l]], axis=0)
    odt = _diff_attn(qtd, kd, vtd, bias, lamv, diff_norm_g[l].reshape(-1, 1), ATTN_TQ, ATTN_TK)
    h = _post_mixer(h, omt, odt, bsz, s, w_up_mla[l], w_up_diff[l], w_gate[l], b_gate[l], w_o[l],
                    ln2_g[l], ln2_b[l])

    kmem, vmem = _mem_kv(mem, mem_w_kv[l])
    h = _mem_attn(h, kmem, vmem, bsz, s, mem_w_q[l], mem_w_o[l], ln3_g[l], ln3_b[l])

    h = _ffn_ln(h, ffn2_w1[l], ffn2_w3[l], ffn2_w2[l], ln4_g[l], ln4_b[l])
    return h.reshape(bsz, s, d)
```

```python
import functools
import math

import jax
import jax.numpy as jnp
from jax import lax
from jax.experimental import pallas as pl
from jax.experimental.pallas import tpu as pltpu

F32 = jnp.float32
BF16 = jnp.bfloat16

DEPTH = 1
MLA_HEADS = 8
MLA_Q_RANK = 384
MLA_KV_RANK = 256
MLA_NOPE = 64
MLA_ROPE = 32
MLA_V = 64
ROPE_BASE = 10000.0
DIFF_HEADS = 4
DIFF_D = 64
REL_BUCKETS = 32
REL_MAX_DIST = 128
MEM_HEADS = 4
ALPHA = (2.0 * DEPTH) ** 0.25
LN_EPS = 1e-5
RMS_EPS = 1e-6
LAM_INIT = 0.8 - 0.6 * math.exp(-0.3 * 0)
LOG2E = math.log2(math.e)

HEAD_PAD = 128
VMEM_LIMIT_BYTES = 56 * 1024 * 1024

ROW_TILE = 1024
WIDE_ROW_TILE = 2048
CHAIN_ROWS = 256
MEM_CHAIN_ROWS = 512
ATTN_TQ = 512
ATTN_TK = 256
MLA_HEADS_PER_STEP = 4
MLA_STAGE_ORDER = "vcs"
DIFF_STAGE_ORDER = "csv"


def _params(semantics):
    return pltpu.CompilerParams(dimension_semantics=semantics, vmem_limit_bytes=VMEM_LIMIT_BYTES)


def _dot(a, b):
    return jnp.dot(a, b, preferred_element_type=F32)


def _dot_nt(a, b):
    return lax.dot_general(a, b, (((1,), (1,)), ((), ())), preferred_element_type=F32)


def _dot_tn(a, b):
    return lax.dot_general(a, b, (((0,), (0,)), ((), ())), preferred_element_type=F32)


def _layer_norm(y, g, b):
    mu = jnp.mean(y, axis=-1, keepdims=True)
    d = y - mu
    var = jnp.mean(d * d, axis=-1, keepdims=True)
    return d * lax.rsqrt(var + LN_EPS) * g + b


def _rms_norm_rows(x, g):
    return x * lax.rsqrt(jnp.mean(x * x, axis=-1, keepdims=True) + RMS_EPS) * g


def _chains(rows, chain=CHAIN_ROWS):
    return [slice(r, r + chain) for r in range(0, rows, chain)]


def _resident(a):
    return pl.BlockSpec(a.shape, lambda *_: (0,) * a.ndim, pipeline_mode=pl.Buffered(1))


def _ffn_ln_kernel(x_ref, w1_ref, w3_ref, w2_ref, g_ref, b_ref, o_ref):
    for rows in _chains(x_ref.shape[0]):
        x = x_ref[rows, :]
        xb = x.astype(BF16)
        u = _dot(xb, w1_ref[...])
        silu = u / (1.0 + jnp.exp(-u))
        act = silu * _dot(xb, w3_ref[...])
        y = ALPHA * x + 0.5 * _dot(act.astype(BF16), w2_ref[...])
        o_ref[rows, :] = _layer_norm(y, g_ref[...], b_ref[...])


def _ffn_ln(x, w1, w3, w2, g, b):
    n, d = x.shape
    tm = ROW_TILE
    assert n % tm == 0
    ws = [w1.astype(BF16), w3.astype(BF16), w2.astype(BF16), g.reshape(1, d), b.reshape(1, d)]
    return pl.pallas_call(
        _ffn_ln_kernel,
        grid=(n // tm,),
        in_specs=[pl.BlockSpec((tm, d), lambda i: (i, 0))] + [_resident(a) for a in ws],
        out_specs=pl.BlockSpec((tm, d), lambda i: (i, 0)),
        out_shape=jax.ShapeDtypeStruct((n, d), F32),
        compiler_params=_params(("arbitrary",)),
        name="ffn_ln",
    )(x, *ws)


def _in_proj_kernel(h_ref, wnat_ref, wt_ref, wq_ref, wkn_ref, wvt_ref, gq_ref, gkv_ref,
                    cosk_ref, sink_ref, cosq_ref, sinq_ref,
                    qtm_ref, km_ref, vtm_ref, qtd_ref, kd_ref, vtd_ref):
    tk = vtm_ref.shape[-1]
    half = MLA_ROPE // 2
    nqd = qtd_ref.shape[0]
    o1 = MLA_Q_RANK
    o2 = o1 + MLA_KV_RANK
    o3 = o2 + HEAD_PAD
    q_scale = (MLA_NOPE + MLA_ROPE) ** -0.5 * LOG2E
    for rows in _chains(h_ref.shape[0]):
        key_tiles = range(rows.start // tk, rows.stop // tk)
        hb = h_ref[rows, :].astype(BF16)

        pt = _dot_nt(wt_ref[...], hb)
        qtd_ref[:, rows] = (pt[:nqd] * (DIFF_D ** -0.5 * LOG2E)).astype(BF16)
        for c, t in enumerate(key_tiles):
            vtd_ref[t] = pt[nqd:, c * tk:(c + 1) * tk].astype(BF16)

        pn = _dot(hb, wnat_ref[...])
        c_q, c_kv, krg, kd = pn[:, :o1], pn[:, o1:o2], pn[:, o2:o3], pn[:, o3:]

        cqn = _rms_norm_rows(c_q, gq_ref[...]).astype(BF16)
        ckvn = _rms_norm_rows(c_kv, gkv_ref[...]).astype(BF16)
        qt = _dot_nt(wq_ref[...], cqn)
        kn = _dot(ckvn, wkn_ref[...])
        vt = _dot_nt(wvt_ref[...], ckvn)

        for hh in range(DIFF_HEADS):
            kd_ref[hh, rows, :] = kd[:, hh * 2 * DIFF_D:(hh + 1) * 2 * DIFF_D].astype(BF16)

        cos_q, sin_q = cosq_ref[:, rows], sinq_ref[:, rows]
        for hh in range(MLA_HEADS):
            lo = hh * HEAD_PAD
            r0 = lo + MLA_NOPE
            r1, r2 = r0 + half, r0 + MLA_ROPE
            x1, x2 = qt[r0:r1], qt[r1:r2]
            qtm_ref[lo:r0, rows] = (qt[lo:r0] * q_scale).astype(BF16)
            qtm_ref[r0:r1, rows] = ((x1 * cos_q - x2 * sin_q) * q_scale).astype(BF16)
            qtm_ref[r1:r2, rows] = ((x2 * cos_q + x1 * sin_q) * q_scale).astype(BF16)
            qtm_ref[r2:lo + HEAD_PAD, rows] = jnp.zeros((lo + HEAD_PAD - r2, CHAIN_ROWS), BF16)

        krh = jnp.concatenate([krg[:, MLA_ROPE:], krg[:, :MLA_ROPE]], axis=1)
        kr = krg * cosk_ref[rows, :] + krh * sink_ref[rows, :]
        for hh in range(MLA_HEADS):
            lo, hi = hh * HEAD_PAD, (hh + 1) * HEAD_PAD
            km_ref[rows, lo:hi] = (kn[:, lo:hi] + kr).astype(BF16)
        for c, t in enumerate(key_tiles):
            vtm_ref[t] = vt[:, c * tk:(c + 1) * tk].astype(BF16)


def _in_proj_weights(w_in, w_q_b, w_kv_b):
    d = w_in.shape[0]
    o_cq = MLA_Q_RANK
    o_ckv = o_cq + MLA_KV_RANK
    o_kr = o_ckv + MLA_ROPE
    n_d = 2 * DIFF_HEADS * DIFF_D
    o_qd = o_kr + n_d
    o_kd = o_qd + n_d
    half = MLA_ROPE // 2
    pad = HEAD_PAD - MLA_NOPE - MLA_ROPE
    assert pad == MLA_ROPE
    kr = w_in[:, o_ckv:o_kr]
    krg = jnp.concatenate([jnp.zeros((d, MLA_NOPE), F32), kr, -kr[:, half:], kr[:, :half]], axis=1)
    w_nat = jnp.concatenate([w_in[:, :o_ckv], krg, w_in[:, o_qd:o_kd]], axis=1).astype(BF16)
    w_t = jnp.concatenate([w_in[:, o_kr:o_qd], w_in[:, o_kd:]], axis=1).T.astype(BF16)

    r = w_q_b.shape[0]
    wq = w_q_b.reshape(r, MLA_HEADS, MLA_NOPE + MLA_ROPE)
    wq = jnp.concatenate([wq, jnp.zeros((r, MLA_HEADS, pad), F32)], axis=-1)
    wq_t = wq.reshape(r, MLA_HEADS * HEAD_PAD).T.astype(BF16)

    rk = w_kv_b.shape[0]
    wkv = w_kv_b.reshape(rk, MLA_HEADS, MLA_NOPE + MLA_V)
    wkn = jnp.concatenate([wkv[..., :MLA_NOPE], jnp.zeros((rk, MLA_HEADS, HEAD_PAD - MLA_NOPE), F32)],
                          axis=-1).reshape(rk, MLA_HEADS * HEAD_PAD).astype(BF16)
    wv_t = wkv[..., MLA_NOPE:].reshape(rk, MLA_HEADS * MLA_V).T.astype(BF16)
    return w_nat, w_t, wq_t, wkn, wv_t


def _rope_tables(s):
    inv_freq = ROPE_BASE ** (-jnp.arange(0, MLA_ROPE, 2, dtype=F32) / MLA_ROPE)
    ang = jnp.arange(s).astype(F32)[:, None] * inv_freq[None, :]
    cos, sin = jnp.cos(ang), jnp.sin(ang)
    z_lo = jnp.zeros((s, MLA_NOPE), F32)
    z_hi = jnp.zeros((s, HEAD_PAD - MLA_NOPE - MLA_ROPE), F32)
    cos_k = jnp.concatenate([z_lo, cos, cos, z_hi], axis=1)
    sin_k = jnp.concatenate([z_lo, sin, sin, z_hi], axis=1)
    return cos_k, sin_k, cos.T, sin.T


def _in_proj(h, bsz, s, w_in, q_norm_g, kv_norm_g, w_q_b, w_kv_b):
    n, d = h.shape
    tm, tk = ROW_TILE, ATTN_TK
    assert s % tm == 0 and CHAIN_ROWS % tk == 0
    w_nat, w_t, wq_t, wkn, wv_t = _in_proj_weights(w_in, w_q_b, w_kv_b)
    cos_k, sin_k, cos_q, sin_q = _rope_tables(s)
    half = MLA_ROPE // 2
    spt = s // tm
    nqm = MLA_HEADS * HEAD_PAD
    nvm = MLA_HEADS * MLA_V
    nd = 2 * DIFF_HEADS * DIFF_D

    gq = q_norm_g.reshape(1, -1)
    gkv = kv_norm_g.reshape(1, -1)
    out_shape = (
        jax.ShapeDtypeStruct((bsz, nqm, s), BF16),
        jax.ShapeDtypeStruct((bsz, s, nqm), BF16),
        jax.ShapeDtypeStruct((bsz, s // tk, nvm, tk), BF16),
        jax.ShapeDtypeStruct((bsz, nd, s), BF16),
        jax.ShapeDtypeStruct((bsz, DIFF_HEADS, s, nd // DIFF_HEADS), BF16),
        jax.ShapeDtypeStruct((bsz, s // tk, nd, tk), BF16),
    )
    out_specs = (
        pl.BlockSpec((None, nqm, tm), lambda b, i: (b, 0, i)),
        pl.BlockSpec((None, tm, nqm), lambda b, i: (b, i, 0)),
        pl.BlockSpec((None, tm // tk, nvm, tk), lambda b, i: (b, i, 0, 0)),
        pl.BlockSpec((None, nd, tm), lambda b, i: (b, 0, i)),
        pl.BlockSpec((None, DIFF_HEADS, tm, nd // DIFF_HEADS), lambda b, i: (b, 0, i, 0)),
        pl.BlockSpec((None, tm // tk, nd, tk), lambda b, i: (b, i, 0, 0)),
    )
    return pl.pallas_call(
        _in_proj_kernel,
        grid=(bsz, spt),
        in_specs=[
            pl.BlockSpec((tm, d), lambda b, i: (b * spt + i, 0)),
            _resident(w_nat), _resident(w_t), _resident(wq_t), _resident(wkn), _resident(wv_t),
            _resident(gq), _resident(gkv),
            pl.BlockSpec((tm, HEAD_PAD), lambda b, i: (i, 0)),
            pl.BlockSpec((tm, HEAD_PAD), lambda b, i: (i, 0)),
            pl.BlockSpec((half, tm), lambda b, i: (0, i)),
            pl.BlockSpec((half, tm), lambda b, i: (0, i)),
        ],
        out_specs=out_specs,
        out_shape=out_shape,
        compiler_params=_params(("arbitrary", "arbitrary")),
        name="in_proj",
    )(h, w_nat, w_t, wq_t, wkn, wv_t, gq, gkv, cos_k, sin_k, cos_q, sin_q)


BIAS_TILES = 3


def _rel_bias_kernel(rb_ref, o_ref, *, tq, tk):
    c = pl.program_id(0) - 1
    kk = lax.broadcasted_iota(jnp.int32, (tk, tq), 0)
    qq = lax.broadcasted_iota(jnp.int32, (tk, tq), 1)
    n = jnp.maximum(qq - kk - c * tk, 0)
    max_exact = REL_BUCKETS // 2
    nf = jnp.maximum(n, 1).astype(F32)
    large = max_exact + (jnp.log(nf / max_exact) / math.log(REL_MAX_DIST / max_exact)
                         * (REL_BUCKETS - max_exact)).astype(jnp.int32)
    large = jnp.minimum(large, REL_BUCKETS - 1)
    bucket = jnp.where(n < max_exact, n, large)
    for m in range(o_ref.shape[0]):
        acc = jnp.full((tk, tq), rb_ref[0, m], F32)
        for bkt in range(1, REL_BUCKETS):
            acc = jnp.where(bucket >= bkt, rb_ref[bkt, m], acc)
        o_ref[m] = (acc - rb_ref[REL_BUCKETS - 1, m]) * LOG2E


def _rel_bias_tiles(rel_bias, tq, tk):
    assert tq == 2 * tk and tk + 1 >= REL_MAX_DIST
    nmaps = rel_bias.shape[1]
    return pl.pallas_call(
        functools.partial(_rel_bias_kernel, tq=tq, tk=tk),
        grid=(BIAS_TILES,),
        in_specs=[pl.BlockSpec(memory_space=pltpu.SMEM)],
        out_specs=pl.BlockSpec((nmaps, None, tk, tq), lambda t: (0, t, 0, 0)),
        out_shape=jax.ShapeDtypeStruct((nmaps, BIAS_TILES, tk, tq), F32),
        compiler_params=_params(("arbitrary",)),
        name="rel_bias",
    )(rel_bias)


SCORES_AHEAD = 3
VALUES_BEHIND = 1
ONES_ROWS = 16


def _flash_pipeline(q_fn, k_fn, vt_fn, logits_fn, finalize_fn, s_ref, mt_ref, p_ref, al_ref, m_ref, acc_ref,
                    *, nstreams, nq, tq, tk, order):
    tiles = [(g, i, t, t - 2 * i) for g in range(nstreams) for i in range(nq) for t in range(2 * i + 2)]
    ones = jnp.ones((ONES_ROWS, tk), BF16)
    ahead = s_ref.shape[0] - 1
    behind = p_ref.shape[0] - 1

    def lane0(c):
        return tk if c == 1 else 0

    def scores(n):
        g, i, t, c = tiles[n]
        slot, lo = n % (ahead + 1), lane0(c)
        s = logits_fn(g, _dot(k_fn(g, t), q_fn(g, i, lo)), c, lo)
        s_ref[slot, :, lo:] = s
        mt_ref[slot, :, lo:] = jnp.max(s, axis=0, keepdims=True)

    def softmax(n):
        g, i, t, c = tiles[n]
        sslot, slot, lo = n % (ahead + 1), n % (behind + 1), lane0(c)
        m_new = mt_ref[sslot, :, lo:]
        if t > 0:
            m_prev = m_ref[:, lo:]
            m_new = jnp.maximum(m_prev, m_new)
            al_ref[slot, :, lo:] = jnp.exp2(m_prev - m_new)
        m_ref[:, lo:] = m_new
        p_ref[slot, :, lo:] = jnp.exp2(s_ref[sslot, :, lo:] - m_new).astype(BF16)

    def values(n):
        g, i, t, c = tiles[n]
        slot, lo = n % (behind + 1), lane0(c)
        pv = _dot(jnp.concatenate([vt_fn(g, t), ones], axis=0), p_ref[slot, :, lo:])
        if t > 0:
            pv = al_ref[slot, :, lo:] * acc_ref[:, lo:] + pv
        acc_ref[:, lo:] = pv
        if c == 1:
            acc = acc_ref[...]
            dv = acc.shape[0] - ONES_ROWS
            finalize_fn(g, i, acc[:dv] / acc[dv:dv + 1])

    for n in range(ahead):
        scores(n)
    assert sorted(order) == ["c", "s", "v"] and (behind > 0 or order.index("s") < order.index("v"))
    for n in range(len(tiles) + behind):
        for stage in order:
            if stage == "v" and behind <= n:
                values(n - behind)
            if stage == "c" and n + ahead < len(tiles):
                scores(n + ahead)
            if stage == "s" and n < len(tiles):
                softmax(n)


def _tri_mask(s):
    kk = lax.broadcasted_iota(jnp.int32, s.shape, 0)
    qq = lax.broadcasted_iota(jnp.int32, s.shape, 1)
    return jnp.where(kk <= qq, s, -jnp.inf)


def _flash_scratch(dv, tq, tk):
    return [pltpu.VMEM((SCORES_AHEAD + 1, tk, tq), F32), pltpu.VMEM((SCORES_AHEAD + 1, 1, tq), F32),
            pltpu.VMEM((VALUES_BEHIND + 1, tk, tq), BF16), pltpu.VMEM((VALUES_BEHIND + 1, 1, tq), F32),
            pltpu.VMEM((1, tq), F32), pltpu.VMEM((dv + ONES_ROWS, tq), F32)]


def _mla_attn_kernel(qt_ref, k_ref, vt_ref, o_ref, *scratch, heads, tq, tk):

    def q_fn(g, i, lo):
        return qt_ref[g * HEAD_PAD:(g + 1) * HEAD_PAD, i * tq + lo:(i + 1) * tq]

    def k_fn(g, t):
        return k_ref[t * tk:(t + 1) * tk, g * HEAD_PAD:(g + 1) * HEAD_PAD]

    def vt_fn(g, t):
        return vt_ref[t, g * MLA_V:(g + 1) * MLA_V, :]

    def logits(g, s, c, lo):
        return _tri_mask(s) if c >= 0 else s

    def finalize(g, i, o):
        o_ref[g * MLA_V:(g + 1) * MLA_V, i * tq:(i + 1) * tq] = o.astype(BF16)

    _flash_pipeline(q_fn, k_fn, vt_fn, logits, finalize, *scratch,
                    nstreams=heads, nq=qt_ref.shape[1] // tq, tq=tq, tk=tk, order=MLA_STAGE_ORDER)


def _mla_attn(qt, k, vt, tq, tk):
    bsz, nq, s = qt.shape
    nkt = s // tk
    hps = MLA_HEADS_PER_STEP
    assert tq == 2 * tk and s % tq == 0 and MLA_HEADS % hps == 0
    return pl.pallas_call(
        functools.partial(_mla_attn_kernel, heads=hps, tq=tq, tk=tk),
        grid=(MLA_HEADS // hps, bsz),
        in_specs=[
            pl.BlockSpec((None, hps * HEAD_PAD, s), lambda h, b: (b, h, 0)),
            pl.BlockSpec((None, s, hps * HEAD_PAD), lambda h, b: (b, 0, h)),
            pl.BlockSpec((None, nkt, hps * MLA_V, tk), lambda h, b: (b, 0, h, 0)),
        ],
        out_specs=pl.BlockSpec((None, hps * MLA_V, s), lambda h, b: (b, h, 0)),
        out_shape=jax.ShapeDtypeStruct((bsz, MLA_HEADS * MLA_V, s), BF16),
        scratch_shapes=_flash_scratch(MLA_V, tq, tk),
        compiler_params=_params(("arbitrary", "arbitrary")),
        name="mla_attn",
    )(qt, k, vt)


def _diff_attn_kernel(qt_ref, k_ref, vt_ref, bias_ref, lamv_ref, gn_ref, o_ref, o0_ref, qm_ref, *scratch, tq, tk):
    qt = qt_ref[...]
    row = lax.broadcasted_iota(jnp.int32, qt.shape, 0)
    for g in range(2):
        keep = jnp.logical_and(row >= g * DIFF_D, row < (g + 1) * DIFF_D)
        qm_ref[g] = jnp.where(keep, qt, jnp.zeros_like(qt))

    lv = lamv_ref[...]
    lam = (jnp.exp(jnp.sum(lv[0:1] * lv[1:2], axis=1, keepdims=True))
           - jnp.exp(jnp.sum(lv[2:3] * lv[3:4], axis=1, keepdims=True)) + LAM_INIT)

    def q_fn(g, i, lo):
        return qm_ref[g, :, i * tq + lo:(i + 1) * tq]

    def k_fn(g, t):
        return k_ref[t * tk:(t + 1) * tk, :]

    def vt_fn(g, t):
        return vt_ref[t]

    def logits(g, s, c, lo):
        if c >= -1:
            s = s + bias_ref[g, c + 1, :, lo:]
        return _tri_mask(s) if c >= 0 else s

    def finalize(g, i, o):
        cols = slice(i * tq, (i + 1) * tq)
        if g == 0:
            o0_ref[:, cols] = o
        else:
            d = o0_ref[:, cols] - lam * o
            d = d * lax.rsqrt(jnp.mean(d * d, axis=0, keepdims=True) + RMS_EPS) * gn_ref[...] * (1.0 - LAM_INIT)
            o_ref[:, cols] = d.astype(BF16)

    _flash_pipeline(q_fn, k_fn, vt_fn, logits, finalize, *scratch,
                    nstreams=2, nq=qt_ref.shape[1] // tq, tq=tq, tk=tk, order=DIFF_STAGE_ORDER)


def _diff_attn(qt, k, vt, bias, lamv, gn, tq, tk):
    bsz, nd, s = qt.shape
    nkt = s // tk
    dv = 2 * DIFF_D
    assert tq == 2 * tk and s % tq == 0
    return pl.pallas_call(
        functools.partial(_diff_attn_kernel, tq=tq, tk=tk),
        grid=(DIFF_HEADS, bsz),
        in_specs=[
            pl.BlockSpec((None, dv, s), lambda h, b: (b, h, 0)),
            pl.BlockSpec((None, None, s, dv), lambda h, b: (b, h, 0, 0)),
            pl.BlockSpec((None, nkt, dv, tk), lambda h, b: (b, 0, h, 0)),
            pl.BlockSpec((2, BIAS_TILES, tk, tq), lambda h, b: (h, 0, 0, 0)),
            pl.BlockSpec(lamv.shape, lambda h, b: (0, 0)),
            pl.BlockSpec(gn.shape, lambda h, b: (0, 0)),
        ],
        out_specs=pl.BlockSpec((None, dv, s), lambda h, b: (b, h, 0)),
        out_shape=jax.ShapeDtypeStruct((bsz, nd, s), BF16),
        scratch_shapes=[pltpu.VMEM((dv, s), F32), pltpu.VMEM((2, dv, s), BF16)] + _flash_scratch(dv, tq, tk),
        compiler_params=_params(("arbitrary", "arbitrary")),
        name="diff_attn",
    )(qt, k, vt, bias, lamv, gn)


def _post_mixer_kernel(h_ref, omt_ref, odt_ref, wum_ref, wud_ref, wg_ref, bg_ref, wo_ref, g_ref, b_ref,
                       o_ref):
    d = h_ref.shape[1]
    for rows in _chains(h_ref.shape[0]):
        h = h_ref[rows, :]
        z = _dot(h.astype(BF16), wg_ref[...]) + bg_ref[...]
        gate = 1.0 / (1.0 + jnp.exp(-z))
        y_m = _dot_tn(omt_ref[:, rows], wum_ref[...])
        y_d = _dot_tn(odt_ref[:, rows], wud_ref[...])
        mixed = gate[:, :d] * y_m + gate[:, d:] * y_d
        mix = _dot(mixed.astype(BF16), wo_ref[...])
        o_ref[rows, :] = _layer_norm(ALPHA * h + mix, g_ref[...], b_ref[...])


def _post_mixer(h, omt, odt, bsz, s, w_up_mla, w_up_diff, w_gate, b_gate, w_o, g, b):
    n, d = h.shape
    tm = ROW_TILE
    spt = s // tm
    ws = [w_up_mla.astype(BF16), w_up_diff.astype(BF16), w_gate.astype(BF16), b_gate.reshape(1, -1),
          w_o.astype(BF16), g.reshape(1, d), b.reshape(1, d)]

    return pl.pallas_call(
        _post_mixer_kernel,
        grid=(bsz, spt),
        in_specs=[
            pl.BlockSpec((tm, d), lambda bb, i: (bb * spt + i, 0)),
            pl.BlockSpec((None, omt.shape[1], tm), lambda bb, i: (bb, 0, i)),
            pl.BlockSpec((None, odt.shape[1], tm), lambda bb, i: (bb, 0, i)),
        ] + [_resident(a) for a in ws],
        out_specs=pl.BlockSpec((tm, d), lambda bb, i: (bb * spt + i, 0)),
        out_shape=jax.ShapeDtypeStruct((n, d), F32),
        compiler_params=_params(("arbitrary", "arbitrary")),
        name="post_mixer",
    )(h, omt, odt, *ws)


def _mem_kv_kernel(mem_ref, w_ref, k_ref, v_ref):
    d = k_ref.shape[1]
    kv = _dot(mem_ref[...].astype(BF16), w_ref[...])
    k_ref[...] = kv[:, :d].astype(BF16)
    v_ref[...] = kv[:, d:].astype(BF16)


def _mem_kv(mem, w_kv):
    bsz, m, d = mem.shape
    w = w_kv.astype(BF16)
    spec = pl.BlockSpec((None, m, d), lambda b: (b, 0, 0))
    return pl.pallas_call(
        _mem_kv_kernel,
        grid=(bsz,),
        in_specs=[spec, pl.BlockSpec(w.shape, lambda b: (0, 0))],
        out_specs=(spec, spec),
        out_shape=(jax.ShapeDtypeStruct((bsz, m, d), BF16), jax.ShapeDtypeStruct((bsz, m, d), BF16)),
        compiler_params=_params(("arbitrary",)),
        name="mem_kv",
    )(mem, w)


def _mem_attn_kernel(h_ref, k_ref, v_ref, wq_ref, wo_ref, g_ref, b_ref, o_ref):
    d = h_ref.shape[1]
    hd = d // MEM_HEADS
    for rows in _chains(h_ref.shape[0], MEM_CHAIN_ROWS):
        h = h_ref[rows, :]
        q = _dot(h.astype(BF16), wq_ref[...]).astype(BF16)
        heads = [slice(hh * hd, (hh + 1) * hd) for hh in range(MEM_HEADS)]
        logits = [_dot_nt(q[:, cols], k_ref[:, cols]) * (hd ** -0.5) for cols in heads]
        probs = []
        for s in logits:
            p = jnp.exp(s - jnp.max(s, axis=-1, keepdims=True))
            probs.append((p / jnp.sum(p, axis=-1, keepdims=True)).astype(BF16))
        outs = [_dot(p, v_ref[:, cols]).astype(BF16) for cols, p in zip(heads, probs)]
        att = _dot(jnp.concatenate(outs, axis=1), wo_ref[...])
        o_ref[rows, :] = _layer_norm(ALPHA * h + att, g_ref[...], b_ref[...])


def _mem_attn(h, km, vm, bsz, s, w_q, w_o, g, b):
    n, d = h.shape
    tm = WIDE_ROW_TILE
    spt = s // tm
    m = km.shape[1]
    ws = [w_q.astype(BF16), w_o.astype(BF16), g.reshape(1, d), b.reshape(1, d)]

    return pl.pallas_call(
        _mem_attn_kernel,
        grid=(bsz, spt),
        in_specs=[
            pl.BlockSpec((tm, d), lambda bb, i: (bb * spt + i, 0)),
            pl.BlockSpec((None, m, d), lambda bb, i: (bb, 0, 0)),
            pl.BlockSpec((None, m, d), lambda bb, i: (bb, 0, 0)),
        ] + [_resident(a) for a in ws],
        out_specs=pl.BlockSpec((tm, d), lambda bb, i: (bb * spt + i, 0)),
        out_shape=jax.ShapeDtypeStruct((n, d), F32),
        compiler_params=_params(("arbitrary", "arbitrary")),
        name="mem_attn",
    )(h, km, vm, *ws)


def kernel(x, mem, rel_bias, ffn1_w1, ffn1_w3, ffn1_w2, ln1_g, ln1_b, w_in, q_norm_g, kv_norm_g, w_q_b, w_kv_b, lam_q1, lam_k1, lam_q2, lam_k2, diff_norm_g, w_gate, b_gate, w_up_mla, w_up_diff, w_o, ln2_g, ln2_b, mem_w_q, mem_w_kv, mem_w_o, ln3_g, ln3_b, ffn2_w1, ffn2_w3, ffn2_w2, ln4_g, ln4_b):
    bsz, s, d = x.shape
    assert ffn1_w1.shape[0] == DEPTH
    l = 0
    h = x.reshape(bsz * s, d)

    h = _ffn_ln(h, ffn1_w1[l], ffn1_w3[l], ffn1_w2[l], ln1_g[l], ln1_b[l])

    qtm, km, vtm, qtd, kd, vtd = _in_proj(h, bsz, s, w_in[l], q_norm_g[l], kv_norm_g[l], w_q_b[l], w_kv_b[l])
    bias = _rel_bias_tiles(rel_bias, ATTN_TQ, ATTN_TK)
    omt = _mla_attn(qtm, km, vtm, ATTN_TQ, ATTN_TK)
    lamv = jnp.stack([lam_q1[l], lam_k1[l], lam_q2[l], lam_k2[l]], axis=0)
    odt = _diff_attn(qtd, kd, vtd, bias, lamv, diff_norm_g[l].reshape(-1, 1), ATTN_TQ, ATTN_TK)
    h = _post_mixer(h, omt, odt, bsz, s, w_up_mla[l], w_up_diff[l], w_gate[l], b_gate[l], w_o[l],
                    ln2_g[l], ln2_b[l])

    kmem, vmem = _mem_kv(mem, mem_w_kv[l])
    h = _mem_attn(h, kmem, vmem, bsz, s, mem_w_q[l], mem_w_o[l], ln3_g[l], ln3_b[l])

    h = _ffn_ln(h, ffn2_w1[l], ffn2_w3[l], ffn2_w2[l], ln4_g[l], ln4_b[l])
    return h.reshape(bsz, s, d)
```

```python
import functools
import math

import jax
import jax.numpy as jnp
from jax import lax
from jax.experimental import pallas as pl
from jax.experimental.pallas import tpu as pltpu

F32 = jnp.float32
BF16 = jnp.bfloat16

DEPTH = 1
MLA_HEADS = 8
MLA_Q_RANK = 384
MLA_KV_RANK = 256
MLA_NOPE = 64
MLA_ROPE = 32
MLA_V = 64
ROPE_BASE = 10000.0
DIFF_HEADS = 4
DIFF_D = 64
REL_BUCKETS = 32
REL_MAX_DIST = 128
MEM_HEADS = 4
ALPHA = (2.0 * DEPTH) ** 0.25
LN_EPS = 1e-5
RMS_EPS = 1e-6
LAM_INIT = 0.8 - 0.6 * math.exp(-0.3 * 0)
LOG2E = math.log2(math.e)

HEAD_PAD = 128
VMEM_LIMIT_BYTES = 56 * 1024 * 1024

ROW_TILE = 1024
WIDE_ROW_TILE = 2048
CHAIN_ROWS = 256
MEM_CHAIN_ROWS = 512
ATTN_TQ = 512
ATTN_TK = 256
MLA_HEADS_PER_STEP = 4
MLA_STAGE_ORDER = "vcs"
DIFF_STAGE_ORDER = "csv"


def _params(semantics):
    return pltpu.CompilerParams(dimension_semantics=semantics, vmem_limit_bytes=VMEM_LIMIT_BYTES)


def _dot(a, b):
    return jnp.dot(a, b, preferred_element_type=F32)


def _dot_nt(a, b):
    return lax.dot_general(a, b, (((1,), (1,)), ((), ())), preferred_element_type=F32)


def _dot_tn(a, b):
    return lax.dot_general(a, b, (((0,), (0,)), ((), ())), preferred_element_type=F32)


def _layer_norm(y, g, b):
    mu = jnp.mean(y, axis=-1, keepdims=True)
    d = y - mu
    var = jnp.mean(d * d, axis=-1, keepdims=True)
    return d * lax.rsqrt(var + LN_EPS) * g + b


def _rms_norm_rows(x, g):
    return x * lax.rsqrt(jnp.mean(x * x, axis=-1, keepdims=True) + RMS_EPS) * g


def _chains(rows, chain=CHAIN_ROWS):
    return [slice(r, r + chain) for r in range(0, rows, chain)]


def _resident(a):
    return pl.BlockSpec(a.shape, lambda *_: (0,) * a.ndim, pipeline_mode=pl.Buffered(1))


def _ffn_ln_kernel(x_ref, w1_ref, w3_ref, w2_ref, g_ref, b_ref, o_ref):
    def up(rows):
        x = x_ref[rows, :]
        xb = x.astype(BF16)
        return x, _dot(xb, w1_ref[...]), _dot(xb, w3_ref[...])

    chains = _chains(x_ref.shape[0])
    nxt = up(chains[0])
    for k, rows in enumerate(chains):
        x, u, v = nxt
        if k + 1 < len(chains):
            nxt = up(chains[k + 1])
        act = (u / (1.0 + jnp.exp(-u))) * v
        y = ALPHA * x + 0.5 * _dot(act.astype(BF16), w2_ref[...])
        o_ref[rows, :] = _layer_norm(y, g_ref[...], b_ref[...])


def _ffn_ln(x, w1, w3, w2, g, b):
    n, d = x.shape
    tm = ROW_TILE
    assert n % tm == 0
    ws = [w1.astype(BF16), w3.astype(BF16), w2.astype(BF16), g.reshape(1, d), b.reshape(1, d)]
    return pl.pallas_call(
        _ffn_ln_kernel,
        grid=(n // tm,),
        in_specs=[pl.BlockSpec((tm, d), lambda i: (i, 0))] + [_resident(a) for a in ws],
        out_specs=pl.BlockSpec((tm, d), lambda i: (i, 0)),
        out_shape=jax.ShapeDtypeStruct((n, d), F32),
        compiler_params=_params(("arbitrary",)),
        name="ffn_ln",
    )(x, *ws)


def _in_proj_kernel(h_ref, wnat_ref, wt_ref, wq_ref, wkn_ref, wvt_ref, gq_ref, gkv_ref,
                    cosk_ref, sink_ref, cosq_ref, sinq_ref,
                    qtm_ref, km_ref, vtm_ref, qtd_ref, kd_ref, vtd_ref):
    tk = vtm_ref.shape[-1]
    half = MLA_ROPE // 2
    nqd = qtd_ref.shape[0]
    o1 = MLA_Q_RANK
    o2 = o1 + MLA_KV_RANK
    o3 = o2 + HEAD_PAD
    q_scale = (MLA_NOPE + MLA_ROPE) ** -0.5 * LOG2E
    for rows in _chains(h_ref.shape[0]):
        key_tiles = range(rows.start // tk, rows.stop // tk)
        hb = h_ref[rows, :].astype(BF16)

        pt = _dot_nt(wt_ref[...], hb)
        qtd_ref[:, rows] = (pt[:nqd] * (DIFF_D ** -0.5 * LOG2E)).astype(BF16)
        for c, t in enumerate(key_tiles):
            vtd_ref[t] = pt[nqd:, c * tk:(c + 1) * tk].astype(BF16)

        pn = _dot(hb, wnat_ref[...])
        c_q, c_kv, krg, kd = pn[:, :o1], pn[:, o1:o2], pn[:, o2:o3], pn[:, o3:]

        cqn = _rms_norm_rows(c_q, gq_ref[...]).astype(BF16)
        ckvn = _rms_norm_rows(c_kv, gkv_ref[...]).astype(BF16)
        qt = _dot_nt(wq_ref[...], cqn)
        kn = _dot(ckvn, wkn_ref[...])
        vt = _dot_nt(wvt_ref[...], ckvn)

        for hh in range(DIFF_HEADS):
            kd_ref[hh, rows, :] = kd[:, hh * 2 * DIFF_D:(hh + 1) * 2 * DIFF_D].astype(BF16)

        cos_q, sin_q = cosq_ref[:, rows], sinq_ref[:, rows]
        for hh in range(MLA_HEADS):
            lo = hh * HEAD_PAD
            r0 = lo + MLA_NOPE
            r1, r2 = r0 + half, r0 + MLA_ROPE
            x1, x2 = qt[r0:r1], qt[r1:r2]
            qtm_ref[lo:r0, rows] = (qt[lo:r0] * q_scale).astype(BF16)
            qtm_ref[r0:r1, rows] = ((x1 * cos_q - x2 * sin_q) * q_scale).astype(BF16)
            qtm_ref[r1:r2, rows] = ((x2 * cos_q + x1 * sin_q) * q_scale).astype(BF16)
            qtm_ref[r2:lo + HEAD_PAD, rows] = jnp.zeros((lo + HEAD_PAD - r2, CHAIN_ROWS), BF16)

        krh = jnp.concatenate([krg[:, MLA_ROPE:], krg[:, :MLA_ROPE]], axis=1)
        kr = krg * cosk_ref[rows, :] + krh * sink_ref[rows, :]
        for hh in range(MLA_HEADS):
            lo, hi = hh * HEAD_PAD, (hh + 1) * HEAD_PAD
            km_ref[rows, lo:hi] = (kn[:, lo:hi] + kr).astype(BF16)
        for c, t in enumerate(key_tiles):
            vtm_ref[t] = vt[:, c * tk:(c + 1) * tk].astype(BF16)


def _in_proj_weights(w_in, w_q_b, w_kv_b):
    d = w_in.shape[0]
    o_cq = MLA_Q_RANK
    o_ckv = o_cq + MLA_KV_RANK
    o_kr = o_ckv + MLA_ROPE
    n_d = 2 * DIFF_HEADS * DIFF_D
    o_qd = o_kr + n_d
    o_kd = o_qd + n_d
    half = MLA_ROPE // 2
    pad = HEAD_PAD - MLA_NOPE - MLA_ROPE
    assert pad == MLA_ROPE
    kr = w_in[:, o_ckv:o_kr]
    krg = jnp.concatenate([jnp.zeros((d, MLA_NOPE), F32), kr, -kr[:, half:], kr[:, :half]], axis=1)
    w_nat = jnp.concatenate([w_in[:, :o_ckv], krg, w_in[:, o_qd:o_kd]], axis=1).astype(BF16)
    w_t = jnp.concatenate([w_in[:, o_kr:o_qd], w_in[:, o_kd:]], axis=1).T.astype(BF16)

    r = w_q_b.shape[0]
    wq = w_q_b.reshape(r, MLA_HEADS, MLA_NOPE + MLA_ROPE)
    wq = jnp.concatenate([wq, jnp.zeros((r, MLA_HEADS, pad), F32)], axis=-1)
    wq_t = wq.reshape(r, MLA_HEADS * HEAD_PAD).T.astype(BF16)

    rk = w_kv_b.shape[0]
    wkv = w_kv_b.reshape(rk, MLA_HEADS, MLA_NOPE + MLA_V)
    wkn = jnp.concatenate([wkv[..., :MLA_NOPE], jnp.zeros((rk, MLA_HEADS, HEAD_PAD - MLA_NOPE), F32)],
                          axis=-1).reshape(rk, MLA_HEADS * HEAD_PAD).astype(BF16)
    wv_t = wkv[..., MLA_NOPE:].reshape(rk, MLA_HEADS * MLA_V).T.astype(BF16)
    return w_nat, w_t, wq_t, wkn, wv_t


def _rope_tables(s):
    inv_freq = ROPE_BASE ** (-jnp.arange(0, MLA_ROPE, 2, dtype=F32) / MLA_ROPE)
    ang = jnp.arange(s).astype(F32)[:, None] * inv_freq[None, :]
    cos, sin = jnp.cos(ang), jnp.sin(ang)
    z_lo = jnp.zeros((s, MLA_NOPE), F32)
    z_hi = jnp.zeros((s, HEAD_PAD - MLA_NOPE - MLA_ROPE), F32)
    cos_k = jnp.concatenate([z_lo, cos, cos, z_hi], axis=1)
    sin_k = jnp.concatenate([z_lo, sin, sin, z_hi], axis=1)
    return cos_k, sin_k, cos.T, sin.T


def _in_proj(h, bsz, s, w_in, q_norm_g, kv_norm_g, w_q_b, w_kv_b):
    n, d = h.shape
    tm, tk = ROW_TILE, ATTN_TK
    assert s % tm == 0 and CHAIN_ROWS % tk == 0
    w_nat, w_t, wq_t, wkn, wv_t = _in_proj_weights(w_in, w_q_b, w_kv_b)
    cos_k, sin_k, cos_q, sin_q = _rope_tables(s)
    half = MLA_ROPE // 2
    spt = s // tm
    nqm = MLA_HEADS * HEAD_PAD
    nvm = MLA_HEADS * MLA_V
    nd = 2 * DIFF_HEADS * DIFF_D

    gq = q_norm_g.reshape(1, -1)
    gkv = kv_norm_g.reshape(1, -1)
    out_shape = (
        jax.ShapeDtypeStruct((bsz, nqm, s), BF16),
        jax.ShapeDtypeStruct((bsz, s, nqm), BF16),
        jax.ShapeDtypeStruct((bsz, s // tk, nvm, tk), BF16),
        jax.ShapeDtypeStruct((bsz, nd, s), BF16),
        jax.ShapeDtypeStruct((bsz, DIFF_HEADS, s, nd // DIFF_HEADS), BF16),
        jax.ShapeDtypeStruct((bsz, s // tk, nd, tk), BF16),
    )
    out_specs = (
        pl.BlockSpec((None, nqm, tm), lambda b, i: (b, 0, i)),
        pl.BlockSpec((None, tm, nqm), lambda b, i: (b, i, 0)),
        pl.BlockSpec((None, tm // tk, nvm, tk), lambda b, i: (b, i, 0, 0)),
        pl.BlockSpec((None, nd, tm), lambda b, i: (b, 0, i)),
        pl.BlockSpec((None, DIFF_HEADS, tm, nd // DIFF_HEADS), lambda b, i: (b, 0, i, 0)),
        pl.BlockSpec((None, tm // tk, nd, tk), lambda b, i: (b, i, 0, 0)),
    )
    return pl.pallas_call(
        _in_proj_kernel,
        grid=(bsz, spt),
        in_specs=[
            pl.BlockSpec((tm, d), lambda b, i: (b * spt + i, 0)),
            _resident(w_nat), _resident(w_t), _resident(wq_t), _resident(wkn), _resident(wv_t),
            _resident(gq), _resident(gkv),
            pl.BlockSpec((tm, HEAD_PAD), lambda b, i: (i, 0)),
            pl.BlockSpec((tm, HEAD_PAD), lambda b, i: (i, 0)),
            pl.BlockSpec((half, tm), lambda b, i: (0, i)),
            pl.BlockSpec((half, tm), lambda b, i: (0, i)),
        ],
        out_specs=out_specs,
        out_shape=out_shape,
        compiler_params=_params(("arbitrary", "arbitrary")),
        name="in_proj",
    )(h, w_nat, w_t, wq_t, wkn, wv_t, gq, gkv, cos_k, sin_k, cos_q, sin_q)


BIAS_TILES = 3


def _rel_bias_kernel(rb_ref, o_ref, *, tq, tk):
    c = pl.program_id(0) - 1
    kk = lax.broadcasted_iota(jnp.int32, (tk, tq), 0)
    qq = lax.broadcasted_iota(jnp.int32, (tk, tq), 1)
    n = jnp.maximum(qq - kk - c * tk, 0)
    max_exact = REL_BUCKETS // 2
    nf = jnp.maximum(n, 1).astype(F32)
    large = max_exact + (jnp.log(nf / max_exact) / math.log(REL_MAX_DIST / max_exact)
                         * (REL_BUCKETS - max_exact)).astype(jnp.int32)
    large = jnp.minimum(large, REL_BUCKETS - 1)
    bucket = jnp.where(n < max_exact, n, large)
    for m in range(o_ref.shape[0]):
        acc = jnp.full((tk, tq), rb_ref[0, m], F32)
        for bkt in range(1, REL_BUCKETS):
            acc = jnp.where(bucket >= bkt, rb_ref[bkt, m], acc)
        o_ref[m] = (acc - rb_ref[REL_BUCKETS - 1, m]) * LOG2E


def _rel_bias_tiles(rel_bias, tq, tk):
    assert tq == 2 * tk and tk + 1 >= REL_MAX_DIST
    nmaps = rel_bias.shape[1]
    return pl.pallas_call(
        functools.partial(_rel_bias_kernel, tq=tq, tk=tk),
        grid=(BIAS_TILES,),
        in_specs=[pl.BlockSpec(memory_space=pltpu.SMEM)],
        out_specs=pl.BlockSpec((nmaps, None, tk, tq), lambda t: (0, t, 0, 0)),
        out_shape=jax.ShapeDtypeStruct((nmaps, BIAS_TILES, tk, tq), F32),
        compiler_params=_params(("arbitrary",)),
        name="rel_bias",
    )(rel_bias)


SCORES_AHEAD = 3
VALUES_BEHIND = 1
ONES_ROWS = 16


def _flash_pipeline(q_fn, k_fn, vt_fn, logits_fn, finalize_fn, s_ref, mt_ref, p_ref, al_ref, m_ref, acc_ref,
                    *, nstreams, nq, tq, tk, order):
    tiles = [(g, i, t, t - 2 * i) for g in range(nstreams) for i in range(nq) for t in range(2 * i + 2)]
    ones = jnp.ones((ONES_ROWS, tk), BF16)
    ahead = s_ref.shape[0] - 1
    behind = p_ref.shape[0] - 1

    def lane0(c):
        return tk if c == 1 else 0

    def scores(n):
        g, i, t, c = tiles[n]
        slot, lo = n % (ahead + 1), lane0(c)
        s = logits_fn(g, _dot(k_fn(g, t), q_fn(g, i, lo)), c, lo)
        s_ref[slot, :, lo:] = s
        mt_ref[slot, :, lo:] = jnp.max(s, axis=0, keepdims=True)

    def softmax(n):
        g, i, t, c = tiles[n]
        sslot, slot, lo = n % (ahead + 1), n % (behind + 1), lane0(c)
        m_new = mt_ref[sslot, :, lo:]
        if t > 0:
            m_prev = m_ref[:, lo:]
            m_new = jnp.maximum(m_prev, m_new)
            al_ref[slot, :, lo:] = jnp.exp2(m_prev - m_new)
        m_ref[:, lo:] = m_new
        p_ref[slot, :, lo:] = jnp.exp2(s_ref[sslot, :, lo:] - m_new).astype(BF16)

    def values(n):
        g, i, t, c = tiles[n]
        slot, lo = n % (behind + 1), lane0(c)
        pv = _dot(jnp.concatenate([vt_fn(g, t), ones], axis=0), p_ref[slot, :, lo:])
        if t > 0:
            pv = al_ref[slot, :, lo:] * acc_ref[:, lo:] + pv
        acc_ref[:, lo:] = pv
        if c == 1:
            acc = acc_ref[...]
            dv = acc.shape[0] - ONES_ROWS
            finalize_fn(g, i, acc[:dv] / acc[dv:dv + 1])

    for n in range(ahead):
        scores(n)
    assert sorted(order) == ["c", "s", "v"] and (behind > 0 or order.index("s") < order.index("v"))
    for n in range(len(tiles) + behind):
        for stage in order:
            if stage == "v" and behind <= n:
                values(n - behind)
            if stage == "c" and n + ahead < len(tiles):
                scores(n + ahead)
            if stage == "s" and n < len(tiles):
                softmax(n)


def _tri_mask(s):
    kk = lax.broadcasted_iota(jnp.int32, s.shape, 0)
    qq = lax.broadcasted_iota(jnp.int32, s.shape, 1)
    return jnp.where(kk <= qq, s, -jnp.inf)


def _flash_scratch(dv, tq, tk):
    return [pltpu.VMEM((SCORES_AHEAD + 1, tk, tq), F32), pltpu.VMEM((SCORES_AHEAD + 1, 1, tq), F32),
            pltpu.VMEM((VALUES_BEHIND + 1, tk, tq), BF16), pltpu.VMEM((VALUES_BEHIND + 1, 1, tq), F32),
            pltpu.VMEM((1, tq), F32), pltpu.VMEM((dv + ONES_ROWS, tq), F32)]


def _mla_attn_kernel(qt_ref, k_ref, vt_ref, o_ref, *scratch, heads, tq, tk):

    def q_fn(g, i, lo):
        return qt_ref[g * HEAD_PAD:(g + 1) * HEAD_PAD, i * tq + lo:(i + 1) * tq]

    def k_fn(g, t):
        return k_ref[t * tk:(t + 1) * tk, g * HEAD_PAD:(g + 1) * HEAD_PAD]

    def vt_fn(g, t):
        return vt_ref[t, g * MLA_V:(g + 1) * MLA_V, :]

    def logits(g, s, c, lo):
        return _tri_mask(s) if c >= 0 else s

    def finalize(g, i, o):
        o_ref[g * MLA_V:(g + 1) * MLA_V, i * tq:(i + 1) * tq] = o.astype(BF16)

    _flash_pipeline(q_fn, k_fn, vt_fn, logits, finalize, *scratch,
                    nstreams=heads, nq=qt_ref.shape[1] // tq, tq=tq, tk=tk, order=MLA_STAGE_ORDER)


def _mla_attn(qt, k, vt, tq, tk):
    bsz, nq, s = qt.shape
    nkt = s // tk
    hps = MLA_HEADS_PER_STEP
    assert tq == 2 * tk and s % tq == 0 and MLA_HEADS % hps == 0
    return pl.pallas_call(
        functools.partial(_mla_attn_kernel, heads=hps, tq=tq, tk=tk),
        grid=(MLA_HEADS // hps, bsz),
        in_specs=[
            pl.BlockSpec((None, hps * HEAD_PAD, s), lambda h, b: (b, h, 0)),
            pl.BlockSpec((None, s, hps * HEAD_PAD), lambda h, b: (b, 0, h)),
            pl.BlockSpec((None, nkt, hps * MLA_V, tk), lambda h, b: (b, 0, h, 0)),
        ],
        out_specs=pl.BlockSpec((None, hps * MLA_V, s), lambda h, b: (b, h, 0)),
        out_shape=jax.ShapeDtypeStruct((bsz, MLA_HEADS * MLA_V, s), BF16),
        scratch_shapes=_flash_scratch(MLA_V, tq, tk),
        compiler_params=_params(("arbitrary", "arbitrary")),
        name="mla_attn",
    )(qt, k, vt)


def _diff_attn_kernel(qt_ref, k_ref, vt_ref, bias_ref, lamv_ref, gn_ref, o_ref, o0_ref, qm_ref, *scratch, tq, tk):
    qt = qt_ref[...]
    row = lax.broadcasted_iota(jnp.int32, qt.shape, 0)
    for g in range(2):
        keep = jnp.logical_and(row >= g * DIFF_D, row < (g + 1) * DIFF_D)
        qm_ref[g] = jnp.where(keep, qt, jnp.zeros_like(qt))

    lv = lamv_ref[...]
    lam = (jnp.exp(jnp.sum(lv[0:1] * lv[1:2], axis=1, keepdims=True))
           - jnp.exp(jnp.sum(lv[2:3] * lv[3:4], axis=1, keepdims=True)) + LAM_INIT)

    def q_fn(g, i, lo):
        return qm_ref[g, :, i * tq + lo:(i + 1) * tq]

    def k_fn(g, t):
        return k_ref[t * tk:(t + 1) * tk, :]

    def vt_fn(g, t):
        return vt_ref[t]

    def logits(g, s, c, lo):
        if c >= -1:
            s = s + bias_ref[g, c + 1, :, lo:]
        return _tri_mask(s) if c >= 0 else s

    def finalize(g, i, o):
        cols = slice(i * tq, (i + 1) * tq)
        if g == 0:
            o0_ref[:, cols] = o
        else:
            d = o0_ref[:, cols] - lam * o
            d = d * lax.rsqrt(jnp.mean(d * d, axis=0, keepdims=True) + RMS_EPS) * gn_ref[...] * (1.0 - LAM_INIT)
            o_ref[:, cols] = d.astype(BF16)

    _flash_pipeline(q_fn, k_fn, vt_fn, logits, finalize, *scratch,
                    nstreams=2, nq=qt_ref.shape[1] // tq, tq=tq, tk=tk, order=DIFF_STAGE_ORDER)


def _diff_attn(qt, k, vt, bias, lamv, gn, tq, tk):
    bsz, nd, s = qt.shape
    nkt = s // tk
    dv = 2 * DIFF_D
    assert tq == 2 * tk and s % tq == 0
    return pl.pallas_call(
        functools.partial(_diff_attn_kernel, tq=tq, tk=tk),
        grid=(DIFF_HEADS, bsz),
        in_specs=[
            pl.BlockSpec((None, dv, s), lambda h, b: (b, h, 0)),
            pl.BlockSpec((None, None, s, dv), lambda h, b: (b, h, 0, 0)),
            pl.BlockSpec((None, nkt, dv, tk), lambda h, b: (b, 0, h, 0)),
            pl.BlockSpec((2, BIAS_TILES, tk, tq), lambda h, b: (h, 0, 0, 0)),
            pl.BlockSpec(lamv.shape, lambda h, b: (0, 0)),
            pl.BlockSpec(gn.shape, lambda h, b: (0, 0)),
        ],
        out_specs=pl.BlockSpec((None, dv, s), lambda h, b: (b, h, 0)),
        out_shape=jax.ShapeDtypeStruct((bsz, nd, s), BF16),
        scratch_shapes=[pltpu.VMEM((dv, s), F32), pltpu.VMEM((2, dv, s), BF16)] + _flash_scratch(dv, tq, tk),
        compiler_params=_params(("arbitrary", "arbitrary")),
        name="diff_attn",
    )(qt, k, vt, bias, lamv, gn)


def _post_mixer_kernel(h_ref, omt_ref, odt_ref, wum_ref, wud_ref, wg_ref, bg_ref, wo_ref, g_ref, b_ref,
                       o_ref):
    d = h_ref.shape[1]
    for rows in _chains(h_ref.shape[0]):
        h = h_ref[rows, :]
        z = _dot(h.astype(BF16), wg_ref[...]) + bg_ref[...]
        gate = 1.0 / (1.0 + jnp.exp(-z))
        y_m = _dot_tn(omt_ref[:, rows], wum_ref[...])
        y_d = _dot_tn(odt_ref[:, rows], wud_ref[...])
        mixed = gate[:, :d] * y_m + gate[:, d:] * y_d
        mix = _dot(mixed.astype(BF16), wo_ref[...])
        o_ref[rows, :] = _layer_norm(ALPHA * h + mix, g_ref[...], b_ref[...])


def _post_mixer(h, omt, odt, bsz, s, w_up_mla, w_up_diff, w_gate, b_gate, w_o, g, b):
    n, d = h.shape
    tm = ROW_TILE
    spt = s // tm
    ws = [w_up_mla.astype(BF16), w_up_diff.astype(BF16), w_gate.astype(BF16), b_gate.reshape(1, -1),
          w_o.astype(BF16), g.reshape(1, d), b.reshape(1, d)]

    return pl.pallas_call(
        _post_mixer_kernel,
        grid=(bsz, spt),
        in_specs=[
            pl.BlockSpec((tm, d), lambda bb, i: (bb * spt + i, 0)),
            pl.BlockSpec((None, omt.shape[1], tm), lambda bb, i: (bb, 0, i)),
            pl.BlockSpec((None, odt.shape[1], tm), lambda bb, i: (bb, 0, i)),
        ] + [_resident(a) for a in ws],
        out_specs=pl.BlockSpec((tm, d), lambda bb, i: (bb * spt + i, 0)),
        out_shape=jax.ShapeDtypeStruct((n, d), F32),
        compiler_params=_params(("arbitrary", "arbitrary")),
        name="post_mixer",
    )(h, omt, odt, *ws)


def _mem_kv_kernel(mem_ref, w_ref, k_ref, v_ref):
    d = k_ref.shape[1]
    kv = _dot(mem_ref[...].astype(BF16), w_ref[...])
    k_ref[...] = kv[:, :d].astype(BF16)
    v_ref[...] = kv[:, d:].astype(BF16)


def _mem_kv(mem, w_kv):
    bsz, m, d = mem.shape
    w = w_kv.astype(BF16)
    spec = pl.BlockSpec((None, m, d), lambda b: (b, 0, 0))
    return pl.pallas_call(
        _mem_kv_kernel,
        grid=(bsz,),
        in_specs=[spec, pl.BlockSpec(w.shape, lambda b: (0, 0))],
        out_specs=(spec, spec),
        out_shape=(jax.ShapeDtypeStruct((bsz, m, d), BF16), jax.ShapeDtypeStruct((bsz, m, d), BF16)),
        compiler_params=_params(("arbitrary",)),
        name="mem_kv",
    )(mem, w)


def _mem_attn_kernel(h_ref, k_ref, v_ref, wq_ref, wo_ref, g_ref, b_ref, o_ref):
    d = h_ref.shape[1]
    hd = d // MEM_HEADS
    for rows in _chains(h_ref.shape[0], MEM_CHAIN_ROWS):
        h = h_ref[rows, :]
        q = _dot(h.astype(BF16), wq_ref[...]).astype(BF16)
        heads = [slice(hh * hd, (hh + 1) * hd) for hh in range(MEM_HEADS)]
        logits = [_dot_nt(q[:, cols], k_ref[:, cols]) * (hd ** -0.5) for cols in heads]
        probs = []
        for s in logits:
            p = jnp.exp(s - jnp.max(s, axis=-1, keepdims=True))
            probs.append((p / jnp.sum(p, axis=-1, keepdims=True)).astype(BF16))
        outs = [_dot(p, v_ref[:, cols]).astype(BF16) for cols, p in zip(heads, probs)]
        att = _dot(jnp.concatenate(outs, axis=1), wo_ref[...])
        o_ref[rows, :] = _layer_norm(ALPHA * h + att, g_ref[...], b_ref[...])


def _mem_attn(h, km, vm, bsz, s, w_q, w_o, g, b):
    n, d = h.shape
    tm = WIDE_ROW_TILE
    spt = s // tm
    m = km.shape[1]
    ws = [w_q.astype(BF16), w_o.astype(BF16), g.reshape(1, d), b.reshape(1, d)]

    return pl.pallas_call(
        _mem_attn_kernel,
        grid=(bsz, spt),
        in_specs=[
            pl.BlockSpec((tm, d), lambda bb, i: (bb * spt + i, 0)),
            pl.BlockSpec((None, m, d), lambda bb, i: (bb, 0, 0)),
            pl.BlockSpec((None, m, d), lambda bb, i: (bb, 0, 0)),
        ] + [_resident(a) for a in ws],
        out_specs=pl.BlockSpec((tm, d), lambda bb, i: (bb * spt + i, 0)),
        out_shape=jax.ShapeDtypeStruct((n, d), F32),
        compiler_params=_params(("arbitrary", "arbitrary")),
        name="mem_attn",
    )(h, km, vm, *ws)


def kernel(x, mem, rel_bias, ffn1_w1, ffn1_w3, ffn1_w2, ln1_g, ln1_b, w_in, q_norm_g, kv_norm_g, w_q_b, w_kv_b, lam_q1, lam_k1, lam_q2, lam_k2, diff_norm_g, w_gate, b_gate, w_up_mla, w_up_diff, w_o, ln2_g, ln2_b, mem_w_q, mem_w_kv, mem_w_o, ln3_g, ln3_b, ffn2_w1, ffn2_w3, ffn2_w2, ln4_g, ln4_b):
    bsz, s, d = x.shape
    assert ffn1_w1.shape[0] == DEPTH
    l = 0
    h = x.reshape(bsz * s, d)

    h = _ffn_ln(h, ffn1_w1[l], ffn1_w3[l], ffn1_w2[l], ln1_g[l], ln1_b[l])

    qtm, km, vtm, qtd, kd, vtd = _in_proj(h, bsz, s, w_in[l], q_norm_g[l], kv_norm_g[l], w_q_b[l], w_kv_b[l])
    bias = _rel_bias_tiles(rel_bias, ATTN_TQ, ATTN_TK)
    omt = _mla_attn(qtm, km, vtm, ATTN_TQ, ATTN_TK)
    lamv = jnp.stack([lam_q1[l], lam_k1[l], lam_q2[l], lam_k2[l]], axis=0)
    odt = _diff_attn(qtd, kd, vtd, bias, lamv, diff_norm_g[l].reshape(-1, 1), ATTN_TQ, ATTN_TK)
    h = _post_mixer(h, omt, odt, bsz, s, w_up_mla[l], w_up_diff[l], w_gate[l], b_gate[l], w_o[l],
                    ln2_g[l], ln2_b[l])

    kmem, vmem = _mem_kv(mem, mem_w_kv[l])
    h = _mem_attn(h, kmem, vmem, bsz, s, mem_w_q[l], mem_w_o[l], ln3_g[l], ln3_b[l])

    h = _ffn_ln(h, ffn2_w1[l], ffn2_w3[l], ffn2_w2[l], ln4_g[l], ln4_b[l])
    return h.reshape(bsz, s, d)
```

```python
import functools
import math

import jax
import jax.numpy as jnp
from jax import lax
from jax.experimental import pallas as pl
from jax.experimental.pallas import tpu as pltpu

F32 = jnp.float32
BF16 = jnp.bfloat16

DEPTH = 1
MLA_HEADS = 8
MLA_Q_RANK = 384
MLA_KV_RANK = 256
MLA_NOPE = 64
MLA_ROPE = 32
MLA_V = 64
ROPE_BASE = 10000.0
DIFF_HEADS = 4
DIFF_D = 64
REL_BUCKETS = 32
REL_MAX_DIST = 128
MEM_HEADS = 4
ALPHA = (2.0 * DEPTH) ** 0.25
LN_EPS = 1e-5
RMS_EPS = 1e-6
LAM_INIT = 0.8 - 0.6 * math.exp(-0.3 * 0)
LOG2E = math.log2(math.e)

HEAD_PAD = 128
VMEM_LIMIT_BYTES = 56 * 1024 * 1024

ROW_TILE = 1024
WIDE_ROW_TILE = 2048
CHAIN_ROWS = 256
MEM_CHAIN_ROWS = 512
ATTN_TQ = 512
ATTN_TK = 256
MLA_HEADS_PER_STEP = 4
MLA_STAGE_ORDER = "vcs"
DIFF_STAGE_ORDER = "csv"


def _params(semantics):
    return pltpu.CompilerParams(dimension_semantics=semantics, vmem_limit_bytes=VMEM_LIMIT_BYTES)


def _dot(a, b):
    return jnp.dot(a, b, preferred_element_type=F32)


def _dot_nt(a, b):
    return lax.dot_general(a, b, (((1,), (1,)), ((), ())), preferred_element_type=F32)


def _dot_tn(a, b):
    return lax.dot_general(a, b, (((0,), (0,)), ((), ())), preferred_element_type=F32)


def _layer_norm(y, g, b):
    mu = jnp.mean(y, axis=-1, keepdims=True)
    d = y - mu
    var = jnp.mean(d * d, axis=-1, keepdims=True)
    return d * lax.rsqrt(var + LN_EPS) * g + b


def _rms_norm_rows(x, g):
    return x * lax.rsqrt(jnp.mean(x * x, axis=-1, keepdims=True) + RMS_EPS) * g


def _chains(rows, chain=CHAIN_ROWS):
    return [slice(r, r + chain) for r in range(0, rows, chain)]


def _resident(a):
    return pl.BlockSpec(a.shape, lambda *_: (0,) * a.ndim, pipeline_mode=pl.Buffered(1))


def _ffn_ln_kernel(x_ref, w1_ref, w3_ref, w2_ref, g_ref, b_ref, o_ref):
    for rows in _chains(x_ref.shape[0]):
        x = x_ref[rows, :]
        xb = x.astype(BF16)
        u = _dot(xb, w1_ref[...])
        silu = u / (1.0 + jnp.exp(-u))
        act = silu * _dot(xb, w3_ref[...])
        y = ALPHA * x + 0.5 * _dot(act.astype(BF16), w2_ref[...])
        o_ref[rows, :] = _layer_norm(y, g_ref[...], b_ref[...])


def _ffn_ln(x, w1, w3, w2, g, b):
    n, d = x.shape
    tm = ROW_TILE
    assert n % tm == 0
    ws = [w1.astype(BF16), w3.astype(BF16), w2.astype(BF16), g.reshape(1, d), b.reshape(1, d)]
    return pl.pallas_call(
        _ffn_ln_kernel,
        grid=(n // tm,),
        in_specs=[pl.BlockSpec((tm, d), lambda i: (i, 0))] + [_resident(a) for a in ws],
        out_specs=pl.BlockSpec((tm, d), lambda i: (i, 0)),
        out_shape=jax.ShapeDtypeStruct((n, d), F32),
        compiler_params=_params(("arbitrary",)),
        name="ffn_ln",
    )(x, *ws)


def _in_proj_kernel(h_ref, wnat_ref, wt_ref, wq_ref, wkn_ref, wvt_ref, gq_ref, gkv_ref,
                    cosk_ref, sink_ref, cosq_ref, sinq_ref,
                    qtm_ref, km_ref, vtm_ref, qtd_ref, kd_ref, vtd_ref):
    tk = vtm_ref.shape[-1]
    half = MLA_ROPE // 2
    nqd = qtd_ref.shape[0]
    o1 = MLA_Q_RANK
    o2 = o1 + MLA_KV_RANK
    o3 = o2 + HEAD_PAD
    q_scale = (MLA_NOPE + MLA_ROPE) ** -0.5 * LOG2E
    for rows in _chains(h_ref.shape[0]):
        key_tiles = range(rows.start // tk, rows.stop // tk)
        hb = h_ref[rows, :].astype(BF16)

        pt = _dot_nt(wt_ref[...], hb)
        qtd_ref[:, rows] = (pt[:nqd] * (DIFF_D ** -0.5 * LOG2E)).astype(BF16)
        for c, t in enumerate(key_tiles):
            vtd_ref[t] = pt[nqd:, c * tk:(c + 1) * tk].astype(BF16)

        pn = _dot(hb, wnat_ref[...])
        c_q, c_kv, krg, kd = pn[:, :o1], pn[:, o1:o2], pn[:, o2:o3], pn[:, o3:]

        cqn = _rms_norm_rows(c_q, gq_ref[...]).astype(BF16)
        ckvn = _rms_norm_rows(c_kv, gkv_ref[...]).astype(BF16)
        qt = _dot_nt(wq_ref[...], cqn)
        kn = _dot(ckvn, wkn_ref[...])
        vt = _dot_nt(wvt_ref[...], ckvn)

        for hh in range(DIFF_HEADS):
            kd_ref[hh, rows, :] = kd[:, hh * 2 * DIFF_D:(hh + 1) * 2 * DIFF_D].astype(BF16)

        cos_q, sin_q = cosq_ref[:, rows], sinq_ref[:, rows]
        for hh in range(MLA_HEADS):
            lo = hh * HEAD_PAD
            r0 = lo + MLA_NOPE
            r1, r2 = r0 + half, r0 + MLA_ROPE
            x1, x2 = qt[r0:r1], qt[r1:r2]
            qtm_ref[lo:r0, rows] = (qt[lo:r0] * q_scale).astype(BF16)
            qtm_ref[r0:r1, rows] = ((x1 * cos_q - x2 * sin_q) * q_scale).astype(BF16)
            qtm_ref[r1:r2, rows] = ((x2 * cos_q + x1 * sin_q) * q_scale).astype(BF16)
            qtm_ref[r2:lo + HEAD_PAD, rows] = jnp.zeros((lo + HEAD_PAD - r2, CHAIN_ROWS), BF16)

        krh = jnp.concatenate([krg[:, MLA_ROPE:], krg[:, :MLA_ROPE]], axis=1)
        kr = krg * cosk_ref[rows, :] + krh * sink_ref[rows, :]
        for hh in range(MLA_HEADS):
            lo, hi = hh * HEAD_PAD, (hh + 1) * HEAD_PAD
            km_ref[rows, lo:hi] = (kn[:, lo:hi] + kr).astype(BF16)
        for c, t in enumerate(key_tiles):
            vtm_ref[t] = vt[:, c * tk:(c + 1) * tk].astype(BF16)


def _in_proj_weights(w_in, w_q_b, w_kv_b):
    d = w_in.shape[0]
    o_cq = MLA_Q_RANK
    o_ckv = o_cq + MLA_KV_RANK
    o_kr = o_ckv + MLA_ROPE
    n_d = 2 * DIFF_HEADS * DIFF_D
    o_qd = o_kr + n_d
    o_kd = o_qd + n_d
    half = MLA_ROPE // 2
    pad = HEAD_PAD - MLA_NOPE - MLA_ROPE
    assert pad == MLA_ROPE
    kr = w_in[:, o_ckv:o_kr]
    krg = jnp.concatenate([jnp.zeros((d, MLA_NOPE), F32), kr, -kr[:, half:], kr[:, :half]], axis=1)
    w_nat = jnp.concatenate([w_in[:, :o_ckv], krg, w_in[:, o_qd:o_kd]], axis=1).astype(BF16)
    w_t = jnp.concatenate([w_in[:, o_kr:o_qd], w_in[:, o_kd:]], axis=1).T.astype(BF16)

    r = w_q_b.shape[0]
    wq = w_q_b.reshape(r, MLA_HEADS, MLA_NOPE + MLA_ROPE)
    wq = jnp.concatenate([wq, jnp.zeros((r, MLA_HEADS, pad), F32)], axis=-1)
    wq_t = wq.reshape(r, MLA_HEADS * HEAD_PAD).T.astype(BF16)

    rk = w_kv_b.shape[0]
    wkv = w_kv_b.reshape(rk, MLA_HEADS, MLA_NOPE + MLA_V)
    wkn = jnp.concatenate([wkv[..., :MLA_NOPE], jnp.zeros((rk, MLA_HEADS, HEAD_PAD - MLA_NOPE), F32)],
                          axis=-1).reshape(rk, MLA_HEADS * HEAD_PAD).astype(BF16)
    wv_t = wkv[..., MLA_NOPE:].reshape(rk, MLA_HEADS * MLA_V).T.astype(BF16)
    return w_nat, w_t, wq_t, wkn, wv_t


def _rope_tables(s):
    inv_freq = ROPE_BASE ** (-jnp.arange(0, MLA_ROPE, 2, dtype=F32) / MLA_ROPE)
    ang = jnp.arange(s).astype(F32)[:, None] * inv_freq[None, :]
    cos, sin = jnp.cos(ang), jnp.sin(ang)
    z_lo = jnp.zeros((s, MLA_NOPE), F32)
    z_hi = jnp.zeros((s, HEAD_PAD - MLA_NOPE - MLA_ROPE), F32)
    cos_k = jnp.concatenate([z_lo, cos, cos, z_hi], axis=1)
    sin_k = jnp.concatenate([z_lo, sin, sin, z_hi], axis=1)
    return cos_k, sin_k, cos.T, sin.T


def _in_proj(h, bsz, s, w_in, q_norm_g, kv_norm_g, w_q_b, w_kv_b):
    n, d = h.shape
    tm, tk = ROW_TILE, ATTN_TK
    assert s % tm == 0 and CHAIN_ROWS % tk == 0
    w_nat, w_t, wq_t, wkn, wv_t = _in_proj_weights(w_in, w_q_b, w_kv_b)
    cos_k, sin_k, cos_q, sin_q = _rope_tables(s)
    half = MLA_ROPE // 2
    spt = s // tm
    nqm = MLA_HEADS * HEAD_PAD
    nvm = MLA_HEADS * MLA_V
    nd = 2 * DIFF_HEADS * DIFF_D

    gq = q_norm_g.reshape(1, -1)
    gkv = kv_norm_g.reshape(1, -1)
    out_shape = (
        jax.ShapeDtypeStruct((bsz, nqm, s), BF16),
        jax.ShapeDtypeStruct((bsz, s, nqm), BF16),
        jax.ShapeDtypeStruct((bsz, s // tk, nvm, tk), BF16),
        jax.ShapeDtypeStruct((bsz, nd, s), BF16),
        jax.ShapeDtypeStruct((bsz, DIFF_HEADS, s, nd // DIFF_HEADS), BF16),
        jax.ShapeDtypeStruct((bsz, s // tk, nd, tk), BF16),
    )
    out_specs = (
        pl.BlockSpec((None, nqm, tm), lambda b, i: (b, 0, i)),
        pl.BlockSpec((None, tm, nqm), lambda b, i: (b, i, 0)),
        pl.BlockSpec((None, tm // tk, nvm, tk), lambda b, i: (b, i, 0, 0)),
        pl.BlockSpec((None, nd, tm), lambda b, i: (b, 0, i)),
        pl.BlockSpec((None, DIFF_HEADS, tm, nd // DIFF_HEADS), lambda b, i: (b, 0, i, 0)),
        pl.BlockSpec((None, tm // tk, nd, tk), lambda b, i: (b, i, 0, 0)),
    )
    return pl.pallas_call(
        _in_proj_kernel,
        grid=(bsz, spt),
        in_specs=[
            pl.BlockSpec((tm, d), lambda b, i: (b * spt + i, 0)),
            _resident(w_nat), _resident(w_t), _resident(wq_t), _resident(wkn), _resident(wv_t),
            _resident(gq), _resident(gkv),
            pl.BlockSpec((tm, HEAD_PAD), lambda b, i: (i, 0)),
            pl.BlockSpec((tm, HEAD_PAD), lambda b, i: (i, 0)),
            pl.BlockSpec((half, tm), lambda b, i: (0, i)),
            pl.BlockSpec((half, tm), lambda b, i: (0, i)),
        ],
        out_specs=out_specs,
        out_shape=out_shape,
        compiler_params=_params(("arbitrary", "arbitrary")),
        name="in_proj",
    )(h, w_nat, w_t, wq_t, wkn, wv_t, gq, gkv, cos_k, sin_k, cos_q, sin_q)


BIAS_TILES = 3


def _rel_bias_kernel(rb_ref, o_ref, *, tq, tk):
    c = pl.program_id(0) - 1
    kk = lax.broadcasted_iota(jnp.int32, (tk, tq), 0)
    qq = lax.broadcasted_iota(jnp.int32, (tk, tq), 1)
    n = jnp.maximum(qq - kk - c * tk, 0)
    max_exact = REL_BUCKETS // 2
    nf = jnp.maximum(n, 1).astype(F32)
    large = max_exact + (jnp.log(nf / max_exact) / math.log(REL_MAX_DIST / max_exact)
                         * (REL_BUCKETS - max_exact)).astype(jnp.int32)
    large = jnp.minimum(large, REL_BUCKETS - 1)
    bucket = jnp.where(n < max_exact, n, large)
    for m in range(o_ref.shape[0]):
        acc = jnp.full((tk, tq), rb_ref[0, m], F32)
        for bkt in range(1, REL_BUCKETS):
            acc = jnp.where(bucket >= bkt, rb_ref[bkt, m], acc)
        o_ref[m] = (acc - rb_ref[REL_BUCKETS - 1, m]) * LOG2E


def _rel_bias_tiles(rel_bias, tq, tk):
    assert tq == 2 * tk and tk + 1 >= REL_MAX_DIST
    nmaps = rel_bias.shape[1]
    return pl.pallas_call(
        functools.partial(_rel_bias_kernel, tq=tq, tk=tk),
        grid=(BIAS_TILES,),
        in_specs=[pl.BlockSpec(memory_space=pltpu.SMEM)],
        out_specs=pl.BlockSpec((nmaps, None, tk, tq), lambda t: (0, t, 0, 0)),
        out_shape=jax.ShapeDtypeStruct((nmaps, BIAS_TILES, tk, tq), F32),
        compiler_params=_params(("arbitrary",)),
        name="rel_bias",
    )(rel_bias)


SCORES_AHEAD = 3
VALUES_BEHIND = 1
ONES_ROWS = 16


def _flash_pipeline(q_fn, k_fn, vt_fn, logits_fn, finalize_fn, s_ref, mt_ref, p_ref, al_ref, m_ref, acc_ref,
                    *, nstreams, nq, tq, tk, order):
    tiles = [(g, i, t, t - 2 * i) for g in range(nstreams) for i in range(nq) for t in range(2 * i + 2)]
    ones = jnp.ones((ONES_ROWS, tk), BF16)
    ahead = s_ref.shape[0] - 1
    behind = p_ref.shape[0] - 1

    def lane0(c):
        return tk if c == 1 else 0

    def scores(n):
        g, i, t, c = tiles[n]
        slot, lo = n % (ahead + 1), lane0(c)
        s = logits_fn(g, _dot(k_fn(g, t), q_fn(g, i, lo)), c, lo)
        s_ref[slot, :, lo:] = s
        mt_ref[slot, :, lo:] = jnp.max(s, axis=0, keepdims=True)

    def softmax(n):
        g, i, t, c = tiles[n]
        sslot, slot, lo = n % (ahead + 1), n % (behind + 1), lane0(c)
        m_new = mt_ref[sslot, :, lo:]
        if t > 0:
            m_prev = m_ref[:, lo:]
            m_new = jnp.maximum(m_prev, m_new)
            al_ref[slot, :, lo:] = jnp.exp2(m_prev - m_new)
        m_ref[:, lo:] = m_new
        p_ref[slot, :, lo:] = jnp.exp2(s_ref[sslot, :, lo:] - m_new).astype(BF16)

    def values(n):
        g, i, t, c = tiles[n]
        slot, lo = n % (behind + 1), lane0(c)
        pv = _dot(jnp.concatenate([vt_fn(g, t), ones], axis=0), p_ref[slot, :, lo:])
        if t > 0:
            pv = al_ref[slot, :, lo:] * acc_ref[:, lo:] + pv
        acc_ref[:, lo:] = pv
        if c == 1:
            acc = acc_ref[...]
            dv = acc.shape[0] - ONES_ROWS
            finalize_fn(g, i, acc[:dv] / acc[dv:dv + 1])

    for n in range(ahead):
        scores(n)
    assert sorted(order) == ["c", "s", "v"] and (behind > 0 or order.index("s") < order.index("v"))
    for n in range(len(tiles) + behind):
        for stage in order:
            if stage == "v" and behind <= n:
                values(n - behind)
            if stage == "c" and n + ahead < len(tiles):
                scores(n + ahead)
            if stage == "s" and n < len(tiles):
                softmax(n)


def _tri_mask(s):
    kk = lax.broadcasted_iota(jnp.int32, s.shape, 0)
    qq = lax.broadcasted_iota(jnp.int32, s.shape, 1)
    return jnp.where(kk <= qq, s, -jnp.inf)


def _flash_scratch(dv, tq, tk):
    return [pltpu.VMEM((SCORES_AHEAD + 1, tk, tq), F32), pltpu.VMEM((SCORES_AHEAD + 1, 1, tq), F32),
            pltpu.VMEM((VALUES_BEHIND + 1, tk, tq), BF16), pltpu.VMEM((VALUES_BEHIND + 1, 1, tq), F32),
            pltpu.VMEM((1, tq), F32), pltpu.VMEM((dv + ONES_ROWS, tq), F32)]


def _mla_attn_kernel(qt_ref, k_ref, vt_ref, o_ref, *scratch, heads, tq, tk):

    def q_fn(g, i, lo):
        return qt_ref[g * HEAD_PAD:(g + 1) * HEAD_PAD, i * tq + lo:(i + 1) * tq]

    def k_fn(g, t):
        return k_ref[t * tk:(t + 1) * tk, g * HEAD_PAD:(g + 1) * HEAD_PAD]

    def vt_fn(g, t):
        return vt_ref[t, g * MLA_V:(g + 1) * MLA_V, :]

    def logits(g, s, c, lo):
        return _tri_mask(s) if c >= 0 else s

    def finalize(g, i, o):
        o_ref[g * MLA_V:(g + 1) * MLA_V, i * tq:(i + 1) * tq] = o.astype(BF16)

    _flash_pipeline(q_fn, k_fn, vt_fn, logits, finalize, *scratch,
                    nstreams=heads, nq=qt_ref.shape[1] // tq, tq=tq, tk=tk, order=MLA_STAGE_ORDER)


def _mla_attn(qt, k, vt, tq, tk):
    bsz, nq, s = qt.shape
    nkt = s // tk
    hps = MLA_HEADS_PER_STEP
    assert tq == 2 * tk and s % tq == 0 and MLA_HEADS % hps == 0
    return pl.pallas_call(
        functools.partial(_mla_attn_kernel, heads=hps, tq=tq, tk=tk),
        grid=(MLA_HEADS // hps, bsz),
        in_specs=[
            pl.BlockSpec((None, hps * HEAD_PAD, s), lambda h, b: (b, h, 0)),
            pl.BlockSpec((None, s, hps * HEAD_PAD), lambda h, b: (b, 0, h)),
            pl.BlockSpec((None, nkt, hps * MLA_V, tk), lambda h, b: (b, 0, h, 0)),
        ],
        out_specs=pl.BlockSpec((None, hps * MLA_V, s), lambda h, b: (b, h, 0)),
        out_shape=jax.ShapeDtypeStruct((bsz, MLA_HEADS * MLA_V, s), BF16),
        scratch_shapes=_flash_scratch(MLA_V, tq, tk),
        compiler_params=_params(("arbitrary", "arbitrary")),
        name="mla_attn",
    )(qt, k, vt)


def _diff_attn_kernel(qt_ref, k_ref, vt_ref, bias_ref, lamv_ref, gn_ref, o_ref, o0_ref, qm_ref, *scratch, tq, tk):
    qt = qt_ref[...]
    row = lax.broadcasted_iota(jnp.int32, qt.shape, 0)
    for g in range(2):
        keep = jnp.logical_and(row >= g * DIFF_D, row < (g + 1) * DIFF_D)
        qm_ref[g] = jnp.where(keep, qt, jnp.zeros_like(qt))

    lv = lamv_ref[...]
    lam = (jnp.exp(jnp.sum(lv[0:1] * lv[1:2], axis=1, keepdims=True))
           - jnp.exp(jnp.sum(lv[2:3] * lv[3:4], axis=1, keepdims=True)) + LAM_INIT)

    def q_fn(g, i, lo):
        return qm_ref[g, :, i * tq + lo:(i + 1) * tq]

    def k_fn(g, t):
        return k_ref[t * tk:(t + 1) * tk, :]

    def vt_fn(g, t):
        return vt_ref[t]

    def logits(g, s, c, lo):
        if c >= -1:
            s = s + bias_ref[g, c + 1, :, lo:]
        return _tri_mask(s) if c >= 0 else s

    def finalize(g, i, o):
        cols = slice(i * tq, (i + 1) * tq)
        if g == 0:
            o0_ref[:, cols] = o
        else:
            d = o0_ref[:, cols] - lam * o
            d = d * lax.rsqrt(jnp.mean(d * d, axis=0, keepdims=True) + RMS_EPS) * gn_ref[...] * (1.0 - LAM_INIT)
            o_ref[:, cols] = d.astype(BF16)

    _flash_pipeline(q_fn, k_fn, vt_fn, logits, finalize, *scratch,
                    nstreams=2, nq=qt_ref.shape[1] // tq, tq=tq, tk=tk, order=DIFF_STAGE_ORDER)


def _diff_attn(qt, k, vt, bias, lamv, gn, tq, tk):
    bsz, nd, s = qt.shape
    nkt = s // tk
    dv = 2 * DIFF_D
    assert tq == 2 * tk and s % tq == 0
    return pl.pallas_call(
        functools.partial(_diff_attn_kernel, tq=tq, tk=tk),
        grid=(DIFF_HEADS, bsz),
        in_specs=[
            pl.BlockSpec((None, dv, s), lambda h, b: (b, h, 0)),
            pl.BlockSpec((None, None, s, dv), lambda h, b: (b, h, 0, 0)),
            pl.BlockSpec((None, nkt, dv, tk), lambda h, b: (b, 0, h, 0)),
            pl.BlockSpec((2, BIAS_TILES, tk, tq), lambda h, b: (h, 0, 0, 0)),
            pl.BlockSpec(lamv.shape, lambda h, b: (0, 0)),
            pl.BlockSpec(gn.shape, lambda h, b: (0, 0)),
        ],
        out_specs=pl.BlockSpec((None, dv, s), lambda h, b: (b, h, 0)),
        out_shape=jax.ShapeDtypeStruct((bsz, nd, s), BF16),
        scratch_shapes=[pltpu.VMEM((dv, s), F32), pltpu.VMEM((2, dv, s), BF16)] + _flash_scratch(dv, tq, tk),
        compiler_params=_params(("arbitrary", "arbitrary")),
        name="diff_attn",
    )(qt, k, vt, bias, lamv, gn)


def _post_mixer_kernel(h_ref, omt_ref, odt_ref, wum_ref, wud_ref, wg_ref, bg_ref, wo_ref, g_ref, b_ref,
                       o_ref):
    d = h_ref.shape[1]
    for rows in _chains(h_ref.shape[0]):
        h = h_ref[rows, :]
        hb = h.astype(BF16)
        g_m = 1.0 / (1.0 + jnp.exp(-(_dot(hb, wg_ref[:, :d]) + bg_ref[:, :d])))
        y_m = _dot_tn(omt_ref[:, rows], wum_ref[...])
        g_d = 1.0 / (1.0 + jnp.exp(-(_dot(hb, wg_ref[:, d:]) + bg_ref[:, d:])))
        y_d = _dot_tn(odt_ref[:, rows], wud_ref[...])
        mixed = g_m * y_m + g_d * y_d
        mix = _dot(mixed.astype(BF16), wo_ref[...])
        o_ref[rows, :] = _layer_norm(ALPHA * h + mix, g_ref[...], b_ref[...])


def _post_mixer(h, omt, odt, bsz, s, w_up_mla, w_up_diff, w_gate, b_gate, w_o, g, b):
    n, d = h.shape
    tm = ROW_TILE
    spt = s // tm
    ws = [w_up_mla.astype(BF16), w_up_diff.astype(BF16), w_gate.astype(BF16), b_gate.reshape(1, -1),
          w_o.astype(BF16), g.reshape(1, d), b.reshape(1, d)]

    return pl.pallas_call(
        _post_mixer_kernel,
        grid=(bsz, spt),
        in_specs=[
            pl.BlockSpec((tm, d), lambda bb, i: (bb * spt + i, 0)),
            pl.BlockSpec((None, omt.shape[1], tm), lambda bb, i: (bb, 0, i)),
            pl.BlockSpec((None, odt.shape[1], tm), lambda bb, i: (bb, 0, i)),
        ] + [_resident(a) for a in ws],
        out_specs=pl.BlockSpec((tm, d), lambda bb, i: (bb * spt + i, 0)),
        out_shape=jax.ShapeDtypeStruct((n, d), F32),
        compiler_params=_params(("arbitrary", "arbitrary")),
        name="post_mixer",
    )(h, omt, odt, *ws)


def _mem_kv_kernel(mem_ref, w_ref, k_ref, v_ref):
    d = k_ref.shape[1]
    kv = _dot(mem_ref[...].astype(BF16), w_ref[...])
    k_ref[...] = kv[:, :d].astype(BF16)
    v_ref[...] = kv[:, d:].astype(BF16)


def _mem_kv(mem, w_kv):
    bsz, m, d = mem.shape
    w = w_kv.astype(BF16)
    spec = pl.BlockSpec((None, m, d), lambda b: (b, 0, 0))
    return pl.pallas_call(
        _mem_kv_kernel,
        grid=(bsz,),
        in_specs=[spec, pl.BlockSpec(w.shape, lambda b: (0, 0))],
        out_specs=(spec, spec),
        out_shape=(jax.ShapeDtypeStruct((bsz, m, d), BF16), jax.ShapeDtypeStruct((bsz, m, d), BF16)),
        compiler_params=_params(("arbitrary",)),
        name="mem_kv",
    )(mem, w)


def _mem_attn_kernel(h_ref, k_ref, v_ref, wq_ref, wo_ref, g_ref, b_ref, o_ref):
    d = h_ref.shape[1]
    hd = d // MEM_HEADS
    for rows in _chains(h_ref.shape[0], MEM_CHAIN_ROWS):
        h = h_ref[rows, :]
        q = _dot(h.astype(BF16), wq_ref[...]).astype(BF16)
        heads = [slice(hh * hd, (hh + 1) * hd) for hh in range(MEM_HEADS)]
        logits = [_dot_nt(q[:, cols], k_ref[:, cols]) * (hd ** -0.5) for cols in heads]
        probs = []
        for s in logits:
            p = jnp.exp(s - jnp.max(s, axis=-1, keepdims=True))
            probs.append((p / jnp.sum(p, axis=-1, keepdims=True)).astype(BF16))
        outs = [_dot(p, v_ref[:, cols]).astype(BF16) for cols, p in zip(heads, probs)]
        att = _dot(jnp.concatenate(outs, axis=1), wo_ref[...])
        o_ref[rows, :] = _layer_norm(ALPHA * h + att, g_ref[...], b_ref[...])


def _mem_attn(h, km, vm, bsz, s, w_q, w_o, g, b):
    n, d = h.shape
    tm = WIDE_ROW_TILE
    spt = s // tm
    m = km.shape[1]
    ws = [w_q.astype(BF16), w_o.astype(BF16), g.reshape(1, d), b.reshape(1, d)]

    return pl.pallas_call(
        _mem_attn_kernel,
        grid=(bsz, spt),
        in_specs=[
            pl.BlockSpec((tm, d), lambda bb, i: (bb * spt + i, 0)),
            pl.BlockSpec((None, m, d), lambda bb, i: (bb, 0, 0)),
            pl.BlockSpec((None, m, d), lambda bb, i: (bb, 0, 0)),
        ] + [_resident(a) for a in ws],
        out_specs=pl.BlockSpec((tm, d), lambda bb, i: (bb * spt + i, 0)),
        out_shape=jax.ShapeDtypeStruct((n, d), F32),
        compiler_params=_params(("arbitrary", "arbitrary")),
        name="mem_attn",
    )(h, km, vm, *ws)


def kernel(x, mem, rel_bias, ffn1_w1, ffn1_w3, ffn1_w2, ln1_g, ln1_b, w_in, q_norm_g, kv_norm_g, w_q_b, w_kv_b, lam_q1, lam_k1, lam_q2, lam_k2, diff_norm_g, w_gate, b_gate, w_up_mla, w_up_diff, w_o, ln2_g, ln2_b, mem_w_q, mem_w_kv, mem_w_o, ln3_g, ln3_b, ffn2_w1, ffn2_w3, ffn2_w2, ln4_g, ln4_b):
    bsz, s, d = x.shape
    assert ffn1_w1.shape[0] == DEPTH
    l = 0
    h = x.reshape(bsz * s, d)

    h = _ffn_ln(h, ffn1_w1[l], ffn1_w3[l], ffn1_w2[l], ln1_g[l], ln1_b[l])

    qtm, km, vtm, qtd, kd, vtd = _in_proj(h, bsz, s, w_in[l], q_norm_g[l], kv_norm_g[l], w_q_b[l], w_kv_b[l])
    bias = _rel_bias_tiles(rel_bias, ATTN_TQ, ATTN_TK)
    omt = _mla_attn(qtm, km, vtm, ATTN_TQ, ATTN_TK)
    lamv = jnp.stack([lam_q1[l], lam_k1[l], lam_q2[l], lam_k2[l]], axis=0)
    odt = _diff_attn(qtd, kd, vtd, bias, lamv, diff_norm_g[l].reshape(-1, 1), ATTN_TQ, ATTN_TK)
    h = _post_mixer(h, omt, odt, bsz, s, w_up_mla[l], w_up_diff[l], w_gate[l], b_gate[l], w_o[l],
                    ln2_g[l], ln2_b[l])

    kmem, vmem = _mem_kv(mem, mem_w_kv[l])
    h = _mem_attn(h, kmem, vmem, bsz, s, mem_w_q[l], mem_w_o[l], ln3_g[l], ln3_b[l])

    h = _ffn_ln(h, ffn2_w1[l], ffn2_w3[l], ffn2_w2[l], ln4_g[l], ln4_b[l])
    return h.reshape(bsz, s, d)
```

```python
import functools
import math

import jax
import jax.numpy as jnp
from jax import lax
from jax.experimental import pallas as pl
from jax.experimental.pallas import tpu as pltpu

F32 = jnp.float32
BF16 = jnp.bfloat16

DEPTH = 1
MLA_HEADS = 8
MLA_Q_RANK = 384
MLA_KV_RANK = 256
MLA_NOPE = 64
MLA_ROPE = 32
MLA_V = 64
ROPE_BASE = 10000.0
DIFF_HEADS = 4
DIFF_D = 64
REL_BUCKETS = 32
REL_MAX_DIST = 128
MEM_HEADS = 4
ALPHA = (2.0 * DEPTH) ** 0.25
LN_EPS = 1e-5
RMS_EPS = 1e-6
LAM_INIT = 0.8 - 0.6 * math.exp(-0.3 * 0)
LOG2E = math.log2(math.e)

HEAD_PAD = 128
VMEM_LIMIT_BYTES = 56 * 1024 * 1024

ROW_TILE = 1024
WIDE_ROW_TILE = 2048
CHAIN_ROWS = 256
MEM_CHAIN_ROWS = 512
ATTN_TQ = 512
ATTN_TK = 256
MLA_HEADS_PER_STEP = 4
MLA_STAGE_ORDER = "vcs"
DIFF_STAGE_ORDER = "csv"


def _params(semantics):
    return pltpu.CompilerParams(dimension_semantics=semantics, vmem_limit_bytes=VMEM_LIMIT_BYTES)


def _dot(a, b):
    return jnp.dot(a, b, preferred_element_type=F32)


def _dot_nt(a, b):
    return lax.dot_general(a, b, (((1,), (1,)), ((), ())), preferred_element_type=F32)


def _dot_tn(a, b):
    return lax.dot_general(a, b, (((0,), (0,)), ((), ())), preferred_element_type=F32)


def _layer_norm(y, g, b):
    mu = jnp.mean(y, axis=-1, keepdims=True)
    d = y - mu
    var = jnp.mean(d * d, axis=-1, keepdims=True)
    return d * lax.rsqrt(var + LN_EPS) * g + b


def _rms_norm_rows(x, g):
    return x * lax.rsqrt(jnp.mean(x * x, axis=-1, keepdims=True) + RMS_EPS) * g


def _chains(rows, chain=CHAIN_ROWS):
    return [slice(r, r + chain) for r in range(0, rows, chain)]


def _resident(a):
    return pl.BlockSpec(a.shape, lambda *_: (0,) * a.ndim, pipeline_mode=pl.Buffered(1))


def _ffn_ln_kernel(x_ref, w1_ref, w3_ref, w2_ref, g_ref, b_ref, o_ref):
    for rows in _chains(x_ref.shape[0]):
        x = x_ref[rows, :]
        xb = x.astype(BF16)
        u = _dot(xb, w1_ref[...])
        silu = u / (1.0 + jnp.exp(-u))
        act = silu * _dot(xb, w3_ref[...])
        y = ALPHA * x + 0.5 * _dot(act.astype(BF16), w2_ref[...])
        o_ref[rows, :] = _layer_norm(y, g_ref[...], b_ref[...])


def _ffn_ln(x, w1, w3, w2, g, b):
    n, d = x.shape
    tm = ROW_TILE
    assert n % tm == 0
    ws = [w1.astype(BF16), w3.astype(BF16), w2.astype(BF16), g.reshape(1, d), b.reshape(1, d)]
    return pl.pallas_call(
        _ffn_ln_kernel,
        grid=(n // tm,),
        in_specs=[pl.BlockSpec((tm, d), lambda i: (i, 0))] + [_resident(a) for a in ws],
        out_specs=pl.BlockSpec((tm, d), lambda i: (i, 0)),
        out_shape=jax.ShapeDtypeStruct((n, d), F32),
        compiler_params=_params(("arbitrary",)),
        name="ffn_ln",
    )(x, *ws)


def _in_proj_kernel(h_ref, wnat_ref, wt_ref, wq_ref, wkn_ref, wvt_ref, gq_ref, gkv_ref,
                    cosk_ref, sink_ref, cosq_ref, sinq_ref,
                    qtm_ref, km_ref, vtm_ref, qtd_ref, kd_ref, vtd_ref):
    tk = vtm_ref.shape[-1]
    half = MLA_ROPE // 2
    nqd = qtd_ref.shape[0]
    o1 = MLA_Q_RANK
    o2 = o1 + MLA_KV_RANK
    o3 = o2 + HEAD_PAD
    q_scale = (MLA_NOPE + MLA_ROPE) ** -0.5 * LOG2E
    for rows in _chains(h_ref.shape[0]):
        key_tiles = range(rows.start // tk, rows.stop // tk)
        hb = h_ref[rows, :].astype(BF16)

        pt = _dot_nt(wt_ref[...], hb)
        qtd_ref[:, rows] = (pt[:nqd] * (DIFF_D ** -0.5 * LOG2E)).astype(BF16)
        for c, t in enumerate(key_tiles):
            vtd_ref[t] = pt[nqd:, c * tk:(c + 1) * tk].astype(BF16)

        pn = _dot(hb, wnat_ref[...])
        c_q, c_kv, krg, kd = pn[:, :o1], pn[:, o1:o2], pn[:, o2:o3], pn[:, o3:]

        cqn = _rms_norm_rows(c_q, gq_ref[...]).astype(BF16)
        ckvn = _rms_norm_rows(c_kv, gkv_ref[...]).astype(BF16)
        qt = _dot_nt(wq_ref[...], cqn)
        kn = _dot(ckvn, wkn_ref[...])
        vt = _dot_nt(wvt_ref[...], ckvn)

        for hh in range(DIFF_HEADS):
            kd_ref[hh, rows, :] = kd[:, hh * 2 * DIFF_D:(hh + 1) * 2 * DIFF_D].astype(BF16)

        cos_q, sin_q = cosq_ref[:, rows], sinq_ref[:, rows]
        for hh in range(MLA_HEADS):
            lo = hh * HEAD_PAD
            r0 = lo + MLA_NOPE
            r1, r2 = r0 + half, r0 + MLA_ROPE
            x1, x2 = qt[r0:r1], qt[r1:r2]
            qtm_ref[lo:r0, rows] = (qt[lo:r0] * q_scale).astype(BF16)
            qtm_ref[r0:r1, rows] = ((x1 * cos_q - x2 * sin_q) * q_scale).astype(BF16)
            qtm_ref[r1:r2, rows] = ((x2 * cos_q + x1 * sin_q) * q_scale).astype(BF16)
            qtm_ref[r2:lo + HEAD_PAD, rows] = jnp.zeros((lo + HEAD_PAD - r2, CHAIN_ROWS), BF16)

        krh = jnp.concatenate([krg[:, MLA_ROPE:], krg[:, :MLA_ROPE]], axis=1)
        kr = krg * cosk_ref[rows, :] + krh * sink_ref[rows, :]
        for hh in range(MLA_HEADS):
            lo, hi = hh * HEAD_PAD, (hh + 1) * HEAD_PAD
            km_ref[rows, lo:hi] = (kn[:, lo:hi] + kr).astype(BF16)
        for c, t in enumerate(key_tiles):
            vtm_ref[t] = vt[:, c * tk:(c + 1) * tk].astype(BF16)


def _in_proj_weights(w_in, w_q_b, w_kv_b):
    d = w_in.shape[0]
    o_cq = MLA_Q_RANK
    o_ckv = o_cq + MLA_KV_RANK
    o_kr = o_ckv + MLA_ROPE
    n_d = 2 * DIFF_HEADS * DIFF_D
    o_qd = o_kr + n_d
    o_kd = o_qd + n_d
    half = MLA_ROPE // 2
    pad = HEAD_PAD - MLA_NOPE - MLA_ROPE
    assert pad == MLA_ROPE
    kr = w_in[:, o_ckv:o_kr]
    krg = jnp.concatenate([jnp.zeros((d, MLA_NOPE), F32), kr, -kr[:, half:], kr[:, :half]], axis=1)
    w_nat = jnp.concatenate([w_in[:, :o_ckv], krg, w_in[:, o_qd:o_kd]], axis=1).astype(BF16)
    w_t = jnp.concatenate([w_in[:, o_kr:o_qd], w_in[:, o_kd:]], axis=1).T.astype(BF16)

    r = w_q_b.shape[0]
    wq = w_q_b.reshape(r, MLA_HEADS, MLA_NOPE + MLA_ROPE)
    wq = jnp.concatenate([wq, jnp.zeros((r, MLA_HEADS, pad), F32)], axis=-1)
    wq_t = wq.reshape(r, MLA_HEADS * HEAD_PAD).T.astype(BF16)

    rk = w_kv_b.shape[0]
    wkv = w_kv_b.reshape(rk, MLA_HEADS, MLA_NOPE + MLA_V)
    wkn = jnp.concatenate([wkv[..., :MLA_NOPE], jnp.zeros((rk, MLA_HEADS, HEAD_PAD - MLA_NOPE), F32)],
                          axis=-1).reshape(rk, MLA_HEADS * HEAD_PAD).astype(BF16)
    wv_t = wkv[..., MLA_NOPE:].reshape(rk, MLA_HEADS * MLA_V).T.astype(BF16)
    return w_nat, w_t, wq_t, wkn, wv_t


def _rope_tables(s):
    inv_freq = ROPE_BASE ** (-jnp.arange(0, MLA_ROPE, 2, dtype=F32) / MLA_ROPE)
    ang = jnp.arange(s).astype(F32)[:, None] * inv_freq[None, :]
    cos, sin = jnp.cos(ang), jnp.sin(ang)
    z_lo = jnp.zeros((s, MLA_NOPE), F32)
    z_hi = jnp.zeros((s, HEAD_PAD - MLA_NOPE - MLA_ROPE), F32)
    cos_k = jnp.concatenate([z_lo, cos, cos, z_hi], axis=1)
    sin_k = jnp.concatenate([z_lo, sin, sin, z_hi], axis=1)
    return cos_k, sin_k, cos.T, sin.T


def _in_proj(h, bsz, s, w_in, q_norm_g, kv_norm_g, w_q_b, w_kv_b):
    n, d = h.shape
    tm, tk = ROW_TILE, ATTN_TK
    assert s % tm == 0 and CHAIN_ROWS % tk == 0
    w_nat, w_t, wq_t, wkn, wv_t = _in_proj_weights(w_in, w_q_b, w_kv_b)
    cos_k, sin_k, cos_q, sin_q = _rope_tables(s)
    half = MLA_ROPE // 2
    spt = s // tm
    nqm = MLA_HEADS * HEAD_PAD
    nvm = MLA_HEADS * MLA_V
    nd = 2 * DIFF_HEADS * DIFF_D

    gq = q_norm_g.reshape(1, -1)
    gkv = kv_norm_g.reshape(1, -1)
    out_shape = (
        jax.ShapeDtypeStruct((bsz, nqm, s), BF16),
        jax.ShapeDtypeStruct((bsz, s, nqm), BF16),
        jax.ShapeDtypeStruct((bsz, s // tk, nvm, tk), BF16),
        jax.ShapeDtypeStruct((bsz, nd, s), BF16),
        jax.ShapeDtypeStruct((bsz, DIFF_HEADS, s, nd // DIFF_HEADS), BF16),
        jax.ShapeDtypeStruct((bsz, s // tk, nd, tk), BF16),
    )
    out_specs = (
        pl.BlockSpec((None, nqm, tm), lambda b, i: (b, 0, i)),
        pl.BlockSpec((None, tm, nqm), lambda b, i: (b, i, 0)),
        pl.BlockSpec((None, tm // tk, nvm, tk), lambda b, i: (b, i, 0, 0)),
        pl.BlockSpec((None, nd, tm), lambda b, i: (b, 0, i)),
        pl.BlockSpec((None, DIFF_HEADS, tm, nd // DIFF_HEADS), lambda b, i: (b, 0, i, 0)),
        pl.BlockSpec((None, tm // tk, nd, tk), lambda b, i: (b, i, 0, 0)),
    )
    return pl.pallas_call(
        _in_proj_kernel,
        grid=(bsz, spt),
        in_specs=[
            pl.BlockSpec((tm, d), lambda b, i: (b * spt + i, 0)),
            _resident(w_nat), _resident(w_t), _resident(wq_t), _resident(wkn), _resident(wv_t),
            _resident(gq), _resident(gkv),
            pl.BlockSpec((tm, HEAD_PAD), lambda b, i: (i, 0)),
            pl.BlockSpec((tm, HEAD_PAD), lambda b, i: (i, 0)),
            pl.BlockSpec((half, tm), lambda b, i: (0, i)),
            pl.BlockSpec((half, tm), lambda b, i: (0, i)),
        ],
        out_specs=out_specs,
        out_shape=out_shape,
        compiler_params=_params(("arbitrary", "arbitrary")),
        name="in_proj",
    )(h, w_nat, w_t, wq_t, wkn, wv_t, gq, gkv, cos_k, sin_k, cos_q, sin_q)


BIAS_TILES = 3


def _rel_bias_kernel(rb_ref, o_ref, *, tq, tk):
    c = pl.program_id(0) - 1
    kk = lax.broadcasted_iota(jnp.int32, (tk, tq), 0)
    qq = lax.broadcasted_iota(jnp.int32, (tk, tq), 1)
    n = jnp.maximum(qq - kk - c * tk, 0)
    max_exact = REL_BUCKETS // 2
    nf = jnp.maximum(n, 1).astype(F32)
    large = max_exact + (jnp.log(nf / max_exact) / math.log(REL_MAX_DIST / max_exact)
                         * (REL_BUCKETS - max_exact)).astype(jnp.int32)
    large = jnp.minimum(large, REL_BUCKETS - 1)
    bucket = jnp.where(n < max_exact, n, large)
    for m in range(o_ref.shape[0]):
        acc = jnp.full((tk, tq), rb_ref[0, m], F32)
        for bkt in range(1, REL_BUCKETS):
            acc = jnp.where(bucket >= bkt, rb_ref[bkt, m], acc)
        o_ref[m] = (acc - rb_ref[REL_BUCKETS - 1, m]) * LOG2E


def _rel_bias_tiles(rel_bias, tq, tk):
    assert tq == 2 * tk and tk + 1 >= REL_MAX_DIST
    nmaps = rel_bias.shape[1]
    return pl.pallas_call(
        functools.partial(_rel_bias_kernel, tq=tq, tk=tk),
        grid=(BIAS_TILES,),
        in_specs=[pl.BlockSpec(memory_space=pltpu.SMEM)],
        out_specs=pl.BlockSpec((nmaps, None, tk, tq), lambda t: (0, t, 0, 0)),
        out_shape=jax.ShapeDtypeStruct((nmaps, BIAS_TILES, tk, tq), F32),
        compiler_params=_params(("arbitrary",)),
        name="rel_bias",
    )(rel_bias)


SCORES_AHEAD = 3
VALUES_BEHIND = 1
ONES_ROWS = 16


def _flash_pipeline(q_fn, k_fn, vt_fn, logits_fn, finalize_fn, s_ref, mt_ref, p_ref, al_ref, m_ref, acc_ref,
                    *, nstreams, nq, tq, tk, order):
    tiles = [(g, i, t, t - 2 * i) for g in range(nstreams) for i in range(nq) for t in range(2 * i + 2)]
    ones = jnp.ones((ONES_ROWS, tk), BF16)
    ahead = s_ref.shape[0] - 1
    behind = p_ref.shape[0] - 1

    def lane0(c):
        return tk if c == 1 else 0

    def scores(n):
        g, i, t, c = tiles[n]
        slot, lo = n % (ahead + 1), lane0(c)
        s = logits_fn(g, _dot(k_fn(g, t), q_fn(g, i, lo)), c, lo)
        s_ref[slot, :, lo:] = s
        mt_ref[slot, :, lo:] = jnp.max(s, axis=0, keepdims=True)

    def softmax(n):
        g, i, t, c = tiles[n]
        sslot, slot, lo = n % (ahead + 1), n % (behind + 1), lane0(c)
        m_new = mt_ref[sslot, :, lo:]
        if t > 0:
            m_prev = m_ref[:, lo:]
            m_new = jnp.maximum(m_prev, m_new)
            al_ref[slot, :, lo:] = jnp.exp2(m_prev - m_new)
        m_ref[:, lo:] = m_new
        p_ref[slot, :, lo:] = jnp.exp2(s_ref[sslot, :, lo:] - m_new).astype(BF16)

    def values(n):
        g, i, t, c = tiles[n]
        slot, lo = n % (behind + 1), lane0(c)
        pv = _dot(jnp.concatenate([vt_fn(g, t), ones], axis=0), p_ref[slot, :, lo:])
        if t > 0:
            pv = al_ref[slot, :, lo:] * acc_ref[:, lo:] + pv
        acc_ref[:, lo:] = pv
        if c == 1:
            acc = acc_ref[...]
            dv = acc.shape[0] - ONES_ROWS
            finalize_fn(g, i, acc[:dv] / acc[dv:dv + 1])

    for n in range(ahead):
        scores(n)
    assert sorted(order) == ["c", "s", "v"] and (behind > 0 or order.index("s") < order.index("v"))
    for n in range(len(tiles) + behind):
        for stage in order:
            if stage == "v" and behind <= n:
                values(n - behind)
            if stage == "c" and n + ahead < len(tiles):
                scores(n + ahead)
            if stage == "s" and n < len(tiles):
                softmax(n)


def _tri_mask(s):
    kk = lax.broadcasted_iota(jnp.int32, s.shape, 0)
    qq = lax.broadcasted_iota(jnp.int32, s.shape, 1)
    return jnp.where(kk <= qq, s, -jnp.inf)


def _flash_scratch(dv, tq, tk):
    return [pltpu.VMEM((SCORES_AHEAD + 1, tk, tq), F32), pltpu.VMEM((SCORES_AHEAD + 1, 1, tq), F32),
            pltpu.VMEM((VALUES_BEHIND + 1, tk, tq), BF16), pltpu.VMEM((VALUES_BEHIND + 1, 1, tq), F32),
            pltpu.VMEM((1, tq), F32), pltpu.VMEM((dv + ONES_ROWS, tq), F32)]


def _mla_attn_kernel(qt_ref, k_ref, vt_ref, o_ref, *scratch, heads, tq, tk):

    def q_fn(g, i, lo):
        return qt_ref[g * HEAD_PAD:(g + 1) * HEAD_PAD, i * tq + lo:(i + 1) * tq]

    def k_fn(g, t):
        return k_ref[t * tk:(t + 1) * tk, g * HEAD_PAD:(g + 1) * HEAD_PAD]

    def vt_fn(g, t):
        return vt_ref[t, g * MLA_V:(g + 1) * MLA_V, :]

    def logits(g, s, c, lo):
        return _tri_mask(s) if c >= 0 else s

    def finalize(g, i, o):
        o_ref[g * MLA_V:(g + 1) * MLA_V, i * tq:(i + 1) * tq] = o.astype(BF16)

    _flash_pipeline(q_fn, k_fn, vt_fn, logits, finalize, *scratch,
                    nstreams=heads, nq=qt_ref.shape[1] // tq, tq=tq, tk=tk, order=MLA_STAGE_ORDER)


def _mla_attn(qt, k, vt, tq, tk):
    bsz, nq, s = qt.shape
    nkt = s // tk
    hps = MLA_HEADS_PER_STEP
    assert tq == 2 * tk and s % tq == 0 and MLA_HEADS % hps == 0
    return pl.pallas_call(
        functools.partial(_mla_attn_kernel, heads=hps, tq=tq, tk=tk),
        grid=(MLA_HEADS // hps, bsz),
        in_specs=[
            pl.BlockSpec((None, hps * HEAD_PAD, s), lambda h, b: (b, h, 0)),
            pl.BlockSpec((None, s, hps * HEAD_PAD), lambda h, b: (b, 0, h)),
            pl.BlockSpec((None, nkt, hps * MLA_V, tk), lambda h, b: (b, 0, h, 0)),
        ],
        out_specs=pl.BlockSpec((None, hps * MLA_V, s), lambda h, b: (b, h, 0)),
        out_shape=jax.ShapeDtypeStruct((bsz, MLA_HEADS * MLA_V, s), BF16),
        scratch_shapes=_flash_scratch(MLA_V, tq, tk),
        compiler_params=_params(("arbitrary", "arbitrary")),
        name="mla_attn",
    )(qt, k, vt)


def _diff_attn_kernel(qt_ref, k_ref, vt_ref, bias_ref, lamv_ref, gn_ref, o_ref, o0_ref, qm_ref, *scratch, tq, tk):
    qt = qt_ref[...]
    row = lax.broadcasted_iota(jnp.int32, qt.shape, 0)
    for g in range(2):
        keep = jnp.logical_and(row >= g * DIFF_D, row < (g + 1) * DIFF_D)
        qm_ref[g] = jnp.where(keep, qt, jnp.zeros_like(qt))

    lv = lamv_ref[...]
    lam = (jnp.exp(jnp.sum(lv[0:1] * lv[1:2], axis=1, keepdims=True))
           - jnp.exp(jnp.sum(lv[2:3] * lv[3:4], axis=1, keepdims=True)) + LAM_INIT)

    def q_fn(g, i, lo):
        return qm_ref[g, :, i * tq + lo:(i + 1) * tq]

    def k_fn(g, t):
        return k_ref[t * tk:(t + 1) * tk, :]

    def vt_fn(g, t):
        return vt_ref[t]

    def logits(g, s, c, lo):
        if c >= -1:
            s = s + bias_ref[g, c + 1, :, lo:]
        return _tri_mask(s) if c >= 0 else s

    def finalize(g, i, o):
        cols = slice(i * tq, (i + 1) * tq)
        if g == 0:
            o0_ref[:, cols] = o
        else:
            d = o0_ref[:, cols] - lam * o
            d = d * lax.rsqrt(jnp.mean(d * d, axis=0, keepdims=True) + RMS_EPS) * gn_ref[...] * (1.0 - LAM_INIT)
            o_ref[:, cols] = d.astype(BF16)

    _flash_pipeline(q_fn, k_fn, vt_fn, logits, finalize, *scratch,
                    nstreams=2, nq=qt_ref.shape[1] // tq, tq=tq, tk=tk, order=DIFF_STAGE_ORDER)


def _diff_attn(qt, k, vt, bias, lamv, gn, tq, tk):
    bsz, nd, s = qt.shape
    nkt = s // tk
    dv = 2 * DIFF_D
    assert tq == 2 * tk and s % tq == 0
    return pl.pallas_call(
        functools.partial(_diff_attn_kernel, tq=tq, tk=tk),
        grid=(DIFF_HEADS, bsz),
        in_specs=[
            pl.BlockSpec((None, dv, s), lambda h, b: (b, h, 0)),
            pl.BlockSpec((None, None, s, dv), lambda h, b: (b, h, 0, 0)),
            pl.BlockSpec((None, nkt, dv, tk), lambda h, b: (b, 0, h, 0)),
            pl.BlockSpec((2, BIAS_TILES, tk, tq), lambda h, b: (h, 0, 0, 0)),
            pl.BlockSpec(lamv.shape, lambda h, b: (0, 0)),
            pl.BlockSpec(gn.shape, lambda h, b: (0, 0)),
        ],
        out_specs=pl.BlockSpec((None, dv, s), lambda h, b: (b, h, 0)),
        out_shape=jax.ShapeDtypeStruct((bsz, nd, s), BF16),
        scratch_shapes=[pltpu.VMEM((dv, s), F32), pltpu.VMEM((2, dv, s), BF16)] + _flash_scratch(dv, tq, tk),
        compiler_params=_params(("arbitrary", "arbitrary")),
        name="diff_attn",
    )(qt, k, vt, bias, lamv, gn)


def _post_mixer_kernel(h_ref, omt_ref, odt_ref, wum_ref, wud_ref, wg_ref, bg_ref, wo_ref, g_ref, b_ref,
                       o_ref):
    d = h_ref.shape[1]
    for rows in _chains(h_ref.shape[0]):
        h = h_ref[rows, :]
        z = _dot(h.astype(BF16), wg_ref[...]) + bg_ref[...]
        gate = 1.0 / (1.0 + jnp.exp(-z))
        y_m = _dot_tn(omt_ref[:, rows], wum_ref[...])
        y_d = _dot_tn(odt_ref[:, rows], wud_ref[...])
        mixed = gate[:, :d] * y_m + gate[:, d:] * y_d
        mix = _dot(mixed.astype(BF16), wo_ref[...])
        o_ref[rows, :] = _layer_norm(ALPHA * h + mix, g_ref[...], b_ref[...])


def _post_mixer(h, omt, odt, bsz, s, w_up_mla, w_up_diff, w_gate, b_gate, w_o, g, b):
    n, d = h.shape
    tm = ROW_TILE
    spt = s // tm
    ws = [w_up_mla.astype(BF16), w_up_diff.astype(BF16), w_gate.astype(BF16), b_gate.reshape(1, -1),
          w_o.astype(BF16), g.reshape(1, d), b.reshape(1, d)]

    return pl.pallas_call(
        _post_mixer_kernel,
        grid=(bsz, spt),
        in_specs=[
            pl.BlockSpec((tm, d), lambda bb, i: (bb * spt + i, 0)),
            pl.BlockSpec((None, omt.shape[1], tm), lambda bb, i: (bb, 0, i)),
            pl.BlockSpec((None, odt.shape[1], tm), lambda bb, i: (bb, 0, i)),
        ] + [_resident(a) for a in ws],
        out_specs=pl.BlockSpec((tm, d), lambda bb, i: (bb * spt + i, 0)),
        out_shape=jax.ShapeDtypeStruct((n, d), F32),
        compiler_params=_params(("arbitrary", "arbitrary")),
        name="post_mixer",
    )(h, omt, odt, *ws)


def _mem_attn_kernel(h_ref, mem_ref, wkv_ref, wq_ref, wo_ref, g_ref, b_ref, o_ref):
    d = h_ref.shape[1]
    hd = d // MEM_HEADS
    kv = _dot(mem_ref[...].astype(BF16), wkv_ref[...])
    k, v = kv[:, :d].astype(BF16), kv[:, d:].astype(BF16)
    for rows in _chains(h_ref.shape[0], MEM_CHAIN_ROWS):
        h = h_ref[rows, :]
        q = _dot(h.astype(BF16), wq_ref[...]).astype(BF16)
        heads = [slice(hh * hd, (hh + 1) * hd) for hh in range(MEM_HEADS)]
        logits = [_dot_nt(q[:, cols], k[:, cols]) * (hd ** -0.5) for cols in heads]
        probs = []
        for s in logits:
            p = jnp.exp(s - jnp.max(s, axis=-1, keepdims=True))
            probs.append((p / jnp.sum(p, axis=-1, keepdims=True)).astype(BF16))
        outs = [_dot(p, v[:, cols]).astype(BF16) for cols, p in zip(heads, probs)]
        att = _dot(jnp.concatenate(outs, axis=1), wo_ref[...])
        o_ref[rows, :] = _layer_norm(ALPHA * h + att, g_ref[...], b_ref[...])


def _mem_attn(h, mem, bsz, s, w_kv, w_q, w_o, g, b):
    n, d = h.shape
    tm = WIDE_ROW_TILE
    assert s == tm
    m = mem.shape[1]
    ws = [w_kv.astype(BF16), w_q.astype(BF16), w_o.astype(BF16), g.reshape(1, d), b.reshape(1, d)]

    return pl.pallas_call(
        _mem_attn_kernel,
        grid=(bsz,),
        in_specs=[
            pl.BlockSpec((tm, d), lambda bb: (bb, 0)),
            pl.BlockSpec((None, m, d), lambda bb: (bb, 0, 0)),
        ] + [_resident(a) for a in ws],
        out_specs=pl.BlockSpec((tm, d), lambda bb: (bb, 0)),
        out_shape=jax.ShapeDtypeStruct((n, d), F32),
        compiler_params=_params(("arbitrary",)),
        name="mem_attn",
    )(h, mem, *ws)


def kernel(x, mem, rel_bias, ffn1_w1, ffn1_w3, ffn1_w2, ln1_g, ln1_b, w_in, q_norm_g, kv_norm_g, w_q_b, w_kv_b, lam_q1, lam_k1, lam_q2, lam_k2, diff_norm_g, w_gate, b_gate, w_up_mla, w_up_diff, w_o, ln2_g, ln2_b, mem_w_q, mem_w_kv, mem_w_o, ln3_g, ln3_b, ffn2_w1, ffn2_w3, ffn2_w2, ln4_g, ln4_b):
    bsz, s, d = x.shape
    assert ffn1_w1.shape[0] == DEPTH
    l = 0
    h = x.reshape(bsz * s, d)

    h = _ffn_ln(h, ffn1_w1[l], ffn1_w3[l], ffn1_w2[l], ln1_g[l], ln1_b[l])

    qtm, km, vtm, qtd, kd, vtd = _in_proj(h, bsz, s, w_in[l], q_norm_g[l], kv_norm_g[l], w_q_b[l], w_kv_b[l])
    bias = _rel_bias_tiles(rel_bias, ATTN_TQ, ATTN_TK)
    omt = _mla_attn(qtm, km, vtm, ATTN_TQ, ATTN_TK)
    lamv = jnp.stack([lam_q1[l], lam_k1[l], lam_q2[l], lam_k2[l]], axis=0)
    odt = _diff_attn(qtd, kd, vtd, bias, lamv, diff_norm_g[l].reshape(-1, 1), ATTN_TQ, ATTN_TK)
    h = _post_mixer(h, omt, odt, bsz, s, w_up_mla[l], w_up_diff[l], w_gate[l], b_gate[l], w_o[l],
                    ln2_g[l], ln2_b[l])

    h = _mem_attn(h, mem, bsz, s, mem_w_kv[l], mem_w_q[l], mem_w_o[l], ln3_g[l], ln3_b[l])

    h = _ffn_ln(h, ffn2_w1[l], ffn2_w3[l], ffn2_w2[l], ln4_g[l], ln4_b[l])
    return h.reshape(bsz, s, d)
```
